```python
import math
import jax, jax.numpy as jnp
from jax import lax
import numpy as np

D_MODEL = 1024
BATCH = 2
SEQ = 8192
DEPTH = 1

POOL_WINDOWS = (2, 4, 8, 16)
POOL_WIDTH = D_MODEL // 2
POOL_GROUP = POOL_WIDTH // len(POOL_WINDOWS)
HEAD_DIM = 64
ATTN_GROUPS = ((128, 1), (512, 4), (2048, 16))
HEADS_PER_GROUP = 4
N_HEADS = HEADS_PER_GROUP * len(ATTN_GROUPS)
ATTN_WIDTH = N_HEADS * HEAD_DIM
ATTN_OUT_WIDTH = HEADS_PER_GROUP * HEAD_DIM
ATTN_BLOCK = 128
ROT_DIM = HEAD_DIM // 4
ROPE_THETA = 500000.0
N_BRANCHES = 2
IN_WIDTH = POOL_WIDTH + 3 * ATTN_WIDTH + N_BRANCHES * D_MODEL
N_EXPERTS = 32
TOP_K = 4
D_EXPERT = D_MODEL
SWIGLU_ALPHA = 1.702
SWIGLU_LIMIT = 7.0
MOE_BLOCK = 256
DN_ALPHA = (2.0 * DEPTH) ** 0.25
DN_BETA = (8.0 * DEPTH) ** -0.25
LN_EPS = 1e-5
NEG_INF = -1e30

kernel_name = "hybrid_pool_dilated_attn_moe_deepnorm_adaln"


def layer_norm(x, g=None, b=None):
    xf = x.astype(jnp.float32)
    mu = jnp.mean(xf, axis=-1, keepdims=True)
    var = jnp.mean(jnp.square(xf - mu), axis=-1, keepdims=True)
    y = (xf - mu) * lax.rsqrt(var + LN_EPS)
    if g is not None:
        y = y * g.astype(jnp.float32) + b.astype(jnp.float32)
    return y.astype(x.dtype)


def pool_mixer(xp, pool_w, pool_scale):
    B, S, _ = xp.shape
    G = len(POOL_WINDOWS)
    xf = xp.reshape(B, S, G, POOL_GROUP).astype(jnp.float32)
    cs = jnp.cumsum(xf, axis=1)
    cs0 = jnp.concatenate([jnp.zeros_like(cs[:, :1]), cs], axis=1)
    t = jnp.arange(S)
    means = []
    for g, w in enumerate(POOL_WINDOWS):
        c_g = cs0[:, :, g]
        lo = jnp.concatenate([jnp.zeros((B, w - 1, POOL_GROUP), jnp.float32),
                              c_g[:, :S - w + 1]], axis=1)
        cnt = jnp.minimum(t + 1, w).astype(jnp.float32)[None, :, None]
        means.append((c_g[:, 1:] - lo) / cnt)
    mixed = (jnp.stack(means, axis=2) - xf).astype(xp.dtype)
    y = jnp.einsum('bsgc,gcd->bsgd', mixed, pool_w).reshape(B, S, POOL_WIDTH)
    return y * pool_scale


def partial_rotary(x, positions):
    half = ROT_DIM // 2
    inv_freq = jnp.power(ROPE_THETA, -jnp.arange(half, dtype=jnp.float32) * (2.0 / ROT_DIM))
    ang = positions.astype(jnp.float32)[..., None] * inv_freq
    cos = jnp.cos(ang)[:, :, None, :]
    sin = jnp.sin(ang)[:, :, None, :]
    xf = x.astype(jnp.float32)
    x1, x2 = xf[..., :half], xf[..., half:ROT_DIM]
    out = jnp.concatenate([x1 * cos - x2 * sin, x2 * cos + x1 * sin, xf[..., ROT_DIM:]], axis=-1)
    return out.astype(x.dtype)


def dilated_window_attention(q, k, v, window, dilation):
    B, S, H, E = q.shape
    n_back = window // dilation
    L = S // dilation
    nb = -(-L // ATTN_BLOCK)
    Lp = nb * ATTN_BLOCK

    def to_blocks(a):
        a = a.reshape(B, L, dilation, H, E).transpose(0, 2, 1, 3, 4)
        a = jnp.pad(a, ((0, 0), (0, 0), (0, Lp - L), (0, 0), (0, 0)))
        return a.reshape(B, dilation, nb, ATTN_BLOCK, H, E)

    def with_prev(a):
        prev = jnp.pad(a[:, :, :-1], ((0, 0), (0, 0), (1, 0), (0, 0), (0, 0), (0, 0)))
        return jnp.concatenate([prev, a], axis=3)

    qb = to_blocks(q)
    kk = with_prev(to_blocks(k))
    vv = with_prev(to_blocks(v))
    s = jnp.einsum('bdnqhe,bdnkhe->bdnhqk', qb, kk,
                   preferred_element_type=jnp.float32) * (E ** -0.5)
    qi = jnp.arange(ATTN_BLOCK)[:, None]
    kj = jnp.arange(2 * ATTN_BLOCK)[None, :]
    dist = ATTN_BLOCK + qi - kj
    band = (dist >= 0) & (dist <= n_back)
    has_prev = (jnp.arange(nb)[:, None, None] > 0) | (kj[None] >= ATTN_BLOCK)
    mask = band[None] & has_prev
    s = jnp.where(mask[None, None, :, None], s, NEG_INF)
    m = jnp.max(s, axis=-1, keepdims=True)
    p = jnp.exp(s - m)
    den = jnp.sum(p, axis=-1, keepdims=True)
    o = jnp.einsum('bdnhqk,bdnkhe->bdnqhe', p.astype(v.dtype), vv,
                   preferred_element_type=jnp.float32)
    o = o / jnp.swapaxes(den, 3, 4)
    lse = jnp.swapaxes((m + jnp.log(den))[..., 0], 3, 4)
    o = o.reshape(B, dilation, Lp, H, E)[:, :, :L].transpose(0, 2, 1, 3, 4).reshape(B, S, H, E)
    lse = lse.reshape(B, dilation, Lp, H)[:, :, :L].transpose(0, 2, 1, 3).reshape(B, S, H)
    return o, lse


def clamped_swiglu_expert(xb, wg, bg, wu, bu, wd, bd):
    g = xb @ wg + bg
    up = xb @ wu + bu
    g = jnp.minimum(g, SWIGLU_LIMIT)
    up = jnp.clip(up, -SWIGLU_LIMIT, SWIGLU_LIMIT)
    h = g * jax.nn.sigmoid(SWIGLU_ALPHA * g) * (up + 1.0)
    return h @ wd + bd


def moe_ffn(u, w_router, b_router, w_gate, b_gate, w_up, b_up, w_down, b_down):
    B, S, D = u.shape
    N = B * S
    A = N * TOP_K
    xt = u.reshape(N, D)
    logits = (xt @ w_router).astype(jnp.float32) + b_router.astype(jnp.float32)
    top_val, top_idx = lax.top_k(logits, TOP_K)
    probs = jax.nn.softmax(top_val, axis=-1)
    eid = top_idx.reshape(A).astype(jnp.int32)
    tid = jnp.arange(A, dtype=jnp.int32) // TOP_K
    wgt = probs.reshape(A)
    order = jnp.argsort(eid)
    se = eid[order]
    counts = jax.ops.segment_sum(jnp.ones((A,), jnp.int32), eid, num_segments=N_EXPERTS)
    offsets = jnp.cumsum(counts) - counts
    padded = (counts + MOE_BLOCK - 1) // MOE_BLOCK * MOE_BLOCK
    pad_ends = jnp.cumsum(padded)
    pad_offsets = pad_ends - padded
    dest = pad_offsets[se] + jnp.arange(A, dtype=jnp.int32) - offsets[se]
    nblk = -(-A // MOE_BLOCK) + N_EXPERTS
    P = nblk * MOE_BLOCK
    row_tok = jnp.full((P,), N, jnp.int32).at[dest].set(tid[order])
    row_w = jnp.zeros((P,), jnp.float32).at[dest].set(wgt[order])
    blk_exp = jnp.minimum(jnp.searchsorted(pad_ends, jnp.arange(nblk) * MOE_BLOCK, side='right'),
                          N_EXPERTS - 1).astype(jnp.int32)
    xpad = jnp.concatenate([xt, jnp.zeros((1, D), xt.dtype)], axis=0)
    xs = xpad[row_tok].reshape(nblk, MOE_BLOCK, D)

    def expert_block(args):
        xb, e = args
        return clamped_swiglu_expert(xb, w_gate[e], b_gate[e], w_up[e], b_up[e], w_down[e], b_down[e])

    ys = lax.map(expert_block, (xs, blk_exp)).reshape(P, D)
    out = jax.ops.segment_sum(ys.astype(jnp.float32) * row_w[:, None], row_tok,
                              num_segments=N + 1)[:N]
    return out.astype(u.dtype).reshape(B, S, D)


def setup_inputs(seed: int = 0) -> dict:
    key = jax.random.key(seed)
    ks = jax.random.split(key, 26)
    nrm = lambda k, shape, s: jax.random.normal(k, shape, jnp.float32) * s
    Dd, L = D_MODEL, DEPTH
    G = len(POOL_WINDOWS)
    x = jax.random.normal(ks[0], (BATCH, SEQ, Dd), jnp.float32)
    c = jax.random.normal(ks[1], (BATCH, Dd), jnp.float32)
    offs = jax.random.randint(ks[2], (BATCH, 1), 0, 4096, jnp.int32)
    positions = (jnp.arange(SEQ, dtype=jnp.int32)[None, :] + offs).astype(jnp.int32)
    return {
        "x": x,
        "c": c,
        "positions": positions,
        "w_ada": nrm(ks[3], (L, Dd, 6 * Dd), 0.1 * Dd ** -0.5),
        "b_ada": nrm(ks[4], (L, 6 * Dd), 0.01),
        "w_in": nrm(ks[5], (L, Dd, IN_WIDTH), Dd ** -0.5),
        "pool_w": nrm(ks[6], (L, G, POOL_GROUP, POOL_GROUP), POOL_GROUP ** -0.5),
        "pool_scale": 1.0 + nrm(ks[7], (L, POOL_WIDTH), 0.1),
        "w_pool_out": nrm(ks[8], (L, POOL_WIDTH, Dd), POOL_WIDTH ** -0.5),
        "w_attn_out": nrm(ks[9], (L, ATTN_OUT_WIDTH, Dd), ATTN_OUT_WIDTH ** -0.5),
        "w_o": nrm(ks[10], (L, Dd, Dd), DN_BETA * Dd ** -0.5),
        "ln1_g": 1.0 + nrm(ks[11], (L, Dd), 0.02),
        "ln1_b": nrm(ks[12], (L, Dd), 0.02),
        "w_router": nrm(ks[13], (L, Dd, N_EXPERTS), Dd ** -0.5),
        "b_router": nrm(ks[14], (L, N_EXPERTS), 0.01),
        "w_gate": nrm(ks[15], (L, N_EXPERTS, Dd, D_EXPERT), Dd ** -0.5),
        "b_gate": nrm(ks[16], (L, N_EXPERTS, D_EXPERT), 0.01),
        "w_up": nrm(ks[17], (L, N_EXPERTS, Dd, D_EXPERT), Dd ** -0.5),
        "b_up": nrm(ks[18], (L, N_EXPERTS, D_EXPERT), 0.01),
        "w_down": nrm(ks[19], (L, N_EXPERTS, D_EXPERT, Dd), DN_BETA * D_EXPERT ** -0.5),
        "b_down": nrm(ks[20], (L, N_EXPERTS, Dd), 0.01),
        "ln2_g": 1.0 + nrm(ks[21], (L, Dd), 0.02),
        "ln2_b": nrm(ks[22], (L, Dd), 0.02),
    }


def reference(x, c, positions, w_ada, b_ada, w_in, pool_w, pool_scale, w_pool_out, w_attn_out,
              w_o, ln1_g, ln1_b, w_router, b_router, w_gate, b_gate, w_up, b_up, w_down, b_down,
              ln2_g, ln2_b):
    B, S, D = x.shape
    splits = [POOL_WIDTH, POOL_WIDTH + ATTN_WIDTH, POOL_WIDTH + 2 * ATTN_WIDTH,
              POOL_WIDTH + 3 * ATTN_WIDTH, POOL_WIDTH + 3 * ATTN_WIDTH + D_MODEL]
    for l in range(DEPTH):
        ada = jax.nn.silu(c) @ w_ada[l] + b_ada[l]
        shift1, scale1, gate1, shift2, scale2, gate2 = jnp.split(ada[:, None, :], 6, axis=-1)

        u = layer_norm(x) * (1.0 + scale1) + shift1
        proj = u @ w_in[l]
        xp, q, k, v, g_p, g_a = jnp.split(proj, splits, axis=-1)

        pool_y = pool_mixer(xp, pool_w[l], pool_scale[l])

        q = partial_rotary(q.reshape(B, S, N_HEADS, HEAD_DIM), positions)
        k = partial_rotary(k.reshape(B, S, N_HEADS, HEAD_DIM), positions)
        v = v.reshape(B, S, N_HEADS, HEAD_DIM)
        outs, lses = [], []
        for gi, (window, dilation) in enumerate(ATTN_GROUPS):
            hs = slice(gi * HEADS_PER_GROUP, (gi + 1) * HEADS_PER_GROUP)
            o, lse = dilated_window_attention(q[:, :, hs], k[:, :, hs], v[:, :, hs], window, dilation)
            outs.append(o)
            lses.append(lse)
        mix_w = jax.nn.softmax(jnp.stack(lses, axis=0), axis=0)
        attn = jnp.sum(mix_w[..., None] * jnp.stack(outs, axis=0), axis=0)
        attn = attn.reshape(B, S, ATTN_OUT_WIDTH).astype(x.dtype)

        merged = (jax.nn.sigmoid(g_p) * (pool_y @ w_pool_out[l])
                  + jax.nn.sigmoid(g_a) * (attn @ w_attn_out[l]))
        mix = merged @ w_o[l]
        x = layer_norm(DN_ALPHA * x + (1.0 + gate1) * mix, ln1_g[l], ln1_b[l])

        u2 = layer_norm(x) * (1.0 + scale2) + shift2
        ffn = moe_ffn(u2, w_router[l], b_router[l], w_gate[l], b_gate[l], w_up[l], b_up[l],
                      w_down[l], b_down[l])
        x = layer_norm(DN_ALPHA * x + (1.0 + gate2) * ffn, ln2_g[l], ln2_b[l])
    return x
```

```python
import functools

import jax
import jax.numpy as jnp
import numpy as np
from jax import lax
from jax.experimental import pallas as pl
from jax.experimental.pallas import tpu as pltpu

F32 = jnp.float32
BF16 = jnp.bfloat16

D_MODEL = 1024
POOL_WINDOWS = (2, 4, 8, 16)
POOL_WIDTH = D_MODEL // 2
POOL_GROUP = POOL_WIDTH // len(POOL_WINDOWS)
POOL_HALO = 16
HEAD_DIM = 64
ATTN_GROUPS = ((128, 1), (512, 4), (2048, 16))
HEADS_PER_GROUP = 4
GROUP_WIDTH = HEADS_PER_GROUP * HEAD_DIM
N_HEADS = HEADS_PER_GROUP * len(ATTN_GROUPS)
ATTN_WIDTH = N_HEADS * HEAD_DIM
ATTN_BLOCK = 128
ROT_DIM = HEAD_DIM // 4
ROPE_THETA = 500000.0
N_EXPERTS = 32
TOP_K = 4
SWIGLU_ALPHA = 1.702
SWIGLU_LIMIT = 7.0
MOE_BLOCK = 256
DEPTH = 1
DN_ALPHA = (2.0 * DEPTH) ** 0.25
LN_EPS = 1e-5
NEG_INF = -1e30

OFF_Q = POOL_WIDTH
OFF_K = OFF_Q + ATTN_WIDTH
OFF_V = OFF_K + ATTN_WIDTH
OFF_GP = OFF_V + ATTN_WIDTH
OFF_GA = OFF_GP + D_MODEL
IN_WIDTH = OFF_GA + D_MODEL

VMEM_LIMIT_BYTES = 56 * 1024 * 1024
LANES = 128

PROJ_TILE = 512
POST_TILE = 256
ATTN_QROWS = 1024
DISPATCH_TILE = 512
COMBINE_TILE = 256


def _layer_norm(x):
    mu = jnp.mean(x, axis=-1, keepdims=True)
    xc = x - mu
    var = jnp.mean(xc * xc, axis=-1, keepdims=True)
    return xc * lax.rsqrt(var + LN_EPS)


def _dot(a, b):
    return jnp.dot(a, b, preferred_element_type=F32)


def _ada_kernel(c_ref, w_ref, b_ref, o_ref):
    c = c_ref[...]
    s = c * jax.nn.sigmoid(c)
    o_ref[...] = jnp.dot(s, w_ref[...], preferred_element_type=F32,
                         precision=lax.Precision.HIGHEST) + b_ref[...]


def _ada(c, w_ada, b_ada):
    B, D = c.shape
    rows = 8
    c_pad = jnp.pad(c, ((0, rows - B), (0, 0)))
    n_out = w_ada.shape[1]
    out = pl.pallas_call(
        _ada_kernel,
        grid=(n_out // D,),
        in_specs=[pl.BlockSpec((rows, D), lambda j: (0, 0)),
                  pl.BlockSpec((D, D), lambda j: (0, j)),
                  pl.BlockSpec((1, D), lambda j: (0, j))],
        out_specs=pl.BlockSpec((rows, D), lambda j: (0, j)),
        out_shape=jax.ShapeDtypeStruct((rows, n_out), F32),
        name="ada",
    )(c_pad, w_ada, b_ada.reshape(1, n_out))
    return out[:B].reshape(B, 6, D)


def _rope_tables():
    lane = np.arange(128)
    li = lane % HEAD_DIM
    half = ROT_DIM // 2
    inv_freq = jnp.power(ROPE_THETA, -jnp.arange(half, dtype=F32) * (2.0 / ROT_DIM))
    invf = inv_freq[li % half][None, :]
    m_cos = (li < ROT_DIM).astype(np.float32)[None, :]
    m_lo = (li < half).astype(np.float32)[None, :]
    m_hi = ((li >= half) & (li < ROT_DIM)).astype(np.float32)[None, :]
    return jnp.concatenate([invf, jnp.asarray(m_cos), jnp.asarray(m_lo), jnp.asarray(m_hi),
                            jnp.zeros((4, 128), F32)], axis=0)


def _proj_kernel(x_ref, xh_ref, pos_ref, ada_ref, rope_ref, win_ref, poolw_ref, pscale_ref, wpo_ref,
                 q1_ref, k1_ref, v1_ref, q4_ref, k4_ref, v4_ref, q16_ref, k16_ref, v16_ref,
                 pg_ref, sga_ref, xpe_ref, cls_ref):
    tm = x_ref.shape[1]
    i = pl.program_id(1)
    shift1 = ada_ref[0, 0:1, :]
    scale1 = ada_ref[0, 1:2, :]

    def modulated(xv):
        return (_layer_norm(xv) * (1.0 + scale1) + shift1).astype(BF16)

    u = modulated(x_ref[0])
    uh = modulated(xh_ref[0])

    xp = _dot(u, win_ref[:, 0:POOL_WIDTH])
    xph = _dot(uh, win_ref[:, 0:POOL_WIDTH])
    xph = jnp.where(i > 0, xph, 0.0)
    xpe_ref[0:POOL_HALO, :] = xph
    xpe_ref[POOL_HALO:, :] = xp
    tok = i * tm + lax.broadcasted_iota(jnp.int32, (tm, 1), 0)
    ys = []
    for g, w in enumerate(POOL_WINDOWS):
        cols = slice(g * POOL_GROUP, (g + 1) * POOL_GROUP)
        xg = xp[:, cols]
        acc = xg
        for j in range(1, w):
            acc = acc + xpe_ref[POOL_HALO - j:POOL_HALO - j + tm, cols]
        cnt = jnp.minimum(tok + 1, w).astype(F32)
        mixed = (acc / cnt - xg).astype(BF16)
        ys.append(_dot(mixed, poolw_ref[g]) * pscale_ref[:, cols])
    y = jnp.concatenate(ys, axis=1).astype(BF16)
    pooled = _dot(y, wpo_ref[...])
    g_p = _dot(u, win_ref[:, OFF_GP:OFF_GP + D_MODEL])
    pg_ref[0] = jax.nn.sigmoid(g_p) * pooled
    g_a = _dot(u, win_ref[:, OFF_GA:OFF_GA + D_MODEL])
    sga_ref[0] = jax.nn.sigmoid(g_a)

    pos = pos_ref[0].astype(F32)
    ang = pos * rope_ref[0:1, :]
    cos = jnp.cos(ang)
    sin = jnp.sin(ang)
    c_mul = jnp.where(rope_ref[1:2, :] > 0, cos, 1.0)
    s_lo = jnp.where(rope_ref[2:3, :] > 0, -sin, 0.0)
    s_hi = jnp.where(rope_ref[3:4, :] > 0, sin, 0.0)
    c_mul = jnp.concatenate([c_mul, c_mul], axis=1)
    s_lo = jnp.concatenate([s_lo, s_lo], axis=1)
    s_hi = jnp.concatenate([s_hi, s_hi], axis=1)
    half = ROT_DIM // 2

    def rotate(a):
        up = pltpu.roll(a, GROUP_WIDTH - half, axis=1)
        dn = pltpu.roll(a, half, axis=1)
        return a * c_mul + up * s_lo + dn * s_hi

    def emit(a, out_ref, dil):
        if dil == 1:
            out_ref[0, 0] = a.astype(BF16)
            return
        for c in range(GROUP_WIDTH // LANES):
            cls_ref[c] = a[:, c * LANES:(c + 1) * LANES]
        for r in range(dil):
            for c in range(GROUP_WIDTH // LANES):
                out_ref[0, r, :, c * LANES:(c + 1) * LANES] = (
                    cls_ref[c, pl.ds(r, tm // dil, stride=dil), :].astype(BF16))

    outs = ((q1_ref, k1_ref, v1_ref), (q4_ref, k4_ref, v4_ref), (q16_ref, k16_ref, v16_ref))
    for gi, (_, dil) in enumerate(ATTN_GROUPS):
        qo, ko, vo = outs[gi]
        c0 = gi * GROUP_WIDTH
        emit(rotate(_dot(u, win_ref[:, OFF_Q + c0:OFF_Q + c0 + GROUP_WIDTH])), qo, dil)
        emit(rotate(_dot(u, win_ref[:, OFF_K + c0:OFF_K + c0 + GROUP_WIDTH])), ko, dil)
        emit(_dot(u, win_ref[:, OFF_V + c0:OFF_V + c0 + GROUP_WIDTH]), vo, dil)


def _proj(x, positions, ada, w_in, pool_w, pool_scale, w_pool_out):
    B, S, D = x.shape
    tm = PROJ_TILE
    nt = S // tm
    halo_blocks = tm // POOL_HALO
    const2 = lambda b, i: (0, 0)
    in_specs = [
        pl.BlockSpec((1, tm, D), lambda b, i: (b, i, 0)),
        pl.BlockSpec((1, POOL_HALO, D), lambda b, i: (b, jnp.maximum(i * halo_blocks - 1, 0), 0)),
        pl.BlockSpec((1, tm, 1), lambda b, i: (b, i, 0)),
        pl.BlockSpec((1, 6, D), lambda b, i: (b, 0, 0)),
        pl.BlockSpec((8, 128), const2),
        pl.BlockSpec((D, IN_WIDTH), const2),
        pl.BlockSpec((len(POOL_WINDOWS), POOL_GROUP, POOL_GROUP), lambda b, i: (0, 0, 0)),
        pl.BlockSpec((1, POOL_WIDTH), const2),
        pl.BlockSpec((POOL_WIDTH, D), const2),
    ]
    out_specs, out_shapes = [], []
    for _, dil in ATTN_GROUPS:
        for _ in range(3):
            out_specs.append(pl.BlockSpec((1, dil, tm // dil, GROUP_WIDTH), lambda b, i: (b, 0, i, 0)))
            out_shapes.append(jax.ShapeDtypeStruct((B, dil, S // dil, GROUP_WIDTH), BF16))
    for _ in range(2):
        out_specs.append(pl.BlockSpec((1, tm, D), lambda b, i: (b, i, 0)))
        out_shapes.append(jax.ShapeDtypeStruct((B, S, D), F32))
    return pl.pallas_call(
        _proj_kernel,
        grid=(B, nt),
        in_specs=in_specs,
        out_specs=out_specs,
        out_shape=out_shapes,
        scratch_shapes=[pltpu.VMEM((tm + POOL_HALO, POOL_WIDTH), F32),
                        pltpu.VMEM((GROUP_WIDTH // LANES, tm, LANES), F32)],
        compiler_params=pltpu.CompilerParams(
            dimension_semantics=("parallel", "parallel"), vmem_limit_bytes=VMEM_LIMIT_BYTES),
        name="proj",
    )(x, x, positions.reshape(B, S, 1), ada, _rope_tables(), w_in.astype(BF16),
      pool_w.astype(BF16), pool_scale.reshape(1, POOL_WIDTH), w_pool_out.astype(BF16))


def _attn_kernel(q_ref, k_ref, v_ref, kh_ref, vh_ref, o_ref, lse_ref, kf_ref, vf_ref):
    qb = q_ref.shape[2]
    n = pl.program_id(2)
    kf_ref[0:ATTN_BLOCK, :] = kh_ref[0, 0]
    kf_ref[ATTN_BLOCK:, :] = k_ref[0, 0]
    vf_ref[0:ATTN_BLOCK, :] = vh_ref[0, 0]
    vf_ref[ATTN_BLOCK:, :] = v_ref[0, 0]
    qi = lax.broadcasted_iota(jnp.int32, (ATTN_BLOCK, 2 * ATTN_BLOCK), 0)
    kj = lax.broadcasted_iota(jnp.int32, (ATTN_BLOCK, 2 * ATTN_BLOCK), 1)
    band = (kj >= qi) & (kj <= qi + ATTN_BLOCK)

    def block(j, carry):
        r0 = pl.multiple_of(j * ATTN_BLOCK, ATTN_BLOCK)
        first_key = jnp.where((n > 0) | (j > 0), 0, ATTN_BLOCK)
        mask = band & (kj >= first_key)
        q = q_ref[0, 0, pl.ds(r0, ATTN_BLOCK), :]
        kk = kf_ref[pl.ds(r0, 2 * ATTN_BLOCK), :]
        vv = vf_ref[pl.ds(r0, 2 * ATTN_BLOCK), :]
        o_parts, l_parts = [], []
        for h in range(HEADS_PER_GROUP):
            hs = slice(h * HEAD_DIM, (h + 1) * HEAD_DIM)
            s = lax.dot_general(q[:, hs], kk[:, hs], (((1,), (1,)), ((), ())),
                                preferred_element_type=F32) * (HEAD_DIM ** -0.5)
            s = jnp.where(mask, s, NEG_INF)
            m = jnp.max(s, axis=-1, keepdims=True)
            p = jnp.exp(s - m)
            den = jnp.sum(p, axis=-1, keepdims=True)
            o = _dot(p.astype(BF16), vv[:, hs]) / den
            o_parts.append(o)
            l_parts.append(jnp.broadcast_to(m + jnp.log(den), (ATTN_BLOCK, HEAD_DIM)))
        o_ref[0, 0, pl.ds(r0, ATTN_BLOCK), :] = jnp.concatenate(o_parts, axis=1)
        lse_ref[0, 0, pl.ds(r0, ATTN_BLOCK), :] = jnp.concatenate(l_parts, axis=1)
        return carry

    lax.fori_loop(0, qb // ATTN_BLOCK, block, 0)


def _attention(q, k, v):
    B, dil, L, W = q.shape
    qb = min(L, ATTN_QROWS)
    per = qb // ATTN_BLOCK
    main = pl.BlockSpec((1, 1, qb, W), lambda b, r, n: (b, r, n, 0))
    halo = pl.BlockSpec((1, 1, ATTN_BLOCK, W), lambda b, r, n: (b, r, jnp.maximum(n * per - 1, 0), 0))
    return pl.pallas_call(
        _attn_kernel,
        grid=(B, dil, L // qb),
        in_specs=[main, main, main, halo, halo],
        out_specs=[main, main],
        out_shape=[jax.ShapeDtypeStruct((B, dil, L, W), F32)] * 2,
        scratch_shapes=[pltpu.VMEM((qb + ATTN_BLOCK, W), BF16)] * 2,
        compiler_params=pltpu.CompilerParams(
            dimension_semantics=("parallel", "parallel", "parallel"), vmem_limit_bytes=VMEM_LIMIT_BYTES),
        name=f"attn_d{dil}",
    )(q, k, v, k, v)


def _post_kernel(o1_ref, l1_ref, o4_ref, l4_ref, o16_ref, l16_ref, pg_ref, sga_ref, x_ref, ada_ref,
                 wao_ref, wo_ref, g1_ref, b1_ref, wrt_ref, brt_ref,
                 x1_ref, u2_ref, eid_ref, prob_ref, pos_ref, cnt_ref,
                 s0, s1, s2, s3, carry_ref):
    tm = x_ref.shape[1]
    first = (pl.program_id(0) == 0) & (pl.program_id(1) == 0)

    @pl.when(first)
    def _():
        carry_ref[...] = jnp.zeros_like(carry_ref)

    def token_major(src_ref, scr_ref, dil):
        if dil == 1:
            return src_ref[0, 0]
        for r in range(dil):
            for c in range(GROUP_WIDTH // LANES):
                scr_ref[c, pl.ds(r, tm // dil, stride=dil), :] = src_ref[0, r, :, c * LANES:(c + 1) * LANES]
        return jnp.concatenate([scr_ref[c] for c in range(GROUP_WIDTH // LANES)], axis=1)

    o1, l1 = o1_ref[0, 0], l1_ref[0, 0]
    o4, l4 = token_major(o4_ref, s0, 4), token_major(l4_ref, s1, 4)
    o16, l16 = token_major(o16_ref, s2, 16), token_major(l16_ref, s3, 16)
    mx = jnp.maximum(jnp.maximum(l1, l4), l16)
    e1, e4, e16 = jnp.exp(l1 - mx), jnp.exp(l4 - mx), jnp.exp(l16 - mx)
    attn = (e1 * o1 + e4 * o4 + e16 * o16) / (e1 + e4 + e16)

    merged = pg_ref[0] + sga_ref[0] * _dot(attn.astype(BF16), wao_ref[...])
    mix = _dot(merged.astype(BF16), wo_ref[...])
    gate1 = ada_ref[0, 2:3, :]
    shift2 = ada_ref[0, 3:4, :]
    scale2 = ada_ref[0, 4:5, :]
    x1 = _layer_norm(DN_ALPHA * x_ref[0] + (1.0 + gate1) * mix) * g1_ref[...] + b1_ref[...]
    x1_ref[0] = x1
    u2 = _layer_norm(x1) * (1.0 + scale2) + shift2
    u2_ref[0] = u2

    logits = lax.dot_general(wrt_ref[...], u2, (((1,), (1,)), ((), ())),
                             preferred_element_type=F32,
                             precision=lax.Precision.HIGHEST) + brt_ref[...]
    eidx = lax.broadcasted_iota(jnp.int32, (N_EXPERTS, tm), 0)
    work = logits
    vals, idxs = [], []
    for _ in range(TOP_K):
        m = jnp.max(work, axis=0, keepdims=True)
        idx = jnp.min(jnp.where(work == m, eidx, N_EXPERTS), axis=0, keepdims=True)
        vals.append(m)
        idxs.append(idx)
        work = jnp.where(eidx == idx, -jnp.inf, work)
    exps = [jnp.exp(vk - vals[0]) for vk in vals]
    tot = exps[0] + exps[1] + exps[2] + exps[3]
    sel = jnp.zeros((N_EXPERTS, tm), F32)
    for idx in idxs:
        sel = sel + (eidx == idx).astype(F32)
    tr = lax.broadcasted_iota(jnp.int32, (tm, tm), 0)
    tc = lax.broadcasted_iota(jnp.int32, (tm, tm), 1)
    before = (tr < tc).astype(BF16)
    prefix = _dot(sel.astype(BF16), before) + carry_ref[...]
    for k in range(TOP_K):
        eid_ref[k:k + 1, :] = idxs[k]
        prob_ref[k:k + 1, :] = exps[k] / tot
        pos_ref[k:k + 1, :] = jnp.sum(jnp.where(eidx == idxs[k], prefix, 0.0), axis=0,
                                      keepdims=True).astype(jnp.int32)
    carry = carry_ref[...] + jnp.sum(sel, axis=1, keepdims=True)
    carry_ref[...] = carry
    cnt_ref[...] = carry.astype(jnp.int32)


def _post(attn_outs, pg, sga, x, ada, w_attn_out, w_o, ln1_g, ln1_b, w_router, b_router):
    B, S, D = x.shape
    tm = POST_TILE
    nt = S // tm
    N = B * S
    const2 = lambda b, i: (0, 0)
    in_specs, args = [], []
    for (o, lse), (_, dil) in zip(attn_outs, ATTN_GROUPS):
        spec = pl.BlockSpec((1, dil, tm // dil, GROUP_WIDTH), lambda b, i: (b, 0, i, 0))
        in_specs += [spec, spec]
        args += [o, lse]
    tok_spec = pl.BlockSpec((1, tm, D), lambda b, i: (b, i, 0))
    in_specs += [tok_spec, tok_spec, tok_spec,
                 pl.BlockSpec((1, 6, D), lambda b, i: (b, 0, 0)),
                 pl.BlockSpec((GROUP_WIDTH, D), const2),
                 pl.BlockSpec((D, D), const2),
                 pl.BlockSpec((1, D), const2),
                 pl.BlockSpec((1, D), const2),
                 pl.BlockSpec((N_EXPERTS, D), const2),
                 pl.BlockSpec((N_EXPERTS, 1), const2)]
    args += [pg, sga, x, ada, w_attn_out.astype(BF16), w_o.astype(BF16),
             ln1_g.reshape(1, D), ln1_b.reshape(1, D), w_router.T, b_router.reshape(N_EXPERTS, 1)]
    route_spec = pl.BlockSpec((TOP_K, tm), lambda b, i: (0, b * nt + i))
    out_specs = [tok_spec, tok_spec, route_spec, route_spec, route_spec,
                 pl.BlockSpec((N_EXPERTS, 1), const2)]
    out_shapes = [jax.ShapeDtypeStruct((B, S, D), F32), jax.ShapeDtypeStruct((B, S, D), F32),
                  jax.ShapeDtypeStruct((TOP_K, N), jnp.int32), jax.ShapeDtypeStruct((TOP_K, N), F32),
                  jax.ShapeDtypeStruct((TOP_K, N), jnp.int32), jax.ShapeDtypeStruct((N_EXPERTS, 1), jnp.int32)]
    return pl.pallas_call(
        _post_kernel,
        grid=(B, nt),
        in_specs=in_specs,
        out_specs=out_specs,
        out_shape=out_shapes,
        scratch_shapes=([pltpu.VMEM((GROUP_WIDTH // LANES, tm, LANES), F32)] * 4
                        + [pltpu.VMEM((N_EXPERTS, 1), F32)]),
        compiler_params=pltpu.CompilerParams(
            dimension_semantics=("arbitrary", "arbitrary"), vmem_limit_bytes=VMEM_LIMIT_BYTES),
        name="post",
    )(*args)


def _row_copy(src_hbm, src_row, dst_ref, dst_row, sem):
    return pltpu.make_async_copy(src_hbm.at[pl.ds(src_row, 1), :], dst_ref.at[pl.ds(dst_row, 1), :], sem)


def _dispatch_kernel(dest_ref, u2_hbm, xs_init_hbm, xs_hbm, sem):
    del xs_init_hbm
    tt = dest_ref.shape[0] // TOP_K
    base = pl.program_id(0) * tt

    def start(a, carry):
        _row_copy(u2_hbm, base + a // TOP_K, xs_hbm, dest_ref[a], sem).start()
        return carry

    def wait(a, carry):
        _row_copy(u2_hbm, base + a // TOP_K, xs_hbm, dest_ref[a], sem).wait()
        return carry

    lax.fori_loop(0, tt * TOP_K, start, 0)
    lax.fori_loop(0, tt * TOP_K, wait, 0)


def _dispatch(dest_flat, u2, n_rows):
    N, D = u2.shape
    tt = DISPATCH_TILE
    return pl.pallas_call(
        _dispatch_kernel,
        grid=(N // tt,),
        in_specs=[pl.BlockSpec((tt * TOP_K,), lambda i: (i,), memory_space=pltpu.SMEM),
                  pl.BlockSpec(memory_space=pl.ANY),
                  pl.BlockSpec(memory_space=pl.ANY)],
        out_specs=pl.BlockSpec(memory_space=pl.ANY),
        out_shape=jax.ShapeDtypeStruct((n_rows, D), F32),
        scratch_shapes=[pltpu.SemaphoreType.DMA(())],
        input_output_aliases={2: 0},
        compiler_params=pltpu.CompilerParams(dimension_semantics=("arbitrary",)),
        name="dispatch",
    )(dest_flat, u2, jnp.zeros((n_rows, D), F32))


def _expert_kernel(blk_exp_ref, nused_ref, xs_ref, wg_ref, bg_ref, wu_ref, bu_ref, wd_ref, bd_ref,
                   ys_ref, wg_s, wu_s, wd_s):
    i = pl.program_id(0)
    used = i < nused_ref[0]
    prev = blk_exp_ref[jnp.maximum(i - 1, 0)]
    fresh = (i == 0) | (blk_exp_ref[i] != prev)

    @pl.when(used & fresh)
    def _():
        wg_s[...] = wg_ref[0].astype(BF16)
        wu_s[...] = wu_ref[0].astype(BF16)
        wd_s[...] = wd_ref[0].astype(BF16)

    @pl.when(used)
    def _():
        xb = xs_ref[...].astype(BF16)
        g = _dot(xb, wg_s[...]) + bg_ref[0]
        up = _dot(xb, wu_s[...]) + bu_ref[0]
        g = jnp.minimum(g, SWIGLU_LIMIT)
        up = jnp.clip(up, -SWIGLU_LIMIT, SWIGLU_LIMIT)
        h = g * jax.nn.sigmoid(SWIGLU_ALPHA * g) * (up + 1.0)
        ys_ref[...] = _dot(h.astype(BF16), wd_s[...]) + bd_ref[0]

    @pl.when(jnp.logical_not(used))
    def _():
        ys_ref[...] = jnp.zeros_like(ys_ref)


def _experts(blk_exp, nused, xs, w_gate, b_gate, w_up, b_up, w_down, b_down):
    P, D = xs.shape
    nblk = P // MOE_BLOCK
    E = w_gate.shape[0]

    def live(i, nu):
        return jnp.minimum(i, nu[0] - 1)

    w_spec = pl.BlockSpec((1, D, D), lambda i, be, nu: (be[live(i, nu)], 0, 0))
    b_spec = pl.BlockSpec((1, 1, D), lambda i, be, nu: (be[live(i, nu)], 0, 0))
    grid_spec = pltpu.PrefetchScalarGridSpec(
        num_scalar_prefetch=2,
        grid=(nblk,),
        in_specs=[pl.BlockSpec((MOE_BLOCK, D), lambda i, be, nu: (live(i, nu), 0)),
                  w_spec, b_spec, w_spec, b_spec, w_spec, b_spec],
        out_specs=pl.BlockSpec((MOE_BLOCK, D), lambda i, be, nu: (i, 0)),
        scratch_shapes=[pltpu.VMEM((D, D), BF16)] * 3,
    )
    return pl.pallas_call(
        _expert_kernel,
        grid_spec=grid_spec,
        out_shape=jax.ShapeDtypeStruct((P, D), F32),
        compiler_params=pltpu.CompilerParams(
            dimension_semantics=("arbitrary",), vmem_limit_bytes=VMEM_LIMIT_BYTES),
        name="experts",
    )(blk_exp, nused, xs, w_gate, b_gate.reshape(E, 1, D), w_up, b_up.reshape(E, 1, D),
      w_down, b_down.reshape(E, 1, D))


def _combine_kernel(dest_ref, ys_hbm, w_ref, x1_ref, ada_ref, g2_ref, b2_ref, out_ref, buf_ref, sem):
    tt = x1_ref.shape[0]

    def copy(a):
        return _row_copy(ys_hbm, dest_ref[a], buf_ref.at[a % TOP_K], a // TOP_K, sem)

    def start(a, carry):
        copy(a).start()
        return carry

    def wait(a, carry):
        copy(a).wait()
        return carry

    lax.fori_loop(0, tt * TOP_K, start, 0)
    lax.fori_loop(0, tt * TOP_K, wait, 0)
    w = w_ref[...]
    ffn = w[:, 0:1] * buf_ref[0]
    for k in range(1, TOP_K):
        ffn = ffn + w[:, k:k + 1] * buf_ref[k]
    gate2 = ada_ref[0, 5:6, :]
    y = DN_ALPHA * x1_ref[...] + (1.0 + gate2) * ffn
    out_ref[...] = _layer_norm(y) * g2_ref[...] + b2_ref[...]


def _combine(dest_flat, ys, w_tok, x1, ada, ln2_g, ln2_b, tiles_per_batch):
    N, D = x1.shape
    tt = COMBINE_TILE
    const = lambda i: (0, 0)
    return pl.pallas_call(
        _combine_kernel,
        grid=(N // tt,),
        in_specs=[pl.BlockSpec((tt * TOP_K,), lambda i: (i,), memory_space=pltpu.SMEM),
                  pl.BlockSpec(memory_space=pl.ANY),
                  pl.BlockSpec((tt, TOP_K), lambda i: (i, 0)),
                  pl.BlockSpec((tt, D), lambda i: (i, 0)),
                  pl.BlockSpec((1, 6, D), lambda i: (i // tiles_per_batch, 0, 0)),
                  pl.BlockSpec((1, D), const),
                  pl.BlockSpec((1, D), const)],
        out_specs=pl.BlockSpec((tt, D), lambda i: (i, 0)),
        out_shape=jax.ShapeDtypeStruct((N, D), F32),
        scratch_shapes=[pltpu.VMEM((TOP_K, tt, D), F32), pltpu.SemaphoreType.DMA(())],
        compiler_params=pltpu.CompilerParams(
            dimension_semantics=("arbitrary",), vmem_limit_bytes=VMEM_LIMIT_BYTES),
        name="combine",
    )(dest_flat, ys, w_tok, x1, ada, ln2_g.reshape(1, D), ln2_b.reshape(1, D))


def kernel(x, c, positions, w_ada, b_ada, w_in, pool_w, pool_scale, w_pool_out, w_attn_out, w_o,
           ln1_g, ln1_b, w_router, b_router, w_gate, b_gate, w_up, b_up, w_down, b_down, ln2_g, ln2_b):
    B, S, D = x.shape
    N = B * S
    assert D == D_MODEL and S % PROJ_TILE == 0 and S % (16 * ATTN_BLOCK) == 0
    for l in range(DEPTH):
        ada = _ada(c, w_ada[l], b_ada[l])
        proj_out = _proj(x, positions, ada, w_in[l], pool_w[l], pool_scale[l], w_pool_out[l])
        qkv, (pg, sga) = proj_out[:9], proj_out[9:]
        attn_outs = [_attention(*qkv[3 * g:3 * g + 3]) for g in range(len(ATTN_GROUPS))]
        x1, u2, eid, prob, pos, counts = _post(attn_outs, pg, sga, x, ada, w_attn_out[l], w_o[l],
                                               ln1_g[l], ln1_b[l], w_router[l], b_router[l])
        counts = counts.reshape(N_EXPERTS)
        padded = (counts + MOE_BLOCK - 1) // MOE_BLOCK * MOE_BLOCK
        pad_ends = jnp.cumsum(padded)
        pad_offsets = pad_ends - padded
        nblk = -(-(N * TOP_K) // MOE_BLOCK) + N_EXPERTS
        blk_exp = jnp.minimum(jnp.searchsorted(pad_ends, jnp.arange(nblk) * MOE_BLOCK, side='right'),
                              N_EXPERTS - 1).astype(jnp.int32)
        nused = (pad_ends[-1:] // MOE_BLOCK).astype(jnp.int32)
        dest_flat = (pad_offsets[eid] + pos).T.reshape(N * TOP_K).astype(jnp.int32)
        xs = _dispatch(dest_flat, u2.reshape(N, D), nblk * MOE_BLOCK)
        ys = _experts(blk_exp, nused, xs, w_gate[l], b_gate[l], w_up[l], b_up[l], w_down[l], b_down[l])
        out = _combine(dest_flat, ys, prob.T, x1.reshape(N, D), ada, ln2_g[l], ln2_b[l],
                       S // COMBINE_TILE)
        x = out.reshape(B, S, D)
    return x
```

```python
import functools

import jax
import jax.numpy as jnp
import numpy as np
from jax import lax
from jax.experimental import pallas as pl
from jax.experimental.pallas import tpu as pltpu

F32 = jnp.float32
BF16 = jnp.bfloat16

D_MODEL = 1024
POOL_WINDOWS = (2, 4, 8, 16)
POOL_WIDTH = D_MODEL // 2
POOL_GROUP = POOL_WIDTH // len(POOL_WINDOWS)
POOL_HALO = 16
HEAD_DIM = 64
ATTN_GROUPS = ((128, 1), (512, 4), (2048, 16))
HEADS_PER_GROUP = 4
GROUP_WIDTH = HEADS_PER_GROUP * HEAD_DIM
N_HEADS = HEADS_PER_GROUP * len(ATTN_GROUPS)
ATTN_WIDTH = N_HEADS * HEAD_DIM
ATTN_BLOCK = 128
ROT_DIM = HEAD_DIM // 4
ROPE_THETA = 500000.0
N_EXPERTS = 32
TOP_K = 4
SWIGLU_ALPHA = 1.702
SWIGLU_LIMIT = 7.0
MOE_BLOCK = 256
DEPTH = 1
DN_ALPHA = (2.0 * DEPTH) ** 0.25
LN_EPS = 1e-5
NEG_INF = -1e30

OFF_Q = POOL_WIDTH
OFF_K = OFF_Q + ATTN_WIDTH
OFF_V = OFF_K + ATTN_WIDTH
OFF_GP = OFF_V + ATTN_WIDTH
OFF_GA = OFF_GP + D_MODEL
IN_WIDTH = OFF_GA + D_MODEL

VMEM_LIMIT_BYTES = 56 * 1024 * 1024
LANES = 128

PROJ_TILE = 512
POST_TILE = 256
ATTN_QROWS = 1024
COMBINE_TILE = 512
COMBINE_MAX_PAIRS = 2 * N_EXPERTS + (COMBINE_TILE * TOP_K) // MOE_BLOCK


def _layer_norm(x):
    mu = jnp.mean(x, axis=-1, keepdims=True)
    xc = x - mu
    var = jnp.mean(xc * xc, axis=-1, keepdims=True)
    return xc * lax.rsqrt(var + LN_EPS)


def _dot(a, b):
    return jnp.dot(a, b, preferred_element_type=F32)


def _ada_kernel(c_ref, w_ref, b_ref, o_ref):
    c = c_ref[...]
    s = c * jax.nn.sigmoid(c)
    o_ref[...] = jnp.dot(s, w_ref[...], preferred_element_type=F32,
                         precision=lax.Precision.HIGHEST) + b_ref[...]


def _ada(c, w_ada, b_ada):
    B, D = c.shape
    rows = 8
    c_pad = jnp.pad(c, ((0, rows - B), (0, 0)))
    n_out = w_ada.shape[1]
    out = pl.pallas_call(
        _ada_kernel,
        grid=(n_out // D,),
        in_specs=[pl.BlockSpec((rows, D), lambda j: (0, 0)),
                  pl.BlockSpec((D, D), lambda j: (0, j)),
                  pl.BlockSpec((1, D), lambda j: (0, j))],
        out_specs=pl.BlockSpec((rows, D), lambda j: (0, j)),
        out_shape=jax.ShapeDtypeStruct((rows, n_out), F32),
        name="ada",
    )(c_pad, w_ada, b_ada.reshape(1, n_out))
    return out[:B].reshape(B, 6, D)


def _rope_tables():
    lane = np.arange(128)
    li = lane % HEAD_DIM
    half = ROT_DIM // 2
    inv_freq = jnp.power(ROPE_THETA, -jnp.arange(half, dtype=F32) * (2.0 / ROT_DIM))
    invf = inv_freq[li % half][None, :]
    m_cos = (li < ROT_DIM).astype(np.float32)[None, :]
    m_lo = (li < half).astype(np.float32)[None, :]
    m_hi = ((li >= half) & (li < ROT_DIM)).astype(np.float32)[None, :]
    return jnp.concatenate([invf, jnp.asarray(m_cos), jnp.asarray(m_lo), jnp.asarray(m_hi),
                            jnp.zeros((4, 128), F32)], axis=0)


def _proj_kernel(x_ref, xh_ref, pos_ref, ada_ref, rope_ref, win_ref, poolw_ref, pscale_ref, wpo_ref,
                 q1_ref, k1_ref, v1_ref, q4_ref, k4_ref, v4_ref, q16_ref, k16_ref, v16_ref,
                 pg_ref, sga_ref, xpe_ref, cls_ref):
    tm = x_ref.shape[1]
    i = pl.program_id(1)
    shift1 = ada_ref[0, 0:1, :]
    scale1 = ada_ref[0, 1:2, :]

    def modulated(xv):
        return (_layer_norm(xv) * (1.0 + scale1) + shift1).astype(BF16)

    u = modulated(x_ref[0])
    uh = modulated(xh_ref[0])

    xp = _dot(u, win_ref[:, 0:POOL_WIDTH])
    xph = _dot(uh, win_ref[:, 0:POOL_WIDTH])
    xph = jnp.where(i > 0, xph, 0.0)
    xpe_ref[0:POOL_HALO, :] = xph
    xpe_ref[POOL_HALO:, :] = xp
    tok = i * tm + lax.broadcasted_iota(jnp.int32, (tm, 1), 0)
    ys = []
    for g, w in enumerate(POOL_WINDOWS):
        cols = slice(g * POOL_GROUP, (g + 1) * POOL_GROUP)
        xg = xp[:, cols]
        acc = xg
        for j in range(1, w):
            acc = acc + xpe_ref[POOL_HALO - j:POOL_HALO - j + tm, cols]
        cnt = jnp.minimum(tok + 1, w).astype(F32)
        mixed = (acc / cnt - xg).astype(BF16)
        ys.append(_dot(mixed, poolw_ref[g]) * pscale_ref[:, cols])
    y = jnp.concatenate(ys, axis=1).astype(BF16)
    pooled = _dot(y, wpo_ref[...])
    g_p = _dot(u, win_ref[:, OFF_GP:OFF_GP + D_MODEL])
    pg_ref[0] = jax.nn.sigmoid(g_p) * pooled
    g_a = _dot(u, win_ref[:, OFF_GA:OFF_GA + D_MODEL])
    sga_ref[0] = jax.nn.sigmoid(g_a)

    pos = pos_ref[0].astype(F32)
    ang = pos * rope_ref[0:1, :]
    cos = jnp.cos(ang)
    sin = jnp.sin(ang)
    c_mul = jnp.where(rope_ref[1:2, :] > 0, cos, 1.0)
    s_lo = jnp.where(rope_ref[2:3, :] > 0, -sin, 0.0)
    s_hi = jnp.where(rope_ref[3:4, :] > 0, sin, 0.0)
    c_mul = jnp.concatenate([c_mul, c_mul], axis=1)
    s_lo = jnp.concatenate([s_lo, s_lo], axis=1)
    s_hi = jnp.concatenate([s_hi, s_hi], axis=1)
    half = ROT_DIM // 2

    def rotate(a):
        up = pltpu.roll(a, GROUP_WIDTH - half, axis=1)
        dn = pltpu.roll(a, half, axis=1)
        return a * c_mul + up * s_lo + dn * s_hi

    def emit(a, out_ref, dil):
        if dil == 1:
            out_ref[0, 0] = a.astype(BF16)
            return
        for c in range(GROUP_WIDTH // LANES):
            cls_ref[c] = a[:, c * LANES:(c + 1) * LANES]
        for r in range(dil):
            for c in range(GROUP_WIDTH // LANES):
                out_ref[0, r, :, c * LANES:(c + 1) * LANES] = (
                    cls_ref[c, pl.ds(r, tm // dil, stride=dil), :].astype(BF16))

    outs = ((q1_ref, k1_ref, v1_ref), (q4_ref, k4_ref, v4_ref), (q16_ref, k16_ref, v16_ref))
    for gi, (_, dil) in enumerate(ATTN_GROUPS):
        qo, ko, vo = outs[gi]
        c0 = gi * GROUP_WIDTH
        emit(rotate(_dot(u, win_ref[:, OFF_Q + c0:OFF_Q + c0 + GROUP_WIDTH])), qo, dil)
        emit(rotate(_dot(u, win_ref[:, OFF_K + c0:OFF_K + c0 + GROUP_WIDTH])), ko, dil)
        emit(_dot(u, win_ref[:, OFF_V + c0:OFF_V + c0 + GROUP_WIDTH]), vo, dil)


def _proj(x, positions, ada, w_in, pool_w, pool_scale, w_pool_out):
    B, S, D = x.shape
    tm = PROJ_TILE
    nt = S // tm
    halo_blocks = tm // POOL_HALO
    const2 = lambda b, i: (0, 0)
    in_specs = [
        pl.BlockSpec((1, tm, D), lambda b, i: (b, i, 0)),
        pl.BlockSpec((1, POOL_HALO, D), lambda b, i: (b, jnp.maximum(i * halo_blocks - 1, 0), 0)),
        pl.BlockSpec((1, tm, 1), lambda b, i: (b, i, 0)),
        pl.BlockSpec((1, 6, D), lambda b, i: (b, 0, 0)),
        pl.BlockSpec((8, 128), const2),
        pl.BlockSpec((D, IN_WIDTH), const2),
        pl.BlockSpec((len(POOL_WINDOWS), POOL_GROUP, POOL_GROUP), lambda b, i: (0, 0, 0)),
        pl.BlockSpec((1, POOL_WIDTH), const2),
        pl.BlockSpec((POOL_WIDTH, D), const2),
    ]
    out_specs, out_shapes = [], []
    for _, dil in ATTN_GROUPS:
        for _ in range(3):
            out_specs.append(pl.BlockSpec((1, dil, tm // dil, GROUP_WIDTH), lambda b, i: (b, 0, i, 0)))
            out_shapes.append(jax.ShapeDtypeStruct((B, dil, S // dil, GROUP_WIDTH), BF16))
    for _ in range(2):
        out_specs.append(pl.BlockSpec((1, tm, D), lambda b, i: (b, i, 0)))
        out_shapes.append(jax.ShapeDtypeStruct((B, S, D), F32))
    return pl.pallas_call(
        _proj_kernel,
        grid=(B, nt),
        in_specs=in_specs,
        out_specs=out_specs,
        out_shape=out_shapes,
        scratch_shapes=[pltpu.VMEM((tm + POOL_HALO, POOL_WIDTH), F32),
                        pltpu.VMEM((GROUP_WIDTH // LANES, tm, LANES), F32)],
        compiler_params=pltpu.CompilerParams(
            dimension_semantics=("parallel", "parallel"), vmem_limit_bytes=VMEM_LIMIT_BYTES),
        name="proj",
    )(x, x, positions.reshape(B, S, 1), ada, _rope_tables(), w_in.astype(BF16),
      pool_w.astype(BF16), pool_scale.reshape(1, POOL_WIDTH), w_pool_out.astype(BF16))


def _attn_kernel(q_ref, k_ref, v_ref, kh_ref, vh_ref, o_ref, lse_ref, kf_ref, vf_ref):
    qb = q_ref.shape[2]
    n = pl.program_id(2)
    kf_ref[0:ATTN_BLOCK, :] = kh_ref[0, 0]
    kf_ref[ATTN_BLOCK:, :] = k_ref[0, 0]
    vf_ref[0:ATTN_BLOCK, :] = vh_ref[0, 0]
    vf_ref[ATTN_BLOCK:, :] = v_ref[0, 0]
    qi = lax.broadcasted_iota(jnp.int32, (ATTN_BLOCK, 2 * ATTN_BLOCK), 0)
    kj = lax.broadcasted_iota(jnp.int32, (ATTN_BLOCK, 2 * ATTN_BLOCK), 1)
    band = (kj >= qi) & (kj <= qi + ATTN_BLOCK)

    def block(j, carry):
        r0 = pl.multiple_of(j * ATTN_BLOCK, ATTN_BLOCK)
        first_key = jnp.where((n > 0) | (j > 0), 0, ATTN_BLOCK)
        mask = band & (kj >= first_key)
        q = q_ref[0, 0, pl.ds(r0, ATTN_BLOCK), :]
        kk = kf_ref[pl.ds(r0, 2 * ATTN_BLOCK), :]
        vv = vf_ref[pl.ds(r0, 2 * ATTN_BLOCK), :]
        o_parts, l_parts = [], []
        for h in range(HEADS_PER_GROUP):
            hs = slice(h * HEAD_DIM, (h + 1) * HEAD_DIM)
            s = lax.dot_general(q[:, hs], kk[:, hs], (((1,), (1,)), ((), ())),
                                preferred_element_type=F32) * (HEAD_DIM ** -0.5)
            s = jnp.where(mask, s, NEG_INF)
            m = jnp.max(s, axis=-1, keepdims=True)
            p = jnp.exp(s - m)
            den = jnp.sum(p, axis=-1, keepdims=True)
            o = _dot(p.astype(BF16), vv[:, hs]) / den
            o_parts.append(o)
            l_parts.append(jnp.broadcast_to(m + jnp.log(den), (ATTN_BLOCK, HEAD_DIM)))
        o_ref[0, 0, pl.ds(r0, ATTN_BLOCK), :] = jnp.concatenate(o_parts, axis=1)
        lse_ref[0, 0, pl.ds(r0, ATTN_BLOCK), :] = jnp.concatenate(l_parts, axis=1)
        return carry

    lax.fori_loop(0, qb // ATTN_BLOCK, block, 0)


def _attention(q, k, v):
    B, dil, L, W = q.shape
    qb = min(L, ATTN_QROWS)
    per = qb // ATTN_BLOCK
    main = pl.BlockSpec((1, 1, qb, W), lambda b, r, n: (b, r, n, 0))
    halo = pl.BlockSpec((1, 1, ATTN_BLOCK, W), lambda b, r, n: (b, r, jnp.maximum(n * per - 1, 0), 0))
    return pl.pallas_call(
        _attn_kernel,
        grid=(B, dil, L // qb),
        in_specs=[main, main, main, halo, halo],
        out_specs=[main, main],
        out_shape=[jax.ShapeDtypeStruct((B, dil, L, W), F32)] * 2,
        scratch_shapes=[pltpu.VMEM((qb + ATTN_BLOCK, W), BF16)] * 2,
        compiler_params=pltpu.CompilerParams(
            dimension_semantics=("parallel", "parallel", "parallel"), vmem_limit_bytes=VMEM_LIMIT_BYTES),
        name=f"attn_d{dil}",
    )(q, k, v, k, v)


def _post_kernel(o1_ref, l1_ref, o4_ref, l4_ref, o16_ref, l16_ref, pg_ref, sga_ref, x_ref, ada_ref,
                 wao_ref, wo_ref, g1_ref, b1_ref, wrt_ref, brt_ref,
                 x1_ref, u2_ref, eid_ref, prob_ref, pos_ref, cum_ref, cnt_ref,
                 s0, s1, s2, s3, carry_ref):
    tm = x_ref.shape[1]
    first = (pl.program_id(0) == 0) & (pl.program_id(1) == 0)

    @pl.when(first)
    def _():
        carry_ref[...] = jnp.zeros_like(carry_ref)

    def token_major(src_ref, scr_ref, dil):
        if dil == 1:
            return src_ref[0, 0]
        for r in range(dil):
            for c in range(GROUP_WIDTH // LANES):
                scr_ref[c, pl.ds(r, tm // dil, stride=dil), :] = src_ref[0, r, :, c * LANES:(c + 1) * LANES]
        return jnp.concatenate([scr_ref[c] for c in range(GROUP_WIDTH // LANES)], axis=1)

    o1, l1 = o1_ref[0, 0], l1_ref[0, 0]
    o4, l4 = token_major(o4_ref, s0, 4), token_major(l4_ref, s1, 4)
    o16, l16 = token_major(o16_ref, s2, 16), token_major(l16_ref, s3, 16)
    mx = jnp.maximum(jnp.maximum(l1, l4), l16)
    e1, e4, e16 = jnp.exp(l1 - mx), jnp.exp(l4 - mx), jnp.exp(l16 - mx)
    attn = (e1 * o1 + e4 * o4 + e16 * o16) / (e1 + e4 + e16)

    merged = pg_ref[0] + sga_ref[0] * _dot(attn.astype(BF16), wao_ref[...])
    mix = _dot(merged.astype(BF16), wo_ref[...])
    gate1 = ada_ref[0, 2:3, :]
    shift2 = ada_ref[0, 3:4, :]
    scale2 = ada_ref[0, 4:5, :]
    x1 = _layer_norm(DN_ALPHA * x_ref[0] + (1.0 + gate1) * mix) * g1_ref[...] + b1_ref[...]
    x1_ref[0] = x1
    u2 = _layer_norm(x1) * (1.0 + scale2) + shift2
    u2_ref[0] = u2.astype(BF16)

    logits = lax.dot_general(wrt_ref[...], u2, (((1,), (1,)), ((), ())),
                             preferred_element_type=F32,
                             precision=lax.Precision.HIGHEST) + brt_ref[...]
    eidx = lax.broadcasted_iota(jnp.int32, (N_EXPERTS, tm), 0)
    work = logits
    vals, idxs = [], []
    for _ in range(TOP_K):
        m = jnp.max(work, axis=0, keepdims=True)
        idx = jnp.min(jnp.where(work == m, eidx, N_EXPERTS), axis=0, keepdims=True)
        vals.append(m)
        idxs.append(idx)
        work = jnp.where(eidx == idx, -jnp.inf, work)
    exps = [jnp.exp(vk - vals[0]) for vk in vals]
    tot = exps[0] + exps[1] + exps[2] + exps[3]
    sel = jnp.zeros((N_EXPERTS, tm), F32)
    for idx in idxs:
        sel = sel + (eidx == idx).astype(F32)
    tr = lax.broadcasted_iota(jnp.int32, (tm, tm), 0)
    tc = lax.broadcasted_iota(jnp.int32, (tm, tm), 1)
    before = (tr < tc).astype(BF16)
    prefix = _dot(sel.astype(BF16), before) + carry_ref[...]
    for k in range(TOP_K):
        eid_ref[0, k:k + 1, :] = idxs[k]
        prob_ref[0, k:k + 1, :] = exps[k] / tot
        pos_ref[0, k:k + 1, :] = jnp.sum(jnp.where(eidx == idxs[k], prefix, 0.0), axis=0,
                                         keepdims=True).astype(jnp.int32)
    cum_ref[0] = carry_ref[...].astype(jnp.int32)
    carry = carry_ref[...] + jnp.sum(sel, axis=1, keepdims=True)
    carry_ref[...] = carry
    cnt_ref[...] = carry.astype(jnp.int32)


def _post(attn_outs, pg, sga, x, ada, w_attn_out, w_o, ln1_g, ln1_b, w_router, b_router):
    B, S, D = x.shape
    tm = POST_TILE
    nt = S // tm
    N = B * S
    const2 = lambda b, i: (0, 0)
    in_specs, args = [], []
    for (o, lse), (_, dil) in zip(attn_outs, ATTN_GROUPS):
        spec = pl.BlockSpec((1, dil, tm // dil, GROUP_WIDTH), lambda b, i: (b, 0, i, 0))
        in_specs += [spec, spec]
        args += [o, lse]
    tok_spec = pl.BlockSpec((1, tm, D), lambda b, i: (b, i, 0))
    in_specs += [tok_spec, tok_spec, tok_spec,
                 pl.BlockSpec((1, 6, D), lambda b, i: (b, 0, 0)),
                 pl.BlockSpec((GROUP_WIDTH, D), const2),
                 pl.BlockSpec((D, D), const2),
                 pl.BlockSpec((1, D), const2),
                 pl.BlockSpec((1, D), const2),
                 pl.BlockSpec((N_EXPERTS, D), const2),
                 pl.BlockSpec((N_EXPERTS, 1), const2)]
    args += [pg, sga, x, ada, w_attn_out.astype(BF16), w_o.astype(BF16),
             ln1_g.reshape(1, D), ln1_b.reshape(1, D), w_router.T, b_router.reshape(N_EXPERTS, 1)]
    nc = N // tm
    route_spec = pl.BlockSpec((1, TOP_K, tm), lambda b, i: (b * nt + i, 0, 0))
    out_specs = [tok_spec, tok_spec, route_spec, route_spec, route_spec,
                 pl.BlockSpec((1, N_EXPERTS, 1), lambda b, i: (b * nt + i, 0, 0)),
                 pl.BlockSpec((N_EXPERTS, 1), const2)]
    out_shapes = [jax.ShapeDtypeStruct((B, S, D), F32), jax.ShapeDtypeStruct((B, S, D), BF16),
                  jax.ShapeDtypeStruct((nc, TOP_K, tm), jnp.int32), jax.ShapeDtypeStruct((nc, TOP_K, tm), F32),
                  jax.ShapeDtypeStruct((nc, TOP_K, tm), jnp.int32),
                  jax.ShapeDtypeStruct((nc, N_EXPERTS, 1), jnp.int32),
                  jax.ShapeDtypeStruct((N_EXPERTS, 1), jnp.int32)]
    return pl.pallas_call(
        _post_kernel,
        grid=(B, nt),
        in_specs=in_specs,
        out_specs=out_specs,
        out_shape=out_shapes,
        scratch_shapes=([pltpu.VMEM((GROUP_WIDTH // LANES, tm, LANES), F32)] * 4
                        + [pltpu.VMEM((N_EXPERTS, 1), F32)]),
        compiler_params=pltpu.CompilerParams(
            dimension_semantics=("arbitrary", "arbitrary"), vmem_limit_bytes=VMEM_LIMIT_BYTES),
        name="post",
    )(*args)


def _expert_kernel(be_ref, row0_ref, clo_ref, chi_ref, nused_ref,
                   eid_ref, pos_ref, prob_ref, u2_hbm, wg_ref, bg_ref, wu_ref, bu_ref, wd_ref, bd_ref,
                   ys_ref, wg_s, wu_s, wd_s, ubuf, xacc, wacc, sem, done_ref):
    i = pl.program_id(0)
    nblk = pl.num_programs(0)
    nused = nused_ref[0]
    used = i < nused
    ch = ubuf.shape[1]

    def chunk_copy(c, slot):
        return pltpu.make_async_copy(u2_hbm.at[c], ubuf.at[slot], sem.at[slot])

    @pl.when(i == 0)
    def _():
        done_ref[0] = 0
        chunk_copy(clo_ref[0], 0).start()

    prev = be_ref[jnp.maximum(i - 1, 0)]
    fresh = (i == 0) | (be_ref[i] != prev)

    @pl.when(used & fresh)
    def _():
        wg_s[...] = wg_ref[0].astype(BF16)
        wu_s[...] = wu_ref[0].astype(BF16)
        wd_s[...] = wd_ref[0].astype(BF16)

    @pl.when(used)
    def _():
        e = be_ref[i]
        row0 = row0_ref[i]
        clo = clo_ref[i]
        n = chi_ref[i] - clo + 1
        done = done_ref[0]
        next_first = clo_ref[jnp.minimum(i + 1, nblk - 1)]
        xacc[...] = jnp.zeros_like(xacc)
        wacc[...] = jnp.zeros_like(wacc)
        row_id = lax.broadcasted_iota(jnp.int32, (MOE_BLOCK, ch), 0)

        def chunk(s, carry):
            c = clo + s
            slot = (done + s) % 2
            chunk_copy(c, slot).wait()
            last = s + 1 == n

            @pl.when(jnp.logical_not(last) | (i + 1 < nused))
            def _():
                chunk_copy(jnp.where(last, next_first, c + 1), 1 - slot).start()

            match = eid_ref[c] == e
            rel = jnp.max(jnp.where(match, pos_ref[c] - row0, -1), axis=0, keepdims=True)
            wsel = jnp.sum(jnp.where(match, prob_ref[c], 0.0), axis=0, keepdims=True)
            hit = row_id == rel
            xacc[...] += _dot(jnp.where(hit, 1.0, 0.0).astype(BF16), ubuf[slot])
            wacc[...] += jnp.sum(jnp.where(hit, wsel, 0.0), axis=1, keepdims=True)
            return carry

        lax.fori_loop(0, n, chunk, 0)
        done_ref[0] = done + n

        xb = xacc[...].astype(BF16)
        g = _dot(xb, wg_s[...]) + bg_ref[0]
        up = _dot(xb, wu_s[...]) + bu_ref[0]
        g = jnp.minimum(g, SWIGLU_LIMIT)
        up = jnp.clip(up, -SWIGLU_LIMIT, SWIGLU_LIMIT)
        h = g * jax.nn.sigmoid(SWIGLU_ALPHA * g) * (up + 1.0)
        ys = _dot(h.astype(BF16), wd_s[...]) + bd_ref[0]
        ys_ref[0] = (ys * wacc[...]).astype(BF16)

    @pl.when(jnp.logical_not(used))
    def _():
        ys_ref[...] = jnp.zeros_like(ys_ref)


def _experts(tables, eid, pos, prob, u2, w_gate, b_gate, w_up, b_up, w_down, b_down, nblk):
    nc, ch, D = u2.shape
    E = w_gate.shape[0]
    n_tables = len(tables)

    def live(i, t):
        return jnp.minimum(i, t[n_tables - 1][0] - 1)

    w_spec = pl.BlockSpec((1, D, D), lambda i, *t: (t[0][live(i, t)], 0, 0))
    b_spec = pl.BlockSpec((1, 1, D), lambda i, *t: (t[0][live(i, t)], 0, 0))
    route_spec = pl.BlockSpec((nc, TOP_K, ch), lambda i, *t: (0, 0, 0))
    grid_spec = pltpu.PrefetchScalarGridSpec(
        num_scalar_prefetch=n_tables,
        grid=(nblk,),
        in_specs=[route_spec, route_spec, route_spec, pl.BlockSpec(memory_space=pl.ANY),
                  w_spec, b_spec, w_spec, b_spec, w_spec, b_spec],
        out_specs=pl.BlockSpec((1, MOE_BLOCK, D), lambda i, *t: (i, 0, 0)),
        scratch_shapes=[pltpu.VMEM((D, D), BF16)] * 3 + [
            pltpu.VMEM((2, ch, D), BF16), pltpu.VMEM((MOE_BLOCK, D), F32), pltpu.VMEM((MOE_BLOCK, 1), F32),
            pltpu.SemaphoreType.DMA((2,)), pltpu.SMEM((1,), jnp.int32)],
    )
    return pl.pallas_call(
        _expert_kernel,
        grid_spec=grid_spec,
        out_shape=jax.ShapeDtypeStruct((nblk, MOE_BLOCK, D), BF16),
        compiler_params=pltpu.CompilerParams(
            dimension_semantics=("arbitrary",), vmem_limit_bytes=VMEM_LIMIT_BYTES),
        name="experts",
    )(*tables, eid, pos, prob, u2, w_gate, b_gate.reshape(E, 1, D), w_up, b_up.reshape(E, 1, D),
      w_down, b_down.reshape(E, 1, D))


def _combine_kernel(jlo_ref, jhi_ref, bstart_ref,
                    eid_ref, pos_ref, x1_ref, ada_ref, g2_ref, b2_ref, ys_hbm,
                    out_ref, ybuf, acc, pairs, sem):
    c = pl.program_id(0)
    cht = x1_ref.shape[0]

    def scan_expert(e, count):
        def add(j, n):
            pairs[0, n] = e
            pairs[1, n] = j
            return n + 1
        return lax.fori_loop(jlo_ref[c * N_EXPERTS + e], jhi_ref[c * N_EXPERTS + e] + 1, add, count)

    npairs = lax.fori_loop(0, N_EXPERTS, scan_expert, 0)

    def block_copy(p, slot):
        blk = bstart_ref[pairs[0, p]] + pairs[1, p]
        return pltpu.make_async_copy(ys_hbm.at[blk], ybuf.at[slot], sem.at[slot])

    @pl.when(npairs > 0)
    def _():
        block_copy(0, 0).start()

    acc[...] = jnp.zeros_like(acc)
    eids = [eid_ref[:, k:k + 1] for k in range(TOP_K)]
    poss = [pos_ref[:, k:k + 1] for k in range(TOP_K)]
    col = lax.broadcasted_iota(jnp.int32, (cht, MOE_BLOCK), 1)

    def pair(p, carry):
        slot = p % 2
        block_copy(p, slot).wait()

        @pl.when(p + 1 < npairs)
        def _():
            block_copy(p + 1, 1 - slot).start()

        e = pairs[0, p]
        row0 = pairs[1, p] * MOE_BLOCK
        rel = jnp.full((cht, 1), -1, jnp.int32)
        for k in range(TOP_K):
            rel = jnp.where(eids[k] == e, poss[k] - row0, rel)
        hit = col == rel
        acc[...] += _dot(jnp.where(hit, 1.0, 0.0).astype(BF16), ybuf[slot])
        return carry

    lax.fori_loop(0, npairs, pair, 0)
    gate2 = ada_ref[0, 5:6, :]
    y = DN_ALPHA * x1_ref[...] + (1.0 + gate2) * acc[...]
    out_ref[...] = _layer_norm(y) * g2_ref[...] + b2_ref[...]


def _combine(tables, eid_tok, pos_tok, x1, ada, ln2_g, ln2_b, ys, tiles_per_batch):
    N, D = x1.shape
    cht = COMBINE_TILE
    const = lambda i, *t: (0, 0)
    tok4 = pl.BlockSpec((cht, TOP_K), lambda i, *t: (i, 0))
    grid_spec = pltpu.PrefetchScalarGridSpec(
        num_scalar_prefetch=len(tables),
        grid=(N // cht,),
        in_specs=[tok4, tok4,
                  pl.BlockSpec((cht, D), lambda i, *t: (i, 0)),
                  pl.BlockSpec((1, 6, D), lambda i, *t: (i // tiles_per_batch, 0, 0)),
                  pl.BlockSpec((1, D), const),
                  pl.BlockSpec((1, D), const),
                  pl.BlockSpec(memory_space=pl.ANY)],
        out_specs=pl.BlockSpec((cht, D), lambda i, *t: (i, 0)),
        scratch_shapes=[pltpu.VMEM((2, MOE_BLOCK, D), BF16), pltpu.VMEM((cht, D), F32),
                        pltpu.SMEM((2, COMBINE_MAX_PAIRS), jnp.int32), pltpu.SemaphoreType.DMA((2,))],
    )
    return pl.pallas_call(
        _combine_kernel,
        grid_spec=grid_spec,
        out_shape=jax.ShapeDtypeStruct((N, D), F32),
        compiler_params=pltpu.CompilerParams(
            dimension_semantics=("arbitrary",), vmem_limit_bytes=VMEM_LIMIT_BYTES),
        name="combine",
    )(*tables, eid_tok, pos_tok, x1, ada, ln2_g.reshape(1, D), ln2_b.reshape(1, D), ys)


def _routing_tables(cum, counts, nblk):
    nc, E = cum.shape
    i32 = jnp.int32
    cum_ext = jnp.concatenate([cum, counts[None, :]], axis=0)
    blocks = (counts + MOE_BLOCK - 1) // MOE_BLOCK
    b_end = jnp.cumsum(blocks)
    b_start = b_end - blocks
    nused = b_end[-1:].astype(i32)
    i = jnp.arange(nblk, dtype=i32)
    be = jnp.minimum(jnp.sum((i[:, None] >= b_end[None, :]).astype(i32), axis=1), E - 1)
    onehot = be[:, None] == jnp.arange(E, dtype=i32)[None, :]
    pick = lambda v: jnp.sum(jnp.where(onehot, v[None, :], 0), axis=1)
    row0 = (i - pick(b_start)) * MOE_BLOCK
    hi = jnp.minimum(row0 + MOE_BLOCK, pick(counts))
    cum_b = jnp.sum(jnp.where(onehot[:, None, :], cum_ext[None, :, :], 0), axis=2)
    clo = jnp.sum((cum_b[:, 1:] <= row0[:, None]).astype(i32), axis=1)
    chi = jnp.sum((cum_b[:, :nc] < hi[:, None]).astype(i32), axis=1) - 1
    clo = jnp.clip(clo, 0, nc - 1)
    chi = jnp.clip(chi, clo, nc - 1)
    expert_tables = (be.astype(i32), row0.astype(i32), clo.astype(i32), chi.astype(i32), nused)
    step = COMBINE_TILE // POST_TILE
    cum_c = cum_ext[::step]
    lo, hi_c = cum_c[:-1], cum_c[1:]
    jlo = lo // MOE_BLOCK
    jhi = jnp.where(hi_c > lo, (hi_c - 1) // MOE_BLOCK, jlo - 1)
    combine_tables = (jlo.reshape(-1).astype(i32), jhi.reshape(-1).astype(i32), b_start.astype(i32))
    return expert_tables, combine_tables


def kernel(x, c, positions, w_ada, b_ada, w_in, pool_w, pool_scale, w_pool_out, w_attn_out, w_o,
           ln1_g, ln1_b, w_router, b_router, w_gate, b_gate, w_up, b_up, w_down, b_down, ln2_g, ln2_b):
    B, S, D = x.shape
    N = B * S
    assert D == D_MODEL and S % PROJ_TILE == 0 and S % (16 * ATTN_BLOCK) == 0
    assert COMBINE_TILE % POST_TILE == 0 and S % COMBINE_TILE == 0
    nblk = -(-(N * TOP_K) // MOE_BLOCK) + N_EXPERTS
    for l in range(DEPTH):
        ada = _ada(c, w_ada[l], b_ada[l])
        proj_out = _proj(x, positions, ada, w_in[l], pool_w[l], pool_scale[l], w_pool_out[l])
        qkv, (pg, sga) = proj_out[:9], proj_out[9:]
        attn_outs = [_attention(*qkv[3 * g:3 * g + 3]) for g in range(len(ATTN_GROUPS))]
        x1, u2, eid, prob, pos, cum, counts = _post(attn_outs, pg, sga, x, ada, w_attn_out[l], w_o[l],
                                                    ln1_g[l], ln1_b[l], w_router[l], b_router[l])
        nc = N // POST_TILE
        expert_tables, combine_tables = _routing_tables(cum.reshape(nc, N_EXPERTS),
                                                        counts.reshape(N_EXPERTS), nblk)
        ys = _experts(expert_tables, eid, pos, prob, u2.reshape(nc, POST_TILE, D),
                      w_gate[l], b_gate[l], w_up[l], b_up[l], w_down[l], b_down[l], nblk)
        token_major = lambda a: a.transpose(0, 2, 1).reshape(N, TOP_K)
        out = _combine(combine_tables, token_major(eid), token_major(pos), x1.reshape(N, D), ada,
                       ln2_g[l], ln2_b[l], ys, S // COMBINE_TILE)
        x = out.reshape(B, S, D)
    return x
```

```python
import functools

import jax
import jax.numpy as jnp
import numpy as np
from jax import lax
from jax.experimental import pallas as pl
from jax.experimental.pallas import tpu as pltpu

F32 = jnp.float32
BF16 = jnp.bfloat16

D_MODEL = 1024
POOL_WINDOWS = (2, 4, 8, 16)
POOL_WIDTH = D_MODEL // 2
POOL_GROUP = POOL_WIDTH // len(POOL_WINDOWS)
POOL_HALO = 16
HEAD_DIM = 64
ATTN_GROUPS = ((128, 1), (512, 4), (2048, 16))
HEADS_PER_GROUP = 4
GROUP_WIDTH = HEADS_PER_GROUP * HEAD_DIM
N_HEADS = HEADS_PER_GROUP * len(ATTN_GROUPS)
ATTN_WIDTH = N_HEADS * HEAD_DIM
ATTN_BLOCK = 128
ROT_DIM = HEAD_DIM // 4
ROPE_THETA = 500000.0
N_EXPERTS = 32
TOP_K = 4
SWIGLU_ALPHA = 1.702
SWIGLU_LIMIT = 7.0
MOE_BLOCK = 256
DEPTH = 1
DN_ALPHA = (2.0 * DEPTH) ** 0.25
LN_EPS = 1e-5
NEG_INF = -1e30

OFF_Q = POOL_WIDTH
OFF_K = OFF_Q + ATTN_WIDTH
OFF_V = OFF_K + ATTN_WIDTH
OFF_GP = OFF_V + ATTN_WIDTH
OFF_GA = OFF_GP + D_MODEL
IN_WIDTH = OFF_GA + D_MODEL

VMEM_LIMIT_BYTES = 56 * 1024 * 1024
LANES = 128

PROJ_TILE = 512
POST_TILE = 512
ATTN_QROWS = 1024
ROW_ALIGN = 16
SORT_CHUNK = 512
LOCAL_ROWS = -(-(POST_TILE * TOP_K + N_EXPERTS * (ROW_ALIGN - 1)) // SORT_CHUNK) * SORT_CHUNK


def _layer_norm(x):
    mu = jnp.mean(x, axis=-1, keepdims=True)
    xc = x - mu
    var = jnp.mean(xc * xc, axis=-1, keepdims=True)
    return xc * lax.rsqrt(var + LN_EPS)


def _dot(a, b):
    return jnp.dot(a, b, preferred_element_type=F32)


def _ada_kernel(c_ref, w_ref, b_ref, o_ref):
    c = c_ref[...]
    s = c * jax.nn.sigmoid(c)
    o_ref[...] = jnp.dot(s, w_ref[...], preferred_element_type=F32,
                         precision=lax.Precision.HIGHEST) + b_ref[...]


def _ada(c, w_ada, b_ada):
    B, D = c.shape
    rows = 8
    c_pad = jnp.pad(c, ((0, rows - B), (0, 0)))
    n_out = w_ada.shape[1]
    out = pl.pallas_call(
        _ada_kernel,
        grid=(n_out // D,),
        in_specs=[pl.BlockSpec((rows, D), lambda j: (0, 0)),
                  pl.BlockSpec((D, D), lambda j: (0, j)),
                  pl.BlockSpec((1, D), lambda j: (0, j))],
        out_specs=pl.BlockSpec((rows, D), lambda j: (0, j)),
        out_shape=jax.ShapeDtypeStruct((rows, n_out), F32),
        name="ada",
    )(c_pad, w_ada, b_ada.reshape(1, n_out))
    return out[:B].reshape(B, 6, D)


def _rope_tables():
    lane = np.arange(128)
    li = lane % HEAD_DIM
    half = ROT_DIM // 2
    inv_freq = jnp.power(ROPE_THETA, -jnp.arange(half, dtype=F32) * (2.0 / ROT_DIM))
    invf = inv_freq[li % half][None, :]
    m_cos = (li < ROT_DIM).astype(np.float32)[None, :]
    m_lo = (li < half).astype(np.float32)[None, :]
    m_hi = ((li >= half) & (li < ROT_DIM)).astype(np.float32)[None, :]
    return jnp.concatenate([invf, jnp.asarray(m_cos), jnp.asarray(m_lo), jnp.asarray(m_hi),
                            jnp.zeros((4, 128), F32)], axis=0)


def _proj_kernel(x_ref, xh_ref, pos_ref, ada_ref, rope_ref, win_ref, poolw_ref, pscale_ref, wpo_ref,
                 q1_ref, k1_ref, v1_ref, q4_ref, k4_ref, v4_ref, q16_ref, k16_ref, v16_ref,
                 pg_ref, sga_ref, xpe_ref, cls_ref):
    tm = x_ref.shape[1]
    i = pl.program_id(1)
    shift1 = ada_ref[0, 0:1, :]
    scale1 = ada_ref[0, 1:2, :]

    def modulated(xv):
        return (_layer_norm(xv) * (1.0 + scale1) + shift1).astype(BF16)

    u = modulated(x_ref[0])
    uh = modulated(xh_ref[0])

    xp = _dot(u, win_ref[:, 0:POOL_WIDTH])
    xph = _dot(uh, win_ref[:, 0:POOL_WIDTH])
    xph = jnp.where(i > 0, xph, 0.0)
    xpe_ref[0:POOL_HALO, :] = xph
    xpe_ref[POOL_HALO:, :] = xp
    tok = i * tm + lax.broadcasted_iota(jnp.int32, (tm, 1), 0)
    ys = []
    for g, w in enumerate(POOL_WINDOWS):
        cols = slice(g * POOL_GROUP, (g + 1) * POOL_GROUP)
        xg = xp[:, cols]
        acc = xg
        for j in range(1, w):
            acc = acc + xpe_ref[POOL_HALO - j:POOL_HALO - j + tm, cols]
        cnt = jnp.minimum(tok + 1, w).astype(F32)
        mixed = (acc / cnt - xg).astype(BF16)
        ys.append(_dot(mixed, poolw_ref[g]) * pscale_ref[:, cols])
    y = jnp.concatenate(ys, axis=1).astype(BF16)
    pooled = _dot(y, wpo_ref[...])
    g_p = _dot(u, win_ref[:, OFF_GP:OFF_GP + D_MODEL])
    pg_ref[0] = jax.nn.sigmoid(g_p) * pooled
    g_a = _dot(u, win_ref[:, OFF_GA:OFF_GA + D_MODEL])
    sga_ref[0] = jax.nn.sigmoid(g_a)

    pos = pos_ref[0].astype(F32)
    ang = pos * rope_ref[0:1, :]
    cos = jnp.cos(ang)
    sin = jnp.sin(ang)
    c_mul = jnp.where(rope_ref[1:2, :] > 0, cos, 1.0)
    s_lo = jnp.where(rope_ref[2:3, :] > 0, -sin, 0.0)
    s_hi = jnp.where(rope_ref[3:4, :] > 0, sin, 0.0)
    c_mul = jnp.concatenate([c_mul, c_mul], axis=1)
    s_lo = jnp.concatenate([s_lo, s_lo], axis=1)
    s_hi = jnp.concatenate([s_hi, s_hi], axis=1)
    half = ROT_DIM // 2

    def rotate(a):
        up = pltpu.roll(a, GROUP_WIDTH - half, axis=1)
        dn = pltpu.roll(a, half, axis=1)
        return a * c_mul + up * s_lo + dn * s_hi

    def emit(a, out_ref, dil):
        if dil == 1:
            out_ref[0, 0] = a.astype(BF16)
            return
        for c in range(GROUP_WIDTH // LANES):
            cls_ref[c] = a[:, c * LANES:(c + 1) * LANES]
        for r in range(dil):
            for c in range(GROUP_WIDTH // LANES):
                out_ref[0, r, :, c * LANES:(c + 1) * LANES] = (
                    cls_ref[c, pl.ds(r, tm // dil, stride=dil), :].astype(BF16))

    outs = ((q1_ref, k1_ref, v1_ref), (q4_ref, k4_ref, v4_ref), (q16_ref, k16_ref, v16_ref))
    for gi, (_, dil) in enumerate(ATTN_GROUPS):
        qo, ko, vo = outs[gi]
        c0 = gi * GROUP_WIDTH
        emit(rotate(_dot(u, win_ref[:, OFF_Q + c0:OFF_Q + c0 + GROUP_WIDTH])), qo, dil)
        emit(rotate(_dot(u, win_ref[:, OFF_K + c0:OFF_K + c0 + GROUP_WIDTH])), ko, dil)
        emit(_dot(u, win_ref[:, OFF_V + c0:OFF_V + c0 + GROUP_WIDTH]), vo, dil)


def _proj(x, positions, ada, w_in, pool_w, pool_scale, w_pool_out):
    B, S, D = x.shape
    tm = PROJ_TILE
    nt = S // tm
    halo_blocks = tm // POOL_HALO
    const2 = lambda b, i: (0, 0)
    in_specs = [
        pl.BlockSpec((1, tm, D), lambda b, i: (b, i, 0)),
        pl.BlockSpec((1, POOL_HALO, D), lambda b, i: (b, jnp.maximum(i * halo_blocks - 1, 0), 0)),
        pl.BlockSpec((1, tm, 1), lambda b, i: (b, i, 0)),
        pl.BlockSpec((1, 6, D), lambda b, i: (b, 0, 0)),
        pl.BlockSpec((8, 128), const2),
        pl.BlockSpec((D, IN_WIDTH), const2),
        pl.BlockSpec((len(POOL_WINDOWS), POOL_GROUP, POOL_GROUP), lambda b, i: (0, 0, 0)),
        pl.BlockSpec((1, POOL_WIDTH), const2),
        pl.BlockSpec((POOL_WIDTH, D), const2),
    ]
    out_specs, out_shapes = [], []
    for _, dil in ATTN_GROUPS:
        for _ in range(3):
            out_specs.append(pl.BlockSpec((1, dil, tm // dil, GROUP_WIDTH), lambda b, i: (b, 0, i, 0)))
            out_shapes.append(jax.ShapeDtypeStruct((B, dil, S // dil, GROUP_WIDTH), BF16))
    for _ in range(2):
        out_specs.append(pl.BlockSpec((1, tm, D), lambda b, i: (b, i, 0)))
        out_shapes.append(jax.ShapeDtypeStruct((B, S, D), F32))
    return pl.pallas_call(
        _proj_kernel,
        grid=(B, nt),
        in_specs=in_specs,
        out_specs=out_specs,
        out_shape=out_shapes,
        scratch_shapes=[pltpu.VMEM((tm + POOL_HALO, POOL_WIDTH), F32),
                        pltpu.VMEM((GROUP_WIDTH // LANES, tm, LANES), F32)],
        compiler_params=pltpu.CompilerParams(
            dimension_semantics=("parallel", "parallel"), vmem_limit_bytes=VMEM_LIMIT_BYTES),
        name="proj",
    )(x, x, positions.reshape(B, S, 1), ada, _rope_tables(), w_in.astype(BF16),
      pool_w.astype(BF16), pool_scale.reshape(1, POOL_WIDTH), w_pool_out.astype(BF16))


def _attn_kernel(q_ref, k_ref, v_ref, kh_ref, vh_ref, o_ref, lse_ref, kf_ref, vf_ref):
    qb = q_ref.shape[2]
    n = pl.program_id(2)
    kf_ref[0:ATTN_BLOCK, :] = kh_ref[0, 0]
    kf_ref[ATTN_BLOCK:, :] = k_ref[0, 0]
    vf_ref[0:ATTN_BLOCK, :] = vh_ref[0, 0]
    vf_ref[ATTN_BLOCK:, :] = v_ref[0, 0]
    qi = lax.broadcasted_iota(jnp.int32, (ATTN_BLOCK, 2 * ATTN_BLOCK), 0)
    kj = lax.broadcasted_iota(jnp.int32, (ATTN_BLOCK, 2 * ATTN_BLOCK), 1)
    band = (kj >= qi) & (kj <= qi + ATTN_BLOCK)

    def block(j, carry):
        r0 = pl.multiple_of(j * ATTN_BLOCK, ATTN_BLOCK)
        first_key = jnp.where((n > 0) | (j > 0), 0, ATTN_BLOCK)
        mask = band & (kj >= first_key)
        q = q_ref[0, 0, pl.ds(r0, ATTN_BLOCK), :]
        kk = kf_ref[pl.ds(r0, 2 * ATTN_BLOCK), :]
        vv = vf_ref[pl.ds(r0, 2 * ATTN_BLOCK), :]
        o_parts, l_parts = [], []
        for h in range(HEADS_PER_GROUP):
            hs = slice(h * HEAD_DIM, (h + 1) * HEAD_DIM)
            s = lax.dot_general(q[:, hs], kk[:, hs], (((1,), (1,)), ((), ())),
                                preferred_element_type=F32) * (HEAD_DIM ** -0.5)
            s = jnp.where(mask, s, NEG_INF)
            m = jnp.max(s, axis=-1, keepdims=True)
            p = jnp.exp(s - m)
            den = jnp.sum(p, axis=-1, keepdims=True)
            o = _dot(p.astype(BF16), vv[:, hs]) / den
            o_parts.append(o)
            l_parts.append(jnp.broadcast_to(m + jnp.log(den), (ATTN_BLOCK, HEAD_DIM)))
        o_ref[0, 0, pl.ds(r0, ATTN_BLOCK), :] = jnp.concatenate(o_parts, axis=1)
        lse_ref[0, 0, pl.ds(r0, ATTN_BLOCK), :] = jnp.concatenate(l_parts, axis=1)
        return carry

    lax.fori_loop(0, qb // ATTN_BLOCK, block, 0)


def _attention(q, k, v):
    B, dil, L, W = q.shape
    qb = min(L, ATTN_QROWS)
    per = qb // ATTN_BLOCK
    main = pl.BlockSpec((1, 1, qb, W), lambda b, r, n: (b, r, n, 0))
    halo = pl.BlockSpec((1, 1, ATTN_BLOCK, W), lambda b, r, n: (b, r, jnp.maximum(n * per - 1, 0), 0))
    return pl.pallas_call(
        _attn_kernel,
        grid=(B, dil, L // qb),
        in_specs=[main, main, main, halo, halo],
        out_specs=[main, main],
        out_shape=[jax.ShapeDtypeStruct((B, dil, L, W), F32)] * 2,
        scratch_shapes=[pltpu.VMEM((qb + ATTN_BLOCK, W), BF16)] * 2,
        compiler_params=pltpu.CompilerParams(
            dimension_semantics=("parallel", "parallel", "parallel"), vmem_limit_bytes=VMEM_LIMIT_BYTES),
        name=f"attn_d{dil}",
    )(q, k, v, k, v)


def _post_kernel(o1_ref, l1_ref, o4_ref, l4_ref, o16_ref, l16_ref, pg_ref, sga_ref, x_ref, ada_ref,
                 wao_ref, wo_ref, g1_ref, b1_ref, wrt_ref, brt_ref,
                 x1_ref, u2_ref, slot_ref, prob_ref, cnt_ref,
                 s0, s1, s2, s3):
    tm = x_ref.shape[1]

    def token_major(src_ref, scr_ref, dil):
        if dil == 1:
            return src_ref[0, 0]
        for r in range(dil):
            for c in range(GROUP_WIDTH // LANES):
                scr_ref[c, pl.ds(r, tm // dil, stride=dil), :] = src_ref[0, r, :, c * LANES:(c + 1) * LANES]
        return jnp.concatenate([scr_ref[c] for c in range(GROUP_WIDTH // LANES)], axis=1)

    o1, l1 = o1_ref[0, 0], l1_ref[0, 0]
    o4, l4 = token_major(o4_ref, s0, 4), token_major(l4_ref, s1, 4)
    o16, l16 = token_major(o16_ref, s2, 16), token_major(l16_ref, s3, 16)
    mx = jnp.maximum(jnp.maximum(l1, l4), l16)
    e1, e4, e16 = jnp.exp(l1 - mx), jnp.exp(l4 - mx), jnp.exp(l16 - mx)
    attn = (e1 * o1 + e4 * o4 + e16 * o16) / (e1 + e4 + e16)

    merged = pg_ref[0] + sga_ref[0] * _dot(attn.astype(BF16), wao_ref[...])
    mix = _dot(merged.astype(BF16), wo_ref[...])
    gate1 = ada_ref[0, 2:3, :]
    shift2 = ada_ref[0, 3:4, :]
    scale2 = ada_ref[0, 4:5, :]
    x1 = _layer_norm(DN_ALPHA * x_ref[0] + (1.0 + gate1) * mix) * g1_ref[...] + b1_ref[...]
    x1_ref[0] = x1
    u2 = _layer_norm(x1) * (1.0 + scale2) + shift2
    u2_ref[0] = u2.astype(BF16)

    logits = lax.dot_general(wrt_ref[...], u2, (((1,), (1,)), ((), ())),
                             preferred_element_type=F32,
                             precision=lax.Precision.HIGHEST) + brt_ref[...]
    eidx = lax.broadcasted_iota(jnp.int32, (N_EXPERTS, tm), 0)
    work = logits
    vals, idxs = [], []
    for _ in range(TOP_K):
        m = jnp.max(work, axis=0, keepdims=True)
        idx = jnp.min(jnp.where(work == m, eidx, N_EXPERTS), axis=0, keepdims=True)
        vals.append(m)
        idxs.append(idx)
        work = jnp.where(eidx == idx, -jnp.inf, work)
    exps = [jnp.exp(vk - vals[0]) for vk in vals]
    tot = exps[0] + exps[1] + exps[2] + exps[3]
    sel = jnp.zeros((N_EXPERTS, tm), F32)
    for idx in idxs:
        sel = sel + (eidx == idx).astype(F32)
    tr = lax.broadcasted_iota(jnp.int32, (tm, tm), 0)
    tc = lax.broadcasted_iota(jnp.int32, (tm, tm), 1)
    rank = _dot(sel.astype(BF16), (tr < tc).astype(BF16))
    cnt = jnp.sum(sel, axis=1, keepdims=True)
    run = jnp.floor((cnt + (ROW_ALIGN - 1)) * (1.0 / ROW_ALIGN)) * ROW_ALIGN
    er = lax.broadcasted_iota(jnp.int32, (N_EXPERTS, N_EXPERTS), 0)
    ec = lax.broadcasted_iota(jnp.int32, (N_EXPERTS, N_EXPERTS), 1)
    run_start = _dot((ec < er).astype(BF16),
                     jnp.broadcast_to(run, (N_EXPERTS, LANES)).astype(BF16))[:, 0:1]
    slot = rank + run_start
    for k in range(TOP_K):
        slot_ref[0, k:k + 1, :] = jnp.sum(jnp.where(eidx == idxs[k], slot, 0.0), axis=0,
                                          keepdims=True).astype(jnp.int32)
        prob_ref[0, k:k + 1, :] = exps[k] / tot
    cnt_ref[0] = cnt.astype(jnp.int32)


def _post(attn_outs, pg, sga, x, ada, w_attn_out, w_o, ln1_g, ln1_b, w_router, b_router):
    B, S, D = x.shape
    tm = POST_TILE
    nt = S // tm
    N = B * S
    const2 = lambda b, i: (0, 0)
    in_specs, args = [], []
    for (o, lse), (_, dil) in zip(attn_outs, ATTN_GROUPS):
        spec = pl.BlockSpec((1, dil, tm // dil, GROUP_WIDTH), lambda b, i: (b, 0, i, 0))
        in_specs += [spec, spec]
        args += [o, lse]
    tok_spec = pl.BlockSpec((1, tm, D), lambda b, i: (b, i, 0))
    in_specs += [tok_spec, tok_spec, tok_spec,
                 pl.BlockSpec((1, 6, D), lambda b, i: (b, 0, 0)),
                 pl.BlockSpec((GROUP_WIDTH, D), const2),
                 pl.BlockSpec((D, D), const2),
                 pl.BlockSpec((1, D), const2),
                 pl.BlockSpec((1, D), const2),
                 pl.BlockSpec((N_EXPERTS, D), const2),
                 pl.BlockSpec((N_EXPERTS, 1), const2)]
    args += [pg, sga, x, ada, w_attn_out.astype(BF16), w_o.astype(BF16),
             ln1_g.reshape(1, D), ln1_b.reshape(1, D), w_router.T, b_router.reshape(N_EXPERTS, 1)]
    nc = N // tm
    route_spec = pl.BlockSpec((1, TOP_K, tm), lambda b, i: (b * nt + i, 0, 0))
    out_specs = [tok_spec, tok_spec, route_spec, route_spec,
                 pl.BlockSpec((1, N_EXPERTS, 1), lambda b, i: (b * nt + i, 0, 0))]
    out_shapes = [jax.ShapeDtypeStruct((B, S, D), F32), jax.ShapeDtypeStruct((B, S, D), BF16),
                  jax.ShapeDtypeStruct((nc, TOP_K, tm), jnp.int32), jax.ShapeDtypeStruct((nc, TOP_K, tm), F32),
                  jax.ShapeDtypeStruct((nc, N_EXPERTS, 1), jnp.int32)]
    return pl.pallas_call(
        _post_kernel,
        grid=(B, nt),
        in_specs=in_specs,
        out_specs=out_specs,
        out_shape=out_shapes,
        scratch_shapes=[pltpu.VMEM((GROUP_WIDTH // LANES, tm, LANES), F32)] * 4,
        compiler_params=pltpu.CompilerParams(
            dimension_semantics=("parallel", "parallel"), vmem_limit_bytes=VMEM_LIMIT_BYTES),
        name="post",
    )(*args)


def _for_each_piece(tile, run_ref, dest_ref, npiece_ref, fn):
    def per_expert(e, count):
        idx = tile * N_EXPERTS + e
        loc, dst = run_ref[idx], dest_ref[idx]

        def per_piece(p, n):
            off = p * ROW_ALIGN
            fn(pl.multiple_of(loc + off, ROW_ALIGN), pl.multiple_of(dst + off, ROW_ALIGN))
            return n + 1

        return lax.fori_loop(0, npiece_ref[idx], per_piece, count)

    return lax.fori_loop(0, N_EXPERTS, per_expert, 0)


def _dispatch_kernel(run_ref, dest_ref, npiece_ref, tail_ref, ntail_ref, nused_ref,
                     slot_ref, u2_ref, xs_hbm, local_ref, zero_ref, sem, zsem):
    tile = pl.program_id(0)
    tt = u2_ref.shape[0]
    u2 = u2_ref[...]
    slots = [slot_ref[0, k:k + 1, :].astype(F32) for k in range(TOP_K)]
    for r0 in range(0, LOCAL_ROWS, SORT_CHUNK):
        row = (r0 + lax.broadcasted_iota(jnp.int32, (SORT_CHUNK, tt), 0)).astype(F32)
        miss = (row - slots[0]) * (row - slots[1]) * (row - slots[2]) * (row - slots[3])
        onehot = jnp.where(miss == 0.0, 1.0, 0.0).astype(BF16)
        local_ref[r0:r0 + SORT_CHUNK, :] = _dot(onehot, u2).astype(BF16)

    def piece(loc, dst):
        return pltpu.make_async_copy(local_ref.at[pl.ds(loc, ROW_ALIGN)], xs_hbm.at[pl.ds(dst, ROW_ALIGN)], sem)

    started = _for_each_piece(tile, run_ref, dest_ref, npiece_ref, lambda loc, dst: piece(loc, dst).start())

    @pl.when(tile == pl.num_programs(0) - 1)
    def _():
        zero_ref[...] = jnp.zeros_like(zero_ref)

        def tail_piece(e, p):
            dst = pl.multiple_of(tail_ref[e] + p * ROW_ALIGN, ROW_ALIGN)
            return pltpu.make_async_copy(zero_ref, xs_hbm.at[pl.ds(dst, ROW_ALIGN)], zsem)

        def start_tail(e, carry):
            return lax.fori_loop(0, ntail_ref[e], lambda p, c: (tail_piece(e, p).start(), c)[1], carry)

        def wait_tail(e, carry):
            return lax.fori_loop(0, ntail_ref[e], lambda p, c: (tail_piece(e, p).wait(), c)[1], carry)

        lax.fori_loop(0, N_EXPERTS, start_tail, 0)
        lax.fori_loop(0, N_EXPERTS, wait_tail, 0)

        def spare_piece(p):
            return pltpu.make_async_copy(zero_ref, xs_hbm.at[pl.ds(pl.multiple_of(p * ROW_ALIGN, ROW_ALIGN),
                                                                   ROW_ALIGN)], zsem)

        first_spare = nused_ref[0] * (MOE_BLOCK // ROW_ALIGN)
        end_spare = xs_hbm.shape[0] // ROW_ALIGN
        lax.fori_loop(first_spare, end_spare, lambda p, c: (spare_piece(p).start(), c)[1], 0)
        lax.fori_loop(first_spare, end_spare, lambda p, c: (spare_piece(p).wait(), c)[1], 0)

    lax.fori_loop(0, started, lambda p, c: (piece(0, 0).wait(), c)[1], 0)


def _dispatch(tables, slot, u2, n_rows):
    N, D = u2.shape
    tt = POST_TILE
    grid_spec = pltpu.PrefetchScalarGridSpec(
        num_scalar_prefetch=len(tables),
        grid=(N // tt,),
        in_specs=[pl.BlockSpec((1, TOP_K, tt), lambda i, *t: (i, 0, 0)),
                  pl.BlockSpec((tt, D), lambda i, *t: (i, 0))],
        out_specs=pl.BlockSpec(memory_space=pl.ANY),
        scratch_shapes=[pltpu.VMEM((LOCAL_ROWS, D), BF16), pltpu.VMEM((ROW_ALIGN, D), BF16),
                        pltpu.SemaphoreType.DMA(()), pltpu.SemaphoreType.DMA(())],
    )
    return pl.pallas_call(
        _dispatch_kernel,
        grid_spec=grid_spec,
        out_shape=jax.ShapeDtypeStruct((n_rows, D), BF16),
        compiler_params=pltpu.CompilerParams(
            dimension_semantics=("arbitrary",), vmem_limit_bytes=VMEM_LIMIT_BYTES),
        name="dispatch",
    )(*tables, slot, u2)


def _expert_kernel(be_ref, nused_ref, xs_ref, wg_ref, bg_ref, wu_ref, bu_ref, wd_ref, bd_ref,
                   ys_ref, wg_s, wu_s, wd_s):
    i = pl.program_id(0)
    used = i < nused_ref[0]
    prev = be_ref[jnp.maximum(i - 1, 0)]
    fresh = (i == 0) | (be_ref[i] != prev)

    @pl.when(used & fresh)
    def _():
        wg_s[...] = wg_ref[0].astype(BF16)
        wu_s[...] = wu_ref[0].astype(BF16)
        wd_s[...] = wd_ref[0].astype(BF16)

    @pl.when(used)
    def _():
        xb = xs_ref[...]
        g = _dot(xb, wg_s[...]) + bg_ref[0]
        up = _dot(xb, wu_s[...]) + bu_ref[0]
        g = jnp.minimum(g, SWIGLU_LIMIT)
        up = jnp.clip(up, -SWIGLU_LIMIT, SWIGLU_LIMIT)
        h = g * jax.nn.sigmoid(SWIGLU_ALPHA * g) * (up + 1.0)
        ys_ref[...] = (_dot(h.astype(BF16), wd_s[...]) + bd_ref[0]).astype(BF16)

    @pl.when(jnp.logical_not(used))
    def _():
        ys_ref[...] = jnp.zeros_like(ys_ref)


def _experts(be, nused, xs, w_gate, b_gate, w_up, b_up, w_down, b_down):
    P, D = xs.shape
    E = w_gate.shape[0]

    def live(i, nu):
        return jnp.minimum(i, nu[0] - 1)

    w_spec = pl.BlockSpec((1, D, D), lambda i, be, nu: (be[live(i, nu)], 0, 0))
    b_spec = pl.BlockSpec((1, 1, D), lambda i, be, nu: (be[live(i, nu)], 0, 0))
    grid_spec = pltpu.PrefetchScalarGridSpec(
        num_scalar_prefetch=2,
        grid=(P // MOE_BLOCK,),
        in_specs=[pl.BlockSpec((MOE_BLOCK, D), lambda i, be, nu: (live(i, nu), 0)),
                  w_spec, b_spec, w_spec, b_spec, w_spec, b_spec],
        out_specs=pl.BlockSpec((MOE_BLOCK, D), lambda i, be, nu: (i, 0)),
        scratch_shapes=[pltpu.VMEM((D, D), BF16)] * 3,
    )
    return pl.pallas_call(
        _expert_kernel,
        grid_spec=grid_spec,
        out_shape=jax.ShapeDtypeStruct((P, D), BF16),
        compiler_params=pltpu.CompilerParams(
            dimension_semantics=("arbitrary",), vmem_limit_bytes=VMEM_LIMIT_BYTES),
        name="experts",
    )(be, nused, xs, w_gate, b_gate.reshape(E, 1, D), w_up, b_up.reshape(E, 1, D),
      w_down, b_down.reshape(E, 1, D))


def _combine_kernel(run_ref, dest_ref, npiece_ref,
                    slot_ref, prob_ref, x1_ref, ada_ref, g2_ref, b2_ref, ys_hbm,
                    out_ref, local_ref, sem):
    tile = pl.program_id(0)
    tt = x1_ref.shape[0]

    @pl.when(tile == 0)
    def _():
        local_ref[...] = jnp.zeros_like(local_ref)

    def piece(loc, dst):
        return pltpu.make_async_copy(ys_hbm.at[pl.ds(dst, ROW_ALIGN)], local_ref.at[pl.ds(loc, ROW_ALIGN)], sem)

    started = _for_each_piece(tile, run_ref, dest_ref, npiece_ref, lambda loc, dst: piece(loc, dst).start())
    slots = [slot_ref[:, k:k + 1] for k in range(TOP_K)]
    probs = [prob_ref[:, k:k + 1] for k in range(TOP_K)]
    lax.fori_loop(0, started, lambda p, c: (piece(0, 0).wait(), c)[1], 0)

    ffn = jnp.zeros((tt, x1_ref.shape[1]), F32)
    for r0 in range(0, LOCAL_ROWS, SORT_CHUNK):
        col = r0 + lax.broadcasted_iota(jnp.int32, (tt, SORT_CHUNK), 1)
        w = jnp.zeros((tt, SORT_CHUNK), F32)
        for k in range(TOP_K):
            w = w + jnp.where(col == slots[k], probs[k], 0.0)
        w_hi = w.astype(BF16)
        w_lo = (w - w_hi.astype(F32)).astype(BF16)
        rows = local_ref[r0:r0 + SORT_CHUNK, :]
        ffn = ffn + _dot(w_hi, rows) + _dot(w_lo, rows)
    gate2 = ada_ref[0, 5:6, :]
    y = DN_ALPHA * x1_ref[...] + (1.0 + gate2) * ffn
    out_ref[...] = _layer_norm(y) * g2_ref[...] + b2_ref[...]


def _combine(tables, slot_tok, prob_tok, x1, ada, ln2_g, ln2_b, ys, tiles_per_batch):
    N, D = x1.shape
    tt = POST_TILE
    const = lambda i, *t: (0, 0)
    tok4 = pl.BlockSpec((tt, TOP_K), lambda i, *t: (i, 0))
    grid_spec = pltpu.PrefetchScalarGridSpec(
        num_scalar_prefetch=len(tables),
        grid=(N // tt,),
        in_specs=[tok4, tok4,
                  pl.BlockSpec((tt, D), lambda i, *t: (i, 0)),
                  pl.BlockSpec((1, 6, D), lambda i, *t: (i // tiles_per_batch, 0, 0)),
                  pl.BlockSpec((1, D), const),
                  pl.BlockSpec((1, D), const),
                  pl.BlockSpec(memory_space=pl.ANY)],
        out_specs=pl.BlockSpec((tt, D), lambda i, *t: (i, 0)),
        scratch_shapes=[pltpu.VMEM((LOCAL_ROWS, D), BF16), pltpu.SemaphoreType.DMA(())],
    )
    return pl.pallas_call(
        _combine_kernel,
        grid_spec=grid_spec,
        out_shape=jax.ShapeDtypeStruct((N, D), F32),
        compiler_params=pltpu.CompilerParams(
            dimension_semantics=("arbitrary",), vmem_limit_bytes=VMEM_LIMIT_BYTES),
        name="combine",
    )(*tables, slot_tok, prob_tok, x1, ada, ln2_g.reshape(1, D), ln2_b.reshape(1, D), ys)


def _routing_tables(cnt, nblk):
    nt, E = cnt.shape
    i32 = jnp.int32
    run = (cnt + ROW_ALIGN - 1) // ROW_ALIGN * ROW_ALIGN
    run_start = jnp.cumsum(run, axis=1) - run
    seg_len = jnp.sum(run, axis=0)
    seg_blocks = (seg_len + MOE_BLOCK - 1) // MOE_BLOCK
    b_end = jnp.cumsum(seg_blocks)
    seg_off = (b_end - seg_blocks) * MOE_BLOCK
    dest = seg_off[None, :] + jnp.cumsum(run, axis=0) - run
    nused = b_end[-1:].astype(i32)
    i = jnp.arange(nblk, dtype=i32)
    be = jnp.minimum(jnp.sum((i[:, None] >= b_end[None, :]).astype(i32), axis=1), E - 1)
    tail = seg_off + seg_len
    ntail = (seg_blocks * MOE_BLOCK - seg_len) // ROW_ALIGN
    piece_tables = (run_start.reshape(-1).astype(i32), dest.reshape(-1).astype(i32),
                    (run // ROW_ALIGN).reshape(-1).astype(i32))
    return piece_tables, (tail.astype(i32), ntail.astype(i32)), (be.astype(i32), nused)


def kernel(x, c, positions, w_ada, b_ada, w_in, pool_w, pool_scale, w_pool_out, w_attn_out, w_o,
           ln1_g, ln1_b, w_router, b_router, w_gate, b_gate, w_up, b_up, w_down, b_down, ln2_g, ln2_b):
    B, S, D = x.shape
    N = B * S
    assert D == D_MODEL and S % PROJ_TILE == 0 and S % (16 * ATTN_BLOCK) == 0
    assert S % POST_TILE == 0 and LOCAL_ROWS % SORT_CHUNK == 0
    nt = N // POST_TILE
    nblk = (N * TOP_K + nt * N_EXPERTS * (ROW_ALIGN - 1)) // MOE_BLOCK + N_EXPERTS
    for l in range(DEPTH):
        ada = _ada(c, w_ada[l], b_ada[l])
        proj_out = _proj(x, positions, ada, w_in[l], pool_w[l], pool_scale[l], w_pool_out[l])
        qkv, (pg, sga) = proj_out[:9], proj_out[9:]
        attn_outs = [_attention(*qkv[3 * g:3 * g + 3]) for g in range(len(ATTN_GROUPS))]
        x1, u2, slot, prob, cnt = _post(attn_outs, pg, sga, x, ada, w_attn_out[l], w_o[l],
                                        ln1_g[l], ln1_b[l], w_router[l], b_router[l])
        piece_tables, tail_tables, block_tables = _routing_tables(cnt.reshape(nt, N_EXPERTS), nblk)
        xs = _dispatch(piece_tables + tail_tables + block_tables[1:], slot, u2.reshape(N, D),
                       nblk * MOE_BLOCK)
        ys = _experts(*block_tables, xs, w_gate[l], b_gate[l], w_up[l], b_up[l], w_down[l], b_down[l])
        token_major = lambda a: a.transpose(0, 2, 1).reshape(N, TOP_K)
        out = _combine(piece_tables, token_major(slot), token_major(prob), x1.reshape(N, D), ada,
                       ln2_g[l], ln2_b[l], ys, S // POST_TILE)
        x = out.reshape(B, S, D)
    return x
```

```python
import functools

import jax
import jax.numpy as jnp
import numpy as np
from jax import lax
from jax.experimental import pallas as pl
from jax.experimental.pallas import tpu as pltpu

F32 = jnp.float32
BF16 = jnp.bfloat16

D_MODEL = 1024
POOL_WINDOWS = (2, 4, 8, 16)
POOL_WIDTH = D_MODEL // 2
POOL_GROUP = POOL_WIDTH // len(POOL_WINDOWS)
POOL_HALO = 16
HEAD_DIM = 64
ATTN_GROUPS = ((128, 1), (512, 4), (2048, 16))
HEADS_PER_GROUP = 4
GROUP_WIDTH = HEADS_PER_GROUP * HEAD_DIM
N_HEADS = HEADS_PER_GROUP * len(ATTN_GROUPS)
ATTN_WIDTH = N_HEADS * HEAD_DIM
ATTN_BLOCK = 128
ROT_DIM = HEAD_DIM // 4
ROPE_THETA = 500000.0
N_EXPERTS = 32
TOP_K = 4
SWIGLU_ALPHA = 1.702
SWIGLU_LIMIT = 7.0
MOE_BLOCK = 256
DEPTH = 1
DN_ALPHA = (2.0 * DEPTH) ** 0.25
LN_EPS = 1e-5
NEG_INF = -1e30

OFF_Q = POOL_WIDTH
OFF_K = OFF_Q + ATTN_WIDTH
OFF_V = OFF_K + ATTN_WIDTH
OFF_GP = OFF_V + ATTN_WIDTH
OFF_GA = OFF_GP + D_MODEL
IN_WIDTH = OFF_GA + D_MODEL

VMEM_LIMIT_BYTES = 56 * 1024 * 1024
LANES = 128

PROJ_TILE = 512
POST_TILE = 512
ATTN_QROWS = 1024
ROW_ALIGN = 16
SORT_CHUNK = 512
LOCAL_ROWS = -(-(POST_TILE * TOP_K + N_EXPERTS * (ROW_ALIGN - 1)) // SORT_CHUNK) * SORT_CHUNK


def _layer_norm(x):
    mu = jnp.mean(x, axis=-1, keepdims=True)
    xc = x - mu
    var = jnp.mean(xc * xc, axis=-1, keepdims=True)
    return xc * lax.rsqrt(var + LN_EPS)


def _dot(a, b):
    return jnp.dot(a, b, preferred_element_type=F32)


def _ada_kernel(c_ref, w_ref, b_ref, o_ref):
    c = c_ref[...]
    s = c * jax.nn.sigmoid(c)
    o_ref[...] = jnp.dot(s, w_ref[...], preferred_element_type=F32,
                         precision=lax.Precision.HIGHEST) + b_ref[...]


def _ada(c, w_ada, b_ada):
    B, D = c.shape
    rows = 8
    c_pad = jnp.pad(c, ((0, rows - B), (0, 0)))
    n_out = w_ada.shape[1]
    out = pl.pallas_call(
        _ada_kernel,
        grid=(n_out // D,),
        in_specs=[pl.BlockSpec((rows, D), lambda j: (0, 0)),
                  pl.BlockSpec((D, D), lambda j: (0, j)),
                  pl.BlockSpec((1, D), lambda j: (0, j))],
        out_specs=pl.BlockSpec((rows, D), lambda j: (0, j)),
        out_shape=jax.ShapeDtypeStruct((rows, n_out), F32),
        name="ada",
    )(c_pad, w_ada, b_ada.reshape(1, n_out))
    return out[:B].reshape(B, 6, D)


def _rope_tables():
    lane = np.arange(128)
    li = lane % HEAD_DIM
    half = ROT_DIM // 2
    inv_freq = jnp.power(ROPE_THETA, -jnp.arange(half, dtype=F32) * (2.0 / ROT_DIM))
    invf = inv_freq[li % half][None, :]
    m_cos = (li < ROT_DIM).astype(np.float32)[None, :]
    m_lo = (li < half).astype(np.float32)[None, :]
    m_hi = ((li >= half) & (li < ROT_DIM)).astype(np.float32)[None, :]
    return jnp.concatenate([invf, jnp.asarray(m_cos), jnp.asarray(m_lo), jnp.asarray(m_hi),
                            jnp.zeros((4, 128), F32)], axis=0)


def _proj_kernel(x_ref, xh_ref, pos_ref, ada_ref, rope_ref, win_ref, poolw_ref, pscale_ref, wpo_ref,
                 q1_ref, k1_ref, v1_ref, q4_ref, k4_ref, v4_ref, q16_ref, k16_ref, v16_ref,
                 pg_ref, sga_ref, xpe_ref, cls_ref):
    tm = x_ref.shape[1]
    i = pl.program_id(1)
    shift1 = ada_ref[0, 0:1, :]
    scale1 = ada_ref[0, 1:2, :]

    def modulated(xv):
        return (_layer_norm(xv) * (1.0 + scale1) + shift1).astype(BF16)

    u = modulated(x_ref[0])
    uh = modulated(xh_ref[0])

    xp = _dot(u, win_ref[:, 0:POOL_WIDTH])
    xph = _dot(uh, win_ref[:, 0:POOL_WIDTH])
    xph = jnp.where(i > 0, xph, 0.0)
    xpe_ref[0:POOL_HALO, :] = xph
    xpe_ref[POOL_HALO:, :] = xp
    tok = i * tm + lax.broadcasted_iota(jnp.int32, (tm, 1), 0)
    ys = []
    for g, w in enumerate(POOL_WINDOWS):
        cols = slice(g * POOL_GROUP, (g + 1) * POOL_GROUP)
        xg = xp[:, cols]
        acc = xg
        for j in range(1, w):
            acc = acc + xpe_ref[POOL_HALO - j:POOL_HALO - j + tm, cols]
        cnt = jnp.minimum(tok + 1, w).astype(F32)
        mixed = (acc / cnt - xg).astype(BF16)
        ys.append(_dot(mixed, poolw_ref[g]) * pscale_ref[:, cols])
    y = jnp.concatenate(ys, axis=1).astype(BF16)
    pooled = _dot(y, wpo_ref[...])
    g_p = _dot(u, win_ref[:, OFF_GP:OFF_GP + D_MODEL])
    pg_ref[0] = jax.nn.sigmoid(g_p) * pooled
    g_a = _dot(u, win_ref[:, OFF_GA:OFF_GA + D_MODEL])
    sga_ref[0] = jax.nn.sigmoid(g_a)

    pos = pos_ref[0].astype(F32)
    ang = pos * rope_ref[0:1, :]
    cos = jnp.cos(ang)
    sin = jnp.sin(ang)
    c_mul = jnp.where(rope_ref[1:2, :] > 0, cos, 1.0)
    s_lo = jnp.where(rope_ref[2:3, :] > 0, -sin, 0.0)
    s_hi = jnp.where(rope_ref[3:4, :] > 0, sin, 0.0)
    c_mul = jnp.concatenate([c_mul, c_mul], axis=1)
    s_lo = jnp.concatenate([s_lo, s_lo], axis=1)
    s_hi = jnp.concatenate([s_hi, s_hi], axis=1)
    half = ROT_DIM // 2

    def rotate(a):
        up = pltpu.roll(a, GROUP_WIDTH - half, axis=1)
        dn = pltpu.roll(a, half, axis=1)
        return a * c_mul + up * s_lo + dn * s_hi

    def emit(a, out_ref, dil):
        if dil == 1:
            out_ref[0, 0] = a.astype(BF16)
            return
        for c in range(GROUP_WIDTH // LANES):
            cls_ref[c] = a[:, c * LANES:(c + 1) * LANES]
        for r in range(dil):
            for c in range(GROUP_WIDTH // LANES):
                out_ref[0, r, :, c * LANES:(c + 1) * LANES] = (
                    cls_ref[c, pl.ds(r, tm // dil, stride=dil), :].astype(BF16))

    outs = ((q1_ref, k1_ref, v1_ref), (q4_ref, k4_ref, v4_ref), (q16_ref, k16_ref, v16_ref))
    for gi, (_, dil) in enumerate(ATTN_GROUPS):
        qo, ko, vo = outs[gi]
        c0 = gi * GROUP_WIDTH
        emit(rotate(_dot(u, win_ref[:, OFF_Q + c0:OFF_Q + c0 + GROUP_WIDTH])), qo, dil)
        emit(rotate(_dot(u, win_ref[:, OFF_K + c0:OFF_K + c0 + GROUP_WIDTH])), ko, dil)
        emit(_dot(u, win_ref[:, OFF_V + c0:OFF_V + c0 + GROUP_WIDTH]), vo, dil)


def _proj(x, positions, ada, w_in, pool_w, pool_scale, w_pool_out):
    B, S, D = x.shape
    tm = PROJ_TILE
    nt = S // tm
    halo_blocks = tm // POOL_HALO
    const2 = lambda b, i: (0, 0)
    in_specs = [
        pl.BlockSpec((1, tm, D), lambda b, i: (b, i, 0)),
        pl.BlockSpec((1, POOL_HALO, D), lambda b, i: (b, jnp.maximum(i * halo_blocks - 1, 0), 0)),
        pl.BlockSpec((1, tm, 1), lambda b, i: (b, i, 0)),
        pl.BlockSpec((1, 6, D), lambda b, i: (b, 0, 0)),
        pl.BlockSpec((8, 128), const2),
        pl.BlockSpec((D, IN_WIDTH), const2),
        pl.BlockSpec((len(POOL_WINDOWS), POOL_GROUP, POOL_GROUP), lambda b, i: (0, 0, 0)),
        pl.BlockSpec((1, POOL_WIDTH), const2),
        pl.BlockSpec((POOL_WIDTH, D), const2),
    ]
    out_specs, out_shapes = [], []
    for _, dil in ATTN_GROUPS:
        for _ in range(3):
            out_specs.append(pl.BlockSpec((1, dil, tm // dil, GROUP_WIDTH), lambda b, i: (b, 0, i, 0)))
            out_shapes.append(jax.ShapeDtypeStruct((B, dil, S // dil, GROUP_WIDTH), BF16))
    for _ in range(2):
        out_specs.append(pl.BlockSpec((1, tm, D), lambda b, i: (b, i, 0)))
        out_shapes.append(jax.ShapeDtypeStruct((B, S, D), F32))
    return pl.pallas_call(
        _proj_kernel,
        grid=(B, nt),
        in_specs=in_specs,
        out_specs=out_specs,
        out_shape=out_shapes,
        scratch_shapes=[pltpu.VMEM((tm + POOL_HALO, POOL_WIDTH), F32),
                        pltpu.VMEM((GROUP_WIDTH // LANES, tm, LANES), F32)],
        compiler_params=pltpu.CompilerParams(
            dimension_semantics=("parallel", "parallel"), vmem_limit_bytes=VMEM_LIMIT_BYTES),
        name="proj",
    )(x, x, positions.reshape(B, S, 1), ada, _rope_tables(), w_in.astype(BF16),
      pool_w.astype(BF16), pool_scale.reshape(1, POOL_WIDTH), w_pool_out.astype(BF16))


def _attn_kernel(q_ref, k_ref, v_ref, kh_ref, vh_ref, o_ref, lse_ref, kf_ref, vf_ref):
    qb = q_ref.shape[2]
    n = pl.program_id(2)
    kf_ref[0:ATTN_BLOCK, :] = kh_ref[0, 0]
    kf_ref[ATTN_BLOCK:, :] = k_ref[0, 0]
    vf_ref[0:ATTN_BLOCK, :] = vh_ref[0, 0]
    vf_ref[ATTN_BLOCK:, :] = v_ref[0, 0]
    qi = lax.broadcasted_iota(jnp.int32, (ATTN_BLOCK, 2 * ATTN_BLOCK), 0)
    kj = lax.broadcasted_iota(jnp.int32, (ATTN_BLOCK, 2 * ATTN_BLOCK), 1)
    band = (kj >= qi) & (kj <= qi + ATTN_BLOCK)

    def block(j, carry):
        r0 = pl.multiple_of(j * ATTN_BLOCK, ATTN_BLOCK)
        first_key = jnp.where((n > 0) | (j > 0), 0, ATTN_BLOCK)
        mask = band & (kj >= first_key)
        q = q_ref[0, 0, pl.ds(r0, ATTN_BLOCK), :]
        kk = kf_ref[pl.ds(r0, 2 * ATTN_BLOCK), :]
        vv = vf_ref[pl.ds(r0, 2 * ATTN_BLOCK), :]
        o_parts, l_parts = [], []
        for h in range(HEADS_PER_GROUP):
            hs = slice(h * HEAD_DIM, (h + 1) * HEAD_DIM)
            s = lax.dot_general(q[:, hs], kk[:, hs], (((1,), (1,)), ((), ())),
                                preferred_element_type=F32) * (HEAD_DIM ** -0.5)
            s = jnp.where(mask, s, NEG_INF)
            m = jnp.max(s, axis=-1, keepdims=True)
            p = jnp.exp(s - m)
            den = jnp.sum(p, axis=-1, keepdims=True)
            o = _dot(p.astype(BF16), vv[:, hs]) / den
            o_parts.append(o)
            l_parts.append(jnp.broadcast_to(m + jnp.log(den), (ATTN_BLOCK, HEAD_DIM)))
        o_ref[0, 0, pl.ds(r0, ATTN_BLOCK), :] = jnp.concatenate(o_parts, axis=1)
        lse_ref[0, 0, pl.ds(r0, ATTN_BLOCK), :] = jnp.concatenate(l_parts, axis=1)
        return carry

    lax.fori_loop(0, qb // ATTN_BLOCK, block, 0, unroll=2)


def _attention(q, k, v):
    B, dil, L, W = q.shape
    qb = min(L, ATTN_QROWS)
    per = qb // ATTN_BLOCK
    main = pl.BlockSpec((1, 1, qb, W), lambda b, r, n: (b, r, n, 0))
    halo = pl.BlockSpec((1, 1, ATTN_BLOCK, W), lambda b, r, n: (b, r, jnp.maximum(n * per - 1, 0), 0))
    return pl.pallas_call(
        _attn_kernel,
        grid=(B, dil, L // qb),
        in_specs=[main, main, main, halo, halo],
        out_specs=[main, main],
        out_shape=[jax.ShapeDtypeStruct((B, dil, L, W), F32)] * 2,
        scratch_shapes=[pltpu.VMEM((qb + ATTN_BLOCK, W), BF16)] * 2,
        compiler_params=pltpu.CompilerParams(
            dimension_semantics=("parallel", "parallel", "parallel"), vmem_limit_bytes=VMEM_LIMIT_BYTES),
        name=f"attn_d{dil}",
    )(q, k, v, k, v)


def _post_kernel(o1_ref, l1_ref, o4_ref, l4_ref, o16_ref, l16_ref, pg_ref, sga_ref, x_ref, ada_ref,
                 wao_ref, wo_ref, g1_ref, b1_ref, wrt_ref, brt_ref,
                 x1_ref, u2_ref, slot_ref, prob_ref, cnt_ref,
                 s0, s1, s2, s3):
    tm = x_ref.shape[1]

    def token_major(src_ref, scr_ref, dil):
        if dil == 1:
            return src_ref[0, 0]
        for r in range(dil):
            for c in range(GROUP_WIDTH // LANES):
                scr_ref[c, pl.ds(r, tm // dil, stride=dil), :] = src_ref[0, r, :, c * LANES:(c + 1) * LANES]
        return jnp.concatenate([scr_ref[c] for c in range(GROUP_WIDTH // LANES)], axis=1)

    o1, l1 = o1_ref[0, 0], l1_ref[0, 0]
    o4, l4 = token_major(o4_ref, s0, 4), token_major(l4_ref, s1, 4)
    o16, l16 = token_major(o16_ref, s2, 16), token_major(l16_ref, s3, 16)
    mx = jnp.maximum(jnp.maximum(l1, l4), l16)
    e1, e4, e16 = jnp.exp(l1 - mx), jnp.exp(l4 - mx), jnp.exp(l16 - mx)
    attn = (e1 * o1 + e4 * o4 + e16 * o16) / (e1 + e4 + e16)

    merged = pg_ref[0] + sga_ref[0] * _dot(attn.astype(BF16), wao_ref[...])
    mix = _dot(merged.astype(BF16), wo_ref[...])
    gate1 = ada_ref[0, 2:3, :]
    shift2 = ada_ref[0, 3:4, :]
    scale2 = ada_ref[0, 4:5, :]
    x1 = _layer_norm(DN_ALPHA * x_ref[0] + (1.0 + gate1) * mix) * g1_ref[...] + b1_ref[...]
    x1_ref[0] = x1
    u2 = _layer_norm(x1) * (1.0 + scale2) + shift2
    u2_ref[0] = u2.astype(BF16)

    logits = lax.dot_general(wrt_ref[...], u2, (((1,), (1,)), ((), ())),
                             preferred_element_type=F32,
                             precision=lax.Precision.HIGHEST) + brt_ref[...]
    eidx = lax.broadcasted_iota(jnp.int32, (N_EXPERTS, tm), 0)
    work = logits
    vals, idxs = [], []
    for _ in range(TOP_K):
        m = jnp.max(work, axis=0, keepdims=True)
        idx = jnp.min(jnp.where(work == m, eidx, N_EXPERTS), axis=0, keepdims=True)
        vals.append(m)
        idxs.append(idx)
        work = jnp.where(eidx == idx, -jnp.inf, work)
    exps = [jnp.exp(vk - vals[0]) for vk in vals]
    tot = exps[0] + exps[1] + exps[2] + exps[3]
    sel = jnp.zeros((N_EXPERTS, tm), F32)
    for idx in idxs:
        sel = sel + (eidx == idx).astype(F32)
    tr = lax.broadcasted_iota(jnp.int32, (tm, tm), 0)
    tc = lax.broadcasted_iota(jnp.int32, (tm, tm), 1)
    rank = _dot(sel.astype(BF16), (tr < tc).astype(BF16))
    cnt = jnp.sum(sel, axis=1, keepdims=True)
    run = jnp.floor((cnt + (ROW_ALIGN - 1)) * (1.0 / ROW_ALIGN)) * ROW_ALIGN
    er = lax.broadcasted_iota(jnp.int32, (N_EXPERTS, N_EXPERTS), 0)
    ec = lax.broadcasted_iota(jnp.int32, (N_EXPERTS, N_EXPERTS), 1)
    run_start = _dot((ec < er).astype(BF16),
                     jnp.broadcast_to(run, (N_EXPERTS, LANES)).astype(BF16))[:, 0:1]
    slot = rank + run_start
    for k in range(TOP_K):
        slot_ref[0, k:k + 1, :] = jnp.sum(jnp.where(eidx == idxs[k], slot, 0.0), axis=0,
                                          keepdims=True).astype(jnp.int32)
        prob_ref[0, k:k + 1, :] = exps[k] / tot
    cnt_ref[0] = cnt.astype(jnp.int32)


def _post(attn_outs, pg, sga, x, ada, w_attn_out, w_o, ln1_g, ln1_b, w_router, b_router):
    B, S, D = x.shape
    tm = POST_TILE
    nt = S // tm
    N = B * S
    const2 = lambda b, i: (0, 0)
    in_specs, args = [], []
    for (o, lse), (_, dil) in zip(attn_outs, ATTN_GROUPS):
        spec = pl.BlockSpec((1, dil, tm // dil, GROUP_WIDTH), lambda b, i: (b, 0, i, 0))
        in_specs += [spec, spec]
        args += [o, lse]
    tok_spec = pl.BlockSpec((1, tm, D), lambda b, i: (b, i, 0))
    in_specs += [tok_spec, tok_spec, tok_spec,
                 pl.BlockSpec((1, 6, D), lambda b, i: (b, 0, 0)),
                 pl.BlockSpec((GROUP_WIDTH, D), const2),
                 pl.BlockSpec((D, D), const2),
                 pl.BlockSpec((1, D), const2),
                 pl.BlockSpec((1, D), const2),
                 pl.BlockSpec((N_EXPERTS, D), const2),
                 pl.BlockSpec((N_EXPERTS, 1), const2)]
    args += [pg, sga, x, ada, w_attn_out.astype(BF16), w_o.astype(BF16),
             ln1_g.reshape(1, D), ln1_b.reshape(1, D), w_router.T, b_router.reshape(N_EXPERTS, 1)]
    nc = N // tm
    route_spec = pl.BlockSpec((1, TOP_K, tm), lambda b, i: (b * nt + i, 0, 0))
    out_specs = [tok_spec, tok_spec, route_spec, route_spec,
                 pl.BlockSpec((1, N_EXPERTS, 1), lambda b, i: (b * nt + i, 0, 0))]
    out_shapes = [jax.ShapeDtypeStruct((B, S, D), F32), jax.ShapeDtypeStruct((B, S, D), BF16),
                  jax.ShapeDtypeStruct((nc, TOP_K, tm), jnp.int32), jax.ShapeDtypeStruct((nc, TOP_K, tm), F32),
                  jax.ShapeDtypeStruct((nc, N_EXPERTS, 1), jnp.int32)]
    return pl.pallas_call(
        _post_kernel,
        grid=(B, nt),
        in_specs=in_specs,
        out_specs=out_specs,
        out_shape=out_shapes,
        scratch_shapes=[pltpu.VMEM((GROUP_WIDTH // LANES, tm, LANES), F32)] * 4,
        compiler_params=pltpu.CompilerParams(
            dimension_semantics=("parallel", "parallel"), vmem_limit_bytes=VMEM_LIMIT_BYTES),
        name="post",
    )(*args)


def _for_each_piece(tile, run_ref, dest_ref, npiece_ref, fn):
    def per_expert(e, count):
        idx = tile * N_EXPERTS + e
        loc, dst = run_ref[idx], dest_ref[idx]

        def per_piece(p, n):
            off = p * ROW_ALIGN
            fn(pl.multiple_of(loc + off, ROW_ALIGN), pl.multiple_of(dst + off, ROW_ALIGN))
            return n + 1

        return lax.fori_loop(0, npiece_ref[idx], per_piece, count)

    return lax.fori_loop(0, N_EXPERTS, per_expert, 0)


def _dispatch_kernel(run_ref, dest_ref, npiece_ref, tail_ref, ntail_ref, nused_ref,
                     slot_ref, prob_ref, u2_ref, xs_hbm, local_ref, zero_ref, sem, zsem):
    tile = pl.program_id(0)
    tt, d = u2_ref.shape
    u2 = u2_ref[...]
    slots = [slot_ref[0, k:k + 1, :] for k in range(TOP_K)]
    probs = [prob_ref[0, k:k + 1, :] for k in range(TOP_K)]
    lane = lax.broadcasted_iota(jnp.int32, (SORT_CHUNK, LANES), 1)
    for r0 in range(0, LOCAL_ROWS, SORT_CHUNK):
        row = r0 + lax.broadcasted_iota(jnp.int32, (SORT_CHUNK, tt), 0)
        w = jnp.zeros((SORT_CHUNK, tt), F32)
        for k in range(TOP_K):
            w = w + jnp.where(row == slots[k], probs[k], 0.0)
        onehot = jnp.where(w != 0.0, 1.0, 0.0).astype(BF16)
        local_ref[r0:r0 + SORT_CHUNK, 0:d] = _dot(onehot, u2).astype(BF16)
        wr = jnp.sum(w, axis=1, keepdims=True)
        hi = wr.astype(BF16).astype(F32)
        mid = (wr - hi).astype(BF16).astype(F32)
        lo = wr - hi - mid
        parts = jnp.where(lane == 0, hi, jnp.where(lane == 1, mid, jnp.where(lane == 2, lo, 0.0)))
        local_ref[r0:r0 + SORT_CHUNK, d:d + LANES] = parts.astype(BF16)

    def piece(loc, dst):
        return pltpu.make_async_copy(local_ref.at[pl.ds(loc, ROW_ALIGN)], xs_hbm.at[pl.ds(dst, ROW_ALIGN)], sem)

    started = _for_each_piece(tile, run_ref, dest_ref, npiece_ref, lambda loc, dst: piece(loc, dst).start())

    @pl.when(tile == pl.num_programs(0) - 1)
    def _():
        zero_ref[...] = jnp.zeros_like(zero_ref)

        def tail_piece(e, p):
            dst = pl.multiple_of(tail_ref[e] + p * ROW_ALIGN, ROW_ALIGN)
            return pltpu.make_async_copy(zero_ref, xs_hbm.at[pl.ds(dst, ROW_ALIGN)], zsem)

        def start_tail(e, carry):
            return lax.fori_loop(0, ntail_ref[e], lambda p, c: (tail_piece(e, p).start(), c)[1], carry)

        def wait_tail(e, carry):
            return lax.fori_loop(0, ntail_ref[e], lambda p, c: (tail_piece(e, p).wait(), c)[1], carry)

        lax.fori_loop(0, N_EXPERTS, start_tail, 0)
        lax.fori_loop(0, N_EXPERTS, wait_tail, 0)

        def spare_piece(p):
            return pltpu.make_async_copy(zero_ref, xs_hbm.at[pl.ds(pl.multiple_of(p * ROW_ALIGN, ROW_ALIGN),
                                                                   ROW_ALIGN)], zsem)

        first_spare = nused_ref[0] * (MOE_BLOCK // ROW_ALIGN)
        end_spare = xs_hbm.shape[0] // ROW_ALIGN
        lax.fori_loop(first_spare, end_spare, lambda p, c: (spare_piece(p).start(), c)[1], 0)
        lax.fori_loop(first_spare, end_spare, lambda p, c: (spare_piece(p).wait(), c)[1], 0)

    lax.fori_loop(0, started, lambda p, c: (piece(0, 0).wait(), c)[1], 0)


def _dispatch(tables, slot, prob, u2, n_rows):
    N, D = u2.shape
    tt = POST_TILE
    route_spec = pl.BlockSpec((1, TOP_K, tt), lambda i, *t: (i, 0, 0))
    grid_spec = pltpu.PrefetchScalarGridSpec(
        num_scalar_prefetch=len(tables),
        grid=(N // tt,),
        in_specs=[route_spec, route_spec, pl.BlockSpec((tt, D), lambda i, *t: (i, 0))],
        out_specs=pl.BlockSpec(memory_space=pl.ANY),
        scratch_shapes=[pltpu.VMEM((LOCAL_ROWS, D + LANES), BF16), pltpu.VMEM((ROW_ALIGN, D + LANES), BF16),
                        pltpu.SemaphoreType.DMA(()), pltpu.SemaphoreType.DMA(())],
    )
    return pl.pallas_call(
        _dispatch_kernel,
        grid_spec=grid_spec,
        out_shape=jax.ShapeDtypeStruct((n_rows, D + LANES), BF16),
        compiler_params=pltpu.CompilerParams(
            dimension_semantics=("arbitrary",), vmem_limit_bytes=VMEM_LIMIT_BYTES),
        name="dispatch",
    )(*tables, slot, prob, u2)


def _expert_kernel(be_ref, next_ref, nused_ref, xs_ref, wg_hbm, bg_ref, wu_hbm, bu_ref, wd_hbm, bd_ref,
                   ys_ref, stage, wg_s, wu_s, wd_s, sem):
    i = pl.program_id(0)
    used = i < nused_ref[0]
    prev = be_ref[jnp.maximum(i - 1, 0)]
    fresh = (i == 0) | (be_ref[i] != prev)

    def fetch(e):
        return [pltpu.make_async_copy(w_hbm.at[e], stage.at[j], sem.at[j])
                for j, w_hbm in enumerate((wg_hbm, wu_hbm, wd_hbm))]

    @pl.when(i == 0)
    def _():
        for copy in fetch(be_ref[0]):
            copy.start()

    @pl.when(used & fresh)
    def _():
        for copy in fetch(be_ref[i]):
            copy.wait()
        wg_s[...] = stage[0].astype(BF16)
        wu_s[...] = stage[1].astype(BF16)
        wd_s[...] = stage[2].astype(BF16)

        @pl.when(next_ref[i] >= 0)
        def _():
            for copy in fetch(next_ref[i]):
                copy.start()

    @pl.when(used)
    def _():
        d = ys_ref.shape[1]
        xb = xs_ref[:, 0:d]
        parts = xs_ref[:, d:d + LANES].astype(F32)
        weight = parts[:, 0:1] + parts[:, 1:2] + parts[:, 2:3]
        g = _dot(xb, wg_s[...]) + bg_ref[0]
        up = _dot(xb, wu_s[...]) + bu_ref[0]
        g = jnp.minimum(g, SWIGLU_LIMIT)
        up = jnp.clip(up, -SWIGLU_LIMIT, SWIGLU_LIMIT)
        h = g * jax.nn.sigmoid(SWIGLU_ALPHA * g) * (up + 1.0)
        ys_ref[...] = ((_dot(h.astype(BF16), wd_s[...]) + bd_ref[0]) * weight).astype(BF16)

    @pl.when(jnp.logical_not(used))
    def _():
        ys_ref[...] = jnp.zeros_like(ys_ref)


def _experts(be, next_expert, nused, xs, w_gate, b_gate, w_up, b_up, w_down, b_down):
    P, width = xs.shape
    D = width - LANES
    E = w_gate.shape[0]

    def live(i, nu):
        return jnp.maximum(jnp.minimum(i, nu[0] - 1), 0)

    w_spec = pl.BlockSpec(memory_space=pl.ANY)
    b_spec = pl.BlockSpec((1, 1, D), lambda i, be, nx, nu: (be[live(i, nu)], 0, 0))
    grid_spec = pltpu.PrefetchScalarGridSpec(
        num_scalar_prefetch=3,
        grid=(P // MOE_BLOCK,),
        in_specs=[pl.BlockSpec((MOE_BLOCK, width), lambda i, be, nx, nu: (live(i, nu), 0)),
                  w_spec, b_spec, w_spec, b_spec, w_spec, b_spec],
        out_specs=pl.BlockSpec((MOE_BLOCK, D), lambda i, be, nx, nu: (i, 0)),
        scratch_shapes=[pltpu.VMEM((3, D, D), F32)] + [pltpu.VMEM((D, D), BF16)] * 3
                       + [pltpu.SemaphoreType.DMA((3,))],
    )
    return pl.pallas_call(
        _expert_kernel,
        grid_spec=grid_spec,
        out_shape=jax.ShapeDtypeStruct((P, D), BF16),
        compiler_params=pltpu.CompilerParams(
            dimension_semantics=("arbitrary",), vmem_limit_bytes=VMEM_LIMIT_BYTES),
        name="experts",
    )(be, next_expert, nused, xs, w_gate, b_gate.reshape(E, 1, D), w_up, b_up.reshape(E, 1, D),
      w_down, b_down.reshape(E, 1, D))


def _combine_kernel(run_ref, dest_ref, npiece_ref,
                    slot_ref, x1_ref, ada_ref, g2_ref, b2_ref, ys_hbm,
                    out_ref, local_ref, sem):
    tile = pl.program_id(0)
    tt = x1_ref.shape[0]

    @pl.when(tile == 0)
    def _():
        local_ref[...] = jnp.zeros_like(local_ref)

    def piece(loc, dst):
        return pltpu.make_async_copy(ys_hbm.at[pl.ds(dst, ROW_ALIGN)], local_ref.at[pl.ds(loc, ROW_ALIGN)], sem)

    started = _for_each_piece(tile, run_ref, dest_ref, npiece_ref, lambda loc, dst: piece(loc, dst).start())
    slots = [slot_ref[:, k:k + 1].astype(F32) for k in range(TOP_K)]
    lax.fori_loop(0, started, lambda p, c: (piece(0, 0).wait(), c)[1], 0)

    ffn = jnp.zeros((tt, x1_ref.shape[1]), F32)
    for r0 in range(0, LOCAL_ROWS, SORT_CHUNK):
        col = (r0 + lax.broadcasted_iota(jnp.int32, (tt, SORT_CHUNK), 1)).astype(F32)
        miss = (col - slots[0]) * (col - slots[1]) * (col - slots[2]) * (col - slots[3])
        onehot = jnp.where(miss == 0.0, 1.0, 0.0).astype(BF16)
        ffn = ffn + _dot(onehot, local_ref[r0:r0 + SORT_CHUNK, :])
    gate2 = ada_ref[0, 5:6, :]
    y = DN_ALPHA * x1_ref[...] + (1.0 + gate2) * ffn
    out_ref[...] = _layer_norm(y) * g2_ref[...] + b2_ref[...]


def _combine(tables, slot_tok, x1, ada, ln2_g, ln2_b, ys, tiles_per_batch):
    N, D = x1.shape
    tt = POST_TILE
    const = lambda i, *t: (0, 0)
    tok4 = pl.BlockSpec((tt, TOP_K), lambda i, *t: (i, 0))
    grid_spec = pltpu.PrefetchScalarGridSpec(
        num_scalar_prefetch=len(tables),
        grid=(N // tt,),
        in_specs=[tok4,
                  pl.BlockSpec((tt, D), lambda i, *t: (i, 0)),
                  pl.BlockSpec((1, 6, D), lambda i, *t: (i // tiles_per_batch, 0, 0)),
                  pl.BlockSpec((1, D), const),
                  pl.BlockSpec((1, D), const),
                  pl.BlockSpec(memory_space=pl.ANY)],
        out_specs=pl.BlockSpec((tt, D), lambda i, *t: (i, 0)),
        scratch_shapes=[pltpu.VMEM((LOCAL_ROWS, D), BF16), pltpu.SemaphoreType.DMA(())],
    )
    return pl.pallas_call(
        _combine_kernel,
        grid_spec=grid_spec,
        out_shape=jax.ShapeDtypeStruct((N, D), F32),
        compiler_params=pltpu.CompilerParams(
            dimension_semantics=("arbitrary",), vmem_limit_bytes=VMEM_LIMIT_BYTES),
        name="combine",
    )(*tables, slot_tok, x1, ada, ln2_g.reshape(1, D), ln2_b.reshape(1, D), ys)


def _routing_tables(cnt, nblk):
    nt, E = cnt.shape
    i32 = jnp.int32
    run = (cnt + ROW_ALIGN - 1) // ROW_ALIGN * ROW_ALIGN
    run_start = jnp.cumsum(run, axis=1) - run
    seg_len = jnp.sum(run, axis=0)
    seg_blocks = (seg_len + MOE_BLOCK - 1) // MOE_BLOCK
    b_end = jnp.cumsum(seg_blocks)
    seg_off = (b_end - seg_blocks) * MOE_BLOCK
    dest = seg_off[None, :] + jnp.cumsum(run, axis=0) - run
    nused = b_end[-1:].astype(i32)
    i = jnp.arange(nblk, dtype=i32)
    be = jnp.minimum(jnp.sum((i[:, None] >= b_end[None, :]).astype(i32), axis=1), E - 1)
    e_ids = jnp.arange(E, dtype=i32)
    later = (e_ids[None, :] > e_ids[:, None]) & (seg_blocks > 0)[None, :]
    next_e = jnp.min(jnp.where(later, e_ids[None, :], E), axis=1)
    next_e = jnp.where(next_e < E, next_e, -1)
    nxt = jnp.sum(jnp.where(be[:, None] == e_ids[None, :], next_e[None, :], 0), axis=1)
    tail = seg_off + seg_len
    ntail = (seg_blocks * MOE_BLOCK - seg_len) // ROW_ALIGN
    piece_tables = (run_start.reshape(-1).astype(i32), dest.reshape(-1).astype(i32),
                    (run // ROW_ALIGN).reshape(-1).astype(i32))
    return piece_tables, (tail.astype(i32), ntail.astype(i32)), (be.astype(i32), nxt.astype(i32), nused)


def kernel(x, c, positions, w_ada, b_ada, w_in, pool_w, pool_scale, w_pool_out, w_attn_out, w_o,
           ln1_g, ln1_b, w_router, b_router, w_gate, b_gate, w_up, b_up, w_down, b_down, ln2_g, ln2_b):
    B, S, D = x.shape
    N = B * S
    assert D == D_MODEL and S % PROJ_TILE == 0 and S % (16 * ATTN_BLOCK) == 0
    assert S % POST_TILE == 0 and LOCAL_ROWS % SORT_CHUNK == 0
    nt = N // POST_TILE
    nblk = (N * TOP_K + nt * N_EXPERTS * (ROW_ALIGN - 1)) // MOE_BLOCK + N_EXPERTS
    for l in range(DEPTH):
        ada = _ada(c, w_ada[l], b_ada[l])
        proj_out = _proj(x, positions, ada, w_in[l], pool_w[l], pool_scale[l], w_pool_out[l])
        qkv, (pg, sga) = proj_out[:9], proj_out[9:]
        attn_outs = [_attention(*qkv[3 * g:3 * g + 3]) for g in range(len(ATTN_GROUPS))]
        x1, u2, slot, prob, cnt = _post(attn_outs, pg, sga, x, ada, w_attn_out[l], w_o[l],
                                        ln1_g[l], ln1_b[l], w_router[l], b_router[l])
        piece_tables, tail_tables, block_tables = _routing_tables(cnt.reshape(nt, N_EXPERTS), nblk)
        xs = _dispatch(piece_tables + tail_tables + block_tables[2:], slot, prob, u2.reshape(N, D),
                       nblk * MOE_BLOCK)
        ys = _experts(*block_tables, xs, w_gate[l], b_gate[l], w_up[l], b_up[l], w_down[l], b_down[l])
        slot_tok = slot.transpose(0, 2, 1).reshape(N, TOP_K)
        out = _combine(piece_tables, slot_tok, x1.reshape(N, D), ada,
                       ln2_g[l], ln2_b[l], ys, S // POST_TILE)
        x = out.reshape(B, S, D)
    return x
```

```python
import functools

import jax
import jax.numpy as jnp
import numpy as np
from jax import lax
from jax.experimental import pallas as pl
from jax.experimental.pallas import tpu as pltpu

F32 = jnp.float32
BF16 = jnp.bfloat16

D_MODEL = 1024
POOL_WINDOWS = (2, 4, 8, 16)
POOL_WIDTH = D_MODEL // 2
POOL_GROUP = POOL_WIDTH // len(POOL_WINDOWS)
POOL_HALO = 16
HEAD_DIM = 64
ATTN_GROUPS = ((128, 1), (512, 4), (2048, 16))
HEADS_PER_GROUP = 4
GROUP_WIDTH = HEADS_PER_GROUP * HEAD_DIM
N_HEADS = HEADS_PER_GROUP * len(ATTN_GROUPS)
ATTN_WIDTH = N_HEADS * HEAD_DIM
ATTN_BLOCK = 128
ROT_DIM = HEAD_DIM // 4
ROPE_THETA = 500000.0
N_EXPERTS = 32
TOP_K = 4
SWIGLU_ALPHA = 1.702
SWIGLU_LIMIT = 7.0
MOE_BLOCK = 256
DEPTH = 1
DN_ALPHA = (2.0 * DEPTH) ** 0.25
LN_EPS = 1e-5
NEG_INF = -1e30

OFF_Q = POOL_WIDTH
OFF_K = OFF_Q + ATTN_WIDTH
OFF_V = OFF_K + ATTN_WIDTH
OFF_GP = OFF_V + ATTN_WIDTH
OFF_GA = OFF_GP + D_MODEL
IN_WIDTH = OFF_GA + D_MODEL

VMEM_LIMIT_BYTES = 56 * 1024 * 1024
LANES = 128

PROJ_TILE = 512
POST_TILE = 512
ATTN_QROWS = 1024
ROW_ALIGN = 16
SORT_CHUNK = 512
LOCAL_ROWS = -(-(POST_TILE * TOP_K + N_EXPERTS * (ROW_ALIGN - 1)) // SORT_CHUNK) * SORT_CHUNK


def _layer_norm(x):
    mu = jnp.mean(x, axis=-1, keepdims=True)
    xc = x - mu
    var = jnp.mean(xc * xc, axis=-1, keepdims=True)
    return xc * lax.rsqrt(var + LN_EPS)


def _dot(a, b):
    return jnp.dot(a, b, preferred_element_type=F32)


def _ada_kernel(c_ref, w_ref, b_ref, o_ref):
    c = c_ref[...]
    s = c * jax.nn.sigmoid(c)
    o_ref[...] = jnp.dot(s, w_ref[...], preferred_element_type=F32,
                         precision=lax.Precision.HIGHEST) + b_ref[...]


def _ada(c, w_ada, b_ada):
    B, D = c.shape
    rows = 8
    c_pad = jnp.pad(c, ((0, rows - B), (0, 0)))
    n_out = w_ada.shape[1]
    out = pl.pallas_call(
        _ada_kernel,
        grid=(n_out // D,),
        in_specs=[pl.BlockSpec((rows, D), lambda j: (0, 0)),
                  pl.BlockSpec((D, D), lambda j: (0, j)),
                  pl.BlockSpec((1, D), lambda j: (0, j))],
        out_specs=pl.BlockSpec((rows, D), lambda j: (0, j)),
        out_shape=jax.ShapeDtypeStruct((rows, n_out), F32),
        name="ada",
    )(c_pad, w_ada, b_ada.reshape(1, n_out))
    return out[:B].reshape(B, 6, D)


def _rope_tables():
    lane = np.arange(128)
    li = lane % HEAD_DIM
    half = ROT_DIM // 2
    inv_freq = jnp.power(ROPE_THETA, -jnp.arange(half, dtype=F32) * (2.0 / ROT_DIM))
    invf = inv_freq[li % half][None, :]
    m_cos = (li < ROT_DIM).astype(np.float32)[None, :]
    m_lo = (li < half).astype(np.float32)[None, :]
    m_hi = ((li >= half) & (li < ROT_DIM)).astype(np.float32)[None, :]
    return jnp.concatenate([invf, jnp.asarray(m_cos), jnp.asarray(m_lo), jnp.asarray(m_hi),
                            jnp.zeros((4, 128), F32)], axis=0)


def _proj_kernel(x_ref, xh_ref, pos_ref, ada_ref, rope_ref, win_ref, poolw_ref, pscale_ref, wpo_ref,
                 q1_ref, k1_ref, v1_ref, q4_ref, k4_ref, v4_ref, q16_ref, k16_ref, v16_ref,
                 pg_ref, sga_ref, xpe_ref, cls_ref):
    tm = x_ref.shape[1]
    i = pl.program_id(1)
    shift1 = ada_ref[0, 0:1, :]
    scale1 = ada_ref[0, 1:2, :]

    def modulated(xv):
        return (_layer_norm(xv) * (1.0 + scale1) + shift1).astype(BF16)

    u = modulated(x_ref[0])
    uh = modulated(xh_ref[0])

    xp = _dot(u, win_ref[:, 0:POOL_WIDTH])
    xph = _dot(uh, win_ref[:, 0:POOL_WIDTH])
    xph = jnp.where(i > 0, xph, 0.0)
    xpe_ref[0:POOL_HALO, :] = xph
    xpe_ref[POOL_HALO:, :] = xp
    tok = i * tm + lax.broadcasted_iota(jnp.int32, (tm, 1), 0)
    ys = []
    for g, w in enumerate(POOL_WINDOWS):
        cols = slice(g * POOL_GROUP, (g + 1) * POOL_GROUP)
        xg = xp[:, cols]
        acc = xg
        for j in range(1, w):
            acc = acc + xpe_ref[POOL_HALO - j:POOL_HALO - j + tm, cols]
        cnt = jnp.minimum(tok + 1, w).astype(F32)
        mixed = (acc / cnt - xg).astype(BF16)
        ys.append(_dot(mixed, poolw_ref[g]) * pscale_ref[:, cols])
    y = jnp.concatenate(ys, axis=1).astype(BF16)
    pooled = _dot(y, wpo_ref[...])
    g_p = _dot(u, win_ref[:, OFF_GP:OFF_GP + D_MODEL])
    pg_ref[0] = jax.nn.sigmoid(g_p) * pooled
    g_a = _dot(u, win_ref[:, OFF_GA:OFF_GA + D_MODEL])
    sga_ref[0] = jax.nn.sigmoid(g_a)

    pos = pos_ref[0].astype(F32)
    ang = pos * rope_ref[0:1, :]
    cos = jnp.cos(ang)
    sin = jnp.sin(ang)
    c_mul = jnp.where(rope_ref[1:2, :] > 0, cos, 1.0)
    s_lo = jnp.where(rope_ref[2:3, :] > 0, -sin, 0.0)
    s_hi = jnp.where(rope_ref[3:4, :] > 0, sin, 0.0)
    c_mul = jnp.concatenate([c_mul, c_mul], axis=1)
    s_lo = jnp.concatenate([s_lo, s_lo], axis=1)
    s_hi = jnp.concatenate([s_hi, s_hi], axis=1)
    half = ROT_DIM // 2

    def rotate(a):
        up = pltpu.roll(a, GROUP_WIDTH - half, axis=1)
        dn = pltpu.roll(a, half, axis=1)
        return a * c_mul + up * s_lo + dn * s_hi

    def emit(a, out_ref, dil):
        if dil == 1:
            out_ref[0, 0] = a.astype(BF16)
            return
        for c in range(GROUP_WIDTH // LANES):
            cls_ref[c] = a[:, c * LANES:(c + 1) * LANES]
        for r in range(dil):
            for c in range(GROUP_WIDTH // LANES):
                out_ref[0, r, :, c * LANES:(c + 1) * LANES] = (
                    cls_ref[c, pl.ds(r, tm // dil, stride=dil), :].astype(BF16))

    outs = ((q1_ref, k1_ref, v1_ref), (q4_ref, k4_ref, v4_ref), (q16_ref, k16_ref, v16_ref))
    for gi, (_, dil) in enumerate(ATTN_GROUPS):
        qo, ko, vo = outs[gi]
        c0 = gi * GROUP_WIDTH
        emit(rotate(_dot(u, win_ref[:, OFF_Q + c0:OFF_Q + c0 + GROUP_WIDTH])), qo, dil)
        emit(rotate(_dot(u, win_ref[:, OFF_K + c0:OFF_K + c0 + GROUP_WIDTH])), ko, dil)
        emit(_dot(u, win_ref[:, OFF_V + c0:OFF_V + c0 + GROUP_WIDTH]), vo, dil)


def _proj(x, positions, ada, w_in, pool_w, pool_scale, w_pool_out):
    B, S, D = x.shape
    tm = PROJ_TILE
    nt = S // tm
    halo_blocks = tm // POOL_HALO
    const2 = lambda b, i: (0, 0)
    in_specs = [
        pl.BlockSpec((1, tm, D), lambda b, i: (b, i, 0)),
        pl.BlockSpec((1, POOL_HALO, D), lambda b, i: (b, jnp.maximum(i * halo_blocks - 1, 0), 0)),
        pl.BlockSpec((1, tm, 1), lambda b, i: (b, i, 0)),
        pl.BlockSpec((1, 6, D), lambda b, i: (b, 0, 0)),
        pl.BlockSpec((8, 128), const2),
        pl.BlockSpec((D, IN_WIDTH), const2),
        pl.BlockSpec((len(POOL_WINDOWS), POOL_GROUP, POOL_GROUP), lambda b, i: (0, 0, 0)),
        pl.BlockSpec((1, POOL_WIDTH), const2),
        pl.BlockSpec((POOL_WIDTH, D), const2),
    ]
    out_specs, out_shapes = [], []
    for _, dil in ATTN_GROUPS:
        for _ in range(3):
            out_specs.append(pl.BlockSpec((1, dil, tm // dil, GROUP_WIDTH), lambda b, i: (b, 0, i, 0)))
            out_shapes.append(jax.ShapeDtypeStruct((B, dil, S // dil, GROUP_WIDTH), BF16))
    for _ in range(2):
        out_specs.append(pl.BlockSpec((1, tm, D), lambda b, i: (b, i, 0)))
        out_shapes.append(jax.ShapeDtypeStruct((B, S, D), F32))
    return pl.pallas_call(
        _proj_kernel,
        grid=(B, nt),
        in_specs=in_specs,
        out_specs=out_specs,
        out_shape=out_shapes,
        scratch_shapes=[pltpu.VMEM((tm + POOL_HALO, POOL_WIDTH), F32),
                        pltpu.VMEM((GROUP_WIDTH // LANES, tm, LANES), F32)],
        compiler_params=pltpu.CompilerParams(
            dimension_semantics=("parallel", "parallel"), vmem_limit_bytes=VMEM_LIMIT_BYTES),
        name="proj",
    )(x, x, positions.reshape(B, S, 1), ada, _rope_tables(), w_in.astype(BF16),
      pool_w.astype(BF16), pool_scale.reshape(1, POOL_WIDTH), w_pool_out.astype(BF16))


def _attn_kernel(q_ref, k_ref, v_ref, kh_ref, vh_ref, o_ref, lse_ref, kf_ref, vf_ref):
    qb = q_ref.shape[2]
    n = pl.program_id(2)
    kf_ref[0:ATTN_BLOCK, :] = kh_ref[0, 0]
    kf_ref[ATTN_BLOCK:, :] = k_ref[0, 0]
    vf_ref[0:ATTN_BLOCK, :] = vh_ref[0, 0]
    vf_ref[ATTN_BLOCK:, :] = v_ref[0, 0]
    qi = lax.broadcasted_iota(jnp.int32, (ATTN_BLOCK, 2 * ATTN_BLOCK), 0)
    kj = lax.broadcasted_iota(jnp.int32, (ATTN_BLOCK, 2 * ATTN_BLOCK), 1)
    band = (kj >= qi) & (kj <= qi + ATTN_BLOCK)
    band_bias = jnp.where(band, 0.0, NEG_INF)
    lane = lax.broadcasted_iota(jnp.int32, (ATTN_BLOCK, GROUP_WIDTH), 1)
    low_lanes = lax.broadcasted_iota(jnp.int32, (ATTN_BLOCK, LANES), 1) < HEAD_DIM
    ones = jnp.ones((2 * ATTN_BLOCK, LANES), BF16)
    nh, blk = HEADS_PER_GROUP, ATTN_BLOCK

    def block(j, carry):
        r0 = pl.multiple_of(j * ATTN_BLOCK, ATTN_BLOCK)
        first_key = jnp.where((n > 0) | (j > 0), 0, ATTN_BLOCK)
        bias = band_bias + jnp.where(kj < first_key, NEG_INF, 0.0)
        q = q_ref[0, 0, pl.ds(r0, blk), :].astype(F32)
        kk = kf_ref[pl.ds(r0, 2 * blk), :]
        vv = vf_ref[pl.ds(r0, 2 * blk), :]
        qs = jnp.concatenate([jnp.where((lane >= h * HEAD_DIM) & (lane < (h + 1) * HEAD_DIM), q, 0.0)
                              for h in range(nh)], axis=0).astype(BF16)
        s = lax.dot_general(qs, kk, (((1,), (1,)), ((), ())), preferred_element_type=F32)
        s = jnp.concatenate([s[h * blk:(h + 1) * blk] * (HEAD_DIM ** -0.5) + bias for h in range(nh)], axis=0)
        m = jnp.max(s, axis=-1, keepdims=True)
        p = jnp.exp(s - m).astype(BF16)
        den = _dot(p, ones)
        lse = m + jnp.log(den)
        for hp in range(GROUP_WIDTH // LANES):
            rows = slice(2 * hp * blk, (2 * hp + 2) * blk)
            ls = slice(hp * LANES, (hp + 1) * LANES)
            o2 = _dot(p[rows], vv[:, ls]) / den[rows]
            l2 = lse[rows]
            o_ref[0, 0, pl.ds(r0, blk), ls] = jnp.where(low_lanes, o2[0:blk], o2[blk:2 * blk])
            lse_ref[0, 0, pl.ds(r0, blk), ls] = jnp.where(low_lanes, l2[0:blk], l2[blk:2 * blk])
        return carry

    lax.fori_loop(0, qb // ATTN_BLOCK, block, 0, unroll=2)


def _attention(q, k, v):
    B, dil, L, W = q.shape
    qb = min(L, ATTN_QROWS)
    per = qb // ATTN_BLOCK
    main = pl.BlockSpec((1, 1, qb, W), lambda b, r, n: (b, r, n, 0))
    halo = pl.BlockSpec((1, 1, ATTN_BLOCK, W), lambda b, r, n: (b, r, jnp.maximum(n * per - 1, 0), 0))
    return pl.pallas_call(
        _attn_kernel,
        grid=(B, dil, L // qb),
        in_specs=[main, main, main, halo, halo],
        out_specs=[main, main],
        out_shape=[jax.ShapeDtypeStruct((B, dil, L, W), F32)] * 2,
        scratch_shapes=[pltpu.VMEM((qb + ATTN_BLOCK, W), BF16)] * 2,
        compiler_params=pltpu.CompilerParams(
            dimension_semantics=("parallel", "parallel", "parallel"), vmem_limit_bytes=VMEM_LIMIT_BYTES),
        name=f"attn_d{dil}",
    )(q, k, v, k, v)


def _post_kernel(o1_ref, l1_ref, o4_ref, l4_ref, o16_ref, l16_ref, pg_ref, sga_ref, x_ref, ada_ref,
                 wao_ref, wo_ref, g1_ref, b1_ref, wrt_ref, brt_ref,
                 x1_ref, u2_ref, slot_ref, prob_ref, cnt_ref,
                 s0, s1, s2, s3):
    tm = x_ref.shape[1]

    def token_major(src_ref, scr_ref, dil):
        if dil == 1:
            return src_ref[0, 0]
        for r in range(dil):
            for c in range(GROUP_WIDTH // LANES):
                scr_ref[c, pl.ds(r, tm // dil, stride=dil), :] = src_ref[0, r, :, c * LANES:(c + 1) * LANES]
        return jnp.concatenate([scr_ref[c] for c in range(GROUP_WIDTH // LANES)], axis=1)

    o1, l1 = o1_ref[0, 0], l1_ref[0, 0]
    o4, l4 = token_major(o4_ref, s0, 4), token_major(l4_ref, s1, 4)
    o16, l16 = token_major(o16_ref, s2, 16), token_major(l16_ref, s3, 16)
    mx = jnp.maximum(jnp.maximum(l1, l4), l16)
    e1, e4, e16 = jnp.exp(l1 - mx), jnp.exp(l4 - mx), jnp.exp(l16 - mx)
    attn = (e1 * o1 + e4 * o4 + e16 * o16) / (e1 + e4 + e16)

    merged = pg_ref[0] + sga_ref[0] * _dot(attn.astype(BF16), wao_ref[...])
    mix = _dot(merged.astype(BF16), wo_ref[...])
    gate1 = ada_ref[0, 2:3, :]
    shift2 = ada_ref[0, 3:4, :]
    scale2 = ada_ref[0, 4:5, :]
    x1 = _layer_norm(DN_ALPHA * x_ref[0] + (1.0 + gate1) * mix) * g1_ref[...] + b1_ref[...]
    x1_ref[0] = x1
    u2 = _layer_norm(x1) * (1.0 + scale2) + shift2
    u2_ref[0] = u2.astype(BF16)

    logits = lax.dot_general(wrt_ref[...], u2, (((1,), (1,)), ((), ())),
                             preferred_element_type=F32,
                             precision=lax.Precision.HIGHEST) + brt_ref[...]
    eidx = lax.broadcasted_iota(jnp.int32, (N_EXPERTS, tm), 0)
    work = logits
    vals, idxs = [], []
    for _ in range(TOP_K):
        m = jnp.max(work, axis=0, keepdims=True)
        idx = jnp.min(jnp.where(work == m, eidx, N_EXPERTS), axis=0, keepdims=True)
        vals.append(m)
        idxs.append(idx)
        work = jnp.where(eidx == idx, -jnp.inf, work)
    exps = [jnp.exp(vk - vals[0]) for vk in vals]
    tot = exps[0] + exps[1] + exps[2] + exps[3]
    sel = jnp.zeros((N_EXPERTS, tm), F32)
    for idx in idxs:
        sel = sel + (eidx == idx).astype(F32)
    tr = lax.broadcasted_iota(jnp.int32, (tm, tm), 0)
    tc = lax.broadcasted_iota(jnp.int32, (tm, tm), 1)
    rank = _dot(sel.astype(BF16), (tr < tc).astype(BF16))
    cnt = jnp.sum(sel, axis=1, keepdims=True)
    run = jnp.floor((cnt + (ROW_ALIGN - 1)) * (1.0 / ROW_ALIGN)) * ROW_ALIGN
    er = lax.broadcasted_iota(jnp.int32, (N_EXPERTS, N_EXPERTS), 0)
    ec = lax.broadcasted_iota(jnp.int32, (N_EXPERTS, N_EXPERTS), 1)
    run_start = _dot((ec < er).astype(BF16),
                     jnp.broadcast_to(run, (N_EXPERTS, LANES)).astype(BF16))[:, 0:1]
    slot = rank + run_start
    for k in range(TOP_K):
        slot_ref[0, k:k + 1, :] = jnp.sum(jnp.where(eidx == idxs[k], slot, 0.0), axis=0,
                                          keepdims=True).astype(jnp.int32)
        prob_ref[0, k:k + 1, :] = exps[k] / tot
    cnt_ref[0] = cnt.astype(jnp.int32)


def _post(attn_outs, pg, sga, x, ada, w_attn_out, w_o, ln1_g, ln1_b, w_router, b_router):
    B, S, D = x.shape
    tm = POST_TILE
    nt = S // tm
    N = B * S
    const2 = lambda b, i: (0, 0)
    in_specs, args = [], []
    for (o, lse), (_, dil) in zip(attn_outs, ATTN_GROUPS):
        spec = pl.BlockSpec((1, dil, tm // dil, GROUP_WIDTH), lambda b, i: (b, 0, i, 0))
        in_specs += [spec, spec]
        args += [o, lse]
    tok_spec = pl.BlockSpec((1, tm, D), lambda b, i: (b, i, 0))
    in_specs += [tok_spec, tok_spec, tok_spec,
                 pl.BlockSpec((1, 6, D), lambda b, i: (b, 0, 0)),
                 pl.BlockSpec((GROUP_WIDTH, D), const2),
                 pl.BlockSpec((D, D), const2),
                 pl.BlockSpec((1, D), const2),
                 pl.BlockSpec((1, D), const2),
                 pl.BlockSpec((N_EXPERTS, D), const2),
                 pl.BlockSpec((N_EXPERTS, 1), const2)]
    args += [pg, sga, x, ada, w_attn_out.astype(BF16), w_o.astype(BF16),
             ln1_g.reshape(1, D), ln1_b.reshape(1, D), w_router.T, b_router.reshape(N_EXPERTS, 1)]
    nc = N // tm
    route_spec = pl.BlockSpec((1, TOP_K, tm), lambda b, i: (b * nt + i, 0, 0))
    out_specs = [tok_spec, tok_spec, route_spec, route_spec,
                 pl.BlockSpec((1, N_EXPERTS, 1), lambda b, i: (b * nt + i, 0, 0))]
    out_shapes = [jax.ShapeDtypeStruct((B, S, D), F32), jax.ShapeDtypeStruct((B, S, D), BF16),
                  jax.ShapeDtypeStruct((nc, TOP_K, tm), jnp.int32), jax.ShapeDtypeStruct((nc, TOP_K, tm), F32),
                  jax.ShapeDtypeStruct((nc, N_EXPERTS, 1), jnp.int32)]
    return pl.pallas_call(
        _post_kernel,
        grid=(B, nt),
        in_specs=in_specs,
        out_specs=out_specs,
        out_shape=out_shapes,
        scratch_shapes=[pltpu.VMEM((GROUP_WIDTH // LANES, tm, LANES), F32)] * 4,
        compiler_params=pltpu.CompilerParams(
            dimension_semantics=("parallel", "parallel"), vmem_limit_bytes=VMEM_LIMIT_BYTES),
        name="post",
    )(*args)


def _for_each_piece(tile, run_ref, dest_ref, npiece_ref, fn):
    def per_expert(e, count):
        idx = tile * N_EXPERTS + e
        loc, dst = run_ref[idx], dest_ref[idx]

        def per_piece(p, n):
            off = p * ROW_ALIGN
            fn(pl.multiple_of(loc + off, ROW_ALIGN), pl.multiple_of(dst + off, ROW_ALIGN))
            return n + 1

        return lax.fori_loop(0, npiece_ref[idx], per_piece, count)

    return lax.fori_loop(0, N_EXPERTS, per_expert, 0)


def _wait_rows(src_ref, dst_ref, rows, sem):
    @pl.when(rows > 0)
    def _():
        n = pl.multiple_of(rows, ROW_ALIGN)
        pltpu.make_async_copy(src_ref.at[pl.ds(0, n)], dst_ref.at[pl.ds(0, n)], sem).wait()


def _dispatch_kernel(run_ref, dest_ref, npiece_ref, tail_ref, ntail_ref, nused_ref,
                     slot_ref, prob_ref, u2_ref, xs_hbm, local2_ref, zero_ref, sem2, zsem, rows_ref):
    tile = pl.program_id(0)
    buf = tile % 2
    local_ref = local2_ref.at[buf]
    sem = sem2.at[buf]
    tt, d = u2_ref.shape
    u2 = u2_ref[...]
    slots = [slot_ref[0, k:k + 1, :] for k in range(TOP_K)]
    probs = [prob_ref[0, k:k + 1, :] for k in range(TOP_K)]
    lane = lax.broadcasted_iota(jnp.int32, (SORT_CHUNK, LANES), 1)
    for r0 in range(0, LOCAL_ROWS, SORT_CHUNK):
        row = r0 + lax.broadcasted_iota(jnp.int32, (SORT_CHUNK, tt), 0)
        w = jnp.zeros((SORT_CHUNK, tt), F32)
        for k in range(TOP_K):
            w = w + jnp.where(row == slots[k], probs[k], 0.0)
        onehot = jnp.where(w != 0.0, 1.0, 0.0).astype(BF16)
        local_ref[r0:r0 + SORT_CHUNK, 0:d] = _dot(onehot, u2).astype(BF16)
        wr = jnp.sum(w, axis=1, keepdims=True)
        hi = wr.astype(BF16).astype(F32)
        mid = (wr - hi).astype(BF16).astype(F32)
        lo = wr - hi - mid
        parts = jnp.where(lane == 0, hi, jnp.where(lane == 1, mid, jnp.where(lane == 2, lo, 0.0)))
        local_ref[r0:r0 + SORT_CHUNK, d:d + LANES] = parts.astype(BF16)

    def piece(loc, dst):
        return pltpu.make_async_copy(local_ref.at[pl.ds(loc, ROW_ALIGN)], xs_hbm.at[pl.ds(dst, ROW_ALIGN)], sem)

    started = _for_each_piece(tile, run_ref, dest_ref, npiece_ref, lambda loc, dst: piece(loc, dst).start())
    rows_ref[buf] = started * ROW_ALIGN

    @pl.when(tile > 0)
    def _():
        _wait_rows(local2_ref.at[1 - buf], xs_hbm, rows_ref[1 - buf], sem2.at[1 - buf])

    @pl.when(tile == pl.num_programs(0) - 1)
    def _():
        _wait_rows(local_ref, xs_hbm, rows_ref[buf], sem)
        zero_ref[...] = jnp.zeros_like(zero_ref)

        def tail_piece(e, p):
            dst = pl.multiple_of(tail_ref[e] + p * ROW_ALIGN, ROW_ALIGN)
            return pltpu.make_async_copy(zero_ref, xs_hbm.at[pl.ds(dst, ROW_ALIGN)], zsem)

        def start_tail(e, carry):
            return lax.fori_loop(0, ntail_ref[e], lambda p, c: (tail_piece(e, p).start(), c)[1], carry)

        def wait_tail(e, carry):
            return lax.fori_loop(0, ntail_ref[e], lambda p, c: (tail_piece(e, p).wait(), c)[1], carry)

        lax.fori_loop(0, N_EXPERTS, start_tail, 0)
        lax.fori_loop(0, N_EXPERTS, wait_tail, 0)

        def spare_piece(p):
            return pltpu.make_async_copy(zero_ref, xs_hbm.at[pl.ds(pl.multiple_of(p * ROW_ALIGN, ROW_ALIGN),
                                                                   ROW_ALIGN)], zsem)

        first_spare = nused_ref[0] * (MOE_BLOCK // ROW_ALIGN)
        end_spare = xs_hbm.shape[0] // ROW_ALIGN
        lax.fori_loop(first_spare, end_spare, lambda p, c: (spare_piece(p).start(), c)[1], 0)
        lax.fori_loop(first_spare, end_spare, lambda p, c: (spare_piece(p).wait(), c)[1], 0)


def _dispatch(tables, slot, prob, u2, n_rows):
    N, D = u2.shape
    tt = POST_TILE
    route_spec = pl.BlockSpec((1, TOP_K, tt), lambda i, *t: (i, 0, 0))
    grid_spec = pltpu.PrefetchScalarGridSpec(
        num_scalar_prefetch=len(tables),
        grid=(N // tt,),
        in_specs=[route_spec, route_spec, pl.BlockSpec((tt, D), lambda i, *t: (i, 0))],
        out_specs=pl.BlockSpec(memory_space=pl.ANY),
        scratch_shapes=[pltpu.VMEM((2, LOCAL_ROWS, D + LANES), BF16), pltpu.VMEM((ROW_ALIGN, D + LANES), BF16),
                        pltpu.SemaphoreType.DMA((2,)), pltpu.SemaphoreType.DMA(()),
                        pltpu.SMEM((2,), jnp.int32)],
    )
    return pl.pallas_call(
        _dispatch_kernel,
        grid_spec=grid_spec,
        out_shape=jax.ShapeDtypeStruct((n_rows, D + LANES), BF16),
        compiler_params=pltpu.CompilerParams(
            dimension_semantics=("arbitrary",), vmem_limit_bytes=VMEM_LIMIT_BYTES),
        name="dispatch",
    )(*tables, slot, prob, u2)


def _expert_kernel(be_ref, next_ref, nused_ref, xs_ref, wg_hbm, bg_ref, wu_hbm, bu_ref, wd_hbm, bd_ref,
                   ys_ref, stage, wg_s, wu_s, wd_s, sem):
    i = pl.program_id(0)
    used = i < nused_ref[0]
    prev = be_ref[jnp.maximum(i - 1, 0)]
    fresh = (i == 0) | (be_ref[i] != prev)

    def fetch(e):
        return [pltpu.make_async_copy(w_hbm.at[e], stage.at[j], sem.at[j])
                for j, w_hbm in enumerate((wg_hbm, wu_hbm, wd_hbm))]

    @pl.when(i == 0)
    def _():
        for copy in fetch(be_ref[0]):
            copy.start()

    @pl.when(used & fresh)
    def _():
        for copy in fetch(be_ref[i]):
            copy.wait()
        wg_s[...] = stage[0].astype(BF16)
        wu_s[...] = stage[1].astype(BF16)
        wd_s[...] = stage[2].astype(BF16)

        @pl.when(next_ref[i] >= 0)
        def _():
            for copy in fetch(next_ref[i]):
                copy.start()

    @pl.when(used)
    def _():
        d = ys_ref.shape[1]
        xb = xs_ref[:, 0:d]
        parts = xs_ref[:, d:d + LANES].astype(F32)
        weight = parts[:, 0:1] + parts[:, 1:2] + parts[:, 2:3]
        g = _dot(xb, wg_s[...]) + bg_ref[0]
        up = _dot(xb, wu_s[...]) + bu_ref[0]
        g = jnp.minimum(g, SWIGLU_LIMIT)
        up = jnp.clip(up, -SWIGLU_LIMIT, SWIGLU_LIMIT)
        h = g * jax.nn.sigmoid(SWIGLU_ALPHA * g) * (up + 1.0)
        ys_ref[...] = ((_dot(h.astype(BF16), wd_s[...]) + bd_ref[0]) * weight).astype(BF16)

    @pl.when(jnp.logical_not(used))
    def _():
        ys_ref[...] = jnp.zeros_like(ys_ref)


def _experts(be, next_expert, nused, xs, w_gate, b_gate, w_up, b_up, w_down, b_down):
    P, width = xs.shape
    D = width - LANES
    E = w_gate.shape[0]

    def live(i, nu):
        return jnp.maximum(jnp.minimum(i, nu[0] - 1), 0)

    w_spec = pl.BlockSpec(memory_space=pl.ANY)
    b_spec = pl.BlockSpec((1, 1, D), lambda i, be, nx, nu: (be[live(i, nu)], 0, 0))
    grid_spec = pltpu.PrefetchScalarGridSpec(
        num_scalar_prefetch=3,
        grid=(P // MOE_BLOCK,),
        in_specs=[pl.BlockSpec((MOE_BLOCK, width), lambda i, be, nx, nu: (live(i, nu), 0)),
                  w_spec, b_spec, w_spec, b_spec, w_spec, b_spec],
        out_specs=pl.BlockSpec((MOE_BLOCK, D), lambda i, be, nx, nu: (i, 0)),
        scratch_shapes=[pltpu.VMEM((3, D, D), F32)] + [pltpu.VMEM((D, D), BF16)] * 3
                       + [pltpu.SemaphoreType.DMA((3,))],
    )
    return pl.pallas_call(
        _expert_kernel,
        grid_spec=grid_spec,
        out_shape=jax.ShapeDtypeStruct((P, D), BF16),
        compiler_params=pltpu.CompilerParams(
            dimension_semantics=("arbitrary",), vmem_limit_bytes=VMEM_LIMIT_BYTES),
        name="experts",
    )(be, next_expert, nused, xs, w_gate, b_gate.reshape(E, 1, D), w_up, b_up.reshape(E, 1, D),
      w_down, b_down.reshape(E, 1, D))


def _combine_kernel(run_ref, dest_ref, npiece_ref,
                    slot_ref, x1_ref, ada_ref, g2_ref, b2_ref, ys_hbm,
                    out_ref, local2_ref, sem2, rows_ref):
    tile = pl.program_id(0)
    buf = tile % 2
    tt = x1_ref.shape[0]

    def fetch(t, b):
        def piece(loc, dst):
            return pltpu.make_async_copy(ys_hbm.at[pl.ds(dst, ROW_ALIGN)],
                                         local2_ref.at[b, pl.ds(loc, ROW_ALIGN)], sem2.at[b])
        started = _for_each_piece(t, run_ref, dest_ref, npiece_ref, lambda loc, dst: piece(loc, dst).start())
        rows_ref[b] = started * ROW_ALIGN

    @pl.when(tile == 0)
    def _():
        local2_ref[...] = jnp.zeros_like(local2_ref)
        fetch(0, 0)

    @pl.when(tile + 1 < pl.num_programs(0))
    def _():
        fetch(tile + 1, 1 - buf)

    local_ref = local2_ref.at[buf]
    _wait_rows(ys_hbm, local_ref, rows_ref[buf], sem2.at[buf])
    slots = [slot_ref[:, k:k + 1].astype(F32) for k in range(TOP_K)]

    ffn = jnp.zeros((tt, x1_ref.shape[1]), F32)
    for r0 in range(0, LOCAL_ROWS, SORT_CHUNK):
        col = (r0 + lax.broadcasted_iota(jnp.int32, (tt, SORT_CHUNK), 1)).astype(F32)
        miss = (col - slots[0]) * (col - slots[1]) * (col - slots[2]) * (col - slots[3])
        onehot = jnp.where(miss == 0.0, 1.0, 0.0).astype(BF16)
        ffn = ffn + _dot(onehot, local_ref[r0:r0 + SORT_CHUNK, :])
    gate2 = ada_ref[0, 5:6, :]
    y = DN_ALPHA * x1_ref[...] + (1.0 + gate2) * ffn
    out_ref[...] = _layer_norm(y) * g2_ref[...] + b2_ref[...]


def _combine(tables, slot_tok, x1, ada, ln2_g, ln2_b, ys, tiles_per_batch):
    N, D = x1.shape
    tt = POST_TILE
    const = lambda i, *t: (0, 0)
    tok4 = pl.BlockSpec((tt, TOP_K), lambda i, *t: (i, 0))
    grid_spec = pltpu.PrefetchScalarGridSpec(
        num_scalar_prefetch=len(tables),
        grid=(N // tt,),
        in_specs=[tok4,
                  pl.BlockSpec((tt, D), lambda i, *t: (i, 0)),
                  pl.BlockSpec((1, 6, D), lambda i, *t: (i // tiles_per_batch, 0, 0)),
                  pl.BlockSpec((1, D), const),
                  pl.BlockSpec((1, D), const),
                  pl.BlockSpec(memory_space=pl.ANY)],
        out_specs=pl.BlockSpec((tt, D), lambda i, *t: (i, 0)),
        scratch_shapes=[pltpu.VMEM((2, LOCAL_ROWS, D), BF16), pltpu.SemaphoreType.DMA((2,)),
                        pltpu.SMEM((2,), jnp.int32)],
    )
    return pl.pallas_call(
        _combine_kernel,
        grid_spec=grid_spec,
        out_shape=jax.ShapeDtypeStruct((N, D), F32),
        compiler_params=pltpu.CompilerParams(
            dimension_semantics=("arbitrary",), vmem_limit_bytes=VMEM_LIMIT_BYTES),
        name="combine",
    )(*tables, slot_tok, x1, ada, ln2_g.reshape(1, D), ln2_b.reshape(1, D), ys)


def _routing_tables(cnt, nblk):
    nt, E = cnt.shape
    i32 = jnp.int32
    run = (cnt + ROW_ALIGN - 1) // ROW_ALIGN * ROW_ALIGN
    run_start = jnp.cumsum(run, axis=1) - run
    seg_len = jnp.sum(run, axis=0)
    seg_blocks = (seg_len + MOE_BLOCK - 1) // MOE_BLOCK
    b_end = jnp.cumsum(seg_blocks)
    seg_off = (b_end - seg_blocks) * MOE_BLOCK
    dest = seg_off[None, :] + jnp.cumsum(run, axis=0) - run
    nused = b_end[-1:].astype(i32)
    i = jnp.arange(nblk, dtype=i32)
    be = jnp.minimum(jnp.sum((i[:, None] >= b_end[None, :]).astype(i32), axis=1), E - 1)
    e_ids = jnp.arange(E, dtype=i32)
    later = (e_ids[None, :] > e_ids[:, None]) & (seg_blocks > 0)[None, :]
    next_e = jnp.min(jnp.where(later, e_ids[None, :], E), axis=1)
    next_e = jnp.where(next_e < E, next_e, -1)
    nxt = jnp.sum(jnp.where(be[:, None] == e_ids[None, :], next_e[None, :], 0), axis=1)
    tail = seg_off + seg_len
    ntail = (seg_blocks * MOE_BLOCK - seg_len) // ROW_ALIGN
    piece_tables = (run_start.reshape(-1).astype(i32), dest.reshape(-1).astype(i32),
                    (run // ROW_ALIGN).reshape(-1).astype(i32))
    return piece_tables, (tail.astype(i32), ntail.astype(i32)), (be.astype(i32), nxt.astype(i32), nused)


def kernel(x, c, positions, w_ada, b_ada, w_in, pool_w, pool_scale, w_pool_out, w_attn_out, w_o,
           ln1_g, ln1_b, w_router, b_router, w_gate, b_gate, w_up, b_up, w_down, b_down, ln2_g, ln2_b):
    B, S, D = x.shape
    N = B * S
    assert D == D_MODEL and S % PROJ_TILE == 0 and S % (16 * ATTN_BLOCK) == 0
    assert S % POST_TILE == 0 and LOCAL_ROWS % SORT_CHUNK == 0
    nt = N // POST_TILE
    nblk = (N * TOP_K + nt * N_EXPERTS * (ROW_ALIGN - 1)) // MOE_BLOCK + N_EXPERTS
    for l in range(DEPTH):
        ada = _ada(c, w_ada[l], b_ada[l])
        proj_out = _proj(x, positions, ada, w_in[l], pool_w[l], pool_scale[l], w_pool_out[l])
        qkv, (pg, sga) = proj_out[:9], proj_out[9:]
        attn_outs = [_attention(*qkv[3 * g:3 * g + 3]) for g in range(len(ATTN_GROUPS))]
        x1, u2, slot, prob, cnt = _post(attn_outs, pg, sga, x, ada, w_attn_out[l], w_o[l],
                                        ln1_g[l], ln1_b[l], w_router[l], b_router[l])
        piece_tables, tail_tables, block_tables = _routing_tables(cnt.reshape(nt, N_EXPERTS), nblk)
        xs = _dispatch(piece_tables + tail_tables + block_tables[2:], slot, prob, u2.reshape(N, D),
                       nblk * MOE_BLOCK)
        ys = _experts(*block_tables, xs, w_gate[l], b_gate[l], w_up[l], b_up[l], w_down[l], b_down[l])
        slot_tok = slot.transpose(0, 2, 1).reshape(N, TOP_K)
        out = _combine(piece_tables, slot_tok, x1.reshape(N, D), ada,
                       ln2_g[l], ln2_b[l], ys, S // POST_TILE)
        x = out.reshape(B, S, D)
    return x
```

```python
import functools

import jax
import jax.numpy as jnp
import numpy as np
from jax import lax
from jax.experimental import pallas as pl
from jax.experimental.pallas import tpu as pltpu

F32 = jnp.float32
BF16 = jnp.bfloat16

D_MODEL = 1024
POOL_WINDOWS = (2, 4, 8, 16)
POOL_WIDTH = D_MODEL // 2
POOL_GROUP = POOL_WIDTH // len(POOL_WINDOWS)
POOL_HALO = 16
HEAD_DIM = 64
ATTN_GROUPS = ((128, 1), (512, 4), (2048, 16))
HEADS_PER_GROUP = 4
GROUP_WIDTH = HEADS_PER_GROUP * HEAD_DIM
N_HEADS = HEADS_PER_GROUP * len(ATTN_GROUPS)
ATTN_WIDTH = N_HEADS * HEAD_DIM
ATTN_BLOCK = 128
ROT_DIM = HEAD_DIM // 4
ROPE_THETA = 500000.0
N_EXPERTS = 32
TOP_K = 4
SWIGLU_ALPHA = 1.702
SWIGLU_LIMIT = 7.0
MOE_BLOCK = 512
DEPTH = 1
DN_ALPHA = (2.0 * DEPTH) ** 0.25
LN_EPS = 1e-5
NEG_INF = -1e30

OFF_Q = POOL_WIDTH
OFF_K = OFF_Q + ATTN_WIDTH
OFF_V = OFF_K + ATTN_WIDTH
OFF_GP = OFF_V + ATTN_WIDTH
OFF_GA = OFF_GP + D_MODEL
IN_WIDTH = OFF_GA + D_MODEL

VMEM_LIMIT_BYTES = 56 * 1024 * 1024
LANES = 128

PROJ_TILE = 512
POST_TILE = 512
ATTN_QROWS = 1024
ROW_ALIGN = 16
SORT_CHUNK = 512
LOCAL_ROWS = -(-(POST_TILE * TOP_K + N_EXPERTS * (ROW_ALIGN - 1)) // SORT_CHUNK) * SORT_CHUNK


def _layer_norm(x):
    mu = jnp.mean(x, axis=-1, keepdims=True)
    xc = x - mu
    var = jnp.mean(xc * xc, axis=-1, keepdims=True)
    return xc * lax.rsqrt(var + LN_EPS)


def _dot(a, b):
    return jnp.dot(a, b, preferred_element_type=F32)


def _ada_kernel(c_ref, w_ref, b_ref, o_ref):
    c = c_ref[...]
    s = c * jax.nn.sigmoid(c)
    o_ref[...] = jnp.dot(s, w_ref[...], preferred_element_type=F32,
                         precision=lax.Precision.HIGHEST) + b_ref[...]


def _ada(c, w_ada, b_ada):
    B, D = c.shape
    rows = 8
    c_pad = jnp.pad(c, ((0, rows - B), (0, 0)))
    n_out = w_ada.shape[1]
    out = pl.pallas_call(
        _ada_kernel,
        grid=(n_out // D,),
        in_specs=[pl.BlockSpec((rows, D), lambda j: (0, 0)),
                  pl.BlockSpec((D, D), lambda j: (0, j)),
                  pl.BlockSpec((1, D), lambda j: (0, j))],
        out_specs=pl.BlockSpec((rows, D), lambda j: (0, j)),
        out_shape=jax.ShapeDtypeStruct((rows, n_out), F32),
        name="ada",
    )(c_pad, w_ada, b_ada.reshape(1, n_out))
    return out[:B].reshape(B, 6, D)


def _rope_tables():
    lane = np.arange(128)
    li = lane % HEAD_DIM
    half = ROT_DIM // 2
    inv_freq = jnp.power(ROPE_THETA, -jnp.arange(half, dtype=F32) * (2.0 / ROT_DIM))
    invf = inv_freq[li % half][None, :]
    m_cos = (li < ROT_DIM).astype(np.float32)[None, :]
    m_lo = (li < half).astype(np.float32)[None, :]
    m_hi = ((li >= half) & (li < ROT_DIM)).astype(np.float32)[None, :]
    return jnp.concatenate([invf, jnp.asarray(m_cos), jnp.asarray(m_lo), jnp.asarray(m_hi),
                            jnp.zeros((4, 128), F32)], axis=0)


def _proj_kernel(x_ref, xh_ref, pos_ref, ada_ref, rope_ref, win_ref, poolw_ref, pscale_ref, wpo_ref,
                 q1_ref, k1_ref, v1_ref, q4_ref, k4_ref, v4_ref, q16_ref, k16_ref, v16_ref,
                 pg_ref, sga_ref, xpe_ref, cls_ref):
    tm = x_ref.shape[1]
    i = pl.program_id(1)
    shift1 = ada_ref[0, 0:1, :]
    scale1 = ada_ref[0, 1:2, :]

    def modulated(xv):
        return (_layer_norm(xv) * (1.0 + scale1) + shift1).astype(BF16)

    u = modulated(x_ref[0])
    uh = modulated(xh_ref[0])

    xp = _dot(u, win_ref[:, 0:POOL_WIDTH])
    xph = _dot(uh, win_ref[:, 0:POOL_WIDTH])
    xph = jnp.where(i > 0, xph, 0.0)
    xpe_ref[0:POOL_HALO, :] = xph
    xpe_ref[POOL_HALO:, :] = xp
    tok = i * tm + lax.broadcasted_iota(jnp.int32, (tm, 1), 0)
    ys = []
    for g, w in enumerate(POOL_WINDOWS):
        cols = slice(g * POOL_GROUP, (g + 1) * POOL_GROUP)
        xg = xp[:, cols]
        acc = xg
        for j in range(1, w):
            acc = acc + xpe_ref[POOL_HALO - j:POOL_HALO - j + tm, cols]
        cnt = jnp.minimum(tok + 1, w).astype(F32)
        mixed = (acc / cnt - xg).astype(BF16)
        ys.append(_dot(mixed, poolw_ref[g]) * pscale_ref[:, cols])
    y = jnp.concatenate(ys, axis=1).astype(BF16)
    pooled = _dot(y, wpo_ref[...])
    g_p = _dot(u, win_ref[:, OFF_GP:OFF_GP + D_MODEL])
    pg_ref[0] = jax.nn.sigmoid(g_p) * pooled
    g_a = _dot(u, win_ref[:, OFF_GA:OFF_GA + D_MODEL])
    sga_ref[0] = jax.nn.sigmoid(g_a)

    pos = pos_ref[0].astype(F32)
    ang = pos * rope_ref[0:1, :]
    cos = jnp.cos(ang)
    sin = jnp.sin(ang)
    c_mul = jnp.where(rope_ref[1:2, :] > 0, cos, 1.0)
    s_lo = jnp.where(rope_ref[2:3, :] > 0, -sin, 0.0)
    s_hi = jnp.where(rope_ref[3:4, :] > 0, sin, 0.0)
    c_mul = jnp.concatenate([c_mul, c_mul], axis=1)
    s_lo = jnp.concatenate([s_lo, s_lo], axis=1)
    s_hi = jnp.concatenate([s_hi, s_hi], axis=1)
    half = ROT_DIM // 2

    def rotate(a):
        up = pltpu.roll(a, GROUP_WIDTH - half, axis=1)
        dn = pltpu.roll(a, half, axis=1)
        return a * c_mul + up * s_lo + dn * s_hi

    def emit(a, out_ref, dil):
        if dil == 1:
            out_ref[0, 0] = a.astype(BF16)
            return
        for c in range(GROUP_WIDTH // LANES):
            cls_ref[c] = a[:, c * LANES:(c + 1) * LANES]
        for r in range(dil):
            for c in range(GROUP_WIDTH // LANES):
                out_ref[0, r, :, c * LANES:(c + 1) * LANES] = (
                    cls_ref[c, pl.ds(r, tm // dil, stride=dil), :].astype(BF16))

    outs = ((q1_ref, k1_ref, v1_ref), (q4_ref, k4_ref, v4_ref), (q16_ref, k16_ref, v16_ref))
    for gi, (_, dil) in enumerate(ATTN_GROUPS):
        qo, ko, vo = outs[gi]
        c0 = gi * GROUP_WIDTH
        emit(rotate(_dot(u, win_ref[:, OFF_Q + c0:OFF_Q + c0 + GROUP_WIDTH])), qo, dil)
        emit(rotate(_dot(u, win_ref[:, OFF_K + c0:OFF_K + c0 + GROUP_WIDTH])), ko, dil)
        emit(_dot(u, win_ref[:, OFF_V + c0:OFF_V + c0 + GROUP_WIDTH]), vo, dil)


def _proj(x, positions, ada, w_in, pool_w, pool_scale, w_pool_out):
    B, S, D = x.shape
    tm = PROJ_TILE
    nt = S // tm
    halo_blocks = tm // POOL_HALO
    const2 = lambda b, i: (0, 0)
    in_specs = [
        pl.BlockSpec((1, tm, D), lambda b, i: (b, i, 0)),
        pl.BlockSpec((1, POOL_HALO, D), lambda b, i: (b, jnp.maximum(i * halo_blocks - 1, 0), 0)),
        pl.BlockSpec((1, tm, 1), lambda b, i: (b, i, 0)),
        pl.BlockSpec((1, 6, D), lambda b, i: (b, 0, 0)),
        pl.BlockSpec((8, 128), const2),
        pl.BlockSpec((D, IN_WIDTH), const2),
        pl.BlockSpec((len(POOL_WINDOWS), POOL_GROUP, POOL_GROUP), lambda b, i: (0, 0, 0)),
        pl.BlockSpec((1, POOL_WIDTH), const2),
        pl.BlockSpec((POOL_WIDTH, D), const2),
    ]
    out_specs, out_shapes = [], []
    for _, dil in ATTN_GROUPS:
        for _ in range(3):
            out_specs.append(pl.BlockSpec((1, dil, tm // dil, GROUP_WIDTH), lambda b, i: (b, 0, i, 0)))
            out_shapes.append(jax.ShapeDtypeStruct((B, dil, S // dil, GROUP_WIDTH), BF16))
    for _ in range(2):
        out_specs.append(pl.BlockSpec((1, tm, D), lambda b, i: (b, i, 0)))
        out_shapes.append(jax.ShapeDtypeStruct((B, S, D), F32))
    return pl.pallas_call(
        _proj_kernel,
        grid=(B, nt),
        in_specs=in_specs,
        out_specs=out_specs,
        out_shape=out_shapes,
        scratch_shapes=[pltpu.VMEM((tm + POOL_HALO, POOL_WIDTH), F32),
                        pltpu.VMEM((GROUP_WIDTH // LANES, tm, LANES), F32)],
        compiler_params=pltpu.CompilerParams(
            dimension_semantics=("parallel", "parallel"), vmem_limit_bytes=VMEM_LIMIT_BYTES),
        name="proj",
    )(x, x, positions.reshape(B, S, 1), ada, _rope_tables(), w_in.astype(BF16),
      pool_w.astype(BF16), pool_scale.reshape(1, POOL_WIDTH), w_pool_out.astype(BF16))


def _attn_kernel(q_ref, k_ref, v_ref, kh_ref, vh_ref, o_ref, lse_ref, kf_ref, vf_ref):
    qb = q_ref.shape[2]
    n = pl.program_id(2)
    kf_ref[0:ATTN_BLOCK, :] = kh_ref[0, 0]
    kf_ref[ATTN_BLOCK:, :] = k_ref[0, 0]
    vf_ref[0:ATTN_BLOCK, :] = vh_ref[0, 0]
    vf_ref[ATTN_BLOCK:, :] = v_ref[0, 0]
    qi = lax.broadcasted_iota(jnp.int32, (ATTN_BLOCK, 2 * ATTN_BLOCK), 0)
    kj = lax.broadcasted_iota(jnp.int32, (ATTN_BLOCK, 2 * ATTN_BLOCK), 1)
    band = (kj >= qi) & (kj <= qi + ATTN_BLOCK)
    band_bias = jnp.where(band, 0.0, NEG_INF)
    lane = lax.broadcasted_iota(jnp.int32, (ATTN_BLOCK, GROUP_WIDTH), 1)
    low_lanes = lax.broadcasted_iota(jnp.int32, (ATTN_BLOCK, LANES), 1) < HEAD_DIM
    ones = jnp.ones((2 * ATTN_BLOCK, LANES), BF16)
    nh, blk = HEADS_PER_GROUP, ATTN_BLOCK

    def block(j, carry):
        r0 = pl.multiple_of(j * ATTN_BLOCK, ATTN_BLOCK)
        first_key = jnp.where((n > 0) | (j > 0), 0, ATTN_BLOCK)
        bias = band_bias + jnp.where(kj < first_key, NEG_INF, 0.0)
        q = q_ref[0, 0, pl.ds(r0, blk), :].astype(F32)
        kk = kf_ref[pl.ds(r0, 2 * blk), :]
        vv = vf_ref[pl.ds(r0, 2 * blk), :]
        qs = jnp.concatenate([jnp.where((lane >= h * HEAD_DIM) & (lane < (h + 1) * HEAD_DIM), q, 0.0)
                              for h in range(nh)], axis=0).astype(BF16)
        s = lax.dot_general(qs, kk, (((1,), (1,)), ((), ())), preferred_element_type=F32)
        s = jnp.concatenate([s[h * blk:(h + 1) * blk] * (HEAD_DIM ** -0.5) + bias for h in range(nh)], axis=0)
        m = jnp.max(s, axis=-1, keepdims=True)
        p = jnp.exp(s - m).astype(BF16)
        den = _dot(p, ones)
        lse = m + jnp.log(den)
        for hp in range(GROUP_WIDTH // LANES):
            rows = slice(2 * hp * blk, (2 * hp + 2) * blk)
            ls = slice(hp * LANES, (hp + 1) * LANES)
            o2 = _dot(p[rows], vv[:, ls]) / den[rows]
            l2 = lse[rows]
            o_ref[0, 0, pl.ds(r0, blk), ls] = jnp.where(low_lanes, o2[0:blk], o2[blk:2 * blk])
            lse_ref[0, 0, pl.ds(r0, blk), ls] = jnp.where(low_lanes, l2[0:blk], l2[blk:2 * blk])
        return carry

    lax.fori_loop(0, qb // ATTN_BLOCK, block, 0, unroll=4)


def _attention(q, k, v):
    B, dil, L, W = q.shape
    qb = min(L, ATTN_QROWS)
    per = qb // ATTN_BLOCK
    main = pl.BlockSpec((1, 1, qb, W), lambda b, r, n: (b, r, n, 0))
    halo = pl.BlockSpec((1, 1, ATTN_BLOCK, W), lambda b, r, n: (b, r, jnp.maximum(n * per - 1, 0), 0))
    return pl.pallas_call(
        _attn_kernel,
        grid=(B, dil, L // qb),
        in_specs=[main, main, main, halo, halo],
        out_specs=[main, main],
        out_shape=[jax.ShapeDtypeStruct((B, dil, L, W), F32)] * 2,
        scratch_shapes=[pltpu.VMEM((qb + ATTN_BLOCK, W), BF16)] * 2,
        compiler_params=pltpu.CompilerParams(
            dimension_semantics=("parallel", "parallel", "parallel"), vmem_limit_bytes=VMEM_LIMIT_BYTES),
        name=f"attn_d{dil}",
    )(q, k, v, k, v)


def _post_kernel(o1_ref, l1_ref, o4_ref, l4_ref, o16_ref, l16_ref, pg_ref, sga_ref, x_ref, ada_ref,
                 wao_ref, wo_ref, g1_ref, b1_ref, wrt_ref, brt_ref,
                 x1_ref, u2_ref, slot_ref, prob_ref, cnt_ref,
                 s0, s1, s2, s3):
    tm = x_ref.shape[1]

    def token_major(src_ref, scr_ref, dil):
        if dil == 1:
            return src_ref[0, 0]
        for r in range(dil):
            for c in range(GROUP_WIDTH // LANES):
                scr_ref[c, pl.ds(r, tm // dil, stride=dil), :] = src_ref[0, r, :, c * LANES:(c + 1) * LANES]
        return jnp.concatenate([scr_ref[c] for c in range(GROUP_WIDTH // LANES)], axis=1)

    o1, l1 = o1_ref[0, 0], l1_ref[0, 0]
    o4, l4 = token_major(o4_ref, s0, 4), token_major(l4_ref, s1, 4)
    o16, l16 = token_major(o16_ref, s2, 16), token_major(l16_ref, s3, 16)
    mx = jnp.maximum(jnp.maximum(l1, l4), l16)
    e1, e4, e16 = jnp.exp(l1 - mx), jnp.exp(l4 - mx), jnp.exp(l16 - mx)
    attn = (e1 * o1 + e4 * o4 + e16 * o16) / (e1 + e4 + e16)

    merged = pg_ref[0] + sga_ref[0] * _dot(attn.astype(BF16), wao_ref[...])
    mix = _dot(merged.astype(BF16), wo_ref[...])
    gate1 = ada_ref[0, 2:3, :]
    shift2 = ada_ref[0, 3:4, :]
    scale2 = ada_ref[0, 4:5, :]
    x1 = _layer_norm(DN_ALPHA * x_ref[0] + (1.0 + gate1) * mix) * g1_ref[...] + b1_ref[...]
    x1_ref[0] = x1
    u2 = _layer_norm(x1) * (1.0 + scale2) + shift2
    u2_ref[0] = u2.astype(BF16)

    logits = lax.dot_general(wrt_ref[...], u2, (((1,), (1,)), ((), ())),
                             preferred_element_type=F32,
                             precision=lax.Precision.HIGHEST) + brt_ref[...]
    eidx = lax.broadcasted_iota(jnp.int32, (N_EXPERTS, tm), 0)
    work = logits
    vals, idxs = [], []
    for _ in range(TOP_K):
        m = jnp.max(work, axis=0, keepdims=True)
        idx = jnp.min(jnp.where(work == m, eidx, N_EXPERTS), axis=0, keepdims=True)
        vals.append(m)
        idxs.append(idx)
        work = jnp.where(eidx == idx, -jnp.inf, work)
    exps = [jnp.exp(vk - vals[0]) for vk in vals]
    tot = exps[0] + exps[1] + exps[2] + exps[3]
    sel = jnp.zeros((N_EXPERTS, tm), F32)
    for idx in idxs:
        sel = sel + (eidx == idx).astype(F32)
    tr = lax.broadcasted_iota(jnp.int32, (tm, tm), 0)
    tc = lax.broadcasted_iota(jnp.int32, (tm, tm), 1)
    rank = _dot(sel.astype(BF16), (tr < tc).astype(BF16))
    cnt = jnp.sum(sel, axis=1, keepdims=True)
    run = jnp.floor((cnt + (ROW_ALIGN - 1)) * (1.0 / ROW_ALIGN)) * ROW_ALIGN
    er = lax.broadcasted_iota(jnp.int32, (N_EXPERTS, N_EXPERTS), 0)
    ec = lax.broadcasted_iota(jnp.int32, (N_EXPERTS, N_EXPERTS), 1)
    run_start = _dot((ec < er).astype(BF16),
                     jnp.broadcast_to(run, (N_EXPERTS, LANES)).astype(BF16))[:, 0:1]
    slot = rank + run_start
    for k in range(TOP_K):
        slot_ref[0, k:k + 1, :] = jnp.sum(jnp.where(eidx == idxs[k], slot, 0.0), axis=0,
                                          keepdims=True).astype(jnp.int32)
        prob_ref[0, k:k + 1, :] = exps[k] / tot
    cnt_ref[0] = cnt.astype(jnp.int32)


def _post(attn_outs, pg, sga, x, ada, w_attn_out, w_o, ln1_g, ln1_b, w_router, b_router):
    B, S, D = x.shape
    tm = POST_TILE
    nt = S // tm
    N = B * S
    const2 = lambda b, i: (0, 0)
    in_specs, args = [], []
    for (o, lse), (_, dil) in zip(attn_outs, ATTN_GROUPS):
        spec = pl.BlockSpec((1, dil, tm // dil, GROUP_WIDTH), lambda b, i: (b, 0, i, 0))
        in_specs += [spec, spec]
        args += [o, lse]
    tok_spec = pl.BlockSpec((1, tm, D), lambda b, i: (b, i, 0))
    in_specs += [tok_spec, tok_spec, tok_spec,
                 pl.BlockSpec((1, 6, D), lambda b, i: (b, 0, 0)),
                 pl.BlockSpec((GROUP_WIDTH, D), const2),
                 pl.BlockSpec((D, D), const2),
                 pl.BlockSpec((1, D), const2),
                 pl.BlockSpec((1, D), const2),
                 pl.BlockSpec((N_EXPERTS, D), const2),
                 pl.BlockSpec((N_EXPERTS, 1), const2)]
    args += [pg, sga, x, ada, w_attn_out.astype(BF16), w_o.astype(BF16),
             ln1_g.reshape(1, D), ln1_b.reshape(1, D), w_router.T, b_router.reshape(N_EXPERTS, 1)]
    nc = N // tm
    route_spec = pl.BlockSpec((1, TOP_K, tm), lambda b, i: (b * nt + i, 0, 0))
    out_specs = [tok_spec, tok_spec, route_spec, route_spec,
                 pl.BlockSpec((1, N_EXPERTS, 1), lambda b, i: (b * nt + i, 0, 0))]
    out_shapes = [jax.ShapeDtypeStruct((B, S, D), F32), jax.ShapeDtypeStruct((B, S, D), BF16),
                  jax.ShapeDtypeStruct((nc, TOP_K, tm), jnp.int32), jax.ShapeDtypeStruct((nc, TOP_K, tm), F32),
                  jax.ShapeDtypeStruct((nc, N_EXPERTS, 1), jnp.int32)]
    return pl.pallas_call(
        _post_kernel,
        grid=(B, nt),
        in_specs=in_specs,
        out_specs=out_specs,
        out_shape=out_shapes,
        scratch_shapes=[pltpu.VMEM((GROUP_WIDTH // LANES, tm, LANES), F32)] * 4,
        compiler_params=pltpu.CompilerParams(
            dimension_semantics=("parallel", "parallel"), vmem_limit_bytes=VMEM_LIMIT_BYTES),
        name="post",
    )(*args)


def _for_each_piece(tile, run_ref, dest_ref, npiece_ref, fn):
    def per_expert(e, count):
        idx = tile * N_EXPERTS + e
        loc, dst = run_ref[idx], dest_ref[idx]

        def per_piece(p, n):
            off = p * ROW_ALIGN
            fn(pl.multiple_of(loc + off, ROW_ALIGN), pl.multiple_of(dst + off, ROW_ALIGN))
            return n + 1

        return lax.fori_loop(0, npiece_ref[idx], per_piece, count)

    return lax.fori_loop(0, N_EXPERTS, per_expert, 0)


def _wait_rows(src_ref, dst_ref, rows, sem):
    @pl.when(rows > 0)
    def _():
        n = pl.multiple_of(rows, ROW_ALIGN)
        pltpu.make_async_copy(src_ref.at[pl.ds(0, n)], dst_ref.at[pl.ds(0, n)], sem).wait()


def _dispatch_kernel(run_ref, dest_ref, npiece_ref, tail_ref, ntail_ref, nused_ref,
                     slot_ref, prob_ref, u2_ref, xs_hbm, local2_ref, zero_ref, sem2, zsem, rows_ref):
    tile = pl.program_id(0)
    buf = tile % 2
    local_ref = local2_ref.at[buf]
    sem = sem2.at[buf]
    tt, d = u2_ref.shape
    u2 = u2_ref[...]
    slots = [slot_ref[0, k:k + 1, :] for k in range(TOP_K)]
    probs = [prob_ref[0, k:k + 1, :] for k in range(TOP_K)]
    lane = lax.broadcasted_iota(jnp.int32, (SORT_CHUNK, LANES), 1)
    for r0 in range(0, LOCAL_ROWS, SORT_CHUNK):
        row = r0 + lax.broadcasted_iota(jnp.int32, (SORT_CHUNK, tt), 0)
        w = jnp.zeros((SORT_CHUNK, tt), F32)
        for k in range(TOP_K):
            w = w + jnp.where(row == slots[k], probs[k], 0.0)
        onehot = jnp.where(w != 0.0, 1.0, 0.0).astype(BF16)
        local_ref[r0:r0 + SORT_CHUNK, 0:d] = _dot(onehot, u2).astype(BF16)
        wr = jnp.sum(w, axis=1, keepdims=True)
        hi = wr.astype(BF16).astype(F32)
        mid = (wr - hi).astype(BF16).astype(F32)
        lo = wr - hi - mid
        parts = jnp.where(lane == 0, hi, jnp.where(lane == 1, mid, jnp.where(lane == 2, lo, 0.0)))
        local_ref[r0:r0 + SORT_CHUNK, d:d + LANES] = parts.astype(BF16)

    def piece(loc, dst):
        return pltpu.make_async_copy(local_ref.at[pl.ds(loc, ROW_ALIGN)], xs_hbm.at[pl.ds(dst, ROW_ALIGN)], sem)

    started = _for_each_piece(tile, run_ref, dest_ref, npiece_ref, lambda loc, dst: piece(loc, dst).start())
    rows_ref[buf] = started * ROW_ALIGN

    @pl.when(tile > 0)
    def _():
        _wait_rows(local2_ref.at[1 - buf], xs_hbm, rows_ref[1 - buf], sem2.at[1 - buf])

    @pl.when(tile == pl.num_programs(0) - 1)
    def _():
        _wait_rows(local_ref, xs_hbm, rows_ref[buf], sem)
        zero_ref[...] = jnp.zeros_like(zero_ref)

        def tail_piece(e, p):
            dst = pl.multiple_of(tail_ref[e] + p * ROW_ALIGN, ROW_ALIGN)
            return pltpu.make_async_copy(zero_ref, xs_hbm.at[pl.ds(dst, ROW_ALIGN)], zsem)

        def start_tail(e, carry):
            return lax.fori_loop(0, ntail_ref[e], lambda p, c: (tail_piece(e, p).start(), c)[1], carry)

        def wait_tail(e, carry):
            return lax.fori_loop(0, ntail_ref[e], lambda p, c: (tail_piece(e, p).wait(), c)[1], carry)

        lax.fori_loop(0, N_EXPERTS, start_tail, 0)
        lax.fori_loop(0, N_EXPERTS, wait_tail, 0)

        def spare_piece(p):
            return pltpu.make_async_copy(zero_ref, xs_hbm.at[pl.ds(pl.multiple_of(p * ROW_ALIGN, ROW_ALIGN),
                                                                   ROW_ALIGN)], zsem)

        first_spare = nused_ref[0] * (MOE_BLOCK // ROW_ALIGN)
        end_spare = xs_hbm.shape[0] // ROW_ALIGN
        lax.fori_loop(first_spare, end_spare, lambda p, c: (spare_piece(p).start(), c)[1], 0)
        lax.fori_loop(first_spare, end_spare, lambda p, c: (spare_piece(p).wait(), c)[1], 0)


def _dispatch(tables, slot, prob, u2, n_rows):
    N, D = u2.shape
    tt = POST_TILE
    route_spec = pl.BlockSpec((1, TOP_K, tt), lambda i, *t: (i, 0, 0))
    grid_spec = pltpu.PrefetchScalarGridSpec(
        num_scalar_prefetch=len(tables),
        grid=(N // tt,),
        in_specs=[route_spec, route_spec, pl.BlockSpec((tt, D), lambda i, *t: (i, 0))],
        out_specs=pl.BlockSpec(memory_space=pl.ANY),
        scratch_shapes=[pltpu.VMEM((2, LOCAL_ROWS, D + LANES), BF16), pltpu.VMEM((ROW_ALIGN, D + LANES), BF16),
                        pltpu.SemaphoreType.DMA((2,)), pltpu.SemaphoreType.DMA(()),
                        pltpu.SMEM((2,), jnp.int32)],
    )
    return pl.pallas_call(
        _dispatch_kernel,
        grid_spec=grid_spec,
        out_shape=jax.ShapeDtypeStruct((n_rows, D + LANES), BF16),
        compiler_params=pltpu.CompilerParams(
            dimension_semantics=("arbitrary",), vmem_limit_bytes=VMEM_LIMIT_BYTES),
        name="dispatch",
    )(*tables, slot, prob, u2)


def _expert_kernel(be_ref, next_ref, nused_ref, xs_ref, wg_hbm, bg_ref, wu_hbm, bu_ref, wd_hbm, bd_ref,
                   ys_ref, stage, wg_s, wu_s, wd_s, sem):
    i = pl.program_id(0)
    used = i < nused_ref[0]
    prev = be_ref[jnp.maximum(i - 1, 0)]
    fresh = (i == 0) | (be_ref[i] != prev)

    def fetch(e):
        return [pltpu.make_async_copy(w_hbm.at[e], stage.at[j], sem.at[j])
                for j, w_hbm in enumerate((wg_hbm, wu_hbm, wd_hbm))]

    @pl.when(i == 0)
    def _():
        for copy in fetch(be_ref[0]):
            copy.start()

    @pl.when(used & fresh)
    def _():
        for copy in fetch(be_ref[i]):
            copy.wait()
        wg_s[...] = stage[0].astype(BF16)
        wu_s[...] = stage[1].astype(BF16)
        wd_s[...] = stage[2].astype(BF16)

        @pl.when(next_ref[i] >= 0)
        def _():
            for copy in fetch(next_ref[i]):
                copy.start()

    @pl.when(used)
    def _():
        d = ys_ref.shape[1]
        xb = xs_ref[:, 0:d]
        parts = xs_ref[:, d:d + LANES].astype(F32)
        weight = parts[:, 0:1] + parts[:, 1:2] + parts[:, 2:3]
        g = _dot(xb, wg_s[...]) + bg_ref[0]
        up = _dot(xb, wu_s[...]) + bu_ref[0]
        g = jnp.minimum(g, SWIGLU_LIMIT)
        up = jnp.clip(up, -SWIGLU_LIMIT, SWIGLU_LIMIT)
        h = g * jax.nn.sigmoid(SWIGLU_ALPHA * g) * (up + 1.0)
        ys_ref[...] = ((_dot(h.astype(BF16), wd_s[...]) + bd_ref[0]) * weight).astype(BF16)

    @pl.when(jnp.logical_not(used))
    def _():
        ys_ref[...] = jnp.zeros_like(ys_ref)


def _experts(be, next_expert, nused, xs, w_gate, b_gate, w_up, b_up, w_down, b_down):
    P, width = xs.shape
    D = width - LANES
    E = w_gate.shape[0]

    def live(i, nu):
        return jnp.maximum(jnp.minimum(i, nu[0] - 1), 0)

    w_spec = pl.BlockSpec(memory_space=pl.ANY)
    b_spec = pl.BlockSpec((1, 1, D), lambda i, be, nx, nu: (be[live(i, nu)], 0, 0))
    grid_spec = pltpu.PrefetchScalarGridSpec(
        num_scalar_prefetch=3,
        grid=(P // MOE_BLOCK,),
        in_specs=[pl.BlockSpec((MOE_BLOCK, width), lambda i, be, nx, nu: (live(i, nu), 0)),
                  w_spec, b_spec, w_spec, b_spec, w_spec, b_spec],
        out_specs=pl.BlockSpec((MOE_BLOCK, D), lambda i, be, nx, nu: (i, 0)),
        scratch_shapes=[pltpu.VMEM((3, D, D), F32)] + [pltpu.VMEM((D, D), BF16)] * 3
                       + [pltpu.SemaphoreType.DMA((3,))],
    )
    return pl.pallas_call(
        _expert_kernel,
        grid_spec=grid_spec,
        out_shape=jax.ShapeDtypeStruct((P, D), BF16),
        compiler_params=pltpu.CompilerParams(
            dimension_semantics=("arbitrary",), vmem_limit_bytes=VMEM_LIMIT_BYTES),
        name="experts",
    )(be, next_expert, nused, xs, w_gate, b_gate.reshape(E, 1, D), w_up, b_up.reshape(E, 1, D),
      w_down, b_down.reshape(E, 1, D))


def _combine_kernel(run_ref, dest_ref, npiece_ref,
                    slot_ref, x1_ref, ada_ref, g2_ref, b2_ref, ys_hbm,
                    out_ref, local2_ref, sem2, rows_ref):
    tile = pl.program_id(0)
    buf = tile % 2
    tt = x1_ref.shape[0]

    def fetch(t, b):
        def piece(loc, dst):
            return pltpu.make_async_copy(ys_hbm.at[pl.ds(dst, ROW_ALIGN)],
                                         local2_ref.at[b, pl.ds(loc, ROW_ALIGN)], sem2.at[b])
        started = _for_each_piece(t, run_ref, dest_ref, npiece_ref, lambda loc, dst: piece(loc, dst).start())
        rows_ref[b] = started * ROW_ALIGN

    @pl.when(tile == 0)
    def _():
        local2_ref[...] = jnp.zeros_like(local2_ref)
        fetch(0, 0)

    @pl.when(tile + 1 < pl.num_programs(0))
    def _():
        fetch(tile + 1, 1 - buf)

    local_ref = local2_ref.at[buf]
    _wait_rows(ys_hbm, local_ref, rows_ref[buf], sem2.at[buf])
    slots = [slot_ref[:, k:k + 1].astype(F32) for k in range(TOP_K)]

    ffn = jnp.zeros((tt, x1_ref.shape[1]), F32)
    for r0 in range(0, LOCAL_ROWS, SORT_CHUNK):
        col = (r0 + lax.broadcasted_iota(jnp.int32, (tt, SORT_CHUNK), 1)).astype(F32)
        miss = (col - slots[0]) * (col - slots[1]) * (col - slots[2]) * (col - slots[3])
        onehot = jnp.where(miss == 0.0, 1.0, 0.0).astype(BF16)
        ffn = ffn + _dot(onehot, local_ref[r0:r0 + SORT_CHUNK, :])
    gate2 = ada_ref[0, 5:6, :]
    y = DN_ALPHA * x1_ref[...] + (1.0 + gate2) * ffn
    out_ref[...] = _layer_norm(y) * g2_ref[...] + b2_ref[...]


def _combine(tables, slot_tok, x1, ada, ln2_g, ln2_b, ys, tiles_per_batch):
    N, D = x1.shape
    tt = POST_TILE
    const = lambda i, *t: (0, 0)
    tok4 = pl.BlockSpec((tt, TOP_K), lambda i, *t: (i, 0))
    grid_spec = pltpu.PrefetchScalarGridSpec(
        num_scalar_prefetch=len(tables),
        grid=(N // tt,),
        in_specs=[tok4,
                  pl.BlockSpec((tt, D), lambda i, *t: (i, 0)),
                  pl.BlockSpec((1, 6, D), lambda i, *t: (i // tiles_per_batch, 0, 0)),
                  pl.BlockSpec((1, D), const),
                  pl.BlockSpec((1, D), const),
                  pl.BlockSpec(memory_space=pl.ANY)],
        out_specs=pl.BlockSpec((tt, D), lambda i, *t: (i, 0)),
        scratch_shapes=[pltpu.VMEM((2, LOCAL_ROWS, D), BF16), pltpu.SemaphoreType.DMA((2,)),
                        pltpu.SMEM((2,), jnp.int32)],
    )
    return pl.pallas_call(
        _combine_kernel,
        grid_spec=grid_spec,
        out_shape=jax.ShapeDtypeStruct((N, D), F32),
        compiler_params=pltpu.CompilerParams(
            dimension_semantics=("arbitrary",), vmem_limit_bytes=VMEM_LIMIT_BYTES),
        name="combine",
    )(*tables, slot_tok, x1, ada, ln2_g.reshape(1, D), ln2_b.reshape(1, D), ys)


def _routing_tables(cnt, nblk):
    nt, E = cnt.shape
    i32 = jnp.int32
    run = (cnt + ROW_ALIGN - 1) // ROW_ALIGN * ROW_ALIGN
    run_start = jnp.cumsum(run, axis=1) - run
    seg_len = jnp.sum(run, axis=0)
    seg_blocks = (seg_len + MOE_BLOCK - 1) // MOE_BLOCK
    b_end = jnp.cumsum(seg_blocks)
    seg_off = (b_end - seg_blocks) * MOE_BLOCK
    dest = seg_off[None, :] + jnp.cumsum(run, axis=0) - run
    nused = b_end[-1:].astype(i32)
    i = jnp.arange(nblk, dtype=i32)
    be = jnp.minimum(jnp.sum((i[:, None] >= b_end[None, :]).astype(i32), axis=1), E - 1)
    e_ids = jnp.arange(E, dtype=i32)
    later = (e_ids[None, :] > e_ids[:, None]) & (seg_blocks > 0)[None, :]
    next_e = jnp.min(jnp.where(later, e_ids[None, :], E), axis=1)
    next_e = jnp.where(next_e < E, next_e, -1)
    nxt = jnp.sum(jnp.where(be[:, None] == e_ids[None, :], next_e[None, :], 0), axis=1)
    tail = seg_off + seg_len
    ntail = (seg_blocks * MOE_BLOCK - seg_len) // ROW_ALIGN
    piece_tables = (run_start.reshape(-1).astype(i32), dest.reshape(-1).astype(i32),
                    (run // ROW_ALIGN).reshape(-1).astype(i32))
    return piece_tables, (tail.astype(i32), ntail.astype(i32)), (be.astype(i32), nxt.astype(i32), nused)


def kernel(x, c, positions, w_ada, b_ada, w_in, pool_w, pool_scale, w_pool_out, w_attn_out, w_o,
           ln1_g, ln1_b, w_router, b_router, w_gate, b_gate, w_up, b_up, w_down, b_down, ln2_g, ln2_b):
    B, S, D = x.shape
    N = B * S
    assert D == D_MODEL and S % PROJ_TILE == 0 and S % (16 * ATTN_BLOCK) == 0
    assert S % POST_TILE == 0 and LOCAL_ROWS % SORT_CHUNK == 0
    nt = N // POST_TILE
    nblk = (N * TOP_K + nt * N_EXPERTS * (ROW_ALIGN - 1)) // MOE_BLOCK + N_EXPERTS
    for l in range(DEPTH):
        ada = _ada(c, w_ada[l], b_ada[l])
        proj_out = _proj(x, positions, ada, w_in[l], pool_w[l], pool_scale[l], w_pool_out[l])
        qkv, (pg, sga) = proj_out[:9], proj_out[9:]
        attn_outs = [_attention(*qkv[3 * g:3 * g + 3]) for g in range(len(ATTN_GROUPS))]
        x1, u2, slot, prob, cnt = _post(attn_outs, pg, sga, x, ada, w_attn_out[l], w_o[l],
                                        ln1_g[l], ln1_b[l], w_router[l], b_router[l])
        piece_tables, tail_tables, block_tables = _routing_tables(cnt.reshape(nt, N_EXPERTS), nblk)
        xs = _dispatch(piece_tables + tail_tables + block_tables[2:], slot, prob, u2.reshape(N, D),
                       nblk * MOE_BLOCK)
        ys = _experts(*block_tables, xs, w_gate[l], b_gate[l], w_up[l], b_up[l], w_down[l], b_down[l])
        slot_tok = slot.transpose(0, 2, 1).reshape(N, TOP_K)
        out = _combine(piece_tables, slot_tok, x1.reshape(N, D), ada,
                       ln2_g[l], ln2_b[l], ys, S // POST_TILE)
        x = out.reshape(B, S, D)
    return x
```

```python
import functools

import jax
import jax.numpy as jnp
import numpy as np
from jax import lax
from jax.experimental import pallas as pl
from jax.experimental.pallas import tpu as pltpu

F32 = jnp.float32
BF16 = jnp.bfloat16

D_MODEL = 1024
POOL_WINDOWS = (2, 4, 8, 16)
POOL_WIDTH = D_MODEL // 2
POOL_GROUP = POOL_WIDTH // len(POOL_WINDOWS)
POOL_HALO = 16
HEAD_DIM = 64
ATTN_GROUPS = ((128, 1), (512, 4), (2048, 16))
HEADS_PER_GROUP = 4
GROUP_WIDTH = HEADS_PER_GROUP * HEAD_DIM
N_HEADS = HEADS_PER_GROUP * len(ATTN_GROUPS)
ATTN_WIDTH = N_HEADS * HEAD_DIM
ATTN_BLOCK = 128
ROT_DIM = HEAD_DIM // 4
ROPE_THETA = 500000.0
N_EXPERTS = 32
TOP_K = 4
SWIGLU_ALPHA = 1.702
SWIGLU_LIMIT = 7.0
MOE_BLOCK = 256
DEPTH = 1
DN_ALPHA = (2.0 * DEPTH) ** 0.25
LN_EPS = 1e-5
NEG_INF = -1e30

OFF_Q = POOL_WIDTH
OFF_K = OFF_Q + ATTN_WIDTH
OFF_V = OFF_K + ATTN_WIDTH
OFF_GP = OFF_V + ATTN_WIDTH
OFF_GA = OFF_GP + D_MODEL
IN_WIDTH = OFF_GA + D_MODEL

VMEM_LIMIT_BYTES = 56 * 1024 * 1024
LANES = 128

PROJ_TILE = 512
POST_TILE = 512
ATTN_QROWS = 1024
ROW_ALIGN = 16
SORT_CHUNK = 512
LOCAL_ROWS = -(-(POST_TILE * TOP_K + N_EXPERTS * (ROW_ALIGN - 1)) // SORT_CHUNK) * SORT_CHUNK


def _layer_norm(x):
    mu = jnp.mean(x, axis=-1, keepdims=True)
    xc = x - mu
    var = jnp.mean(xc * xc, axis=-1, keepdims=True)
    return xc * lax.rsqrt(var + LN_EPS)


def _dot(a, b):
    return jnp.dot(a, b, preferred_element_type=F32)


def _ada_kernel(c_ref, w_ref, b_ref, o_ref):
    c = c_ref[...]
    s = c * jax.nn.sigmoid(c)
    o_ref[...] = jnp.dot(s, w_ref[...], preferred_element_type=F32,
                         precision=lax.Precision.HIGHEST) + b_ref[...]


def _ada(c, w_ada, b_ada):
    B, D = c.shape
    rows = 8
    c_pad = jnp.pad(c, ((0, rows - B), (0, 0)))
    n_out = w_ada.shape[1]
    out = pl.pallas_call(
        _ada_kernel,
        grid=(n_out // D,),
        in_specs=[pl.BlockSpec((rows, D), lambda j: (0, 0)),
                  pl.BlockSpec((D, D), lambda j: (0, j)),
                  pl.BlockSpec((1, D), lambda j: (0, j))],
        out_specs=pl.BlockSpec((rows, D), lambda j: (0, j)),
        out_shape=jax.ShapeDtypeStruct((rows, n_out), F32),
        name="ada",
    )(c_pad, w_ada, b_ada.reshape(1, n_out))
    return out[:B].reshape(B, 6, D)


def _rope_tables():
    lane = np.arange(128)
    li = lane % HEAD_DIM
    half = ROT_DIM // 2
    inv_freq = jnp.power(ROPE_THETA, -jnp.arange(half, dtype=F32) * (2.0 / ROT_DIM))
    invf = inv_freq[li % half][None, :]
    m_cos = (li < ROT_DIM).astype(np.float32)[None, :]
    m_lo = (li < half).astype(np.float32)[None, :]
    m_hi = ((li >= half) & (li < ROT_DIM)).astype(np.float32)[None, :]
    return jnp.concatenate([invf, jnp.asarray(m_cos), jnp.asarray(m_lo), jnp.asarray(m_hi),
                            jnp.zeros((4, 128), F32)], axis=0)


def _proj_kernel(x_ref, xh_ref, pos_ref, ada_ref, rope_ref, win_ref, poolw_ref, pscale_ref, wpo_ref,
                 q1_ref, k1_ref, v1_ref, q4_ref, k4_ref, v4_ref, q16_ref, k16_ref, v16_ref,
                 pg_ref, sga_ref, xpe_ref, cls_ref):
    tm = x_ref.shape[1]
    i = pl.program_id(1)
    shift1 = ada_ref[0, 0:1, :]
    scale1 = ada_ref[0, 1:2, :]

    def modulated(xv):
        return (_layer_norm(xv) * (1.0 + scale1) + shift1).astype(BF16)

    u = modulated(x_ref[0])
    uh = modulated(xh_ref[0])

    xp = _dot(u, win_ref[:, 0:POOL_WIDTH])
    xph = _dot(uh, win_ref[:, 0:POOL_WIDTH])
    xph = jnp.where(i > 0, xph, 0.0)
    xpe_ref[0:POOL_HALO, :] = xph
    xpe_ref[POOL_HALO:, :] = xp
    tok = i * tm + lax.broadcasted_iota(jnp.int32, (tm, 1), 0)
    ys = []
    for g, w in enumerate(POOL_WINDOWS):
        cols = slice(g * POOL_GROUP, (g + 1) * POOL_GROUP)
        xg = xp[:, cols]
        acc = xg
        for j in range(1, w):
            acc = acc + xpe_ref[POOL_HALO - j:POOL_HALO - j + tm, cols]
        cnt = jnp.minimum(tok + 1, w).astype(F32)
        mixed = (acc / cnt - xg).astype(BF16)
        ys.append(_dot(mixed, poolw_ref[g]) * pscale_ref[:, cols])
    y = jnp.concatenate(ys, axis=1).astype(BF16)
    pooled = _dot(y, wpo_ref[...])
    g_p = _dot(u, win_ref[:, OFF_GP:OFF_GP + D_MODEL])
    pg_ref[0] = jax.nn.sigmoid(g_p) * pooled
    g_a = _dot(u, win_ref[:, OFF_GA:OFF_GA + D_MODEL])
    sga_ref[0] = jax.nn.sigmoid(g_a)

    pos = pos_ref[0].astype(F32)
    ang = pos * rope_ref[0:1, :]
    cos = jnp.cos(ang)
    sin = jnp.sin(ang)
    c_mul = jnp.where(rope_ref[1:2, :] > 0, cos, 1.0)
    s_lo = jnp.where(rope_ref[2:3, :] > 0, -sin, 0.0)
    s_hi = jnp.where(rope_ref[3:4, :] > 0, sin, 0.0)
    c_mul = jnp.concatenate([c_mul, c_mul], axis=1)
    s_lo = jnp.concatenate([s_lo, s_lo], axis=1)
    s_hi = jnp.concatenate([s_hi, s_hi], axis=1)
    half = ROT_DIM // 2

    def rotate(a):
        up = pltpu.roll(a, GROUP_WIDTH - half, axis=1)
        dn = pltpu.roll(a, half, axis=1)
        return a * c_mul + up * s_lo + dn * s_hi

    def emit(a, out_ref, dil):
        if dil == 1:
            out_ref[0, 0] = a.astype(BF16)
            return
        for c in range(GROUP_WIDTH // LANES):
            cls_ref[c] = a[:, c * LANES:(c + 1) * LANES]
        for r in range(dil):
            for c in range(GROUP_WIDTH // LANES):
                out_ref[0, r, :, c * LANES:(c + 1) * LANES] = (
                    cls_ref[c, pl.ds(r, tm // dil, stride=dil), :].astype(BF16))

    outs = ((q1_ref, k1_ref, v1_ref), (q4_ref, k4_ref, v4_ref), (q16_ref, k16_ref, v16_ref))
    for gi, (_, dil) in enumerate(ATTN_GROUPS):
        qo, ko, vo = outs[gi]
        c0 = gi * GROUP_WIDTH
        emit(rotate(_dot(u, win_ref[:, OFF_Q + c0:OFF_Q + c0 + GROUP_WIDTH])), qo, dil)
        emit(rotate(_dot(u, win_ref[:, OFF_K + c0:OFF_K + c0 + GROUP_WIDTH])), ko, dil)
        emit(_dot(u, win_ref[:, OFF_V + c0:OFF_V + c0 + GROUP_WIDTH]), vo, dil)


def _proj(x, positions, ada, w_in, pool_w, pool_scale, w_pool_out):
    B, S, D = x.shape
    tm = PROJ_TILE
    nt = S // tm
    halo_blocks = tm // POOL_HALO
    const2 = lambda b, i: (0, 0)
    in_specs = [
        pl.BlockSpec((1, tm, D), lambda b, i: (b, i, 0)),
        pl.BlockSpec((1, POOL_HALO, D), lambda b, i: (b, jnp.maximum(i * halo_blocks - 1, 0), 0)),
        pl.BlockSpec((1, tm, 1), lambda b, i: (b, i, 0)),
        pl.BlockSpec((1, 6, D), lambda b, i: (b, 0, 0)),
        pl.BlockSpec((8, 128), const2),
        pl.BlockSpec((D, IN_WIDTH), const2),
        pl.BlockSpec((len(POOL_WINDOWS), POOL_GROUP, POOL_GROUP), lambda b, i: (0, 0, 0)),
        pl.BlockSpec((1, POOL_WIDTH), const2),
        pl.BlockSpec((POOL_WIDTH, D), const2),
    ]
    out_specs, out_shapes = [], []
    for _, dil in ATTN_GROUPS:
        for _ in range(3):
            out_specs.append(pl.BlockSpec((1, dil, tm // dil, GROUP_WIDTH), lambda b, i: (b, 0, i, 0)))
            out_shapes.append(jax.ShapeDtypeStruct((B, dil, S // dil, GROUP_WIDTH), BF16))
    for _ in range(2):
        out_specs.append(pl.BlockSpec((1, tm, D), lambda b, i: (b, i, 0)))
        out_shapes.append(jax.ShapeDtypeStruct((B, S, D), F32))
    return pl.pallas_call(
        _proj_kernel,
        grid=(B, nt),
        in_specs=in_specs,
        out_specs=out_specs,
        out_shape=out_shapes,
        scratch_shapes=[pltpu.VMEM((tm + POOL_HALO, POOL_WIDTH), F32),
                        pltpu.VMEM((GROUP_WIDTH // LANES, tm, LANES), F32)],
        compiler_params=pltpu.CompilerParams(
            dimension_semantics=("parallel", "parallel"), vmem_limit_bytes=VMEM_LIMIT_BYTES),
        name="proj",
    )(x, x, positions.reshape(B, S, 1), ada, _rope_tables(), w_in.astype(BF16),
      pool_w.astype(BF16), pool_scale.reshape(1, POOL_WIDTH), w_pool_out.astype(BF16))


def _attn_kernel(q_ref, k_ref, v_ref, kh_ref, vh_ref, o_ref, lse_ref, kf_ref, vf_ref):
    qb = q_ref.shape[2]
    n = pl.program_id(2)
    kf_ref[0:ATTN_BLOCK, :] = kh_ref[0, 0]
    kf_ref[ATTN_BLOCK:, :] = k_ref[0, 0]
    vf_ref[0:ATTN_BLOCK, :] = vh_ref[0, 0]
    vf_ref[ATTN_BLOCK:, :] = v_ref[0, 0]
    qi = lax.broadcasted_iota(jnp.int32, (ATTN_BLOCK, 2 * ATTN_BLOCK), 0)
    kj = lax.broadcasted_iota(jnp.int32, (ATTN_BLOCK, 2 * ATTN_BLOCK), 1)
    band = (kj >= qi) & (kj <= qi + ATTN_BLOCK)
    band_bias = jnp.where(band, 0.0, NEG_INF)
    lane = lax.broadcasted_iota(jnp.int32, (ATTN_BLOCK, GROUP_WIDTH), 1)
    low_lanes = lax.broadcasted_iota(jnp.int32, (ATTN_BLOCK, LANES), 1) < HEAD_DIM
    ones = jnp.ones((2 * ATTN_BLOCK, LANES), BF16)
    nh, blk = HEADS_PER_GROUP, ATTN_BLOCK

    def block(j, carry):
        r0 = pl.multiple_of(j * ATTN_BLOCK, ATTN_BLOCK)
        first_key = jnp.where((n > 0) | (j > 0), 0, ATTN_BLOCK)
        bias = band_bias + jnp.where(kj < first_key, NEG_INF, 0.0)
        q = q_ref[0, 0, pl.ds(r0, blk), :].astype(F32)
        kk = kf_ref[pl.ds(r0, 2 * blk), :]
        vv = vf_ref[pl.ds(r0, 2 * blk), :]
        qs = jnp.concatenate([jnp.where((lane >= h * HEAD_DIM) & (lane < (h + 1) * HEAD_DIM), q, 0.0)
                              for h in range(nh)], axis=0).astype(BF16)
        s = lax.dot_general(qs, kk, (((1,), (1,)), ((), ())), preferred_element_type=F32)
        s = jnp.concatenate([s[h * blk:(h + 1) * blk] * (HEAD_DIM ** -0.5) + bias for h in range(nh)], axis=0)
        m = jnp.max(s, axis=-1, keepdims=True)
        p = jnp.exp(s - m).astype(BF16)
        den = _dot(p, ones)
        lse = m + jnp.log(den)
        for hp in range(GROUP_WIDTH // LANES):
            rows = slice(2 * hp * blk, (2 * hp + 2) * blk)
            ls = slice(hp * LANES, (hp + 1) * LANES)
            o2 = _dot(p[rows], vv[:, ls]) / den[rows]
            l2 = lse[rows]
            o_ref[0, 0, pl.ds(r0, blk), ls] = jnp.where(low_lanes, o2[0:blk], o2[blk:2 * blk])
            lse_ref[0, 0, pl.ds(r0, blk), ls] = jnp.where(low_lanes, l2[0:blk], l2[blk:2 * blk])
        return carry

    lax.fori_loop(0, qb // ATTN_BLOCK, block, 0, unroll=4)


def _attention(q, k, v):
    B, dil, L, W = q.shape
    qb = min(L, ATTN_QROWS)
    per = qb // ATTN_BLOCK
    main = pl.BlockSpec((1, 1, qb, W), lambda b, r, n: (b, r, n, 0))
    halo = pl.BlockSpec((1, 1, ATTN_BLOCK, W), lambda b, r, n: (b, r, jnp.maximum(n * per - 1, 0), 0))
    return pl.pallas_call(
        _attn_kernel,
        grid=(B, dil, L // qb),
        in_specs=[main, main, main, halo, halo],
        out_specs=[main, main],
        out_shape=[jax.ShapeDtypeStruct((B, dil, L, W), F32)] * 2,
        scratch_shapes=[pltpu.VMEM((qb + ATTN_BLOCK, W), BF16)] * 2,
        compiler_params=pltpu.CompilerParams(
            dimension_semantics=("parallel", "parallel", "parallel"), vmem_limit_bytes=VMEM_LIMIT_BYTES),
        name=f"attn_d{dil}",
    )(q, k, v, k, v)


def _post_kernel(o1_ref, l1_ref, o4_ref, l4_ref, o16_ref, l16_ref, pg_ref, sga_ref, x_ref, ada_ref,
                 wao_ref, wo_ref, g1_ref, b1_ref, wrt_ref, brt_ref,
                 x1_ref, u2_ref, slot_ref, prob_ref, cnt_ref,
                 s0, s1, s2, s3):
    tm = x_ref.shape[1]

    def token_major(src_ref, scr_ref, dil):
        if dil == 1:
            return src_ref[0, 0]
        for r in range(dil):
            for c in range(GROUP_WIDTH // LANES):
                scr_ref[c, pl.ds(r, tm // dil, stride=dil), :] = src_ref[0, r, :, c * LANES:(c + 1) * LANES]
        return jnp.concatenate([scr_ref[c] for c in range(GROUP_WIDTH // LANES)], axis=1)

    o1, l1 = o1_ref[0, 0], l1_ref[0, 0]
    o4, l4 = token_major(o4_ref, s0, 4), token_major(l4_ref, s1, 4)
    o16, l16 = token_major(o16_ref, s2, 16), token_major(l16_ref, s3, 16)
    mx = jnp.maximum(jnp.maximum(l1, l4), l16)
    e1, e4, e16 = jnp.exp(l1 - mx), jnp.exp(l4 - mx), jnp.exp(l16 - mx)
    attn = (e1 * o1 + e4 * o4 + e16 * o16) / (e1 + e4 + e16)

    merged = pg_ref[0] + sga_ref[0] * _dot(attn.astype(BF16), wao_ref[...])
    mix = _dot(merged.astype(BF16), wo_ref[...])
    gate1 = ada_ref[0, 2:3, :]
    shift2 = ada_ref[0, 3:4, :]
    scale2 = ada_ref[0, 4:5, :]
    x1 = _layer_norm(DN_ALPHA * x_ref[0] + (1.0 + gate1) * mix) * g1_ref[...] + b1_ref[...]
    x1_ref[0] = x1
    u2 = _layer_norm(x1) * (1.0 + scale2) + shift2
    u2_ref[0] = u2.astype(BF16)

    logits = lax.dot_general(wrt_ref[...], u2, (((1,), (1,)), ((), ())),
                             preferred_element_type=F32,
                             precision=lax.Precision.HIGHEST) + brt_ref[...]
    eidx = lax.broadcasted_iota(jnp.int32, (N_EXPERTS, tm), 0)
    work = logits
    vals, idxs = [], []
    for _ in range(TOP_K):
        m = jnp.max(work, axis=0, keepdims=True)
        idx = jnp.min(jnp.where(work == m, eidx, N_EXPERTS), axis=0, keepdims=True)
        vals.append(m)
        idxs.append(idx)
        work = jnp.where(eidx == idx, -jnp.inf, work)
    exps = [jnp.exp(vk - vals[0]) for vk in vals]
    tot = exps[0] + exps[1] + exps[2] + exps[3]
    sel = jnp.zeros((N_EXPERTS, tm), F32)
    for idx in idxs:
        sel = sel + (eidx == idx).astype(F32)
    tr = lax.broadcasted_iota(jnp.int32, (tm, tm), 0)
    tc = lax.broadcasted_iota(jnp.int32, (tm, tm), 1)
    rank = _dot(sel.astype(BF16), (tr < tc).astype(BF16))
    cnt = jnp.sum(sel, axis=1, keepdims=True)
    run = jnp.floor((cnt + (ROW_ALIGN - 1)) * (1.0 / ROW_ALIGN)) * ROW_ALIGN
    er = lax.broadcasted_iota(jnp.int32, (N_EXPERTS, N_EXPERTS), 0)
    ec = lax.broadcasted_iota(jnp.int32, (N_EXPERTS, N_EXPERTS), 1)
    run_start = _dot((ec < er).astype(BF16),
                     jnp.broadcast_to(run, (N_EXPERTS, LANES)).astype(BF16))[:, 0:1]
    slot = rank + run_start
    for k in range(TOP_K):
        slot_ref[0, k:k + 1, :] = jnp.sum(jnp.where(eidx == idxs[k], slot, 0.0), axis=0,
                                          keepdims=True).astype(jnp.int32)
        prob_ref[0, k:k + 1, :] = exps[k] / tot
    cnt_ref[0] = cnt.astype(jnp.int32)


def _post(attn_outs, pg, sga, x, ada, w_attn_out, w_o, ln1_g, ln1_b, w_router, b_router):
    B, S, D = x.shape
    tm = POST_TILE
    nt = S // tm
    N = B * S
    const2 = lambda b, i: (0, 0)
    in_specs, args = [], []
    for (o, lse), (_, dil) in zip(attn_outs, ATTN_GROUPS):
        spec = pl.BlockSpec((1, dil, tm // dil, GROUP_WIDTH), lambda b, i: (b, 0, i, 0))
        in_specs += [spec, spec]
        args += [o, lse]
    tok_spec = pl.BlockSpec((1, tm, D), lambda b, i: (b, i, 0))
    in_specs += [tok_spec, tok_spec, tok_spec,
                 pl.BlockSpec((1, 6, D), lambda b, i: (b, 0, 0)),
                 pl.BlockSpec((GROUP_WIDTH, D), const2),
                 pl.BlockSpec((D, D), const2),
                 pl.BlockSpec((1, D), const2),
                 pl.BlockSpec((1, D), const2),
                 pl.BlockSpec((N_EXPERTS, D), const2),
                 pl.BlockSpec((N_EXPERTS, 1), const2)]
    args += [pg, sga, x, ada, w_attn_out.astype(BF16), w_o.astype(BF16),
             ln1_g.reshape(1, D), ln1_b.reshape(1, D), w_router.T, b_router.reshape(N_EXPERTS, 1)]
    nc = N // tm
    route_spec = pl.BlockSpec((1, TOP_K, tm), lambda b, i: (b * nt + i, 0, 0))
    out_specs = [tok_spec, tok_spec, route_spec, route_spec,
                 pl.BlockSpec((1, N_EXPERTS, 1), lambda b, i: (b * nt + i, 0, 0))]
    out_shapes = [jax.ShapeDtypeStruct((B, S, D), F32), jax.ShapeDtypeStruct((B, S, D), BF16),
                  jax.ShapeDtypeStruct((nc, TOP_K, tm), jnp.int32), jax.ShapeDtypeStruct((nc, TOP_K, tm), F32),
                  jax.ShapeDtypeStruct((nc, N_EXPERTS, 1), jnp.int32)]
    return pl.pallas_call(
        _post_kernel,
        grid=(B, nt),
        in_specs=in_specs,
        out_specs=out_specs,
        out_shape=out_shapes,
        scratch_shapes=[pltpu.VMEM((GROUP_WIDTH // LANES, tm, LANES), F32)] * 4,
        compiler_params=pltpu.CompilerParams(
            dimension_semantics=("parallel", "parallel"), vmem_limit_bytes=VMEM_LIMIT_BYTES),
        name="post",
    )(*args)


def _for_each_piece(tile, run_ref, dest_ref, npiece_ref, fn):
    def per_expert(e, count):
        idx = tile * N_EXPERTS + e
        loc, dst = run_ref[idx], dest_ref[idx]

        def per_piece(p, n):
            off = p * ROW_ALIGN
            fn(pl.multiple_of(loc + off, ROW_ALIGN), pl.multiple_of(dst + off, ROW_ALIGN))
            return n + 1

        return lax.fori_loop(0, npiece_ref[idx], per_piece, count)

    return lax.fori_loop(0, N_EXPERTS, per_expert, 0)


def _wait_rows(src_ref, dst_ref, rows, sem):
    @pl.when(rows > 0)
    def _():
        n = pl.multiple_of(rows, ROW_ALIGN)
        pltpu.make_async_copy(src_ref.at[pl.ds(0, n)], dst_ref.at[pl.ds(0, n)], sem).wait()


def _dispatch_kernel(run_ref, dest_ref, npiece_ref, tail_ref, ntail_ref, nused_ref,
                     slot_ref, prob_ref, u2_ref, xs_hbm, local2_ref, zero_ref, sem2, zsem, rows_ref):
    tile = pl.program_id(0)
    buf = tile % 2
    local_ref = local2_ref.at[buf]
    sem = sem2.at[buf]
    tt, d = u2_ref.shape
    u2 = u2_ref[...]
    slots = [slot_ref[0, k:k + 1, :] for k in range(TOP_K)]
    probs = [prob_ref[0, k:k + 1, :] for k in range(TOP_K)]
    lane = lax.broadcasted_iota(jnp.int32, (SORT_CHUNK, LANES), 1)
    for r0 in range(0, LOCAL_ROWS, SORT_CHUNK):
        row = r0 + lax.broadcasted_iota(jnp.int32, (SORT_CHUNK, tt), 0)
        w = jnp.zeros((SORT_CHUNK, tt), F32)
        for k in range(TOP_K):
            w = w + jnp.where(row == slots[k], probs[k], 0.0)
        onehot = jnp.where(w != 0.0, 1.0, 0.0).astype(BF16)
        local_ref[r0:r0 + SORT_CHUNK, 0:d] = _dot(onehot, u2).astype(BF16)
        wr = jnp.sum(w, axis=1, keepdims=True)
        hi = wr.astype(BF16).astype(F32)
        mid = (wr - hi).astype(BF16).astype(F32)
        lo = wr - hi - mid
        parts = jnp.where(lane == 0, hi, jnp.where(lane == 1, mid, jnp.where(lane == 2, lo, 0.0)))
        local_ref[r0:r0 + SORT_CHUNK, d:d + LANES] = parts.astype(BF16)

    def piece(loc, dst):
        return pltpu.make_async_copy(local_ref.at[pl.ds(loc, ROW_ALIGN)], xs_hbm.at[pl.ds(dst, ROW_ALIGN)], sem)

    started = _for_each_piece(tile, run_ref, dest_ref, npiece_ref, lambda loc, dst: piece(loc, dst).start())
    rows_ref[buf] = started * ROW_ALIGN

    @pl.when(tile > 0)
    def _():
        _wait_rows(local2_ref.at[1 - buf], xs_hbm, rows_ref[1 - buf], sem2.at[1 - buf])

    @pl.when(tile == pl.num_programs(0) - 1)
    def _():
        _wait_rows(local_ref, xs_hbm, rows_ref[buf], sem)
        zero_ref[...] = jnp.zeros_like(zero_ref)

        def tail_piece(e, p):
            dst = pl.multiple_of(tail_ref[e] + p * ROW_ALIGN, ROW_ALIGN)
            return pltpu.make_async_copy(zero_ref, xs_hbm.at[pl.ds(dst, ROW_ALIGN)], zsem)

        def start_tail(e, carry):
            return lax.fori_loop(0, ntail_ref[e], lambda p, c: (tail_piece(e, p).start(), c)[1], carry)

        def wait_tail(e, carry):
            return lax.fori_loop(0, ntail_ref[e], lambda p, c: (tail_piece(e, p).wait(), c)[1], carry)

        lax.fori_loop(0, N_EXPERTS, start_tail, 0)
        lax.fori_loop(0, N_EXPERTS, wait_tail, 0)

        def spare_piece(p):
            return pltpu.make_async_copy(zero_ref, xs_hbm.at[pl.ds(pl.multiple_of(p * ROW_ALIGN, ROW_ALIGN),
                                                                   ROW_ALIGN)], zsem)

        first_spare = nused_ref[0] * (MOE_BLOCK // ROW_ALIGN)
        end_spare = xs_hbm.shape[0] // ROW_ALIGN
        lax.fori_loop(first_spare, end_spare, lambda p, c: (spare_piece(p).start(), c)[1], 0)
        lax.fori_loop(first_spare, end_spare, lambda p, c: (spare_piece(p).wait(), c)[1], 0)


def _dispatch(tables, slot, prob, u2, n_rows):
    N, D = u2.shape
    tt = POST_TILE
    route_spec = pl.BlockSpec((1, TOP_K, tt), lambda i, *t: (i, 0, 0))
    grid_spec = pltpu.PrefetchScalarGridSpec(
        num_scalar_prefetch=len(tables),
        grid=(N // tt,),
        in_specs=[route_spec, route_spec, pl.BlockSpec((tt, D), lambda i, *t: (i, 0))],
        out_specs=pl.BlockSpec(memory_space=pl.ANY),
        scratch_shapes=[pltpu.VMEM((2, LOCAL_ROWS, D + LANES), BF16), pltpu.VMEM((ROW_ALIGN, D + LANES), BF16),
                        pltpu.SemaphoreType.DMA((2,)), pltpu.SemaphoreType.DMA(()),
                        pltpu.SMEM((2,), jnp.int32)],
    )
    return pl.pallas_call(
        _dispatch_kernel,
        grid_spec=grid_spec,
        out_shape=jax.ShapeDtypeStruct((n_rows, D + LANES), BF16),
        compiler_params=pltpu.CompilerParams(
            dimension_semantics=("arbitrary",), vmem_limit_bytes=VMEM_LIMIT_BYTES),
        name="dispatch",
    )(*tables, slot, prob, u2)


def _expert_kernel(first_ref, count_ref, nused_ref, xs_hbm, wg_hbm, bg_ref, wu_hbm, bu_ref, wd_hbm, bd_ref,
                   ys_hbm, stage, wg_s, wu_s, wd_s, xbuf, ybuf, wsem, isem, osem):
    e = pl.program_id(0)
    n_experts = pl.num_programs(0)
    nused = nused_ref[0]
    d = ybuf.shape[2]

    def fetch(x):
        return [pltpu.make_async_copy(w_hbm.at[x], stage.at[j], wsem.at[j])
                for j, w_hbm in enumerate((wg_hbm, wu_hbm, wd_hbm))]

    def load(g, slot):
        return pltpu.make_async_copy(xs_hbm.at[pl.ds(pl.multiple_of(g * MOE_BLOCK, MOE_BLOCK), MOE_BLOCK)],
                                     xbuf.at[slot], isem.at[slot])

    def store(g, slot):
        return pltpu.make_async_copy(ybuf.at[slot],
                                     ys_hbm.at[pl.ds(pl.multiple_of(g * MOE_BLOCK, MOE_BLOCK), MOE_BLOCK)],
                                     osem.at[slot])

    @pl.when(e == 0)
    def _():
        for copy in fetch(0):
            copy.start()
        load(0, 0).start()

    for copy in fetch(e):
        copy.wait()

    @pl.when(count_ref[e] > 0)
    def _():
        wg_s[...] = stage[0].astype(BF16)
        wu_s[...] = stage[1].astype(BF16)
        wd_s[...] = stage[2].astype(BF16)

    @pl.when(e + 1 < n_experts)
    def _():
        for copy in fetch(e + 1):
            copy.start()

    def block(j, carry):
        g = first_ref[e] + j
        slot = g % 2
        load(g, slot).wait()

        @pl.when(g + 1 < nused)
        def _():
            load(g + 1, 1 - slot).start()

        xb = xbuf[slot, :, 0:d]
        parts = xbuf[slot, :, d:d + LANES].astype(F32)
        weight = parts[:, 0:1] + parts[:, 1:2] + parts[:, 2:3]
        gate = _dot(xb, wg_s[...]) + bg_ref[0]
        up = _dot(xb, wu_s[...]) + bu_ref[0]
        gate = jnp.minimum(gate, SWIGLU_LIMIT)
        up = jnp.clip(up, -SWIGLU_LIMIT, SWIGLU_LIMIT)
        h = gate * jax.nn.sigmoid(SWIGLU_ALPHA * gate) * (up + 1.0)
        ys = ((_dot(h.astype(BF16), wd_s[...]) + bd_ref[0]) * weight).astype(BF16)

        @pl.when(g >= 2)
        def _():
            store(g - 2, slot).wait()

        ybuf[slot] = ys
        store(g, slot).start()
        return carry

    lax.fori_loop(0, count_ref[e], block, 0)

    @pl.when(e == n_experts - 1)
    def _():
        @pl.when(nused >= 2)
        def _():
            store(nused - 2, nused % 2).wait()

        store(nused - 1, (nused - 1) % 2).wait()
        ybuf[0] = jnp.zeros_like(ybuf[0])
        total = ys_hbm.shape[0] // MOE_BLOCK
        lax.fori_loop(nused, total, lambda g, c: (store(g, 0).start(), c)[1], 0)
        lax.fori_loop(nused, total, lambda g, c: (store(g, 0).wait(), c)[1], 0)


def _experts(first_block, block_count, nused, xs, w_gate, b_gate, w_up, b_up, w_down, b_down):
    P, width = xs.shape
    D = width - LANES
    E = w_gate.shape[0]
    any_spec = pl.BlockSpec(memory_space=pl.ANY)
    b_spec = pl.BlockSpec((1, 1, D), lambda e, *t: (e, 0, 0))
    grid_spec = pltpu.PrefetchScalarGridSpec(
        num_scalar_prefetch=3,
        grid=(E,),
        in_specs=[any_spec, any_spec, b_spec, any_spec, b_spec, any_spec, b_spec],
        out_specs=any_spec,
        scratch_shapes=[pltpu.VMEM((3, D, D), F32)] + [pltpu.VMEM((D, D), BF16)] * 3
                       + [pltpu.VMEM((2, MOE_BLOCK, width), BF16), pltpu.VMEM((2, MOE_BLOCK, D), BF16),
                          pltpu.SemaphoreType.DMA((3,)), pltpu.SemaphoreType.DMA((2,)),
                          pltpu.SemaphoreType.DMA((2,))],
    )
    return pl.pallas_call(
        _expert_kernel,
        grid_spec=grid_spec,
        out_shape=jax.ShapeDtypeStruct((P, D), BF16),
        compiler_params=pltpu.CompilerParams(
            dimension_semantics=("arbitrary",), vmem_limit_bytes=VMEM_LIMIT_BYTES),
        name="experts",
    )(first_block, block_count, nused, xs, w_gate, b_gate.reshape(E, 1, D), w_up, b_up.reshape(E, 1, D),
      w_down, b_down.reshape(E, 1, D))


def _combine_kernel(run_ref, dest_ref, npiece_ref,
                    slot_ref, x1_ref, ada_ref, g2_ref, b2_ref, ys_hbm,
                    out_ref, local2_ref, sem2, rows_ref):
    tile = pl.program_id(0)
    buf = tile % 2
    tt = x1_ref.shape[0]

    def fetch(t, b):
        def piece(loc, dst):
            return pltpu.make_async_copy(ys_hbm.at[pl.ds(dst, ROW_ALIGN)],
                                         local2_ref.at[b, pl.ds(loc, ROW_ALIGN)], sem2.at[b])
        started = _for_each_piece(t, run_ref, dest_ref, npiece_ref, lambda loc, dst: piece(loc, dst).start())
        rows_ref[b] = started * ROW_ALIGN

    @pl.when(tile == 0)
    def _():
        local2_ref[...] = jnp.zeros_like(local2_ref)
        fetch(0, 0)

    @pl.when(tile + 1 < pl.num_programs(0))
    def _():
        fetch(tile + 1, 1 - buf)

    local_ref = local2_ref.at[buf]
    _wait_rows(ys_hbm, local_ref, rows_ref[buf], sem2.at[buf])
    slots = [slot_ref[:, k:k + 1].astype(F32) for k in range(TOP_K)]

    ffn = jnp.zeros((tt, x1_ref.shape[1]), F32)
    for r0 in range(0, LOCAL_ROWS, SORT_CHUNK):
        col = (r0 + lax.broadcasted_iota(jnp.int32, (tt, SORT_CHUNK), 1)).astype(F32)
        miss = (col - slots[0]) * (col - slots[1]) * (col - slots[2]) * (col - slots[3])
        onehot = jnp.where(miss == 0.0, 1.0, 0.0).astype(BF16)
        ffn = ffn + _dot(onehot, local_ref[r0:r0 + SORT_CHUNK, :])
    gate2 = ada_ref[0, 5:6, :]
    y = DN_ALPHA * x1_ref[...] + (1.0 + gate2) * ffn
    out_ref[...] = _layer_norm(y) * g2_ref[...] + b2_ref[...]


def _combine(tables, slot_tok, x1, ada, ln2_g, ln2_b, ys, tiles_per_batch):
    N, D = x1.shape
    tt = POST_TILE
    const = lambda i, *t: (0, 0)
    tok4 = pl.BlockSpec((tt, TOP_K), lambda i, *t: (i, 0))
    grid_spec = pltpu.PrefetchScalarGridSpec(
        num_scalar_prefetch=len(tables),
        grid=(N // tt,),
        in_specs=[tok4,
                  pl.BlockSpec((tt, D), lambda i, *t: (i, 0)),
                  pl.BlockSpec((1, 6, D), lambda i, *t: (i // tiles_per_batch, 0, 0)),
                  pl.BlockSpec((1, D), const),
                  pl.BlockSpec((1, D), const),
                  pl.BlockSpec(memory_space=pl.ANY)],
        out_specs=pl.BlockSpec((tt, D), lambda i, *t: (i, 0)),
        scratch_shapes=[pltpu.VMEM((2, LOCAL_ROWS, D), BF16), pltpu.SemaphoreType.DMA((2,)),
                        pltpu.SMEM((2,), jnp.int32)],
    )
    return pl.pallas_call(
        _combine_kernel,
        grid_spec=grid_spec,
        out_shape=jax.ShapeDtypeStruct((N, D), F32),
        compiler_params=pltpu.CompilerParams(
            dimension_semantics=("arbitrary",), vmem_limit_bytes=VMEM_LIMIT_BYTES),
        name="combine",
    )(*tables, slot_tok, x1, ada, ln2_g.reshape(1, D), ln2_b.reshape(1, D), ys)


def _routing_tables(cnt):
    nt, E = cnt.shape
    i32 = jnp.int32
    run = (cnt + ROW_ALIGN - 1) // ROW_ALIGN * ROW_ALIGN
    run_start = jnp.cumsum(run, axis=1) - run
    seg_len = jnp.sum(run, axis=0)
    seg_blocks = (seg_len + MOE_BLOCK - 1) // MOE_BLOCK
    b_end = jnp.cumsum(seg_blocks)
    seg_off = (b_end - seg_blocks) * MOE_BLOCK
    dest = seg_off[None, :] + jnp.cumsum(run, axis=0) - run
    nused = b_end[-1:].astype(i32)
    tail = seg_off + seg_len
    ntail = (seg_blocks * MOE_BLOCK - seg_len) // ROW_ALIGN
    piece_tables = (run_start.reshape(-1).astype(i32), dest.reshape(-1).astype(i32),
                    (run // ROW_ALIGN).reshape(-1).astype(i32))
    block_tables = ((b_end - seg_blocks).astype(i32), seg_blocks.astype(i32), nused)
    return piece_tables, (tail.astype(i32), ntail.astype(i32)), block_tables


def kernel(x, c, positions, w_ada, b_ada, w_in, pool_w, pool_scale, w_pool_out, w_attn_out, w_o,
           ln1_g, ln1_b, w_router, b_router, w_gate, b_gate, w_up, b_up, w_down, b_down, ln2_g, ln2_b):
    B, S, D = x.shape
    N = B * S
    assert D == D_MODEL and S % PROJ_TILE == 0 and S % (16 * ATTN_BLOCK) == 0
    assert S % POST_TILE == 0 and LOCAL_ROWS % SORT_CHUNK == 0
    nt = N // POST_TILE
    nblk = (N * TOP_K + nt * N_EXPERTS * (ROW_ALIGN - 1)) // MOE_BLOCK + N_EXPERTS
    for l in range(DEPTH):
        ada = _ada(c, w_ada[l], b_ada[l])
        proj_out = _proj(x, positions, ada, w_in[l], pool_w[l], pool_scale[l], w_pool_out[l])
        qkv, (pg, sga) = proj_out[:9], proj_out[9:]
        attn_outs = [_attention(*qkv[3 * g:3 * g + 3]) for g in range(len(ATTN_GROUPS))]
        x1, u2, slot, prob, cnt = _post(attn_outs, pg, sga, x, ada, w_attn_out[l], w_o[l],
                                        ln1_g[l], ln1_b[l], w_router[l], b_router[l])
        piece_tables, tail_tables, block_tables = _routing_tables(cnt.reshape(nt, N_EXPERTS))
        xs = _dispatch(piece_tables + tail_tables + block_tables[2:], slot, prob, u2.reshape(N, D),
                       nblk * MOE_BLOCK)
        ys = _experts(*block_tables, xs, w_gate[l], b_gate[l], w_up[l], b_up[l], w_down[l], b_down[l])
        slot_tok = slot.transpose(0, 2, 1).reshape(N, TOP_K)
        out = _combine(piece_tables, slot_tok, x1.reshape(N, D), ada,
                       ln2_g[l], ln2_b[l], ys, S // POST_TILE)
        x = out.reshape(B, S, D)
    return x
```

```python
import functools

import jax
import jax.numpy as jnp
import numpy as np
from jax import lax
from jax.experimental import pallas as pl
from jax.experimental.pallas import tpu as pltpu

F32 = jnp.float32
BF16 = jnp.bfloat16

D_MODEL = 1024
POOL_WINDOWS = (2, 4, 8, 16)
POOL_WIDTH = D_MODEL // 2
POOL_GROUP = POOL_WIDTH // len(POOL_WINDOWS)
POOL_HALO = 16
HEAD_DIM = 64
ATTN_GROUPS = ((128, 1), (512, 4), (2048, 16))
HEADS_PER_GROUP = 4
GROUP_WIDTH = HEADS_PER_GROUP * HEAD_DIM
N_HEADS = HEADS_PER_GROUP * len(ATTN_GROUPS)
ATTN_WIDTH = N_HEADS * HEAD_DIM
ATTN_BLOCK = 128
ROT_DIM = HEAD_DIM // 4
ROPE_THETA = 500000.0
N_EXPERTS = 32
TOP_K = 4
SWIGLU_ALPHA = 1.702
SWIGLU_LIMIT = 7.0
MOE_BLOCK = 512
DEPTH = 1
DN_ALPHA = (2.0 * DEPTH) ** 0.25
LN_EPS = 1e-5
NEG_INF = -1e30

OFF_Q = POOL_WIDTH
OFF_K = OFF_Q + ATTN_WIDTH
OFF_V = OFF_K + ATTN_WIDTH
OFF_GP = OFF_V + ATTN_WIDTH
OFF_GA = OFF_GP + D_MODEL
IN_WIDTH = OFF_GA + D_MODEL

VMEM_LIMIT_BYTES = 56 * 1024 * 1024
LANES = 128

PROJ_TILE = 512
POST_TILE = 512
ATTN_QROWS = 1024
ROW_ALIGN = 16
PIECE_ROWS = 64
SORT_CHUNK = 512
LOCAL_ROWS = -(-(POST_TILE * TOP_K + N_EXPERTS * (ROW_ALIGN - 1)) // SORT_CHUNK) * SORT_CHUNK


def _layer_norm(x):
    mu = jnp.mean(x, axis=-1, keepdims=True)
    xc = x - mu
    var = jnp.mean(xc * xc, axis=-1, keepdims=True)
    return xc * lax.rsqrt(var + LN_EPS)


def _dot(a, b):
    return jnp.dot(a, b, preferred_element_type=F32)


def _ada_kernel(c_ref, w_ref, b_ref, o_ref):
    c = c_ref[...]
    s = c * jax.nn.sigmoid(c)
    o_ref[...] = jnp.dot(s, w_ref[...], preferred_element_type=F32,
                         precision=lax.Precision.HIGHEST) + b_ref[...]


def _ada(c, w_ada, b_ada):
    B, D = c.shape
    rows = 8
    c_pad = jnp.pad(c, ((0, rows - B), (0, 0)))
    n_out = w_ada.shape[1]
    out = pl.pallas_call(
        _ada_kernel,
        grid=(n_out // D,),
        in_specs=[pl.BlockSpec((rows, D), lambda j: (0, 0)),
                  pl.BlockSpec((D, D), lambda j: (0, j)),
                  pl.BlockSpec((1, D), lambda j: (0, j))],
        out_specs=pl.BlockSpec((rows, D), lambda j: (0, j)),
        out_shape=jax.ShapeDtypeStruct((rows, n_out), F32),
        name="ada",
    )(c_pad, w_ada, b_ada.reshape(1, n_out))
    return out[:B].reshape(B, 6, D)


def _rope_tables():
    lane = np.arange(128)
    li = lane % HEAD_DIM
    half = ROT_DIM // 2
    inv_freq = jnp.power(ROPE_THETA, -jnp.arange(half, dtype=F32) * (2.0 / ROT_DIM))
    invf = inv_freq[li % half][None, :]
    m_cos = (li < ROT_DIM).astype(np.float32)[None, :]
    m_lo = (li < half).astype(np.float32)[None, :]
    m_hi = ((li >= half) & (li < ROT_DIM)).astype(np.float32)[None, :]
    return jnp.concatenate([invf, jnp.asarray(m_cos), jnp.asarray(m_lo), jnp.asarray(m_hi),
                            jnp.zeros((4, 128), F32)], axis=0)


def _proj_kernel(x_ref, xh_ref, pos_ref, ada_ref, rope_ref, win_ref, poolw_ref, pscale_ref, wpo_ref,
                 q1_ref, k1_ref, v1_ref, q4_ref, k4_ref, v4_ref, q16_ref, k16_ref, v16_ref,
                 pg_ref, sga_ref, xpe_ref, cls_ref):
    tm = x_ref.shape[1]
    i = pl.program_id(1)
    shift1 = ada_ref[0, 0:1, :]
    scale1 = ada_ref[0, 1:2, :]

    def modulated(xv):
        return (_layer_norm(xv) * (1.0 + scale1) + shift1).astype(BF16)

    u = modulated(x_ref[0])
    uh = modulated(xh_ref[0])

    xp = _dot(u, win_ref[:, 0:POOL_WIDTH])
    xph = _dot(uh, win_ref[:, 0:POOL_WIDTH])
    xph = jnp.where(i > 0, xph, 0.0)
    xpe_ref[0:POOL_HALO, :] = xph
    xpe_ref[POOL_HALO:, :] = xp
    tok = i * tm + lax.broadcasted_iota(jnp.int32, (tm, 1), 0)
    ys = []
    for g, w in enumerate(POOL_WINDOWS):
        cols = slice(g * POOL_GROUP, (g + 1) * POOL_GROUP)
        xg = xp[:, cols]
        acc = xg
        for j in range(1, w):
            acc = acc + xpe_ref[POOL_HALO - j:POOL_HALO - j + tm, cols]
        cnt = jnp.minimum(tok + 1, w).astype(F32)
        mixed = (acc / cnt - xg).astype(BF16)
        ys.append(_dot(mixed, poolw_ref[g]) * pscale_ref[:, cols])
    y = jnp.concatenate(ys, axis=1).astype(BF16)
    pooled = _dot(y, wpo_ref[...])
    g_p = _dot(u, win_ref[:, OFF_GP:OFF_GP + D_MODEL])
    pg_ref[0] = jax.nn.sigmoid(g_p) * pooled
    g_a = _dot(u, win_ref[:, OFF_GA:OFF_GA + D_MODEL])
    sga_ref[0] = jax.nn.sigmoid(g_a)

    pos = pos_ref[0].astype(F32)
    ang = pos * rope_ref[0:1, :]
    cos = jnp.cos(ang)
    sin = jnp.sin(ang)
    c_mul = jnp.where(rope_ref[1:2, :] > 0, cos, 1.0)
    s_lo = jnp.where(rope_ref[2:3, :] > 0, -sin, 0.0)
    s_hi = jnp.where(rope_ref[3:4, :] > 0, sin, 0.0)
    c_mul = jnp.concatenate([c_mul, c_mul], axis=1)
    s_lo = jnp.concatenate([s_lo, s_lo], axis=1)
    s_hi = jnp.concatenate([s_hi, s_hi], axis=1)
    half = ROT_DIM // 2

    def rotate(a):
        up = pltpu.roll(a, GROUP_WIDTH - half, axis=1)
        dn = pltpu.roll(a, half, axis=1)
        return a * c_mul + up * s_lo + dn * s_hi

    def emit(a, out_ref, dil):
        if dil == 1:
            out_ref[0, 0] = a.astype(BF16)
            return
        for c in range(GROUP_WIDTH // LANES):
            cls_ref[c] = a[:, c * LANES:(c + 1) * LANES]
        for r in range(dil):
            for c in range(GROUP_WIDTH // LANES):
                out_ref[0, r, :, c * LANES:(c + 1) * LANES] = (
                    cls_ref[c, pl.ds(r, tm // dil, stride=dil), :].astype(BF16))

    outs = ((q1_ref, k1_ref, v1_ref), (q4_ref, k4_ref, v4_ref), (q16_ref, k16_ref, v16_ref))
    for gi, (_, dil) in enumerate(ATTN_GROUPS):
        qo, ko, vo = outs[gi]
        c0 = gi * GROUP_WIDTH
        emit(rotate(_dot(u, win_ref[:, OFF_Q + c0:OFF_Q + c0 + GROUP_WIDTH])), qo, dil)
        emit(rotate(_dot(u, win_ref[:, OFF_K + c0:OFF_K + c0 + GROUP_WIDTH])), ko, dil)
        emit(_dot(u, win_ref[:, OFF_V + c0:OFF_V + c0 + GROUP_WIDTH]), vo, dil)


def _proj(x, positions, ada, w_in, pool_w, pool_scale, w_pool_out):
    B, S, D = x.shape
    tm = PROJ_TILE
    nt = S // tm
    halo_blocks = tm // POOL_HALO
    const2 = lambda b, i: (0, 0)
    in_specs = [
        pl.BlockSpec((1, tm, D), lambda b, i: (b, i, 0)),
        pl.BlockSpec((1, POOL_HALO, D), lambda b, i: (b, jnp.maximum(i * halo_blocks - 1, 0), 0)),
        pl.BlockSpec((1, tm, 1), lambda b, i: (b, i, 0)),
        pl.BlockSpec((1, 6, D), lambda b, i: (b, 0, 0)),
        pl.BlockSpec((8, 128), const2),
        pl.BlockSpec((D, IN_WIDTH), const2),
        pl.BlockSpec((len(POOL_WINDOWS), POOL_GROUP, POOL_GROUP), lambda b, i: (0, 0, 0)),
        pl.BlockSpec((1, POOL_WIDTH), const2),
        pl.BlockSpec((POOL_WIDTH, D), const2),
    ]
    out_specs, out_shapes = [], []
    for _, dil in ATTN_GROUPS:
        for _ in range(3):
            out_specs.append(pl.BlockSpec((1, dil, tm // dil, GROUP_WIDTH), lambda b, i: (b, 0, i, 0)))
            out_shapes.append(jax.ShapeDtypeStruct((B, dil, S // dil, GROUP_WIDTH), BF16))
    for _ in range(2):
        out_specs.append(pl.BlockSpec((1, tm, D), lambda b, i: (b, i, 0)))
        out_shapes.append(jax.ShapeDtypeStruct((B, S, D), F32))
    return pl.pallas_call(
        _proj_kernel,
        grid=(B, nt),
        in_specs=in_specs,
        out_specs=out_specs,
        out_shape=out_shapes,
        scratch_shapes=[pltpu.VMEM((tm + POOL_HALO, POOL_WIDTH), F32),
                        pltpu.VMEM((GROUP_WIDTH // LANES, tm, LANES), F32)],
        compiler_params=pltpu.CompilerParams(
            dimension_semantics=("parallel", "parallel"), vmem_limit_bytes=VMEM_LIMIT_BYTES),
        name="proj",
    )(x, x, positions.reshape(B, S, 1), ada, _rope_tables(), w_in.astype(BF16),
      pool_w.astype(BF16), pool_scale.reshape(1, POOL_WIDTH), w_pool_out.astype(BF16))


def _attn_kernel(q_ref, k_ref, v_ref, kh_ref, vh_ref, o_ref, lse_ref, kf_ref, vf_ref):
    qb = q_ref.shape[2]
    n = pl.program_id(2)
    kf_ref[0:ATTN_BLOCK, :] = kh_ref[0, 0]
    kf_ref[ATTN_BLOCK:, :] = k_ref[0, 0]
    vf_ref[0:ATTN_BLOCK, :] = vh_ref[0, 0]
    vf_ref[ATTN_BLOCK:, :] = v_ref[0, 0]
    qi = lax.broadcasted_iota(jnp.int32, (ATTN_BLOCK, 2 * ATTN_BLOCK), 0)
    kj = lax.broadcasted_iota(jnp.int32, (ATTN_BLOCK, 2 * ATTN_BLOCK), 1)
    band = (kj >= qi) & (kj <= qi + ATTN_BLOCK)
    band_bias = jnp.where(band, 0.0, NEG_INF)
    lane = lax.broadcasted_iota(jnp.int32, (ATTN_BLOCK, GROUP_WIDTH), 1)
    low_lanes = lax.broadcasted_iota(jnp.int32, (ATTN_BLOCK, LANES), 1) < HEAD_DIM
    ones = jnp.ones((2 * ATTN_BLOCK, LANES), BF16)
    nh, blk = HEADS_PER_GROUP, ATTN_BLOCK

    def block(j, carry):
        r0 = pl.multiple_of(j * ATTN_BLOCK, ATTN_BLOCK)
        first_key = jnp.where((n > 0) | (j > 0), 0, ATTN_BLOCK)
        bias = band_bias + jnp.where(kj < first_key, NEG_INF, 0.0)
        q = q_ref[0, 0, pl.ds(r0, blk), :].astype(F32)
        kk = kf_ref[pl.ds(r0, 2 * blk), :]
        vv = vf_ref[pl.ds(r0, 2 * blk), :]
        qs = jnp.concatenate([jnp.where((lane >= h * HEAD_DIM) & (lane < (h + 1) * HEAD_DIM), q, 0.0)
                              for h in range(nh)], axis=0).astype(BF16)
        s = lax.dot_general(qs, kk, (((1,), (1,)), ((), ())), preferred_element_type=F32)
        s = jnp.concatenate([s[h * blk:(h + 1) * blk] * (HEAD_DIM ** -0.5) + bias for h in range(nh)], axis=0)
        m = jnp.max(s, axis=-1, keepdims=True)
        p = jnp.exp(s - m).astype(BF16)
        den = _dot(p, ones)
        lse = m + jnp.log(den)
        for hp in range(GROUP_WIDTH // LANES):
            rows = slice(2 * hp * blk, (2 * hp + 2) * blk)
            ls = slice(hp * LANES, (hp + 1) * LANES)
            o2 = _dot(p[rows], vv[:, ls]) / den[rows]
            l2 = lse[rows]
            o_ref[0, 0, pl.ds(r0, blk), ls] = jnp.where(low_lanes, o2[0:blk], o2[blk:2 * blk])
            lse_ref[0, 0, pl.ds(r0, blk), ls] = jnp.where(low_lanes, l2[0:blk], l2[blk:2 * blk])
        return carry

    lax.fori_loop(0, qb // ATTN_BLOCK, block, 0, unroll=4)


def _attention(q, k, v):
    B, dil, L, W = q.shape
    qb = min(L, ATTN_QROWS)
    per = qb // ATTN_BLOCK
    main = pl.BlockSpec((1, 1, qb, W), lambda b, r, n: (b, r, n, 0))
    halo = pl.BlockSpec((1, 1, ATTN_BLOCK, W), lambda b, r, n: (b, r, jnp.maximum(n * per - 1, 0), 0))
    return pl.pallas_call(
        _attn_kernel,
        grid=(B, dil, L // qb),
        in_specs=[main, main, main, halo, halo],
        out_specs=[main, main],
        out_shape=[jax.ShapeDtypeStruct((B, dil, L, W), F32)] * 2,
        scratch_shapes=[pltpu.VMEM((qb + ATTN_BLOCK, W), BF16)] * 2,
        compiler_params=pltpu.CompilerParams(
            dimension_semantics=("parallel", "parallel", "parallel"), vmem_limit_bytes=VMEM_LIMIT_BYTES),
        name=f"attn_d{dil}",
    )(q, k, v, k, v)


def _post_kernel(o1_ref, l1_ref, o4_ref, l4_ref, o16_ref, l16_ref, pg_ref, sga_ref, x_ref, ada_ref,
                 wao_ref, wo_ref, g1_ref, b1_ref, wrt_ref, brt_ref,
                 x1_ref, u2_ref, slot_ref, prob_ref, cnt_ref,
                 s0, s1, s2, s3):
    tm = x_ref.shape[1]

    def token_major(src_ref, scr_ref, dil):
        if dil == 1:
            return src_ref[0, 0]
        for r in range(dil):
            for c in range(GROUP_WIDTH // LANES):
                scr_ref[c, pl.ds(r, tm // dil, stride=dil), :] = src_ref[0, r, :, c * LANES:(c + 1) * LANES]
        return jnp.concatenate([scr_ref[c] for c in range(GROUP_WIDTH // LANES)], axis=1)

    o1, l1 = o1_ref[0, 0], l1_ref[0, 0]
    o4, l4 = token_major(o4_ref, s0, 4), token_major(l4_ref, s1, 4)
    o16, l16 = token_major(o16_ref, s2, 16), token_major(l16_ref, s3, 16)
    mx = jnp.maximum(jnp.maximum(l1, l4), l16)
    e1, e4, e16 = jnp.exp(l1 - mx), jnp.exp(l4 - mx), jnp.exp(l16 - mx)
    attn = (e1 * o1 + e4 * o4 + e16 * o16) / (e1 + e4 + e16)

    merged = pg_ref[0] + sga_ref[0] * _dot(attn.astype(BF16), wao_ref[...])
    mix = _dot(merged.astype(BF16), wo_ref[...])
    gate1 = ada_ref[0, 2:3, :]
    shift2 = ada_ref[0, 3:4, :]
    scale2 = ada_ref[0, 4:5, :]
    x1 = _layer_norm(DN_ALPHA * x_ref[0] + (1.0 + gate1) * mix) * g1_ref[...] + b1_ref[...]
    x1_ref[0] = x1
    u2 = _layer_norm(x1) * (1.0 + scale2) + shift2
    u2_hi = u2.astype(BF16)
    u2_ref[0] = u2_hi

    u2_lo = (u2 - u2_hi.astype(F32)).astype(BF16)
    nt_dot = lambda a, b: lax.dot_general(a, b, (((1,), (1,)), ((), ())), preferred_element_type=F32)
    logits = (nt_dot(wrt_ref[0], u2_hi) + nt_dot(wrt_ref[0], u2_lo) + nt_dot(wrt_ref[1], u2_hi)
              + brt_ref[...])
    eidx = lax.broadcasted_iota(jnp.int32, (N_EXPERTS, tm), 0)
    work = logits
    vals, idxs = [], []
    for _ in range(TOP_K):
        m = jnp.max(work, axis=0, keepdims=True)
        idx = jnp.min(jnp.where(work == m, eidx, N_EXPERTS), axis=0, keepdims=True)
        vals.append(m)
        idxs.append(idx)
        work = jnp.where(eidx == idx, -jnp.inf, work)
    exps = [jnp.exp(vk - vals[0]) for vk in vals]
    tot = exps[0] + exps[1] + exps[2] + exps[3]
    sel = jnp.zeros((N_EXPERTS, tm), F32)
    for idx in idxs:
        sel = sel + (eidx == idx).astype(F32)
    tr = lax.broadcasted_iota(jnp.int32, (tm, tm), 0)
    tc = lax.broadcasted_iota(jnp.int32, (tm, tm), 1)
    rank = _dot(sel.astype(BF16), (tr < tc).astype(BF16))
    cnt = jnp.sum(sel, axis=1, keepdims=True)
    run = jnp.floor((cnt + (ROW_ALIGN - 1)) * (1.0 / ROW_ALIGN)) * ROW_ALIGN
    er = lax.broadcasted_iota(jnp.int32, (N_EXPERTS, N_EXPERTS), 0)
    ec = lax.broadcasted_iota(jnp.int32, (N_EXPERTS, N_EXPERTS), 1)
    run_start = _dot((ec < er).astype(BF16),
                     jnp.broadcast_to(run, (N_EXPERTS, LANES)).astype(BF16))[:, 0:1]
    slot = rank + run_start
    for k in range(TOP_K):
        slot_ref[0, k:k + 1, :] = jnp.sum(jnp.where(eidx == idxs[k], slot, 0.0), axis=0,
                                          keepdims=True).astype(jnp.int32)
        prob_ref[0, k:k + 1, :] = exps[k] / tot
    cnt_ref[0] = cnt.astype(jnp.int32)


def _post(attn_outs, pg, sga, x, ada, w_attn_out, w_o, ln1_g, ln1_b, w_router, b_router):
    B, S, D = x.shape
    tm = POST_TILE
    nt = S // tm
    N = B * S
    const2 = lambda b, i: (0, 0)
    in_specs, args = [], []
    for (o, lse), (_, dil) in zip(attn_outs, ATTN_GROUPS):
        spec = pl.BlockSpec((1, dil, tm // dil, GROUP_WIDTH), lambda b, i: (b, 0, i, 0))
        in_specs += [spec, spec]
        args += [o, lse]
    tok_spec = pl.BlockSpec((1, tm, D), lambda b, i: (b, i, 0))
    in_specs += [tok_spec, tok_spec, tok_spec,
                 pl.BlockSpec((1, 6, D), lambda b, i: (b, 0, 0)),
                 pl.BlockSpec((GROUP_WIDTH, D), const2),
                 pl.BlockSpec((D, D), const2),
                 pl.BlockSpec((1, D), const2),
                 pl.BlockSpec((1, D), const2),
                 pl.BlockSpec((2, N_EXPERTS, D), lambda b, i: (0, 0, 0)),
                 pl.BlockSpec((N_EXPERTS, 1), const2)]
    wr_hi = w_router.T.astype(BF16)
    wr_lo = (w_router.T - wr_hi.astype(F32)).astype(BF16)
    args += [pg, sga, x, ada, w_attn_out.astype(BF16), w_o.astype(BF16),
             ln1_g.reshape(1, D), ln1_b.reshape(1, D), jnp.stack([wr_hi, wr_lo]),
             b_router.reshape(N_EXPERTS, 1)]
    nc = N // tm
    route_spec = pl.BlockSpec((1, TOP_K, tm), lambda b, i: (b * nt + i, 0, 0))
    out_specs = [tok_spec, tok_spec, route_spec, route_spec,
                 pl.BlockSpec((1, N_EXPERTS, 1), lambda b, i: (b * nt + i, 0, 0))]
    out_shapes = [jax.ShapeDtypeStruct((B, S, D), F32), jax.ShapeDtypeStruct((B, S, D), BF16),
                  jax.ShapeDtypeStruct((nc, TOP_K, tm), jnp.int32), jax.ShapeDtypeStruct((nc, TOP_K, tm), F32),
                  jax.ShapeDtypeStruct((nc, N_EXPERTS, 1), jnp.int32)]
    return pl.pallas_call(
        _post_kernel,
        grid=(B, nt),
        in_specs=in_specs,
        out_specs=out_specs,
        out_shape=out_shapes,
        scratch_shapes=[pltpu.VMEM((GROUP_WIDTH // LANES, tm, LANES), F32)] * 4,
        compiler_params=pltpu.CompilerParams(
            dimension_semantics=("parallel", "parallel"), vmem_limit_bytes=VMEM_LIMIT_BYTES),
        name="post",
    )(*args)


def _for_each_piece(tile, run_ref, dest_ref, len_ref, fn):
    def per_expert(e, total):
        idx = tile * N_EXPERTS + e
        loc, dst, length = run_ref[idx], dest_ref[idx], len_ref[idx]
        n_big = length // PIECE_ROWS
        rest = n_big * PIECE_ROWS

        def big(p, c):
            off = p * PIECE_ROWS
            fn(pl.multiple_of(loc + off, ROW_ALIGN), pl.multiple_of(dst + off, ROW_ALIGN), PIECE_ROWS)
            return c

        def small(p, c):
            off = rest + p * ROW_ALIGN
            fn(pl.multiple_of(loc + off, ROW_ALIGN), pl.multiple_of(dst + off, ROW_ALIGN), ROW_ALIGN)
            return c

        lax.fori_loop(0, n_big, big, 0)
        lax.fori_loop(0, (length - rest) // ROW_ALIGN, small, 0)
        return total + length

    return lax.fori_loop(0, N_EXPERTS, per_expert, 0)


def _wait_rows(src_ref, dst_ref, rows, sem):
    @pl.when(rows > 0)
    def _():
        n = pl.multiple_of(rows, ROW_ALIGN)
        pltpu.make_async_copy(src_ref.at[pl.ds(0, n)], dst_ref.at[pl.ds(0, n)], sem).wait()


def _dispatch_kernel(run_ref, dest_ref, len_ref, tail_ref, ntail_ref, nused_ref,
                     slot_ref, prob_ref, u2_ref, xs_hbm, local2_ref, zero_ref, sem2, zsem, rows_ref):
    tile = pl.program_id(0)
    buf = tile % 2
    local_ref = local2_ref.at[buf]
    sem = sem2.at[buf]
    tt, d = u2_ref.shape
    u2 = u2_ref[...]
    slots = [slot_ref[0, k:k + 1, :] for k in range(TOP_K)]
    probs = [prob_ref[0, k:k + 1, :] for k in range(TOP_K)]
    lane = lax.broadcasted_iota(jnp.int32, (SORT_CHUNK, LANES), 1)
    for r0 in range(0, LOCAL_ROWS, SORT_CHUNK):
        row = r0 + lax.broadcasted_iota(jnp.int32, (SORT_CHUNK, tt), 0)
        w = jnp.zeros((SORT_CHUNK, tt), F32)
        for k in range(TOP_K):
            w = jnp.where(row == slots[k], probs[k], w)
        onehot = jnp.where(w != 0.0, 1.0, 0.0).astype(BF16)
        local_ref[r0:r0 + SORT_CHUNK, 0:d] = _dot(onehot, u2).astype(BF16)
        wr = jnp.sum(w, axis=1, keepdims=True)
        hi = wr.astype(BF16).astype(F32)
        mid = (wr - hi).astype(BF16).astype(F32)
        lo = wr - hi - mid
        parts = jnp.where(lane == 0, hi, jnp.where(lane == 1, mid, jnp.where(lane == 2, lo, 0.0)))
        local_ref[r0:r0 + SORT_CHUNK, d:d + LANES] = parts.astype(BF16)

    def start_piece(loc, dst, rows):
        pltpu.make_async_copy(local_ref.at[pl.ds(loc, rows)], xs_hbm.at[pl.ds(dst, rows)], sem).start()

    rows_ref[buf] = _for_each_piece(tile, run_ref, dest_ref, len_ref, start_piece)

    @pl.when(tile > 0)
    def _():
        _wait_rows(local2_ref.at[1 - buf], xs_hbm, rows_ref[1 - buf], sem2.at[1 - buf])

    @pl.when(tile == pl.num_programs(0) - 1)
    def _():
        _wait_rows(local_ref, xs_hbm, rows_ref[buf], sem)
        zero_ref[...] = jnp.zeros_like(zero_ref)

        def tail_piece(e, p):
            dst = pl.multiple_of(tail_ref[e] + p * ROW_ALIGN, ROW_ALIGN)
            return pltpu.make_async_copy(zero_ref, xs_hbm.at[pl.ds(dst, ROW_ALIGN)], zsem)

        def start_tail(e, carry):
            return lax.fori_loop(0, ntail_ref[e], lambda p, c: (tail_piece(e, p).start(), c)[1], carry)

        def wait_tail(e, carry):
            return lax.fori_loop(0, ntail_ref[e], lambda p, c: (tail_piece(e, p).wait(), c)[1], carry)

        lax.fori_loop(0, N_EXPERTS, start_tail, 0)
        lax.fori_loop(0, N_EXPERTS, wait_tail, 0)

        def spare_piece(p):
            return pltpu.make_async_copy(zero_ref, xs_hbm.at[pl.ds(pl.multiple_of(p * ROW_ALIGN, ROW_ALIGN),
                                                                   ROW_ALIGN)], zsem)

        first_spare = nused_ref[0] * (MOE_BLOCK // ROW_ALIGN)
        end_spare = xs_hbm.shape[0] // ROW_ALIGN
        lax.fori_loop(first_spare, end_spare, lambda p, c: (spare_piece(p).start(), c)[1], 0)
        lax.fori_loop(first_spare, end_spare, lambda p, c: (spare_piece(p).wait(), c)[1], 0)


def _dispatch(tables, slot, prob, u2, n_rows):
    N, D = u2.shape
    tt = POST_TILE
    route_spec = pl.BlockSpec((1, TOP_K, tt), lambda i, *t: (i, 0, 0))
    grid_spec = pltpu.PrefetchScalarGridSpec(
        num_scalar_prefetch=len(tables),
        grid=(N // tt,),
        in_specs=[route_spec, route_spec, pl.BlockSpec((tt, D), lambda i, *t: (i, 0))],
        out_specs=pl.BlockSpec(memory_space=pl.ANY),
        scratch_shapes=[pltpu.VMEM((2, LOCAL_ROWS, D + LANES), BF16), pltpu.VMEM((ROW_ALIGN, D + LANES), BF16),
                        pltpu.SemaphoreType.DMA((2,)), pltpu.SemaphoreType.DMA(()),
                        pltpu.SMEM((2,), jnp.int32)],
    )
    return pl.pallas_call(
        _dispatch_kernel,
        grid_spec=grid_spec,
        out_shape=jax.ShapeDtypeStruct((n_rows, D + LANES), BF16),
        compiler_params=pltpu.CompilerParams(
            dimension_semantics=("arbitrary",), vmem_limit_bytes=VMEM_LIMIT_BYTES),
        name="dispatch",
    )(*tables, slot, prob, u2)


def _expert_kernel(be_ref, next_ref, nused_ref, xs_ref, wg_hbm, bg_ref, wu_hbm, bu_ref, wd_hbm, bd_ref,
                   ys_ref, stage, wg_s, wu_s, wd_s, sem):
    i = pl.program_id(0)
    used = i < nused_ref[0]
    prev = be_ref[jnp.maximum(i - 1, 0)]
    fresh = (i == 0) | (be_ref[i] != prev)

    def fetch(e):
        return [pltpu.make_async_copy(w_hbm.at[e], stage.at[j], sem.at[j])
                for j, w_hbm in enumerate((wg_hbm, wu_hbm, wd_hbm))]

    @pl.when(i == 0)
    def _():
        for copy in fetch(be_ref[0]):
            copy.start()

    @pl.when(used & fresh)
    def _():
        for copy in fetch(be_ref[i]):
            copy.wait()
        wg_s[...] = stage[0].astype(BF16)
        wu_s[...] = stage[1].astype(BF16)
        wd_s[...] = stage[2].astype(BF16)

        @pl.when(next_ref[i] >= 0)
        def _():
            for copy in fetch(next_ref[i]):
                copy.start()

    @pl.when(used)
    def _():
        d = ys_ref.shape[1]
        xb = xs_ref[:, 0:d]
        parts = xs_ref[:, d:d + LANES].astype(F32)
        weight = parts[:, 0:1] + parts[:, 1:2] + parts[:, 2:3]
        g = _dot(xb, wg_s[...]) + bg_ref[0]
        up = _dot(xb, wu_s[...]) + bu_ref[0]
        g = jnp.minimum(g, SWIGLU_LIMIT)
        up = jnp.clip(up, -SWIGLU_LIMIT, SWIGLU_LIMIT)
        h = g * jax.nn.sigmoid(SWIGLU_ALPHA * g) * (up + 1.0)
        ys_ref[...] = ((_dot(h.astype(BF16), wd_s[...]) + bd_ref[0]) * weight).astype(BF16)

    @pl.when(jnp.logical_not(used))
    def _():
        ys_ref[...] = jnp.zeros_like(ys_ref)


def _experts(be, next_expert, nused, xs, w_gate, b_gate, w_up, b_up, w_down, b_down):
    P, width = xs.shape
    D = width - LANES
    E = w_gate.shape[0]

    def live(i, nu):
        return jnp.maximum(jnp.minimum(i, nu[0] - 1), 0)

    w_spec = pl.BlockSpec(memory_space=pl.ANY)
    b_spec = pl.BlockSpec((1, 1, D), lambda i, be, nx, nu: (be[live(i, nu)], 0, 0))
    grid_spec = pltpu.PrefetchScalarGridSpec(
        num_scalar_prefetch=3,
        grid=(P // MOE_BLOCK,),
        in_specs=[pl.BlockSpec((MOE_BLOCK, width), lambda i, be, nx, nu: (live(i, nu), 0)),
                  w_spec, b_spec, w_spec, b_spec, w_spec, b_spec],
        out_specs=pl.BlockSpec((MOE_BLOCK, D), lambda i, be, nx, nu: (i, 0)),
        scratch_shapes=[pltpu.VMEM((3, D, D), F32)] + [pltpu.VMEM((D, D), BF16)] * 3
                       + [pltpu.SemaphoreType.DMA((3,))],
    )
    return pl.pallas_call(
        _expert_kernel,
        grid_spec=grid_spec,
        out_shape=jax.ShapeDtypeStruct((P, D), BF16),
        compiler_params=pltpu.CompilerParams(
            dimension_semantics=("arbitrary",), vmem_limit_bytes=VMEM_LIMIT_BYTES),
        name="experts",
    )(be, next_expert, nused, xs, w_gate, b_gate.reshape(E, 1, D), w_up, b_up.reshape(E, 1, D),
      w_down, b_down.reshape(E, 1, D))


def _combine_kernel(run_ref, dest_ref, len_ref,
                    slot_ref, x1_ref, ada_ref, g2_ref, b2_ref, ys_hbm,
                    out_ref, local2_ref, sem2, rows_ref):
    tile = pl.program_id(0)
    buf = tile % 2
    tt = x1_ref.shape[0]

    def fetch(t, b):
        def start_piece(loc, dst, rows):
            pltpu.make_async_copy(ys_hbm.at[pl.ds(dst, rows)], local2_ref.at[b, pl.ds(loc, rows)],
                                  sem2.at[b]).start()
        rows_ref[b] = _for_each_piece(t, run_ref, dest_ref, len_ref, start_piece)

    @pl.when(tile == 0)
    def _():
        local2_ref[...] = jnp.zeros_like(local2_ref)
        fetch(0, 0)

    @pl.when(tile + 1 < pl.num_programs(0))
    def _():
        fetch(tile + 1, 1 - buf)

    local_ref = local2_ref.at[buf]
    _wait_rows(ys_hbm, local_ref, rows_ref[buf], sem2.at[buf])
    slots = [slot_ref[:, k:k + 1] for k in range(TOP_K)]

    ffn = jnp.zeros((tt, x1_ref.shape[1]), F32)
    for r0 in range(0, LOCAL_ROWS, SORT_CHUNK):
        col = r0 + lax.broadcasted_iota(jnp.int32, (tt, SORT_CHUNK), 1)
        onehot = jnp.zeros((tt, SORT_CHUNK), F32)
        for k in range(TOP_K):
            onehot = jnp.where(col == slots[k], 1.0, onehot)
        ffn = ffn + _dot(onehot.astype(BF16), local_ref[r0:r0 + SORT_CHUNK, :])
    gate2 = ada_ref[0, 5:6, :]
    y = DN_ALPHA * x1_ref[...] + (1.0 + gate2) * ffn
    out_ref[...] = _layer_norm(y) * g2_ref[...] + b2_ref[...]


def _combine(tables, slot_tok, x1, ada, ln2_g, ln2_b, ys, tiles_per_batch):
    N, D = x1.shape
    tt = POST_TILE
    const = lambda i, *t: (0, 0)
    tok4 = pl.BlockSpec((tt, TOP_K), lambda i, *t: (i, 0))
    grid_spec = pltpu.PrefetchScalarGridSpec(
        num_scalar_prefetch=len(tables),
        grid=(N // tt,),
        in_specs=[tok4,
                  pl.BlockSpec((tt, D), lambda i, *t: (i, 0)),
                  pl.BlockSpec((1, 6, D), lambda i, *t: (i // tiles_per_batch, 0, 0)),
                  pl.BlockSpec((1, D), const),
                  pl.BlockSpec((1, D), const),
                  pl.BlockSpec(memory_space=pl.ANY)],
        out_specs=pl.BlockSpec((tt, D), lambda i, *t: (i, 0)),
        scratch_shapes=[pltpu.VMEM((2, LOCAL_ROWS, D), BF16), pltpu.SemaphoreType.DMA((2,)),
                        pltpu.SMEM((2,), jnp.int32)],
    )
    return pl.pallas_call(
        _combine_kernel,
        grid_spec=grid_spec,
        out_shape=jax.ShapeDtypeStruct((N, D), F32),
        compiler_params=pltpu.CompilerParams(
            dimension_semantics=("arbitrary",), vmem_limit_bytes=VMEM_LIMIT_BYTES),
        name="combine",
    )(*tables, slot_tok, x1, ada, ln2_g.reshape(1, D), ln2_b.reshape(1, D), ys)


def _routing_tables(cnt, nblk):
    nt, E = cnt.shape
    i32 = jnp.int32
    run = (cnt + ROW_ALIGN - 1) // ROW_ALIGN * ROW_ALIGN
    run_start = jnp.cumsum(run, axis=1) - run
    seg_len = jnp.sum(run, axis=0)
    seg_blocks = (seg_len + MOE_BLOCK - 1) // MOE_BLOCK
    b_end = jnp.cumsum(seg_blocks)
    seg_off = (b_end - seg_blocks) * MOE_BLOCK
    dest = seg_off[None, :] + jnp.cumsum(run, axis=0) - run
    nused = b_end[-1:].astype(i32)
    i = jnp.arange(nblk, dtype=i32)
    be = jnp.minimum(jnp.sum((i[:, None] >= b_end[None, :]).astype(i32), axis=1), E - 1)
    e_ids = jnp.arange(E, dtype=i32)
    later = (e_ids[None, :] > e_ids[:, None]) & (seg_blocks > 0)[None, :]
    next_e = jnp.min(jnp.where(later, e_ids[None, :], E), axis=1)
    next_e = jnp.where(next_e < E, next_e, -1)
    nxt = jnp.sum(jnp.where(be[:, None] == e_ids[None, :], next_e[None, :], 0), axis=1)
    tail = seg_off + seg_len
    ntail = (seg_blocks * MOE_BLOCK - seg_len) // ROW_ALIGN
    piece_tables = (run_start.reshape(-1).astype(i32), dest.reshape(-1).astype(i32),
                    run.reshape(-1).astype(i32))
    return piece_tables, (tail.astype(i32), ntail.astype(i32)), (be.astype(i32), nxt.astype(i32), nused)


def kernel(x, c, positions, w_ada, b_ada, w_in, pool_w, pool_scale, w_pool_out, w_attn_out, w_o,
           ln1_g, ln1_b, w_router, b_router, w_gate, b_gate, w_up, b_up, w_down, b_down, ln2_g, ln2_b):
    B, S, D = x.shape
    N = B * S
    assert D == D_MODEL and S % PROJ_TILE == 0 and S % (16 * ATTN_BLOCK) == 0
    assert S % POST_TILE == 0 and LOCAL_ROWS % SORT_CHUNK == 0
    nt = N // POST_TILE
    nblk = (N * TOP_K + nt * N_EXPERTS * (ROW_ALIGN - 1)) // MOE_BLOCK + N_EXPERTS
    for l in range(DEPTH):
        ada = _ada(c, w_ada[l], b_ada[l])
        proj_out = _proj(x, positions, ada, w_in[l], pool_w[l], pool_scale[l], w_pool_out[l])
        qkv, (pg, sga) = proj_out[:9], proj_out[9:]
        attn_outs = [_attention(*qkv[3 * g:3 * g + 3]) for g in range(len(ATTN_GROUPS))]
        x1, u2, slot, prob, cnt = _post(attn_outs, pg, sga, x, ada, w_attn_out[l], w_o[l],
                                        ln1_g[l], ln1_b[l], w_router[l], b_router[l])
        piece_tables, tail_tables, block_tables = _routing_tables(cnt.reshape(nt, N_EXPERTS), nblk)
        xs = _dispatch(piece_tables + tail_tables + block_tables[2:], slot, prob, u2.reshape(N, D),
                       nblk * MOE_BLOCK)
        ys = _experts(*block_tables, xs, w_gate[l], b_gate[l], w_up[l], b_up[l], w_down[l], b_down[l])
        slot_tok = slot.transpose(0, 2, 1).reshape(N, TOP_K)
        out = _combine(piece_tables, slot_tok, x1.reshape(N, D), ada,
                       ln2_g[l], ln2_b[l], ys, S // POST_TILE)
        x = out.reshape(B, S, D)
    return x
```

```python
import functools

import jax
import jax.numpy as jnp
import numpy as np
from jax import lax
from jax.experimental import pallas as pl
from jax.experimental.pallas import tpu as pltpu

F32 = jnp.float32
BF16 = jnp.bfloat16

D_MODEL = 1024
POOL_WINDOWS = (2, 4, 8, 16)
POOL_WIDTH = D_MODEL // 2
POOL_GROUP = POOL_WIDTH // len(POOL_WINDOWS)
POOL_HALO = 16
HEAD_DIM = 64
ATTN_GROUPS = ((128, 1), (512, 4), (2048, 16))
HEADS_PER_GROUP = 4
GROUP_WIDTH = HEADS_PER_GROUP * HEAD_DIM
N_HEADS = HEADS_PER_GROUP * len(ATTN_GROUPS)
ATTN_WIDTH = N_HEADS * HEAD_DIM
ATTN_BLOCK = 128
ROT_DIM = HEAD_DIM // 4
ROPE_THETA = 500000.0
N_EXPERTS = 32
TOP_K = 4
SWIGLU_ALPHA = 1.702
SWIGLU_LIMIT = 7.0
MOE_BLOCK = 512
DEPTH = 1
DN_ALPHA = (2.0 * DEPTH) ** 0.25
LN_EPS = 1e-5
NEG_INF = -1e30

OFF_Q = POOL_WIDTH
OFF_K = OFF_Q + ATTN_WIDTH
OFF_V = OFF_K + ATTN_WIDTH
OFF_GP = OFF_V + ATTN_WIDTH
OFF_GA = OFF_GP + D_MODEL
IN_WIDTH = OFF_GA + D_MODEL

VMEM_LIMIT_BYTES = 56 * 1024 * 1024
LANES = 128

PROJ_TILE = 512
POST_TILE = 512
ATTN_QROWS = 1024
ROW_ALIGN = 16
PIECE_ROWS = 64
SORT_CHUNK = 512
LOCAL_ROWS = -(-(POST_TILE * TOP_K + N_EXPERTS * (ROW_ALIGN - 1)) // SORT_CHUNK) * SORT_CHUNK


def _layer_norm(x):
    mu = jnp.mean(x, axis=-1, keepdims=True)
    xc = x - mu
    var = jnp.mean(xc * xc, axis=-1, keepdims=True)
    return xc * lax.rsqrt(var + LN_EPS)


def _dot(a, b):
    return jnp.dot(a, b, preferred_element_type=F32)


def _ada_kernel(c_ref, w_ref, b_ref, o_ref):
    c = c_ref[...]
    s = c * jax.nn.sigmoid(c)
    o_ref[...] = jnp.dot(s, w_ref[...], preferred_element_type=F32,
                         precision=lax.Precision.HIGHEST) + b_ref[...]


def _ada(c, w_ada, b_ada):
    B, D = c.shape
    rows = 8
    c_pad = jnp.pad(c, ((0, rows - B), (0, 0)))
    n_out = w_ada.shape[1]
    out = pl.pallas_call(
        _ada_kernel,
        grid=(n_out // D,),
        in_specs=[pl.BlockSpec((rows, D), lambda j: (0, 0)),
                  pl.BlockSpec((D, D), lambda j: (0, j)),
                  pl.BlockSpec((1, D), lambda j: (0, j))],
        out_specs=pl.BlockSpec((rows, D), lambda j: (0, j)),
        out_shape=jax.ShapeDtypeStruct((rows, n_out), F32),
        name="ada",
    )(c_pad, w_ada, b_ada.reshape(1, n_out))
    return out[:B].reshape(B, 6, D)


ROPE_PART_ROWS = 32


def _rope_tables():
    lane = np.arange(LANES)
    li = lane % HEAD_DIM
    half = ROT_DIM // 2
    inv_freq = jnp.power(ROPE_THETA, -jnp.arange(half, dtype=F32) * (2.0 / ROT_DIM))
    invf = jnp.broadcast_to(inv_freq[:, None], (half, LANES))
    freq = np.arange(ROPE_PART_ROWS)[:, None]
    live = (freq < 3 * half) & (freq % half == (li % half)[None, :])
    place = np.stack([live & (li < ROT_DIM)[None, :],
                      -1.0 * (live & (li < half)[None, :]),
                      live & ((li >= half) & (li < ROT_DIM))[None, :]]).astype(np.float32)
    keep = (li >= ROT_DIM).astype(np.float32)[None, :]
    return invf, jnp.asarray(place, BF16), jnp.asarray(keep)


def _proj_kernel(x_ref, xh_ref, pos_ref, ada_ref, invf_ref, place_ref, keep_ref,
                 win_ref, poolw_ref, pscale_ref, wpo_ref,
                 q1_ref, k1_ref, v1_ref, q4_ref, k4_ref, v4_ref, q16_ref, k16_ref, v16_ref,
                 pg_ref, sga_ref, xpe_ref, cls_ref):
    tm = x_ref.shape[1]
    i = pl.program_id(1)
    shift1 = ada_ref[0, 0:1, :]
    scale1 = ada_ref[0, 1:2, :]

    def modulated(xv):
        return (_layer_norm(xv) * (1.0 + scale1) + shift1).astype(BF16)

    u = modulated(x_ref[0])
    uh = modulated(xh_ref[0])

    xp = _dot(u, win_ref[:, 0:POOL_WIDTH])
    xph = _dot(uh, win_ref[:, 0:POOL_WIDTH])
    xph = jnp.where(i > 0, xph, 0.0)
    xpe_ref[0:POOL_HALO, :] = xph
    xpe_ref[POOL_HALO:, :] = xp
    tok = i * tm + lax.broadcasted_iota(jnp.int32, (tm, 1), 0)
    ys = []
    for g, w in enumerate(POOL_WINDOWS):
        cols = slice(g * POOL_GROUP, (g + 1) * POOL_GROUP)
        xg = xp[:, cols]
        acc = xg
        for j in range(1, w):
            acc = acc + xpe_ref[POOL_HALO - j:POOL_HALO - j + tm, cols]
        cnt = jnp.minimum(tok + 1, w).astype(F32)
        mixed = (acc / cnt - xg).astype(BF16)
        ys.append(_dot(mixed, poolw_ref[g]) * pscale_ref[:, cols])
    y = jnp.concatenate(ys, axis=1).astype(BF16)
    pooled = _dot(y, wpo_ref[...])
    g_p = _dot(u, win_ref[:, OFF_GP:OFF_GP + D_MODEL])
    pg_ref[0] = jax.nn.sigmoid(g_p) * pooled
    g_a = _dot(u, win_ref[:, OFF_GA:OFF_GA + D_MODEL])
    sga_ref[0] = jax.nn.sigmoid(g_a)

    ang = invf_ref[:, 0:1] * pos_ref[0, 0].astype(F32)

    def on_lanes(table, j):
        hi = table.astype(BF16).astype(F32)
        mid = (table - hi).astype(BF16).astype(F32)
        lo = table - hi - mid
        pad = jnp.zeros((ROPE_PART_ROWS - 3 * table.shape[0], tm), F32)
        parts = jnp.concatenate([hi, mid, lo, pad], axis=0).astype(BF16)
        return lax.dot_general(parts, place_ref[j], (((0,), (0,)), ((), ())), preferred_element_type=F32)

    cos = jnp.cos(ang)
    sin = jnp.sin(ang)
    c_mul = on_lanes(cos, 0) + keep_ref[...]
    s_lo = on_lanes(sin, 1)
    s_hi = on_lanes(sin, 2)
    c_mul = jnp.concatenate([c_mul, c_mul], axis=1)
    s_lo = jnp.concatenate([s_lo, s_lo], axis=1)
    s_hi = jnp.concatenate([s_hi, s_hi], axis=1)
    half = ROT_DIM // 2

    def rotate(a):
        up = pltpu.roll(a, GROUP_WIDTH - half, axis=1)
        dn = pltpu.roll(a, half, axis=1)
        return a * c_mul + up * s_lo + dn * s_hi

    def emit(a, out_ref, dil):
        if dil == 1:
            out_ref[0, 0] = a.astype(BF16)
            return
        for c in range(GROUP_WIDTH // LANES):
            cls_ref[c] = a[:, c * LANES:(c + 1) * LANES]
        for r in range(dil):
            for c in range(GROUP_WIDTH // LANES):
                out_ref[0, r, :, c * LANES:(c + 1) * LANES] = (
                    cls_ref[c, pl.ds(r, tm // dil, stride=dil), :].astype(BF16))

    outs = ((q1_ref, k1_ref, v1_ref), (q4_ref, k4_ref, v4_ref), (q16_ref, k16_ref, v16_ref))
    for gi, (_, dil) in enumerate(ATTN_GROUPS):
        qo, ko, vo = outs[gi]
        c0 = gi * GROUP_WIDTH
        emit(rotate(_dot(u, win_ref[:, OFF_Q + c0:OFF_Q + c0 + GROUP_WIDTH])), qo, dil)
        emit(rotate(_dot(u, win_ref[:, OFF_K + c0:OFF_K + c0 + GROUP_WIDTH])), ko, dil)
        emit(_dot(u, win_ref[:, OFF_V + c0:OFF_V + c0 + GROUP_WIDTH]), vo, dil)


def _proj(x, positions, ada, w_in, pool_w, pool_scale, w_pool_out):
    B, S, D = x.shape
    tm = PROJ_TILE
    nt = S // tm
    halo_blocks = tm // POOL_HALO
    const2 = lambda b, i: (0, 0)
    in_specs = [
        pl.BlockSpec((1, tm, D), lambda b, i: (b, i, 0)),
        pl.BlockSpec((1, POOL_HALO, D), lambda b, i: (b, jnp.maximum(i * halo_blocks - 1, 0), 0)),
        pl.BlockSpec((1, 1, 1, tm), lambda b, i: (b, i, 0, 0)),
        pl.BlockSpec((1, 6, D), lambda b, i: (b, 0, 0)),
        pl.BlockSpec((ROT_DIM // 2, LANES), const2),
        pl.BlockSpec((3, ROPE_PART_ROWS, LANES), lambda b, i: (0, 0, 0)),
        pl.BlockSpec((1, LANES), const2),
        pl.BlockSpec((D, IN_WIDTH), const2),
        pl.BlockSpec((len(POOL_WINDOWS), POOL_GROUP, POOL_GROUP), lambda b, i: (0, 0, 0)),
        pl.BlockSpec((1, POOL_WIDTH), const2),
        pl.BlockSpec((POOL_WIDTH, D), const2),
    ]
    out_specs, out_shapes = [], []
    for _, dil in ATTN_GROUPS:
        for _ in range(3):
            out_specs.append(pl.BlockSpec((1, dil, tm // dil, GROUP_WIDTH), lambda b, i: (b, 0, i, 0)))
            out_shapes.append(jax.ShapeDtypeStruct((B, dil, S // dil, GROUP_WIDTH), BF16))
    for _ in range(2):
        out_specs.append(pl.BlockSpec((1, tm, D), lambda b, i: (b, i, 0)))
        out_shapes.append(jax.ShapeDtypeStruct((B, S, D), F32))
    return pl.pallas_call(
        _proj_kernel,
        grid=(B, nt),
        in_specs=in_specs,
        out_specs=out_specs,
        out_shape=out_shapes,
        scratch_shapes=[pltpu.VMEM((tm + POOL_HALO, POOL_WIDTH), F32),
                        pltpu.VMEM((GROUP_WIDTH // LANES, tm, LANES), F32)],
        compiler_params=pltpu.CompilerParams(
            dimension_semantics=("parallel", "parallel"), vmem_limit_bytes=VMEM_LIMIT_BYTES),
        name="proj",
    )(x, x, positions.reshape(B, nt, 1, tm), ada, *_rope_tables(), w_in.astype(BF16),
      pool_w.astype(BF16), pool_scale.reshape(1, POOL_WIDTH), w_pool_out.astype(BF16))


def _attn_kernel(q_ref, k_ref, v_ref, kh_ref, vh_ref, o_ref, lse_ref, kf_ref, vf_ref):
    qb = q_ref.shape[2]
    n = pl.program_id(2)
    kf_ref[0:ATTN_BLOCK, :] = kh_ref[0, 0]
    kf_ref[ATTN_BLOCK:, :] = k_ref[0, 0]
    vf_ref[0:ATTN_BLOCK, :] = vh_ref[0, 0]
    vf_ref[ATTN_BLOCK:, :] = v_ref[0, 0]
    qi = lax.broadcasted_iota(jnp.int32, (ATTN_BLOCK, 2 * ATTN_BLOCK), 0)
    kj = lax.broadcasted_iota(jnp.int32, (ATTN_BLOCK, 2 * ATTN_BLOCK), 1)
    band = (kj >= qi) & (kj <= qi + ATTN_BLOCK)
    band_bias = jnp.where(band, 0.0, NEG_INF)
    lane = lax.broadcasted_iota(jnp.int32, (ATTN_BLOCK, GROUP_WIDTH), 1)
    low_lanes = lax.broadcasted_iota(jnp.int32, (ATTN_BLOCK, LANES), 1) < HEAD_DIM
    ones = jnp.ones((2 * ATTN_BLOCK, LANES), BF16)
    nh, blk = HEADS_PER_GROUP, ATTN_BLOCK

    def block(j, carry):
        r0 = pl.multiple_of(j * ATTN_BLOCK, ATTN_BLOCK)
        first_key = jnp.where((n > 0) | (j > 0), 0, ATTN_BLOCK)
        bias = band_bias + jnp.where(kj < first_key, NEG_INF, 0.0)
        q = q_ref[0, 0, pl.ds(r0, blk), :].astype(F32)
        kk = kf_ref[pl.ds(r0, 2 * blk), :]
        vv = vf_ref[pl.ds(r0, 2 * blk), :]
        qs = jnp.concatenate([jnp.where((lane >= h * HEAD_DIM) & (lane < (h + 1) * HEAD_DIM), q, 0.0)
                              for h in range(nh)], axis=0).astype(BF16)
        s = lax.dot_general(qs, kk, (((1,), (1,)), ((), ())), preferred_element_type=F32)
        s = jnp.concatenate([s[h * blk:(h + 1) * blk] * (HEAD_DIM ** -0.5) + bias for h in range(nh)], axis=0)
        m = jnp.max(s, axis=-1, keepdims=True)
        p = jnp.exp(s - m).astype(BF16)
        den = _dot(p, ones)
        lse = m + jnp.log(den)
        for hp in range(GROUP_WIDTH // LANES):
            rows = slice(2 * hp * blk, (2 * hp + 2) * blk)
            ls = slice(hp * LANES, (hp + 1) * LANES)
            o2 = _dot(p[rows], vv[:, ls]) / den[rows]
            l2 = lse[rows]
            o_ref[0, 0, pl.ds(r0, blk), ls] = jnp.where(low_lanes, o2[0:blk], o2[blk:2 * blk])
            lse_ref[0, 0, pl.ds(r0, blk), ls] = jnp.where(low_lanes, l2[0:blk], l2[blk:2 * blk])
        return carry

    lax.fori_loop(0, qb // ATTN_BLOCK, block, 0, unroll=4)


def _attention(q, k, v):
    B, dil, L, W = q.shape
    qb = min(L, ATTN_QROWS)
    per = qb // ATTN_BLOCK
    main = pl.BlockSpec((1, 1, qb, W), lambda b, r, n: (b, r, n, 0))
    halo = pl.BlockSpec((1, 1, ATTN_BLOCK, W), lambda b, r, n: (b, r, jnp.maximum(n * per - 1, 0), 0))
    return pl.pallas_call(
        _attn_kernel,
        grid=(B, dil, L // qb),
        in_specs=[main, main, main, halo, halo],
        out_specs=[main, main],
        out_shape=[jax.ShapeDtypeStruct((B, dil, L, W), F32)] * 2,
        scratch_shapes=[pltpu.VMEM((qb + ATTN_BLOCK, W), BF16)] * 2,
        compiler_params=pltpu.CompilerParams(
            dimension_semantics=("parallel", "parallel", "parallel"), vmem_limit_bytes=VMEM_LIMIT_BYTES),
        name=f"attn_d{dil}",
    )(q, k, v, k, v)


def _post_kernel(o1_ref, l1_ref, o4_ref, l4_ref, o16_ref, l16_ref, pg_ref, sga_ref, x_ref, ada_ref,
                 wao_ref, wo_ref, g1_ref, b1_ref, wrt_ref, brt_ref,
                 x1_ref, u2_ref, slot_ref, prob_ref, cnt_ref,
                 s0, s1, s2, s3):
    tm = x_ref.shape[1]

    def token_major(src_ref, scr_ref, dil):
        if dil == 1:
            return src_ref[0, 0]
        for r in range(dil):
            for c in range(GROUP_WIDTH // LANES):
                scr_ref[c, pl.ds(r, tm // dil, stride=dil), :] = src_ref[0, r, :, c * LANES:(c + 1) * LANES]
        return jnp.concatenate([scr_ref[c] for c in range(GROUP_WIDTH // LANES)], axis=1)

    o1, l1 = o1_ref[0, 0], l1_ref[0, 0]
    o4, l4 = token_major(o4_ref, s0, 4), token_major(l4_ref, s1, 4)
    o16, l16 = token_major(o16_ref, s2, 16), token_major(l16_ref, s3, 16)
    mx = jnp.maximum(jnp.maximum(l1, l4), l16)
    e1, e4, e16 = jnp.exp(l1 - mx), jnp.exp(l4 - mx), jnp.exp(l16 - mx)
    attn = (e1 * o1 + e4 * o4 + e16 * o16) / (e1 + e4 + e16)

    merged = pg_ref[0] + sga_ref[0] * _dot(attn.astype(BF16), wao_ref[...])
    mix = _dot(merged.astype(BF16), wo_ref[...])
    gate1 = ada_ref[0, 2:3, :]
    shift2 = ada_ref[0, 3:4, :]
    scale2 = ada_ref[0, 4:5, :]
    x1 = _layer_norm(DN_ALPHA * x_ref[0] + (1.0 + gate1) * mix) * g1_ref[...] + b1_ref[...]
    x1_ref[0] = x1
    u2 = _layer_norm(x1) * (1.0 + scale2) + shift2
    u2_hi = u2.astype(BF16)
    u2_ref[0] = u2_hi

    u2_lo = (u2 - u2_hi.astype(F32)).astype(BF16)
    nt_dot = lambda a, b: lax.dot_general(a, b, (((1,), (1,)), ((), ())), preferred_element_type=F32)
    logits = (nt_dot(wrt_ref[0], u2_hi) + nt_dot(wrt_ref[0], u2_lo) + nt_dot(wrt_ref[1], u2_hi)
              + brt_ref[...])
    eidx = lax.broadcasted_iota(jnp.int32, (N_EXPERTS, tm), 0)
    work = logits
    vals, idxs = [], []
    for _ in range(TOP_K):
        m = jnp.max(work, axis=0, keepdims=True)
        idx = jnp.min(jnp.where(work == m, eidx, N_EXPERTS), axis=0, keepdims=True)
        vals.append(m)
        idxs.append(idx)
        work = jnp.where(eidx == idx, -jnp.inf, work)
    exps = [jnp.exp(vk - vals[0]) for vk in vals]
    tot = exps[0] + exps[1] + exps[2] + exps[3]
    sel = jnp.zeros((N_EXPERTS, tm), F32)
    for idx in idxs:
        sel = sel + (eidx == idx).astype(F32)
    tr = lax.broadcasted_iota(jnp.int32, (tm, tm), 0)
    tc = lax.broadcasted_iota(jnp.int32, (tm, tm), 1)
    rank = _dot(sel.astype(BF16), (tr < tc).astype(BF16))
    cnt = jnp.sum(sel, axis=1, keepdims=True)
    run = jnp.floor((cnt + (ROW_ALIGN - 1)) * (1.0 / ROW_ALIGN)) * ROW_ALIGN
    er = lax.broadcasted_iota(jnp.int32, (N_EXPERTS, N_EXPERTS), 0)
    ec = lax.broadcasted_iota(jnp.int32, (N_EXPERTS, N_EXPERTS), 1)
    run_start = _dot((ec < er).astype(BF16),
                     jnp.broadcast_to(run, (N_EXPERTS, LANES)).astype(BF16))[:, 0:1]
    slot = rank + run_start
    for k in range(TOP_K):
        slot_ref[0, k:k + 1, :] = jnp.sum(jnp.where(eidx == idxs[k], slot, 0.0), axis=0,
                                          keepdims=True).astype(jnp.int32)
        prob_ref[0, k:k + 1, :] = exps[k] / tot
    cnt_ref[0] = cnt.astype(jnp.int32)


def _post(attn_outs, pg, sga, x, ada, w_attn_out, w_o, ln1_g, ln1_b, w_router, b_router):
    B, S, D = x.shape
    tm = POST_TILE
    nt = S // tm
    N = B * S
    const2 = lambda b, i: (0, 0)
    in_specs, args = [], []
    for (o, lse), (_, dil) in zip(attn_outs, ATTN_GROUPS):
        spec = pl.BlockSpec((1, dil, tm // dil, GROUP_WIDTH), lambda b, i: (b, 0, i, 0))
        in_specs += [spec, spec]
        args += [o, lse]
    tok_spec = pl.BlockSpec((1, tm, D), lambda b, i: (b, i, 0))
    in_specs += [tok_spec, tok_spec, tok_spec,
                 pl.BlockSpec((1, 6, D), lambda b, i: (b, 0, 0)),
                 pl.BlockSpec((GROUP_WIDTH, D), const2),
                 pl.BlockSpec((D, D), const2),
                 pl.BlockSpec((1, D), const2),
                 pl.BlockSpec((1, D), const2),
                 pl.BlockSpec((2, N_EXPERTS, D), lambda b, i: (0, 0, 0)),
                 pl.BlockSpec((N_EXPERTS, 1), const2)]
    wr_hi = w_router.T.astype(BF16)
    wr_lo = (w_router.T - wr_hi.astype(F32)).astype(BF16)
    args += [pg, sga, x, ada, w_attn_out.astype(BF16), w_o.astype(BF16),
             ln1_g.reshape(1, D), ln1_b.reshape(1, D), jnp.stack([wr_hi, wr_lo]),
             b_router.reshape(N_EXPERTS, 1)]
    nc = N // tm
    route_spec = pl.BlockSpec((1, TOP_K, tm), lambda b, i: (b * nt + i, 0, 0))
    out_specs = [tok_spec, tok_spec, route_spec, route_spec,
                 pl.BlockSpec((1, N_EXPERTS, 1), lambda b, i: (b * nt + i, 0, 0))]
    out_shapes = [jax.ShapeDtypeStruct((B, S, D), F32), jax.ShapeDtypeStruct((B, S, D), BF16),
                  jax.ShapeDtypeStruct((nc, TOP_K, tm), jnp.int32), jax.ShapeDtypeStruct((nc, TOP_K, tm), F32),
                  jax.ShapeDtypeStruct((nc, N_EXPERTS, 1), jnp.int32)]
    return pl.pallas_call(
        _post_kernel,
        grid=(B, nt),
        in_specs=in_specs,
        out_specs=out_specs,
        out_shape=out_shapes,
        scratch_shapes=[pltpu.VMEM((GROUP_WIDTH // LANES, tm, LANES), F32)] * 4,
        compiler_params=pltpu.CompilerParams(
            dimension_semantics=("parallel", "parallel"), vmem_limit_bytes=VMEM_LIMIT_BYTES),
        name="post",
    )(*args)


def _for_each_piece(tile, run_ref, dest_ref, len_ref, fn):
    def per_expert(e, total):
        idx = tile * N_EXPERTS + e
        loc, dst, length = run_ref[idx], dest_ref[idx], len_ref[idx]
        n_big = length // PIECE_ROWS
        rest = n_big * PIECE_ROWS

        def big(p, c):
            off = p * PIECE_ROWS
            fn(pl.multiple_of(loc + off, ROW_ALIGN), pl.multiple_of(dst + off, ROW_ALIGN), PIECE_ROWS)
            return c

        def small(p, c):
            off = rest + p * ROW_ALIGN
            fn(pl.multiple_of(loc + off, ROW_ALIGN), pl.multiple_of(dst + off, ROW_ALIGN), ROW_ALIGN)
            return c

        lax.fori_loop(0, n_big, big, 0)
        lax.fori_loop(0, (length - rest) // ROW_ALIGN, small, 0)
        return total + length

    return lax.fori_loop(0, N_EXPERTS, per_expert, 0)


def _wait_rows(src_ref, dst_ref, rows, sem):
    @pl.when(rows > 0)
    def _():
        n = pl.multiple_of(rows, ROW_ALIGN)
        pltpu.make_async_copy(src_ref.at[pl.ds(0, n)], dst_ref.at[pl.ds(0, n)], sem).wait()


def _dispatch_kernel(run_ref, dest_ref, len_ref, tail_ref, ntail_ref, nused_ref,
                     slot_ref, prob_ref, u2_ref, xs_hbm, local2_ref, zero_ref, sem2, zsem, rows_ref):
    tile = pl.program_id(0)
    buf = tile % 2
    local_ref = local2_ref.at[buf]
    sem = sem2.at[buf]
    tt, d = u2_ref.shape
    u2 = u2_ref[...]
    slots = [slot_ref[0, k:k + 1, :] for k in range(TOP_K)]
    probs = [prob_ref[0, k:k + 1, :] for k in range(TOP_K)]
    lane = lax.broadcasted_iota(jnp.int32, (SORT_CHUNK, LANES), 1)
    for r0 in range(0, LOCAL_ROWS, SORT_CHUNK):
        row = r0 + lax.broadcasted_iota(jnp.int32, (SORT_CHUNK, tt), 0)
        w = jnp.zeros((SORT_CHUNK, tt), F32)
        for k in range(TOP_K):
            w = jnp.where(row == slots[k], probs[k], w)
        onehot = jnp.where(w != 0.0, 1.0, 0.0).astype(BF16)
        local_ref[r0:r0 + SORT_CHUNK, 0:d] = _dot(onehot, u2).astype(BF16)
        wr = jnp.sum(w, axis=1, keepdims=True)
        hi = wr.astype(BF16).astype(F32)
        mid = (wr - hi).astype(BF16).astype(F32)
        lo = wr - hi - mid
        parts = jnp.where(lane == 0, hi, jnp.where(lane == 1, mid, jnp.where(lane == 2, lo, 0.0)))
        local_ref[r0:r0 + SORT_CHUNK, d:d + LANES] = parts.astype(BF16)

    def start_piece(loc, dst, rows):
        pltpu.make_async_copy(local_ref.at[pl.ds(loc, rows)], xs_hbm.at[pl.ds(dst, rows)], sem).start()

    rows_ref[buf] = _for_each_piece(tile, run_ref, dest_ref, len_ref, start_piece)

    @pl.when(tile > 0)
    def _():
        _wait_rows(local2_ref.at[1 - buf], xs_hbm, rows_ref[1 - buf], sem2.at[1 - buf])

    @pl.when(tile == pl.num_programs(0) - 1)
    def _():
        _wait_rows(local_ref, xs_hbm, rows_ref[buf], sem)
        zero_ref[...] = jnp.zeros_like(zero_ref)

        def tail_piece(e, p):
            dst = pl.multiple_of(tail_ref[e] + p * ROW_ALIGN, ROW_ALIGN)
            return pltpu.make_async_copy(zero_ref, xs_hbm.at[pl.ds(dst, ROW_ALIGN)], zsem)

        def start_tail(e, carry):
            return lax.fori_loop(0, ntail_ref[e], lambda p, c: (tail_piece(e, p).start(), c)[1], carry)

        def wait_tail(e, carry):
            return lax.fori_loop(0, ntail_ref[e], lambda p, c: (tail_piece(e, p).wait(), c)[1], carry)

        lax.fori_loop(0, N_EXPERTS, start_tail, 0)
        lax.fori_loop(0, N_EXPERTS, wait_tail, 0)

        def spare_piece(p):
            return pltpu.make_async_copy(zero_ref, xs_hbm.at[pl.ds(pl.multiple_of(p * ROW_ALIGN, ROW_ALIGN),
                                                                   ROW_ALIGN)], zsem)

        first_spare = nused_ref[0] * (MOE_BLOCK // ROW_ALIGN)
        end_spare = xs_hbm.shape[0] // ROW_ALIGN
        lax.fori_loop(first_spare, end_spare, lambda p, c: (spare_piece(p).start(), c)[1], 0)
        lax.fori_loop(first_spare, end_spare, lambda p, c: (spare_piece(p).wait(), c)[1], 0)


def _dispatch(tables, slot, prob, u2, n_rows):
    N, D = u2.shape
    tt = POST_TILE
    route_spec = pl.BlockSpec((1, TOP_K, tt), lambda i, *t: (i, 0, 0))
    grid_spec = pltpu.PrefetchScalarGridSpec(
        num_scalar_prefetch=len(tables),
        grid=(N // tt,),
        in_specs=[route_spec, route_spec, pl.BlockSpec((tt, D), lambda i, *t: (i, 0))],
        out_specs=pl.BlockSpec(memory_space=pl.ANY),
        scratch_shapes=[pltpu.VMEM((2, LOCAL_ROWS, D + LANES), BF16), pltpu.VMEM((ROW_ALIGN, D + LANES), BF16),
                        pltpu.SemaphoreType.DMA((2,)), pltpu.SemaphoreType.DMA(()),
                        pltpu.SMEM((2,), jnp.int32)],
    )
    return pl.pallas_call(
        _dispatch_kernel,
        grid_spec=grid_spec,
        out_shape=jax.ShapeDtypeStruct((n_rows, D + LANES), BF16),
        compiler_params=pltpu.CompilerParams(
            dimension_semantics=("arbitrary",), vmem_limit_bytes=VMEM_LIMIT_BYTES),
        name="dispatch",
    )(*tables, slot, prob, u2)


def _expert_kernel(be_ref, next_ref, half_ref, nused_ref, xs_ref, wg_hbm, bg_ref, wu_hbm, bu_ref, wd_hbm, bd_ref,
                   ys_ref, stage, wg_s, wu_s, wd_s, sem):
    i = pl.program_id(0)
    used = i < nused_ref[0]
    prev = be_ref[jnp.maximum(i - 1, 0)]
    fresh = (i == 0) | (be_ref[i] != prev)

    def fetch(e):
        return [pltpu.make_async_copy(w_hbm.at[e], stage.at[j], sem.at[j])
                for j, w_hbm in enumerate((wg_hbm, wu_hbm, wd_hbm))]

    @pl.when(i == 0)
    def _():
        for copy in fetch(be_ref[0]):
            copy.start()

    @pl.when(used & fresh)
    def _():
        for copy in fetch(be_ref[i]):
            copy.wait()
        wg_s[...] = stage[0].astype(BF16)
        wu_s[...] = stage[1].astype(BF16)
        wd_s[...] = stage[2].astype(BF16)

        @pl.when(next_ref[i] >= 0)
        def _():
            for copy in fetch(next_ref[i]):
                copy.start()

    d = ys_ref.shape[1]
    half_rows = MOE_BLOCK // 2

    def expert(rows):
        xb = xs_ref[0:rows, 0:d]
        parts = xs_ref[0:rows, d:d + LANES].astype(F32)
        weight = parts[:, 0:1] + parts[:, 1:2] + parts[:, 2:3]
        g = _dot(xb, wg_s[...]) + bg_ref[0]
        up = _dot(xb, wu_s[...]) + bu_ref[0]
        g = jnp.minimum(g, SWIGLU_LIMIT)
        up = jnp.clip(up, -SWIGLU_LIMIT, SWIGLU_LIMIT)
        h = g * jax.nn.sigmoid(SWIGLU_ALPHA * g) * (up + 1.0)
        return ((_dot(h.astype(BF16), wd_s[...]) + bd_ref[0]) * weight).astype(BF16)

    @pl.when(used & (half_ref[i] == 0))
    def _():
        ys_ref[...] = expert(MOE_BLOCK)

    @pl.when(used & (half_ref[i] != 0))
    def _():
        ys_ref[0:half_rows, :] = expert(half_rows)
        ys_ref[half_rows:, :] = jnp.zeros((MOE_BLOCK - half_rows, d), BF16)

    @pl.when(jnp.logical_not(used))
    def _():
        ys_ref[...] = jnp.zeros_like(ys_ref)


def _experts(be, next_expert, half, nused, xs, w_gate, b_gate, w_up, b_up, w_down, b_down):
    P, width = xs.shape
    D = width - LANES
    E = w_gate.shape[0]

    def live(i, nu):
        return jnp.maximum(jnp.minimum(i, nu[0] - 1), 0)

    w_spec = pl.BlockSpec(memory_space=pl.ANY)
    b_spec = pl.BlockSpec((1, 1, D), lambda i, be, nx, hf, nu: (be[live(i, nu)], 0, 0))
    grid_spec = pltpu.PrefetchScalarGridSpec(
        num_scalar_prefetch=4,
        grid=(P // MOE_BLOCK,),
        in_specs=[pl.BlockSpec((MOE_BLOCK, width), lambda i, be, nx, hf, nu: (live(i, nu), 0)),
                  w_spec, b_spec, w_spec, b_spec, w_spec, b_spec],
        out_specs=pl.BlockSpec((MOE_BLOCK, D), lambda i, be, nx, hf, nu: (i, 0)),
        scratch_shapes=[pltpu.VMEM((3, D, D), F32)] + [pltpu.VMEM((D, D), BF16)] * 3
                       + [pltpu.SemaphoreType.DMA((3,))],
    )
    return pl.pallas_call(
        _expert_kernel,
        grid_spec=grid_spec,
        out_shape=jax.ShapeDtypeStruct((P, D), BF16),
        compiler_params=pltpu.CompilerParams(
            dimension_semantics=("arbitrary",), vmem_limit_bytes=VMEM_LIMIT_BYTES),
        name="experts",
    )(be, next_expert, half, nused, xs, w_gate, b_gate.reshape(E, 1, D), w_up, b_up.reshape(E, 1, D),
      w_down, b_down.reshape(E, 1, D))


def _combine_kernel(run_ref, dest_ref, len_ref,
                    slot_ref, x1_ref, ada_ref, g2_ref, b2_ref, ys_hbm,
                    out_ref, local2_ref, sem2, rows_ref):
    tile = pl.program_id(0)
    buf = tile % 2
    tt = x1_ref.shape[0]

    def fetch(t, b):
        def start_piece(loc, dst, rows):
            pltpu.make_async_copy(ys_hbm.at[pl.ds(dst, rows)], local2_ref.at[b, pl.ds(loc, rows)],
                                  sem2.at[b]).start()
        rows_ref[b] = _for_each_piece(t, run_ref, dest_ref, len_ref, start_piece)

    @pl.when(tile == 0)
    def _():
        local2_ref[...] = jnp.zeros_like(local2_ref)
        fetch(0, 0)

    @pl.when(tile + 1 < pl.num_programs(0))
    def _():
        fetch(tile + 1, 1 - buf)

    local_ref = local2_ref.at[buf]
    _wait_rows(ys_hbm, local_ref, rows_ref[buf], sem2.at[buf])
    slots = [slot_ref[:, k:k + 1] for k in range(TOP_K)]

    ffn = jnp.zeros((tt, x1_ref.shape[1]), F32)
    for r0 in range(0, LOCAL_ROWS, SORT_CHUNK):
        col = r0 + lax.broadcasted_iota(jnp.int32, (tt, SORT_CHUNK), 1)
        onehot = jnp.zeros((tt, SORT_CHUNK), F32)
        for k in range(TOP_K):
            onehot = jnp.where(col == slots[k], 1.0, onehot)
        ffn = ffn + _dot(onehot.astype(BF16), local_ref[r0:r0 + SORT_CHUNK, :])
    gate2 = ada_ref[0, 5:6, :]
    y = DN_ALPHA * x1_ref[...] + (1.0 + gate2) * ffn
    out_ref[...] = _layer_norm(y) * g2_ref[...] + b2_ref[...]


def _combine(tables, slot_tok, x1, ada, ln2_g, ln2_b, ys, tiles_per_batch):
    N, D = x1.shape
    tt = POST_TILE
    const = lambda i, *t: (0, 0)
    tok4 = pl.BlockSpec((tt, TOP_K), lambda i, *t: (i, 0))
    grid_spec = pltpu.PrefetchScalarGridSpec(
        num_scalar_prefetch=len(tables),
        grid=(N // tt,),
        in_specs=[tok4,
                  pl.BlockSpec((tt, D), lambda i, *t: (i, 0)),
                  pl.BlockSpec((1, 6, D), lambda i, *t: (i // tiles_per_batch, 0, 0)),
                  pl.BlockSpec((1, D), const),
                  pl.BlockSpec((1, D), const),
                  pl.BlockSpec(memory_space=pl.ANY)],
        out_specs=pl.BlockSpec((tt, D), lambda i, *t: (i, 0)),
        scratch_shapes=[pltpu.VMEM((2, LOCAL_ROWS, D), BF16), pltpu.SemaphoreType.DMA((2,)),
                        pltpu.SMEM((2,), jnp.int32)],
    )
    return pl.pallas_call(
        _combine_kernel,
        grid_spec=grid_spec,
        out_shape=jax.ShapeDtypeStruct((N, D), F32),
        compiler_params=pltpu.CompilerParams(
            dimension_semantics=("arbitrary",), vmem_limit_bytes=VMEM_LIMIT_BYTES),
        name="combine",
    )(*tables, slot_tok, x1, ada, ln2_g.reshape(1, D), ln2_b.reshape(1, D), ys)


def _routing_tables(cnt, nblk):
    nt, E = cnt.shape
    i32 = jnp.int32
    run = (cnt + ROW_ALIGN - 1) // ROW_ALIGN * ROW_ALIGN
    run_start = jnp.cumsum(run, axis=1) - run
    seg_len = jnp.sum(run, axis=0)
    seg_blocks = (seg_len + MOE_BLOCK - 1) // MOE_BLOCK
    b_end = jnp.cumsum(seg_blocks)
    seg_off = (b_end - seg_blocks) * MOE_BLOCK
    dest = seg_off[None, :] + jnp.cumsum(run, axis=0) - run
    nused = b_end[-1:].astype(i32)
    i = jnp.arange(nblk, dtype=i32)
    be = jnp.minimum(jnp.sum((i[:, None] >= b_end[None, :]).astype(i32), axis=1), E - 1)
    e_ids = jnp.arange(E, dtype=i32)
    later = (e_ids[None, :] > e_ids[:, None]) & (seg_blocks > 0)[None, :]
    next_e = jnp.min(jnp.where(later, e_ids[None, :], E), axis=1)
    next_e = jnp.where(next_e < E, next_e, -1)
    mine = be[:, None] == e_ids[None, :]
    pick = lambda v: jnp.sum(jnp.where(mine, v[None, :], 0), axis=1)
    nxt = pick(next_e)
    rows_left = pick(seg_len) - (i - pick(b_end - seg_blocks)) * MOE_BLOCK
    half = (rows_left <= MOE_BLOCK // 2).astype(i32)
    tail = seg_off + seg_len
    ntail = (seg_blocks * MOE_BLOCK - seg_len) // ROW_ALIGN
    piece_tables = (run_start.reshape(-1).astype(i32), dest.reshape(-1).astype(i32),
                    run.reshape(-1).astype(i32))
    return piece_tables, (tail.astype(i32), ntail.astype(i32)), (be.astype(i32), nxt.astype(i32), half, nused)


def kernel(x, c, positions, w_ada, b_ada, w_in, pool_w, pool_scale, w_pool_out, w_attn_out, w_o,
           ln1_g, ln1_b, w_router, b_router, w_gate, b_gate, w_up, b_up, w_down, b_down, ln2_g, ln2_b):
    B, S, D = x.shape
    N = B * S
    assert D == D_MODEL and S % PROJ_TILE == 0 and S % (16 * ATTN_BLOCK) == 0
    assert S % POST_TILE == 0 and LOCAL_ROWS % SORT_CHUNK == 0
    nt = N // POST_TILE
    nblk = (N * TOP_K + nt * N_EXPERTS * (ROW_ALIGN - 1)) // MOE_BLOCK + N_EXPERTS
    for l in range(DEPTH):
        ada = _ada(c, w_ada[l], b_ada[l])
        proj_out = _proj(x, positions, ada, w_in[l], pool_w[l], pool_scale[l], w_pool_out[l])
        qkv, (pg, sga) = proj_out[:9], proj_out[9:]
        attn_outs = [_attention(*qkv[3 * g:3 * g + 3]) for g in range(len(ATTN_GROUPS))]
        x1, u2, slot, prob, cnt = _post(attn_outs, pg, sga, x, ada, w_attn_out[l], w_o[l],
                                        ln1_g[l], ln1_b[l], w_router[l], b_router[l])
        piece_tables, tail_tables, block_tables = _routing_tables(cnt.reshape(nt, N_EXPERTS), nblk)
        xs = _dispatch(piece_tables + tail_tables + block_tables[3:], slot, prob, u2.reshape(N, D),
                       nblk * MOE_BLOCK)
        ys = _experts(*block_tables, xs, w_gate[l], b_gate[l], w_up[l], b_up[l], w_down[l], b_down[l])
        slot_tok = slot.transpose(0, 2, 1).reshape(N, TOP_K)
        out = _combine(piece_tables, slot_tok, x1.reshape(N, D), ada,
                       ln2_g[l], ln2_b[l], ys, S // POST_TILE)
        x = out.reshape(B, S, D)
    return x
```

```python
import functools

import jax
import jax.numpy as jnp
import numpy as np
from jax import lax
from jax.experimental import pallas as pl
from jax.experimental.pallas import tpu as pltpu

F32 = jnp.float32
BF16 = jnp.bfloat16

D_MODEL = 1024
POOL_WINDOWS = (2, 4, 8, 16)
POOL_WIDTH = D_MODEL // 2
POOL_GROUP = POOL_WIDTH // len(POOL_WINDOWS)
POOL_HALO = 16
HEAD_DIM = 64
ATTN_GROUPS = ((128, 1), (512, 4), (2048, 16))
HEADS_PER_GROUP = 4
GROUP_WIDTH = HEADS_PER_GROUP * HEAD_DIM
N_HEADS = HEADS_PER_GROUP * len(ATTN_GROUPS)
ATTN_WIDTH = N_HEADS * HEAD_DIM
ATTN_BLOCK = 128
ROT_DIM = HEAD_DIM // 4
ROPE_THETA = 500000.0
N_EXPERTS = 32
TOP_K = 4
SWIGLU_ALPHA = 1.702
SWIGLU_LIMIT = 7.0
MOE_BLOCK = 512
EXPERT_PART = 128
DEPTH = 1
DN_ALPHA = (2.0 * DEPTH) ** 0.25
LN_EPS = 1e-5
NEG_INF = -1e30

OFF_Q = POOL_WIDTH
OFF_K = OFF_Q + ATTN_WIDTH
OFF_V = OFF_K + ATTN_WIDTH
OFF_GP = OFF_V + ATTN_WIDTH
OFF_GA = OFF_GP + D_MODEL
IN_WIDTH = OFF_GA + D_MODEL

VMEM_LIMIT_BYTES = 56 * 1024 * 1024
LANES = 128

PROJ_TILE = 512
POST_TILE = 512
ATTN_QROWS = 1024
ROW_ALIGN = 16
PIECE_ROWS = 64
SORT_CHUNK = 512
LOCAL_ROWS = -(-(POST_TILE * TOP_K + N_EXPERTS * (ROW_ALIGN - 1)) // SORT_CHUNK) * SORT_CHUNK


def _layer_norm(x):
    mu = jnp.mean(x, axis=-1, keepdims=True)
    xc = x - mu
    var = jnp.mean(xc * xc, axis=-1, keepdims=True)
    return xc * lax.rsqrt(var + LN_EPS)


def _dot(a, b):
    return jnp.dot(a, b, preferred_element_type=F32)


def _ada_kernel(c_ref, w_ref, b_ref, o_ref):
    c = c_ref[...]
    s = c * jax.nn.sigmoid(c)
    o_ref[...] = jnp.dot(s, w_ref[...], preferred_element_type=F32,
                         precision=lax.Precision.HIGHEST) + b_ref[...]


def _ada(c, w_ada, b_ada):
    B, D = c.shape
    rows = 8
    c_pad = jnp.pad(c, ((0, rows - B), (0, 0)))
    n_out = w_ada.shape[1]
    out = pl.pallas_call(
        _ada_kernel,
        grid=(n_out // D,),
        in_specs=[pl.BlockSpec((rows, D), lambda j: (0, 0)),
                  pl.BlockSpec((D, D), lambda j: (0, j)),
                  pl.BlockSpec((1, D), lambda j: (0, j))],
        out_specs=pl.BlockSpec((rows, D), lambda j: (0, j)),
        out_shape=jax.ShapeDtypeStruct((rows, n_out), F32),
        name="ada",
    )(c_pad, w_ada, b_ada.reshape(1, n_out))
    return out[:B].reshape(B, 6, D)


ROPE_PART_ROWS = 32


def _rope_tables():
    lane = np.arange(LANES)
    li = lane % HEAD_DIM
    half = ROT_DIM // 2
    inv_freq = jnp.power(ROPE_THETA, -jnp.arange(half, dtype=F32) * (2.0 / ROT_DIM))
    invf = jnp.broadcast_to(inv_freq[:, None], (half, LANES))
    freq = np.arange(ROPE_PART_ROWS)[:, None]
    live = (freq < 3 * half) & (freq % half == (li % half)[None, :])
    place = np.stack([live & (li < ROT_DIM)[None, :],
                      -1.0 * (live & (li < half)[None, :]),
                      live & ((li >= half) & (li < ROT_DIM))[None, :]]).astype(np.float32)
    keep = (li >= ROT_DIM).astype(np.float32)[None, :]
    return invf, jnp.asarray(place, BF16), jnp.asarray(keep)


def _proj_kernel(x_ref, xh_ref, pos_ref, ada_ref, invf_ref, place_ref, keep_ref,
                 win_ref, poolw_ref, pscale_ref, wpo_ref,
                 q1_ref, k1_ref, v1_ref, q4_ref, k4_ref, v4_ref, q16_ref, k16_ref, v16_ref,
                 pg_ref, sga_ref, xpe_ref, cls_ref):
    tm = x_ref.shape[1]
    i = pl.program_id(1)
    shift1 = ada_ref[0, 0:1, :]
    scale1 = ada_ref[0, 1:2, :]

    def modulated(xv):
        return (_layer_norm(xv) * (1.0 + scale1) + shift1).astype(BF16)

    u = modulated(x_ref[0])
    uh = modulated(xh_ref[0])

    xp = _dot(u, win_ref[:, 0:POOL_WIDTH])
    xph = _dot(uh, win_ref[:, 0:POOL_WIDTH])
    xph = jnp.where(i > 0, xph, 0.0)
    xpe_ref[0:POOL_HALO, :] = xph
    xpe_ref[POOL_HALO:, :] = xp
    tok = i * tm + lax.broadcasted_iota(jnp.int32, (tm, 1), 0)
    ys = []
    for g, w in enumerate(POOL_WINDOWS):
        cols = slice(g * POOL_GROUP, (g + 1) * POOL_GROUP)
        xg = xp[:, cols]
        acc = xg
        for j in range(1, w):
            acc = acc + xpe_ref[POOL_HALO - j:POOL_HALO - j + tm, cols]
        cnt = jnp.minimum(tok + 1, w).astype(F32)
        mixed = (acc / cnt - xg).astype(BF16)
        ys.append(_dot(mixed, poolw_ref[g]) * pscale_ref[:, cols])
    y = jnp.concatenate(ys, axis=1).astype(BF16)
    pooled = _dot(y, wpo_ref[...])
    g_p = _dot(u, win_ref[:, OFF_GP:OFF_GP + D_MODEL])
    pg_ref[0] = jax.nn.sigmoid(g_p) * pooled
    g_a = _dot(u, win_ref[:, OFF_GA:OFF_GA + D_MODEL])
    sga_ref[0] = jax.nn.sigmoid(g_a)

    ang = invf_ref[:, 0:1] * pos_ref[0, 0].astype(F32)

    def on_lanes(table, j):
        hi = table.astype(BF16).astype(F32)
        mid = (table - hi).astype(BF16).astype(F32)
        lo = table - hi - mid
        pad = jnp.zeros((ROPE_PART_ROWS - 3 * table.shape[0], tm), F32)
        parts = jnp.concatenate([hi, mid, lo, pad], axis=0).astype(BF16)
        return lax.dot_general(parts, place_ref[j], (((0,), (0,)), ((), ())), preferred_element_type=F32)

    cos = jnp.cos(ang)
    sin = jnp.sin(ang)
    c_mul = on_lanes(cos, 0) + keep_ref[...]
    s_lo = on_lanes(sin, 1)
    s_hi = on_lanes(sin, 2)
    c_mul = jnp.concatenate([c_mul, c_mul], axis=1)
    s_lo = jnp.concatenate([s_lo, s_lo], axis=1)
    s_hi = jnp.concatenate([s_hi, s_hi], axis=1)
    half = ROT_DIM // 2

    def rotate(a):
        up = pltpu.roll(a, GROUP_WIDTH - half, axis=1)
        dn = pltpu.roll(a, half, axis=1)
        return a * c_mul + up * s_lo + dn * s_hi

    def emit(a, out_ref, dil):
        if dil == 1:
            out_ref[0, 0] = a.astype(BF16)
            return
        for c in range(GROUP_WIDTH // LANES):
            cls_ref[c] = a[:, c * LANES:(c + 1) * LANES]
        for r in range(dil):
            for c in range(GROUP_WIDTH // LANES):
                out_ref[0, r, :, c * LANES:(c + 1) * LANES] = (
                    cls_ref[c, pl.ds(r, tm // dil, stride=dil), :].astype(BF16))

    outs = ((q1_ref, k1_ref, v1_ref), (q4_ref, k4_ref, v4_ref), (q16_ref, k16_ref, v16_ref))
    for gi, (_, dil) in enumerate(ATTN_GROUPS):
        qo, ko, vo = outs[gi]
        c0 = gi * GROUP_WIDTH
        emit(rotate(_dot(u, win_ref[:, OFF_Q + c0:OFF_Q + c0 + GROUP_WIDTH])), qo, dil)
        emit(rotate(_dot(u, win_ref[:, OFF_K + c0:OFF_K + c0 + GROUP_WIDTH])), ko, dil)
        emit(_dot(u, win_ref[:, OFF_V + c0:OFF_V + c0 + GROUP_WIDTH]), vo, dil)


def _proj(x, positions, ada, w_in, pool_w, pool_scale, w_pool_out):
    B, S, D = x.shape
    tm = PROJ_TILE
    nt = S // tm
    halo_blocks = tm // POOL_HALO
    const2 = lambda b, i: (0, 0)
    in_specs = [
        pl.BlockSpec((1, tm, D), lambda b, i: (b, i, 0)),
        pl.BlockSpec((1, POOL_HALO, D), lambda b, i: (b, jnp.maximum(i * halo_blocks - 1, 0), 0)),
        pl.BlockSpec((1, 1, 1, tm), lambda b, i: (b, i, 0, 0)),
        pl.BlockSpec((1, 6, D), lambda b, i: (b, 0, 0)),
        pl.BlockSpec((ROT_DIM // 2, LANES), const2),
        pl.BlockSpec((3, ROPE_PART_ROWS, LANES), lambda b, i: (0, 0, 0)),
        pl.BlockSpec((1, LANES), const2),
        pl.BlockSpec((D, IN_WIDTH), const2),
        pl.BlockSpec((len(POOL_WINDOWS), POOL_GROUP, POOL_GROUP), lambda b, i: (0, 0, 0)),
        pl.BlockSpec((1, POOL_WIDTH), const2),
        pl.BlockSpec((POOL_WIDTH, D), const2),
    ]
    out_specs, out_shapes = [], []
    for _, dil in ATTN_GROUPS:
        for _ in range(3):
            out_specs.append(pl.BlockSpec((1, dil, tm // dil, GROUP_WIDTH), lambda b, i: (b, 0, i, 0)))
            out_shapes.append(jax.ShapeDtypeStruct((B, dil, S // dil, GROUP_WIDTH), BF16))
    for _ in range(2):
        out_specs.append(pl.BlockSpec((1, tm, D), lambda b, i: (b, i, 0)))
        out_shapes.append(jax.ShapeDtypeStruct((B, S, D), F32))
    return pl.pallas_call(
        _proj_kernel,
        grid=(B, nt),
        in_specs=in_specs,
        out_specs=out_specs,
        out_shape=out_shapes,
        scratch_shapes=[pltpu.VMEM((tm + POOL_HALO, POOL_WIDTH), F32),
                        pltpu.VMEM((GROUP_WIDTH // LANES, tm, LANES), F32)],
        compiler_params=pltpu.CompilerParams(
            dimension_semantics=("parallel", "parallel"), vmem_limit_bytes=VMEM_LIMIT_BYTES),
        name="proj",
    )(x, x, positions.reshape(B, nt, 1, tm), ada, *_rope_tables(), w_in.astype(BF16),
      pool_w.astype(BF16), pool_scale.reshape(1, POOL_WIDTH), w_pool_out.astype(BF16))


def _attn_kernel(q_ref, k_ref, v_ref, kh_ref, vh_ref, o_ref, lse_ref, kf_ref, vf_ref):
    qb = q_ref.shape[2]
    n = pl.program_id(2)
    kf_ref[0:ATTN_BLOCK, :] = kh_ref[0, 0]
    kf_ref[ATTN_BLOCK:, :] = k_ref[0, 0]
    vf_ref[0:ATTN_BLOCK, :] = vh_ref[0, 0]
    vf_ref[ATTN_BLOCK:, :] = v_ref[0, 0]
    qi = lax.broadcasted_iota(jnp.int32, (ATTN_BLOCK, 2 * ATTN_BLOCK), 0)
    kj = lax.broadcasted_iota(jnp.int32, (ATTN_BLOCK, 2 * ATTN_BLOCK), 1)
    band = (kj >= qi) & (kj <= qi + ATTN_BLOCK)
    band_bias = jnp.where(band, 0.0, NEG_INF)
    lane = lax.broadcasted_iota(jnp.int32, (ATTN_BLOCK, GROUP_WIDTH), 1)
    low_lanes = lax.broadcasted_iota(jnp.int32, (ATTN_BLOCK, LANES), 1) < HEAD_DIM
    ones = jnp.ones((2 * ATTN_BLOCK, LANES), BF16)
    nh, blk = HEADS_PER_GROUP, ATTN_BLOCK

    def block(j, carry):
        r0 = pl.multiple_of(j * ATTN_BLOCK, ATTN_BLOCK)
        first_key = jnp.where((n > 0) | (j > 0), 0, ATTN_BLOCK)
        bias = band_bias + jnp.where(kj < first_key, NEG_INF, 0.0)
        q = q_ref[0, 0, pl.ds(r0, blk), :].astype(F32)
        kk = kf_ref[pl.ds(r0, 2 * blk), :]
        vv = vf_ref[pl.ds(r0, 2 * blk), :]
        qs = jnp.concatenate([jnp.where((lane >= h * HEAD_DIM) & (lane < (h + 1) * HEAD_DIM), q, 0.0)
                              for h in range(nh)], axis=0).astype(BF16)
        s = lax.dot_general(qs, kk, (((1,), (1,)), ((), ())), preferred_element_type=F32)
        s = jnp.concatenate([s[h * blk:(h + 1) * blk] * (HEAD_DIM ** -0.5) + bias for h in range(nh)], axis=0)
        m = jnp.max(s, axis=-1, keepdims=True)
        p = jnp.exp(s - m).astype(BF16)
        den = _dot(p, ones)
        lse = m + jnp.log(den)
        for hp in range(GROUP_WIDTH // LANES):
            rows = slice(2 * hp * blk, (2 * hp + 2) * blk)
            ls = slice(hp * LANES, (hp + 1) * LANES)
            o2 = _dot(p[rows], vv[:, ls]) / den[rows]
            l2 = lse[rows]
            o_ref[0, 0, pl.ds(r0, blk), ls] = jnp.where(low_lanes, o2[0:blk], o2[blk:2 * blk])
            lse_ref[0, 0, pl.ds(r0, blk), ls] = jnp.where(low_lanes, l2[0:blk], l2[blk:2 * blk])
        return carry

    lax.fori_loop(0, qb // ATTN_BLOCK, block, 0, unroll=8)


def _attention(q, k, v):
    B, dil, L, W = q.shape
    qb = min(L, ATTN_QROWS)
    per = qb // ATTN_BLOCK
    main = pl.BlockSpec((1, 1, qb, W), lambda b, r, n: (b, r, n, 0))
    halo = pl.BlockSpec((1, 1, ATTN_BLOCK, W), lambda b, r, n: (b, r, jnp.maximum(n * per - 1, 0), 0))
    return pl.pallas_call(
        _attn_kernel,
        grid=(B, dil, L // qb),
        in_specs=[main, main, main, halo, halo],
        out_specs=[main, main],
        out_shape=[jax.ShapeDtypeStruct((B, dil, L, W), F32)] * 2,
        scratch_shapes=[pltpu.VMEM((qb + ATTN_BLOCK, W), BF16)] * 2,
        compiler_params=pltpu.CompilerParams(
            dimension_semantics=("parallel", "parallel", "parallel"), vmem_limit_bytes=VMEM_LIMIT_BYTES),
        name=f"attn_d{dil}",
    )(q, k, v, k, v)


def _post_kernel(o1_ref, l1_ref, o4_ref, l4_ref, o16_ref, l16_ref, pg_ref, sga_ref, x_ref, ada_ref,
                 wao_ref, wo_ref, g1_ref, b1_ref, wrt_ref, brt_ref,
                 x1_ref, u2_ref, slot_ref, prob_ref, cnt_ref,
                 s0, s1, s2, s3):
    tm = x_ref.shape[1]

    def token_major(src_ref, scr_ref, dil):
        if dil == 1:
            return src_ref[0, 0]
        for r in range(dil):
            for c in range(GROUP_WIDTH // LANES):
                scr_ref[c, pl.ds(r, tm // dil, stride=dil), :] = src_ref[0, r, :, c * LANES:(c + 1) * LANES]
        return jnp.concatenate([scr_ref[c] for c in range(GROUP_WIDTH // LANES)], axis=1)

    o1, l1 = o1_ref[0, 0], l1_ref[0, 0]
    o4, l4 = token_major(o4_ref, s0, 4), token_major(l4_ref, s1, 4)
    o16, l16 = token_major(o16_ref, s2, 16), token_major(l16_ref, s3, 16)
    mx = jnp.maximum(jnp.maximum(l1, l4), l16)
    e1, e4, e16 = jnp.exp(l1 - mx), jnp.exp(l4 - mx), jnp.exp(l16 - mx)
    attn = (e1 * o1 + e4 * o4 + e16 * o16) / (e1 + e4 + e16)

    merged = pg_ref[0] + sga_ref[0] * _dot(attn.astype(BF16), wao_ref[...])
    mix = _dot(merged.astype(BF16), wo_ref[...])
    gate1 = ada_ref[0, 2:3, :]
    shift2 = ada_ref[0, 3:4, :]
    scale2 = ada_ref[0, 4:5, :]
    x1 = _layer_norm(DN_ALPHA * x_ref[0] + (1.0 + gate1) * mix) * g1_ref[...] + b1_ref[...]
    x1_ref[0] = x1
    u2 = _layer_norm(x1) * (1.0 + scale2) + shift2
    u2_hi = u2.astype(BF16)
    u2_ref[0] = u2_hi

    u2_lo = (u2 - u2_hi.astype(F32)).astype(BF16)
    nt_dot = lambda a, b: lax.dot_general(a, b, (((1,), (1,)), ((), ())), preferred_element_type=F32)
    logits = (nt_dot(wrt_ref[0], u2_hi) + nt_dot(wrt_ref[0], u2_lo) + nt_dot(wrt_ref[1], u2_hi)
              + brt_ref[...])
    eidx = lax.broadcasted_iota(jnp.int32, (N_EXPERTS, tm), 0)
    work = logits
    vals, idxs = [], []
    for _ in range(TOP_K):
        m = jnp.max(work, axis=0, keepdims=True)
        idx = jnp.min(jnp.where(work == m, eidx, N_EXPERTS), axis=0, keepdims=True)
        vals.append(m)
        idxs.append(idx)
        work = jnp.where(eidx == idx, -jnp.inf, work)
    exps = [jnp.exp(vk - vals[0]) for vk in vals]
    tot = exps[0] + exps[1] + exps[2] + exps[3]
    sel = jnp.zeros((N_EXPERTS, tm), F32)
    for idx in idxs:
        sel = sel + (eidx == idx).astype(F32)
    tr = lax.broadcasted_iota(jnp.int32, (tm, tm), 0)
    tc = lax.broadcasted_iota(jnp.int32, (tm, tm), 1)
    rank = _dot(sel.astype(BF16), (tr < tc).astype(BF16))
    cnt = jnp.sum(sel, axis=1, keepdims=True)
    run = jnp.floor((cnt + (ROW_ALIGN - 1)) * (1.0 / ROW_ALIGN)) * ROW_ALIGN
    er = lax.broadcasted_iota(jnp.int32, (N_EXPERTS, N_EXPERTS), 0)
    ec = lax.broadcasted_iota(jnp.int32, (N_EXPERTS, N_EXPERTS), 1)
    run_start = _dot((ec < er).astype(BF16),
                     jnp.broadcast_to(run, (N_EXPERTS, LANES)).astype(BF16))[:, 0:1]
    slot = rank + run_start
    for k in range(TOP_K):
        slot_ref[0, k:k + 1, :] = jnp.sum(jnp.where(eidx == idxs[k], slot, 0.0), axis=0,
                                          keepdims=True).astype(jnp.int32)
        prob_ref[0, k:k + 1, :] = exps[k] / tot
    cnt_ref[0] = cnt.astype(jnp.int32)


def _post(attn_outs, pg, sga, x, ada, w_attn_out, w_o, ln1_g, ln1_b, w_router, b_router):
    B, S, D = x.shape
    tm = POST_TILE
    nt = S // tm
    N = B * S
    const2 = lambda b, i: (0, 0)
    in_specs, args = [], []
    for (o, lse), (_, dil) in zip(attn_outs, ATTN_GROUPS):
        spec = pl.BlockSpec((1, dil, tm // dil, GROUP_WIDTH), lambda b, i: (b, 0, i, 0))
        in_specs += [spec, spec]
        args += [o, lse]
    tok_spec = pl.BlockSpec((1, tm, D), lambda b, i: (b, i, 0))
    in_specs += [tok_spec, tok_spec, tok_spec,
                 pl.BlockSpec((1, 6, D), lambda b, i: (b, 0, 0)),
                 pl.BlockSpec((GROUP_WIDTH, D), const2),
                 pl.BlockSpec((D, D), const2),
                 pl.BlockSpec((1, D), const2),
                 pl.BlockSpec((1, D), const2),
                 pl.BlockSpec((2, N_EXPERTS, D), lambda b, i: (0, 0, 0)),
                 pl.BlockSpec((N_EXPERTS, 1), const2)]
    wr_hi = w_router.T.astype(BF16)
    wr_lo = (w_router.T - wr_hi.astype(F32)).astype(BF16)
    args += [pg, sga, x, ada, w_attn_out.astype(BF16), w_o.astype(BF16),
             ln1_g.reshape(1, D), ln1_b.reshape(1, D), jnp.stack([wr_hi, wr_lo]),
             b_router.reshape(N_EXPERTS, 1)]
    nc = N // tm
    route_spec = pl.BlockSpec((1, TOP_K, tm), lambda b, i: (b * nt + i, 0, 0))
    out_specs = [tok_spec, tok_spec, route_spec, route_spec,
                 pl.BlockSpec((1, N_EXPERTS, 1), lambda b, i: (b * nt + i, 0, 0))]
    out_shapes = [jax.ShapeDtypeStruct((B, S, D), F32), jax.ShapeDtypeStruct((B, S, D), BF16),
                  jax.ShapeDtypeStruct((nc, TOP_K, tm), jnp.int32), jax.ShapeDtypeStruct((nc, TOP_K, tm), F32),
                  jax.ShapeDtypeStruct((nc, N_EXPERTS, 1), jnp.int32)]
    return pl.pallas_call(
        _post_kernel,
        grid=(B, nt),
        in_specs=in_specs,
        out_specs=out_specs,
        out_shape=out_shapes,
        scratch_shapes=[pltpu.VMEM((GROUP_WIDTH // LANES, tm, LANES), F32)] * 4,
        compiler_params=pltpu.CompilerParams(
            dimension_semantics=("parallel", "parallel"), vmem_limit_bytes=VMEM_LIMIT_BYTES),
        name="post",
    )(*args)


def _for_each_piece(tile, run_ref, dest_ref, len_ref, fn):
    def per_expert(e, total):
        idx = tile * N_EXPERTS + e
        loc, dst, length = run_ref[idx], dest_ref[idx], len_ref[idx]
        n_big = length // PIECE_ROWS
        rest = n_big * PIECE_ROWS

        def big(p, c):
            off = p * PIECE_ROWS
            fn(pl.multiple_of(loc + off, ROW_ALIGN), pl.multiple_of(dst + off, ROW_ALIGN), PIECE_ROWS)
            return c

        def small(p, c):
            off = rest + p * ROW_ALIGN
            fn(pl.multiple_of(loc + off, ROW_ALIGN), pl.multiple_of(dst + off, ROW_ALIGN), ROW_ALIGN)
            return c

        lax.fori_loop(0, n_big, big, 0)
        lax.fori_loop(0, (length - rest) // ROW_ALIGN, small, 0)
        return total + length

    return lax.fori_loop(0, N_EXPERTS, per_expert, 0)


def _wait_rows(src_ref, dst_ref, rows, sem):
    @pl.when(rows > 0)
    def _():
        n = pl.multiple_of(rows, ROW_ALIGN)
        pltpu.make_async_copy(src_ref.at[pl.ds(0, n)], dst_ref.at[pl.ds(0, n)], sem).wait()


def _dispatch_kernel(run_ref, dest_ref, len_ref, tail_ref, ntail_ref, nused_ref,
                     slot_ref, prob_ref, u2_ref, xs_hbm, local2_ref, zero_ref, sem2, zsem, rows_ref):
    tile = pl.program_id(0)
    buf = tile % 2
    local_ref = local2_ref.at[buf]
    sem = sem2.at[buf]
    tt, d = u2_ref.shape
    u2 = u2_ref[...]
    slots = [slot_ref[0, k:k + 1, :] for k in range(TOP_K)]
    probs = [prob_ref[0, k:k + 1, :] for k in range(TOP_K)]
    lane = lax.broadcasted_iota(jnp.int32, (SORT_CHUNK, LANES), 1)
    for r0 in range(0, LOCAL_ROWS, SORT_CHUNK):
        row = r0 + lax.broadcasted_iota(jnp.int32, (SORT_CHUNK, tt), 0)
        w = jnp.zeros((SORT_CHUNK, tt), F32)
        for k in range(TOP_K):
            w = jnp.where(row == slots[k], probs[k], w)
        onehot = jnp.where(w != 0.0, 1.0, 0.0).astype(BF16)
        local_ref[r0:r0 + SORT_CHUNK, 0:d] = _dot(onehot, u2).astype(BF16)
        wr = jnp.sum(w, axis=1, keepdims=True)
        hi = wr.astype(BF16).astype(F32)
        mid = (wr - hi).astype(BF16).astype(F32)
        lo = wr - hi - mid
        parts = jnp.where(lane == 0, hi, jnp.where(lane == 1, mid, jnp.where(lane == 2, lo, 0.0)))
        local_ref[r0:r0 + SORT_CHUNK, d:d + LANES] = parts.astype(BF16)

    def start_piece(loc, dst, rows):
        pltpu.make_async_copy(local_ref.at[pl.ds(loc, rows)], xs_hbm.at[pl.ds(dst, rows)], sem).start()

    rows_ref[buf] = _for_each_piece(tile, run_ref, dest_ref, len_ref, start_piece)

    @pl.when(tile > 0)
    def _():
        _wait_rows(local2_ref.at[1 - buf], xs_hbm, rows_ref[1 - buf], sem2.at[1 - buf])

    @pl.when(tile == pl.num_programs(0) - 1)
    def _():
        _wait_rows(local_ref, xs_hbm, rows_ref[buf], sem)
        zero_ref[...] = jnp.zeros_like(zero_ref)

        def tail_piece(e, p):
            dst = pl.multiple_of(tail_ref[e] + p * ROW_ALIGN, ROW_ALIGN)
            return pltpu.make_async_copy(zero_ref, xs_hbm.at[pl.ds(dst, ROW_ALIGN)], zsem)

        def start_tail(e, carry):
            return lax.fori_loop(0, ntail_ref[e], lambda p, c: (tail_piece(e, p).start(), c)[1], carry)

        def wait_tail(e, carry):
            return lax.fori_loop(0, ntail_ref[e], lambda p, c: (tail_piece(e, p).wait(), c)[1], carry)

        lax.fori_loop(0, N_EXPERTS, start_tail, 0)
        lax.fori_loop(0, N_EXPERTS, wait_tail, 0)

        def spare_piece(p):
            return pltpu.make_async_copy(zero_ref, xs_hbm.at[pl.ds(pl.multiple_of(p * ROW_ALIGN, ROW_ALIGN),
                                                                   ROW_ALIGN)], zsem)

        first_spare = nused_ref[0] * (MOE_BLOCK // ROW_ALIGN)
        end_spare = xs_hbm.shape[0] // ROW_ALIGN
        lax.fori_loop(first_spare, end_spare, lambda p, c: (spare_piece(p).start(), c)[1], 0)
        lax.fori_loop(first_spare, end_spare, lambda p, c: (spare_piece(p).wait(), c)[1], 0)


def _dispatch(tables, slot, prob, u2, n_rows):
    N, D = u2.shape
    tt = POST_TILE
    route_spec = pl.BlockSpec((1, TOP_K, tt), lambda i, *t: (i, 0, 0))
    grid_spec = pltpu.PrefetchScalarGridSpec(
        num_scalar_prefetch=len(tables),
        grid=(N // tt,),
        in_specs=[route_spec, route_spec, pl.BlockSpec((tt, D), lambda i, *t: (i, 0))],
        out_specs=pl.BlockSpec(memory_space=pl.ANY),
        scratch_shapes=[pltpu.VMEM((2, LOCAL_ROWS, D + LANES), BF16), pltpu.VMEM((ROW_ALIGN, D + LANES), BF16),
                        pltpu.SemaphoreType.DMA((2,)), pltpu.SemaphoreType.DMA(()),
                        pltpu.SMEM((2,), jnp.int32)],
    )
    return pl.pallas_call(
        _dispatch_kernel,
        grid_spec=grid_spec,
        out_shape=jax.ShapeDtypeStruct((n_rows, D + LANES), BF16),
        compiler_params=pltpu.CompilerParams(
            dimension_semantics=("arbitrary",), vmem_limit_bytes=VMEM_LIMIT_BYTES),
        name="dispatch",
    )(*tables, slot, prob, u2)


def _expert_kernel(be_ref, next_ref, live_ref, nused_ref, xs_ref, wg_hbm, bg_ref, wu_hbm, bu_ref, wd_hbm, bd_ref,
                   ys_ref, stage, wg_s, wu_s, wd_s, sem):
    i = pl.program_id(0)
    used = i < nused_ref[0]
    prev = be_ref[jnp.maximum(i - 1, 0)]
    fresh = (i == 0) | (be_ref[i] != prev)

    def fetch(e):
        return [pltpu.make_async_copy(w_hbm.at[e], stage.at[j], sem.at[j])
                for j, w_hbm in enumerate((wg_hbm, wu_hbm, wd_hbm))]

    @pl.when(i == 0)
    def _():
        for copy in fetch(be_ref[0]):
            copy.start()

    @pl.when(used & fresh)
    def _():
        for copy in fetch(be_ref[i]):
            copy.wait()
        wg_s[...] = stage[0].astype(BF16)
        wu_s[...] = stage[1].astype(BF16)
        wd_s[...] = stage[2].astype(BF16)

        @pl.when(next_ref[i] >= 0)
        def _():
            for copy in fetch(next_ref[i]):
                copy.start()

    d = ys_ref.shape[1]

    def expert(rows):
        xb = xs_ref[0:rows, 0:d]
        parts = xs_ref[0:rows, d:d + LANES].astype(F32)
        weight = parts[:, 0:1] + parts[:, 1:2] + parts[:, 2:3]
        g = _dot(xb, wg_s[...]) + bg_ref[0]
        up = _dot(xb, wu_s[...]) + bu_ref[0]
        g = jnp.minimum(g, SWIGLU_LIMIT)
        up = jnp.clip(up, -SWIGLU_LIMIT, SWIGLU_LIMIT)
        h = g * jax.nn.sigmoid(SWIGLU_ALPHA * g) * (up + 1.0)
        return ((_dot(h.astype(BF16), wd_s[...]) + bd_ref[0]) * weight).astype(BF16)

    for parts_live in range(1, MOE_BLOCK // EXPERT_PART + 1):
        rows = parts_live * EXPERT_PART

        @pl.when(used & (live_ref[i] == parts_live))
        def _():
            if rows == MOE_BLOCK:
                ys_ref[...] = expert(rows)
            else:
                ys_ref[0:rows, :] = expert(rows)
                ys_ref[rows:, :] = jnp.zeros((MOE_BLOCK - rows, d), BF16)

    @pl.when(jnp.logical_not(used))
    def _():
        ys_ref[...] = jnp.zeros_like(ys_ref)


def _experts(be, next_expert, live_parts, nused, xs, w_gate, b_gate, w_up, b_up, w_down, b_down):
    P, width = xs.shape
    D = width - LANES
    E = w_gate.shape[0]

    def live(i, nu):
        return jnp.maximum(jnp.minimum(i, nu[0] - 1), 0)

    w_spec = pl.BlockSpec(memory_space=pl.ANY)
    b_spec = pl.BlockSpec((1, 1, D), lambda i, be, nx, hf, nu: (be[live(i, nu)], 0, 0))
    grid_spec = pltpu.PrefetchScalarGridSpec(
        num_scalar_prefetch=4,
        grid=(P // MOE_BLOCK,),
        in_specs=[pl.BlockSpec((MOE_BLOCK, width), lambda i, be, nx, hf, nu: (live(i, nu), 0)),
                  w_spec, b_spec, w_spec, b_spec, w_spec, b_spec],
        out_specs=pl.BlockSpec((MOE_BLOCK, D), lambda i, be, nx, hf, nu: (i, 0)),
        scratch_shapes=[pltpu.VMEM((3, D, D), F32)] + [pltpu.VMEM((D, D), BF16)] * 3
                       + [pltpu.SemaphoreType.DMA((3,))],
    )
    return pl.pallas_call(
        _expert_kernel,
        grid_spec=grid_spec,
        out_shape=jax.ShapeDtypeStruct((P, D), BF16),
        compiler_params=pltpu.CompilerParams(
            dimension_semantics=("arbitrary",), vmem_limit_bytes=VMEM_LIMIT_BYTES),
        name="experts",
    )(be, next_expert, live_parts, nused, xs, w_gate, b_gate.reshape(E, 1, D), w_up, b_up.reshape(E, 1, D),
      w_down, b_down.reshape(E, 1, D))


def _combine_kernel(run_ref, dest_ref, len_ref,
                    slot_ref, x1_ref, ada_ref, g2_ref, b2_ref, ys_hbm,
                    out_ref, local2_ref, sem2, rows_ref):
    tile = pl.program_id(0)
    buf = tile % 2
    tt = x1_ref.shape[0]

    def fetch(t, b):
        def start_piece(loc, dst, rows):
            pltpu.make_async_copy(ys_hbm.at[pl.ds(dst, rows)], local2_ref.at[b, pl.ds(loc, rows)],
                                  sem2.at[b]).start()
        rows_ref[b] = _for_each_piece(t, run_ref, dest_ref, len_ref, start_piece)

    @pl.when(tile == 0)
    def _():
        local2_ref[...] = jnp.zeros_like(local2_ref)
        fetch(0, 0)

    @pl.when(tile + 1 < pl.num_programs(0))
    def _():
        fetch(tile + 1, 1 - buf)

    local_ref = local2_ref.at[buf]
    _wait_rows(ys_hbm, local_ref, rows_ref[buf], sem2.at[buf])
    slots = [slot_ref[:, k:k + 1] for k in range(TOP_K)]

    ffn = jnp.zeros((tt, x1_ref.shape[1]), F32)
    for r0 in range(0, LOCAL_ROWS, SORT_CHUNK):
        col = r0 + lax.broadcasted_iota(jnp.int32, (tt, SORT_CHUNK), 1)
        onehot = jnp.zeros((tt, SORT_CHUNK), F32)
        for k in range(TOP_K):
            onehot = jnp.where(col == slots[k], 1.0, onehot)
        ffn = ffn + _dot(onehot.astype(BF16), local_ref[r0:r0 + SORT_CHUNK, :])
    gate2 = ada_ref[0, 5:6, :]
    y = DN_ALPHA * x1_ref[...] + (1.0 + gate2) * ffn
    out_ref[...] = _layer_norm(y) * g2_ref[...] + b2_ref[...]


def _combine(tables, slot_tok, x1, ada, ln2_g, ln2_b, ys, tiles_per_batch):
    N, D = x1.shape
    tt = POST_TILE
    const = lambda i, *t: (0, 0)
    tok4 = pl.BlockSpec((tt, TOP_K), lambda i, *t: (i, 0))
    grid_spec = pltpu.PrefetchScalarGridSpec(
        num_scalar_prefetch=len(tables),
        grid=(N // tt,),
        in_specs=[tok4,
                  pl.BlockSpec((tt, D), lambda i, *t: (i, 0)),
                  pl.BlockSpec((1, 6, D), lambda i, *t: (i // tiles_per_batch, 0, 0)),
                  pl.BlockSpec((1, D), const),
                  pl.BlockSpec((1, D), const),
                  pl.BlockSpec(memory_space=pl.ANY)],
        out_specs=pl.BlockSpec((tt, D), lambda i, *t: (i, 0)),
        scratch_shapes=[pltpu.VMEM((2, LOCAL_ROWS, D), BF16), pltpu.SemaphoreType.DMA((2,)),
                        pltpu.SMEM((2,), jnp.int32)],
    )
    return pl.pallas_call(
        _combine_kernel,
        grid_spec=grid_spec,
        out_shape=jax.ShapeDtypeStruct((N, D), F32),
        compiler_params=pltpu.CompilerParams(
            dimension_semantics=("arbitrary",), vmem_limit_bytes=VMEM_LIMIT_BYTES),
        name="combine",
    )(*tables, slot_tok, x1, ada, ln2_g.reshape(1, D), ln2_b.reshape(1, D), ys)


def _routing_tables(cnt, nblk):
    nt, E = cnt.shape
    i32 = jnp.int32
    run = (cnt + ROW_ALIGN - 1) // ROW_ALIGN * ROW_ALIGN
    run_start = jnp.cumsum(run, axis=1) - run
    seg_len = jnp.sum(run, axis=0)
    seg_blocks = (seg_len + MOE_BLOCK - 1) // MOE_BLOCK
    b_end = jnp.cumsum(seg_blocks)
    seg_off = (b_end - seg_blocks) * MOE_BLOCK
    dest = seg_off[None, :] + jnp.cumsum(run, axis=0) - run
    nused = b_end[-1:].astype(i32)
    i = jnp.arange(nblk, dtype=i32)
    be = jnp.minimum(jnp.sum((i[:, None] >= b_end[None, :]).astype(i32), axis=1), E - 1)
    e_ids = jnp.arange(E, dtype=i32)
    later = (e_ids[None, :] > e_ids[:, None]) & (seg_blocks > 0)[None, :]
    next_e = jnp.min(jnp.where(later, e_ids[None, :], E), axis=1)
    next_e = jnp.where(next_e < E, next_e, -1)
    mine = be[:, None] == e_ids[None, :]
    pick = lambda v: jnp.sum(jnp.where(mine, v[None, :], 0), axis=1)
    nxt = pick(next_e)
    rows_left = pick(seg_len) - (i - pick(b_end - seg_blocks)) * MOE_BLOCK
    live_parts = jnp.clip((rows_left + EXPERT_PART - 1) // EXPERT_PART, 1, MOE_BLOCK // EXPERT_PART).astype(i32)
    tail = seg_off + seg_len
    ntail = (seg_blocks * MOE_BLOCK - seg_len) // ROW_ALIGN
    piece_tables = (run_start.reshape(-1).astype(i32), dest.reshape(-1).astype(i32),
                    run.reshape(-1).astype(i32))
    return piece_tables, (tail.astype(i32), ntail.astype(i32)), (be.astype(i32), nxt.astype(i32), live_parts, nused)


def kernel(x, c, positions, w_ada, b_ada, w_in, pool_w, pool_scale, w_pool_out, w_attn_out, w_o,
           ln1_g, ln1_b, w_router, b_router, w_gate, b_gate, w_up, b_up, w_down, b_down, ln2_g, ln2_b):
    B, S, D = x.shape
    N = B * S
    assert D == D_MODEL and S % PROJ_TILE == 0 and S % (16 * ATTN_BLOCK) == 0
    assert S % POST_TILE == 0 and LOCAL_ROWS % SORT_CHUNK == 0
    nt = N // POST_TILE
    nblk = (N * TOP_K + nt * N_EXPERTS * (ROW_ALIGN - 1)) // MOE_BLOCK + N_EXPERTS
    for l in range(DEPTH):
        ada = _ada(c, w_ada[l], b_ada[l])
        proj_out = _proj(x, positions, ada, w_in[l], pool_w[l], pool_scale[l], w_pool_out[l])
        qkv, (pg, sga) = proj_out[:9], proj_out[9:]
        attn_outs = [_attention(*qkv[3 * g:3 * g + 3]) for g in range(len(ATTN_GROUPS))]
        x1, u2, slot, prob, cnt = _post(attn_outs, pg, sga, x, ada, w_attn_out[l], w_o[l],
                                        ln1_g[l], ln1_b[l], w_router[l], b_router[l])
        piece_tables, tail_tables, block_tables = _routing_tables(cnt.reshape(nt, N_EXPERTS), nblk)
        xs = _dispatch(piece_tables + tail_tables + block_tables[3:], slot, prob, u2.reshape(N, D),
                       nblk * MOE_BLOCK)
        ys = _experts(*block_tables, xs, w_gate[l], b_gate[l], w_up[l], b_up[l], w_down[l], b_down[l])
        slot_tok = slot.transpose(0, 2, 1).reshape(N, TOP_K)
        out = _combine(piece_tables, slot_tok, x1.reshape(N, D), ada,
                       ln2_g[l], ln2_b[l], ys, S // POST_TILE)
        x = out.reshape(B, S, D)
    return x
```

```python
import functools

import jax
import jax.numpy as jnp
import numpy as np
from jax import lax
from jax.experimental import pallas as pl
from jax.experimental.pallas import tpu as pltpu

F32 = jnp.float32
BF16 = jnp.bfloat16

D_MODEL = 1024
POOL_WINDOWS = (2, 4, 8, 16)
POOL_WIDTH = D_MODEL // 2
POOL_GROUP = POOL_WIDTH // len(POOL_WINDOWS)
POOL_HALO = 16
HEAD_DIM = 64
ATTN_GROUPS = ((128, 1), (512, 4), (2048, 16))
HEADS_PER_GROUP = 4
GROUP_WIDTH = HEADS_PER_GROUP * HEAD_DIM
N_HEADS = HEADS_PER_GROUP * len(ATTN_GROUPS)
ATTN_WIDTH = N_HEADS * HEAD_DIM
ATTN_BLOCK = 128
ROT_DIM = HEAD_DIM // 4
ROPE_THETA = 500000.0
N_EXPERTS = 32
TOP_K = 4
SWIGLU_ALPHA = 1.702
SWIGLU_LIMIT = 7.0
MOE_BLOCK = 1024
EXPERT_PART = 128
DEPTH = 1
DN_ALPHA = (2.0 * DEPTH) ** 0.25
LN_EPS = 1e-5
NEG_INF = -1e30

OFF_Q = POOL_WIDTH
OFF_K = OFF_Q + ATTN_WIDTH
OFF_V = OFF_K + ATTN_WIDTH
OFF_GP = OFF_V + ATTN_WIDTH
OFF_GA = OFF_GP + D_MODEL
IN_WIDTH = OFF_GA + D_MODEL

VMEM_LIMIT_BYTES = 56 * 1024 * 1024
LANES = 128

PROJ_TILE = 512
POST_TILE = 512
ATTN_QROWS = 1024
ROW_ALIGN = 16
PIECE_ROWS = 64
ZERO_ROWS = 128
SORT_CHUNK = 512
LOCAL_ROWS = -(-(POST_TILE * TOP_K + N_EXPERTS * (ROW_ALIGN - 1)) // SORT_CHUNK) * SORT_CHUNK


def _layer_norm(x):
    mu = jnp.mean(x, axis=-1, keepdims=True)
    xc = x - mu
    var = jnp.mean(xc * xc, axis=-1, keepdims=True)
    return xc * lax.rsqrt(var + LN_EPS)


def _dot(a, b):
    return jnp.dot(a, b, preferred_element_type=F32)


def _ada_kernel(c_ref, w_ref, b_ref, o_ref):
    c = c_ref[...]
    s = c * jax.nn.sigmoid(c)
    o_ref[...] = jnp.dot(s, w_ref[...], preferred_element_type=F32,
                         precision=lax.Precision.HIGHEST) + b_ref[...]


def _ada(c, w_ada, b_ada):
    B, D = c.shape
    rows = 8
    c_pad = jnp.pad(c, ((0, rows - B), (0, 0)))
    n_out = w_ada.shape[1]
    out = pl.pallas_call(
        _ada_kernel,
        grid=(n_out // D,),
        in_specs=[pl.BlockSpec((rows, D), lambda j: (0, 0)),
                  pl.BlockSpec((D, D), lambda j: (0, j)),
                  pl.BlockSpec((1, D), lambda j: (0, j))],
        out_specs=pl.BlockSpec((rows, D), lambda j: (0, j)),
        out_shape=jax.ShapeDtypeStruct((rows, n_out), F32),
        name="ada",
    )(c_pad, w_ada, b_ada.reshape(1, n_out))
    return out[:B].reshape(B, 6, D)


ROPE_PART_ROWS = 32


def _rope_tables():
    lane = np.arange(LANES)
    li = lane % HEAD_DIM
    half = ROT_DIM // 2
    inv_freq = jnp.power(ROPE_THETA, -jnp.arange(half, dtype=F32) * (2.0 / ROT_DIM))
    invf = jnp.broadcast_to(inv_freq[:, None], (half, LANES))
    freq = np.arange(ROPE_PART_ROWS)[:, None]
    live = (freq < 3 * half) & (freq % half == (li % half)[None, :])
    place = np.stack([live & (li < ROT_DIM)[None, :],
                      -1.0 * (live & (li < half)[None, :]),
                      live & ((li >= half) & (li < ROT_DIM))[None, :]]).astype(np.float32)
    keep = (li >= ROT_DIM).astype(np.float32)[None, :]
    return invf, jnp.asarray(place, BF16), jnp.asarray(keep)


def _proj_kernel(x_ref, xh_ref, pos_ref, ada_ref, invf_ref, place_ref, keep_ref,
                 win_ref, poolw_ref, pscale_ref, wpo_ref,
                 q1_ref, k1_ref, v1_ref, q4_ref, k4_ref, v4_ref, q16_ref, k16_ref, v16_ref,
                 pg_ref, sga_ref, xpe_ref, cls_ref):
    tm = x_ref.shape[1]
    i = pl.program_id(1)
    shift1 = ada_ref[0, 0:1, :]
    scale1 = ada_ref[0, 1:2, :]

    def modulated(xv):
        return (_layer_norm(xv) * (1.0 + scale1) + shift1).astype(BF16)

    u = modulated(x_ref[0])
    uh = modulated(xh_ref[0])

    xp = _dot(u, win_ref[:, 0:POOL_WIDTH])
    xph = _dot(uh, win_ref[:, 0:POOL_WIDTH])
    xph = jnp.where(i > 0, xph, 0.0)
    xpe_ref[0:POOL_HALO, :] = xph
    xpe_ref[POOL_HALO:, :] = xp
    tok = i * tm + lax.broadcasted_iota(jnp.int32, (tm, 1), 0)
    ys = []
    for g, w in enumerate(POOL_WINDOWS):
        cols = slice(g * POOL_GROUP, (g + 1) * POOL_GROUP)
        xg = xp[:, cols]
        acc = xg
        for j in range(1, w):
            acc = acc + xpe_ref[POOL_HALO - j:POOL_HALO - j + tm, cols]
        cnt = jnp.minimum(tok + 1, w).astype(F32)
        mixed = (acc / cnt - xg).astype(BF16)
        ys.append(_dot(mixed, poolw_ref[g]) * pscale_ref[:, cols])
    y = jnp.concatenate(ys, axis=1).astype(BF16)
    pooled = _dot(y, wpo_ref[...])
    g_p = _dot(u, win_ref[:, OFF_GP:OFF_GP + D_MODEL])
    pg_ref[0] = jax.nn.sigmoid(g_p) * pooled
    g_a = _dot(u, win_ref[:, OFF_GA:OFF_GA + D_MODEL])
    sga_ref[0] = jax.nn.sigmoid(g_a)

    ang = invf_ref[:, 0:1] * pos_ref[0, 0].astype(F32)

    def on_lanes(table, j):
        hi = table.astype(BF16).astype(F32)
        mid = (table - hi).astype(BF16).astype(F32)
        lo = table - hi - mid
        pad = jnp.zeros((ROPE_PART_ROWS - 3 * table.shape[0], tm), F32)
        parts = jnp.concatenate([hi, mid, lo, pad], axis=0).astype(BF16)
        return lax.dot_general(parts, place_ref[j], (((0,), (0,)), ((), ())), preferred_element_type=F32)

    cos = jnp.cos(ang)
    sin = jnp.sin(ang)
    c_mul = on_lanes(cos, 0) + keep_ref[...]
    s_lo = on_lanes(sin, 1)
    s_hi = on_lanes(sin, 2)
    c_mul = jnp.concatenate([c_mul, c_mul], axis=1)
    s_lo = jnp.concatenate([s_lo, s_lo], axis=1)
    s_hi = jnp.concatenate([s_hi, s_hi], axis=1)
    half = ROT_DIM // 2

    def rotate(a):
        up = pltpu.roll(a, GROUP_WIDTH - half, axis=1)
        dn = pltpu.roll(a, half, axis=1)
        return a * c_mul + up * s_lo + dn * s_hi

    def emit(a, out_ref, dil):
        if dil == 1:
            out_ref[0, 0] = a.astype(BF16)
            return
        for c in range(GROUP_WIDTH // LANES):
            cls_ref[c] = a[:, c * LANES:(c + 1) * LANES]
        for r in range(dil):
            for c in range(GROUP_WIDTH // LANES):
                out_ref[0, r, :, c * LANES:(c + 1) * LANES] = (
                    cls_ref[c, pl.ds(r, tm // dil, stride=dil), :].astype(BF16))

    outs = ((q1_ref, k1_ref, v1_ref), (q4_ref, k4_ref, v4_ref), (q16_ref, k16_ref, v16_ref))
    for gi, (_, dil) in enumerate(ATTN_GROUPS):
        qo, ko, vo = outs[gi]
        c0 = gi * GROUP_WIDTH
        emit(rotate(_dot(u, win_ref[:, OFF_Q + c0:OFF_Q + c0 + GROUP_WIDTH])), qo, dil)
        emit(rotate(_dot(u, win_ref[:, OFF_K + c0:OFF_K + c0 + GROUP_WIDTH])), ko, dil)
        emit(_dot(u, win_ref[:, OFF_V + c0:OFF_V + c0 + GROUP_WIDTH]), vo, dil)


def _proj(x, positions, ada, w_in, pool_w, pool_scale, w_pool_out):
    B, S, D = x.shape
    tm = PROJ_TILE
    nt = S // tm
    halo_blocks = tm // POOL_HALO
    const2 = lambda b, i: (0, 0)
    in_specs = [
        pl.BlockSpec((1, tm, D), lambda b, i: (b, i, 0)),
        pl.BlockSpec((1, POOL_HALO, D), lambda b, i: (b, jnp.maximum(i * halo_blocks - 1, 0), 0)),
        pl.BlockSpec((1, 1, 1, tm), lambda b, i: (b, i, 0, 0)),
        pl.BlockSpec((1, 6, D), lambda b, i: (b, 0, 0)),
        pl.BlockSpec((ROT_DIM // 2, LANES), const2),
        pl.BlockSpec((3, ROPE_PART_ROWS, LANES), lambda b, i: (0, 0, 0)),
        pl.BlockSpec((1, LANES), const2),
        pl.BlockSpec((D, IN_WIDTH), const2),
        pl.BlockSpec((len(POOL_WINDOWS), POOL_GROUP, POOL_GROUP), lambda b, i: (0, 0, 0)),
        pl.BlockSpec((1, POOL_WIDTH), const2),
        pl.BlockSpec((POOL_WIDTH, D), const2),
    ]
    out_specs, out_shapes = [], []
    for _, dil in ATTN_GROUPS:
        for _ in range(3):
            out_specs.append(pl.BlockSpec((1, dil, tm // dil, GROUP_WIDTH), lambda b, i: (b, 0, i, 0)))
            out_shapes.append(jax.ShapeDtypeStruct((B, dil, S // dil, GROUP_WIDTH), BF16))
    for _ in range(2):
        out_specs.append(pl.BlockSpec((1, tm, D), lambda b, i: (b, i, 0)))
        out_shapes.append(jax.ShapeDtypeStruct((B, S, D), F32))
    return pl.pallas_call(
        _proj_kernel,
        grid=(B, nt),
        in_specs=in_specs,
        out_specs=out_specs,
        out_shape=out_shapes,
        scratch_shapes=[pltpu.VMEM((tm + POOL_HALO, POOL_WIDTH), F32),
                        pltpu.VMEM((GROUP_WIDTH // LANES, tm, LANES), F32)],
        compiler_params=pltpu.CompilerParams(
            dimension_semantics=("parallel", "parallel"), vmem_limit_bytes=VMEM_LIMIT_BYTES),
        name="proj",
    )(x, x, positions.reshape(B, nt, 1, tm), ada, *_rope_tables(), w_in.astype(BF16),
      pool_w.astype(BF16), pool_scale.reshape(1, POOL_WIDTH), w_pool_out.astype(BF16))


def _attn_kernel(q_ref, k_ref, v_ref, kh_ref, vh_ref, o_ref, lse_ref, kf_ref, vf_ref):
    qb = q_ref.shape[2]
    n = pl.program_id(2)
    kf_ref[0:ATTN_BLOCK, :] = kh_ref[0, 0]
    kf_ref[ATTN_BLOCK:, :] = k_ref[0, 0]
    vf_ref[0:ATTN_BLOCK, :] = vh_ref[0, 0]
    vf_ref[ATTN_BLOCK:, :] = v_ref[0, 0]
    qi = lax.broadcasted_iota(jnp.int32, (ATTN_BLOCK, 2 * ATTN_BLOCK), 0)
    kj = lax.broadcasted_iota(jnp.int32, (ATTN_BLOCK, 2 * ATTN_BLOCK), 1)
    band = (kj >= qi) & (kj <= qi + ATTN_BLOCK)
    band_bias = jnp.where(band, 0.0, NEG_INF)
    lane = lax.broadcasted_iota(jnp.int32, (ATTN_BLOCK, GROUP_WIDTH), 1)
    low_lanes = lax.broadcasted_iota(jnp.int32, (ATTN_BLOCK, LANES), 1) < HEAD_DIM
    ones = jnp.ones((2 * ATTN_BLOCK, LANES), BF16)
    nh, blk = HEADS_PER_GROUP, ATTN_BLOCK

    def block(j, carry):
        r0 = pl.multiple_of(j * ATTN_BLOCK, ATTN_BLOCK)
        first_key = jnp.where((n > 0) | (j > 0), 0, ATTN_BLOCK)
        bias = band_bias + jnp.where(kj < first_key, NEG_INF, 0.0)
        q = q_ref[0, 0, pl.ds(r0, blk), :].astype(F32)
        kk = kf_ref[pl.ds(r0, 2 * blk), :]
        vv = vf_ref[pl.ds(r0, 2 * blk), :]
        qs = jnp.concatenate([jnp.where((lane >= h * HEAD_DIM) & (lane < (h + 1) * HEAD_DIM), q, 0.0)
                              for h in range(nh)], axis=0).astype(BF16)
        s = lax.dot_general(qs, kk, (((1,), (1,)), ((), ())), preferred_element_type=F32)
        s = jnp.concatenate([s[h * blk:(h + 1) * blk] * (HEAD_DIM ** -0.5) + bias for h in range(nh)], axis=0)
        m = jnp.max(s, axis=-1, keepdims=True)
        p = jnp.exp(s - m).astype(BF16)
        den = _dot(p, ones)
        lse = m + jnp.log(den)
        for hp in range(GROUP_WIDTH // LANES):
            rows = slice(2 * hp * blk, (2 * hp + 2) * blk)
            ls = slice(hp * LANES, (hp + 1) * LANES)
            o2 = _dot(p[rows], vv[:, ls]) / den[rows]
            l2 = lse[rows]
            o_ref[0, 0, pl.ds(r0, blk), ls] = jnp.where(low_lanes, o2[0:blk], o2[blk:2 * blk])
            lse_ref[0, 0, pl.ds(r0, blk), ls] = jnp.where(low_lanes, l2[0:blk], l2[blk:2 * blk])
        return carry

    lax.fori_loop(0, qb // ATTN_BLOCK, block, 0, unroll=8)


def _attention(q, k, v):
    B, dil, L, W = q.shape
    qb = min(L, ATTN_QROWS)
    per = qb // ATTN_BLOCK
    main = pl.BlockSpec((1, 1, qb, W), lambda b, r, n: (b, r, n, 0))
    halo = pl.BlockSpec((1, 1, ATTN_BLOCK, W), lambda b, r, n: (b, r, jnp.maximum(n * per - 1, 0), 0))
    return pl.pallas_call(
        _attn_kernel,
        grid=(B, dil, L // qb),
        in_specs=[main, main, main, halo, halo],
        out_specs=[main, main],
        out_shape=[jax.ShapeDtypeStruct((B, dil, L, W), F32)] * 2,
        scratch_shapes=[pltpu.VMEM((qb + ATTN_BLOCK, W), BF16)] * 2,
        compiler_params=pltpu.CompilerParams(
            dimension_semantics=("parallel", "parallel", "parallel"), vmem_limit_bytes=VMEM_LIMIT_BYTES),
        name=f"attn_d{dil}",
    )(q, k, v, k, v)


def _post_kernel(o1_ref, l1_ref, o4_ref, l4_ref, o16_ref, l16_ref, pg_ref, sga_ref, x_ref, ada_ref,
                 wao_ref, wo_ref, g1_ref, b1_ref, wrt_ref, brt_ref,
                 x1_ref, u2_ref, slot_ref, prob_ref, cnt_ref,
                 s0, s1, s2, s3):
    tm = x_ref.shape[1]

    def token_major(src_ref, scr_ref, dil):
        if dil == 1:
            return src_ref[0, 0]
        for r in range(dil):
            for c in range(GROUP_WIDTH // LANES):
                scr_ref[c, pl.ds(r, tm // dil, stride=dil), :] = src_ref[0, r, :, c * LANES:(c + 1) * LANES]
        return jnp.concatenate([scr_ref[c] for c in range(GROUP_WIDTH // LANES)], axis=1)

    o1, l1 = o1_ref[0, 0], l1_ref[0, 0]
    o4, l4 = token_major(o4_ref, s0, 4), token_major(l4_ref, s1, 4)
    o16, l16 = token_major(o16_ref, s2, 16), token_major(l16_ref, s3, 16)
    mx = jnp.maximum(jnp.maximum(l1, l4), l16)
    e1, e4, e16 = jnp.exp(l1 - mx), jnp.exp(l4 - mx), jnp.exp(l16 - mx)
    attn = (e1 * o1 + e4 * o4 + e16 * o16) / (e1 + e4 + e16)

    merged = pg_ref[0] + sga_ref[0] * _dot(attn.astype(BF16), wao_ref[...])
    mix = _dot(merged.astype(BF16), wo_ref[...])
    gate1 = ada_ref[0, 2:3, :]
    shift2 = ada_ref[0, 3:4, :]
    scale2 = ada_ref[0, 4:5, :]
    x1 = _layer_norm(DN_ALPHA * x_ref[0] + (1.0 + gate1) * mix) * g1_ref[...] + b1_ref[...]
    x1_ref[0] = x1
    u2 = _layer_norm(x1) * (1.0 + scale2) + shift2
    u2_hi = u2.astype(BF16)
    u2_ref[0] = u2_hi

    u2_lo = (u2 - u2_hi.astype(F32)).astype(BF16)
    nt_dot = lambda a, b: lax.dot_general(a, b, (((1,), (1,)), ((), ())), preferred_element_type=F32)
    logits = (nt_dot(wrt_ref[0], u2_hi) + nt_dot(wrt_ref[0], u2_lo) + nt_dot(wrt_ref[1], u2_hi)
              + brt_ref[...])
    eidx = lax.broadcasted_iota(jnp.int32, (N_EXPERTS, tm), 0)
    work = logits
    vals, idxs = [], []
    for _ in range(TOP_K):
        m = jnp.max(work, axis=0, keepdims=True)
        idx = jnp.min(jnp.where(work == m, eidx, N_EXPERTS), axis=0, keepdims=True)
        vals.append(m)
        idxs.append(idx)
        work = jnp.where(eidx == idx, -jnp.inf, work)
    exps = [jnp.exp(vk - vals[0]) for vk in vals]
    tot = exps[0] + exps[1] + exps[2] + exps[3]
    sel = jnp.zeros((N_EXPERTS, tm), F32)
    for idx in idxs:
        sel = sel + (eidx == idx).astype(F32)
    tr = lax.broadcasted_iota(jnp.int32, (tm, tm), 0)
    tc = lax.broadcasted_iota(jnp.int32, (tm, tm), 1)
    rank = _dot(sel.astype(BF16), (tr < tc).astype(BF16))
    cnt = jnp.sum(sel, axis=1, keepdims=True)
    run = jnp.floor((cnt + (ROW_ALIGN - 1)) * (1.0 / ROW_ALIGN)) * ROW_ALIGN
    er = lax.broadcasted_iota(jnp.int32, (N_EXPERTS, N_EXPERTS), 0)
    ec = lax.broadcasted_iota(jnp.int32, (N_EXPERTS, N_EXPERTS), 1)
    run_start = _dot((ec < er).astype(BF16),
                     jnp.broadcast_to(run, (N_EXPERTS, LANES)).astype(BF16))[:, 0:1]
    slot = rank + run_start
    for k in range(TOP_K):
        slot_ref[0, k:k + 1, :] = jnp.sum(jnp.where(eidx == idxs[k], slot, 0.0), axis=0,
                                          keepdims=True).astype(jnp.int32)
        prob_ref[0, k:k + 1, :] = exps[k] / tot
    cnt_ref[0] = cnt.astype(jnp.int32)


def _post(attn_outs, pg, sga, x, ada, w_attn_out, w_o, ln1_g, ln1_b, w_router, b_router):
    B, S, D = x.shape
    tm = POST_TILE
    nt = S // tm
    N = B * S
    const2 = lambda b, i: (0, 0)
    in_specs, args = [], []
    for (o, lse), (_, dil) in zip(attn_outs, ATTN_GROUPS):
        spec = pl.BlockSpec((1, dil, tm // dil, GROUP_WIDTH), lambda b, i: (b, 0, i, 0))
        in_specs += [spec, spec]
        args += [o, lse]
    tok_spec = pl.BlockSpec((1, tm, D), lambda b, i: (b, i, 0))
    in_specs += [tok_spec, tok_spec, tok_spec,
                 pl.BlockSpec((1, 6, D), lambda b, i: (b, 0, 0)),
                 pl.BlockSpec((GROUP_WIDTH, D), const2),
                 pl.BlockSpec((D, D), const2),
                 pl.BlockSpec((1, D), const2),
                 pl.BlockSpec((1, D), const2),
                 pl.BlockSpec((2, N_EXPERTS, D), lambda b, i: (0, 0, 0)),
                 pl.BlockSpec((N_EXPERTS, 1), const2)]
    wr_hi = w_router.T.astype(BF16)
    wr_lo = (w_router.T - wr_hi.astype(F32)).astype(BF16)
    args += [pg, sga, x, ada, w_attn_out.astype(BF16), w_o.astype(BF16),
             ln1_g.reshape(1, D), ln1_b.reshape(1, D), jnp.stack([wr_hi, wr_lo]),
             b_router.reshape(N_EXPERTS, 1)]
    nc = N // tm
    route_spec = pl.BlockSpec((1, TOP_K, tm), lambda b, i: (b * nt + i, 0, 0))
    out_specs = [tok_spec, tok_spec, route_spec, route_spec,
                 pl.BlockSpec((1, N_EXPERTS, 1), lambda b, i: (b * nt + i, 0, 0))]
    out_shapes = [jax.ShapeDtypeStruct((B, S, D), F32), jax.ShapeDtypeStruct((B, S, D), BF16),
                  jax.ShapeDtypeStruct((nc, TOP_K, tm), jnp.int32), jax.ShapeDtypeStruct((nc, TOP_K, tm), F32),
                  jax.ShapeDtypeStruct((nc, N_EXPERTS, 1), jnp.int32)]
    return pl.pallas_call(
        _post_kernel,
        grid=(B, nt),
        in_specs=in_specs,
        out_specs=out_specs,
        out_shape=out_shapes,
        scratch_shapes=[pltpu.VMEM((GROUP_WIDTH // LANES, tm, LANES), F32)] * 4,
        compiler_params=pltpu.CompilerParams(
            dimension_semantics=("parallel", "parallel"), vmem_limit_bytes=VMEM_LIMIT_BYTES),
        name="post",
    )(*args)


def _for_each_piece(tile, run_ref, dest_ref, len_ref, fn):
    def per_expert(e, total):
        idx = tile * N_EXPERTS + e
        loc, dst, length = run_ref[idx], dest_ref[idx], len_ref[idx]
        n_big = length // PIECE_ROWS
        rest = n_big * PIECE_ROWS

        def big(p, c):
            off = p * PIECE_ROWS
            fn(pl.multiple_of(loc + off, ROW_ALIGN), pl.multiple_of(dst + off, ROW_ALIGN), PIECE_ROWS)
            return c

        def small(p, c):
            off = rest + p * ROW_ALIGN
            fn(pl.multiple_of(loc + off, ROW_ALIGN), pl.multiple_of(dst + off, ROW_ALIGN), ROW_ALIGN)
            return c

        lax.fori_loop(0, n_big, big, 0)
        lax.fori_loop(0, (length - rest) // ROW_ALIGN, small, 0)
        return total + length

    return lax.fori_loop(0, N_EXPERTS, per_expert, 0)


def _wait_rows(src_ref, dst_ref, rows, sem):
    @pl.when(rows > 0)
    def _():
        n = pl.multiple_of(rows, ROW_ALIGN)
        pltpu.make_async_copy(src_ref.at[pl.ds(0, n)], dst_ref.at[pl.ds(0, n)], sem).wait()


def _dispatch_kernel(run_ref, dest_ref, len_ref, tail_ref, ntail_ref, nused_ref,
                     slot_ref, prob_ref, u2_ref, xs_hbm, local2_ref, zero_ref, sem2, zsem, rows_ref):
    tile = pl.program_id(0)
    buf = tile % 2
    local_ref = local2_ref.at[buf]
    sem = sem2.at[buf]
    tt, d = u2_ref.shape
    u2 = u2_ref[...]
    slots = [slot_ref[0, k:k + 1, :] for k in range(TOP_K)]
    probs = [prob_ref[0, k:k + 1, :] for k in range(TOP_K)]
    lane = lax.broadcasted_iota(jnp.int32, (SORT_CHUNK, LANES), 1)
    for r0 in range(0, LOCAL_ROWS, SORT_CHUNK):
        row = r0 + lax.broadcasted_iota(jnp.int32, (SORT_CHUNK, tt), 0)
        w = jnp.zeros((SORT_CHUNK, tt), F32)
        for k in range(TOP_K):
            w = jnp.where(row == slots[k], probs[k], w)
        onehot = jnp.where(w != 0.0, 1.0, 0.0).astype(BF16)
        local_ref[r0:r0 + SORT_CHUNK, 0:d] = _dot(onehot, u2).astype(BF16)
        wr = jnp.sum(w, axis=1, keepdims=True)
        hi = wr.astype(BF16).astype(F32)
        mid = (wr - hi).astype(BF16).astype(F32)
        lo = wr - hi - mid
        parts = jnp.where(lane == 0, hi, jnp.where(lane == 1, mid, jnp.where(lane == 2, lo, 0.0)))
        local_ref[r0:r0 + SORT_CHUNK, d:d + LANES] = parts.astype(BF16)

    def start_piece(loc, dst, rows):
        pltpu.make_async_copy(local_ref.at[pl.ds(loc, rows)], xs_hbm.at[pl.ds(dst, rows)], sem).start()

    rows_ref[buf] = _for_each_piece(tile, run_ref, dest_ref, len_ref, start_piece)

    @pl.when(tile > 0)
    def _():
        _wait_rows(local2_ref.at[1 - buf], xs_hbm, rows_ref[1 - buf], sem2.at[1 - buf])

    @pl.when(tile == pl.num_programs(0) - 1)
    def _():
        _wait_rows(local_ref, xs_hbm, rows_ref[buf], sem)
        zero_ref[...] = jnp.zeros_like(zero_ref)

        def zero_fill(first_row, n_rows, act):
            def copy(row, rows):
                return pltpu.make_async_copy(zero_ref.at[pl.ds(0, rows)],
                                             xs_hbm.at[pl.ds(pl.multiple_of(row, ROW_ALIGN), rows)], zsem)
            n_big = n_rows // ZERO_ROWS
            rest = first_row + n_big * ZERO_ROWS
            lax.fori_loop(0, n_big, lambda p, c: (act(copy(first_row + p * ZERO_ROWS, ZERO_ROWS)), c)[1], 0)
            lax.fori_loop(0, (n_rows - n_big * ZERO_ROWS) // ROW_ALIGN,
                          lambda p, c: (act(copy(rest + p * ROW_ALIGN, ROW_ALIGN)), c)[1], 0)

        first_spare = nused_ref[0] * MOE_BLOCK
        for act in (lambda c: c.start(), lambda c: c.wait()):
            lax.fori_loop(0, N_EXPERTS, lambda e, c: (zero_fill(tail_ref[e], ntail_ref[e], act), c)[1], 0)
            zero_fill(first_spare, xs_hbm.shape[0] - first_spare, act)


def _dispatch(tables, slot, prob, u2, n_rows):
    N, D = u2.shape
    tt = POST_TILE
    route_spec = pl.BlockSpec((1, TOP_K, tt), lambda i, *t: (i, 0, 0))
    grid_spec = pltpu.PrefetchScalarGridSpec(
        num_scalar_prefetch=len(tables),
        grid=(N // tt,),
        in_specs=[route_spec, route_spec, pl.BlockSpec((tt, D), lambda i, *t: (i, 0))],
        out_specs=pl.BlockSpec(memory_space=pl.ANY),
        scratch_shapes=[pltpu.VMEM((2, LOCAL_ROWS, D + LANES), BF16), pltpu.VMEM((ZERO_ROWS, D + LANES), BF16),
                        pltpu.SemaphoreType.DMA((2,)), pltpu.SemaphoreType.DMA(()),
                        pltpu.SMEM((2,), jnp.int32)],
    )
    return pl.pallas_call(
        _dispatch_kernel,
        grid_spec=grid_spec,
        out_shape=jax.ShapeDtypeStruct((n_rows, D + LANES), BF16),
        compiler_params=pltpu.CompilerParams(
            dimension_semantics=("arbitrary",), vmem_limit_bytes=VMEM_LIMIT_BYTES),
        name="dispatch",
    )(*tables, slot, prob, u2)


def _expert_kernel(be_ref, next_ref, live_ref, nused_ref, xs_ref, wg_hbm, bg_ref, wu_hbm, bu_ref, wd_hbm, bd_ref,
                   ys_ref, stage, wg_s, wu_s, wd_s, sem):
    i = pl.program_id(0)
    used = i < nused_ref[0]
    prev = be_ref[jnp.maximum(i - 1, 0)]
    fresh = (i == 0) | (be_ref[i] != prev)

    def fetch(e):
        return [pltpu.make_async_copy(w_hbm.at[e], stage.at[j], sem.at[j])
                for j, w_hbm in enumerate((wg_hbm, wu_hbm, wd_hbm))]

    @pl.when(i == 0)
    def _():
        for copy in fetch(be_ref[0]):
            copy.start()

    @pl.when(used & fresh)
    def _():
        for copy in fetch(be_ref[i]):
            copy.wait()
        wg_s[...] = stage[0].astype(BF16)
        wu_s[...] = stage[1].astype(BF16)
        wd_s[...] = stage[2].astype(BF16)

        @pl.when(next_ref[i] >= 0)
        def _():
            for copy in fetch(next_ref[i]):
                copy.start()

    d = ys_ref.shape[1]

    def expert(rows):
        xb = xs_ref[0:rows, 0:d]
        parts = xs_ref[0:rows, d:d + LANES].astype(F32)
        weight = parts[:, 0:1] + parts[:, 1:2] + parts[:, 2:3]
        g = _dot(xb, wg_s[...]) + bg_ref[0]
        up = _dot(xb, wu_s[...]) + bu_ref[0]
        g = jnp.minimum(g, SWIGLU_LIMIT)
        up = jnp.clip(up, -SWIGLU_LIMIT, SWIGLU_LIMIT)
        h = g * jax.nn.sigmoid(SWIGLU_ALPHA * g) * (up + 1.0)
        return ((_dot(h.astype(BF16), wd_s[...]) + bd_ref[0]) * weight).astype(BF16)

    for parts_live in range(1, MOE_BLOCK // EXPERT_PART + 1):
        rows = parts_live * EXPERT_PART

        @pl.when(used & (live_ref[i] == parts_live))
        def _():
            if rows == MOE_BLOCK:
                ys_ref[...] = expert(rows)
            else:
                ys_ref[0:rows, :] = expert(rows)
                ys_ref[rows:, :] = jnp.zeros((MOE_BLOCK - rows, d), BF16)

    @pl.when(jnp.logical_not(used))
    def _():
        ys_ref[...] = jnp.zeros_like(ys_ref)


def _experts(be, next_expert, live_parts, nused, xs, w_gate, b_gate, w_up, b_up, w_down, b_down):
    P, width = xs.shape
    D = width - LANES
    E = w_gate.shape[0]

    def live(i, nu):
        return jnp.maximum(jnp.minimum(i, nu[0] - 1), 0)

    w_spec = pl.BlockSpec(memory_space=pl.ANY)
    b_spec = pl.BlockSpec((1, 1, D), lambda i, be, nx, hf, nu: (be[live(i, nu)], 0, 0))
    grid_spec = pltpu.PrefetchScalarGridSpec(
        num_scalar_prefetch=4,
        grid=(P // MOE_BLOCK,),
        in_specs=[pl.BlockSpec((MOE_BLOCK, width), lambda i, be, nx, hf, nu: (live(i, nu), 0)),
                  w_spec, b_spec, w_spec, b_spec, w_spec, b_spec],
        out_specs=pl.BlockSpec((MOE_BLOCK, D), lambda i, be, nx, hf, nu: (i, 0)),
        scratch_shapes=[pltpu.VMEM((3, D, D), F32)] + [pltpu.VMEM((D, D), BF16)] * 3
                       + [pltpu.SemaphoreType.DMA((3,))],
    )
    return pl.pallas_call(
        _expert_kernel,
        grid_spec=grid_spec,
        out_shape=jax.ShapeDtypeStruct((P, D), BF16),
        compiler_params=pltpu.CompilerParams(
            dimension_semantics=("arbitrary",), vmem_limit_bytes=VMEM_LIMIT_BYTES),
        name="experts",
    )(be, next_expert, live_parts, nused, xs, w_gate, b_gate.reshape(E, 1, D), w_up, b_up.reshape(E, 1, D),
      w_down, b_down.reshape(E, 1, D))


def _combine_kernel(run_ref, dest_ref, len_ref,
                    slot_ref, x1_ref, ada_ref, g2_ref, b2_ref, ys_hbm,
                    out_ref, local2_ref, sem2, rows_ref):
    tile = pl.program_id(0)
    buf = tile % 2
    tt = x1_ref.shape[0]

    def fetch(t, b):
        def start_piece(loc, dst, rows):
            pltpu.make_async_copy(ys_hbm.at[pl.ds(dst, rows)], local2_ref.at[b, pl.ds(loc, rows)],
                                  sem2.at[b]).start()
        rows_ref[b] = _for_each_piece(t, run_ref, dest_ref, len_ref, start_piece)

    @pl.when(tile == 0)
    def _():
        local2_ref[...] = jnp.zeros_like(local2_ref)
        fetch(0, 0)

    @pl.when(tile + 1 < pl.num_programs(0))
    def _():
        fetch(tile + 1, 1 - buf)

    local_ref = local2_ref.at[buf]
    _wait_rows(ys_hbm, local_ref, rows_ref[buf], sem2.at[buf])
    slots = [slot_ref[:, k:k + 1] for k in range(TOP_K)]

    ffn = jnp.zeros((tt, x1_ref.shape[1]), F32)
    for r0 in range(0, LOCAL_ROWS, SORT_CHUNK):
        col = r0 + lax.broadcasted_iota(jnp.int32, (tt, SORT_CHUNK), 1)
        onehot = jnp.zeros((tt, SORT_CHUNK), F32)
        for k in range(TOP_K):
            onehot = jnp.where(col == slots[k], 1.0, onehot)
        ffn = ffn + _dot(onehot.astype(BF16), local_ref[r0:r0 + SORT_CHUNK, :])
    gate2 = ada_ref[0, 5:6, :]
    y = DN_ALPHA * x1_ref[...] + (1.0 + gate2) * ffn
    out_ref[...] = _layer_norm(y) * g2_ref[...] + b2_ref[...]


def _combine(tables, slot_tok, x1, ada, ln2_g, ln2_b, ys, tiles_per_batch):
    N, D = x1.shape
    tt = POST_TILE
    const = lambda i, *t: (0, 0)
    tok4 = pl.BlockSpec((tt, TOP_K), lambda i, *t: (i, 0))
    grid_spec = pltpu.PrefetchScalarGridSpec(
        num_scalar_prefetch=len(tables),
        grid=(N // tt,),
        in_specs=[tok4,
                  pl.BlockSpec((tt, D), lambda i, *t: (i, 0)),
                  pl.BlockSpec((1, 6, D), lambda i, *t: (i // tiles_per_batch, 0, 0)),
                  pl.BlockSpec((1, D), const),
                  pl.BlockSpec((1, D), const),
                  pl.BlockSpec(memory_space=pl.ANY)],
        out_specs=pl.BlockSpec((tt, D), lambda i, *t: (i, 0)),
        scratch_shapes=[pltpu.VMEM((2, LOCAL_ROWS, D), BF16), pltpu.SemaphoreType.DMA((2,)),
                        pltpu.SMEM((2,), jnp.int32)],
    )
    return pl.pallas_call(
        _combine_kernel,
        grid_spec=grid_spec,
        out_shape=jax.ShapeDtypeStruct((N, D), F32),
        compiler_params=pltpu.CompilerParams(
            dimension_semantics=("arbitrary",), vmem_limit_bytes=VMEM_LIMIT_BYTES),
        name="combine",
    )(*tables, slot_tok, x1, ada, ln2_g.reshape(1, D), ln2_b.reshape(1, D), ys)


def _routing_tables(cnt, nblk):
    nt, E = cnt.shape
    i32 = jnp.int32
    run = (cnt + ROW_ALIGN - 1) // ROW_ALIGN * ROW_ALIGN
    run_start = jnp.cumsum(run, axis=1) - run
    seg_len = jnp.sum(run, axis=0)
    seg_blocks = (seg_len + MOE_BLOCK - 1) // MOE_BLOCK
    b_end = jnp.cumsum(seg_blocks)
    seg_off = (b_end - seg_blocks) * MOE_BLOCK
    dest = seg_off[None, :] + jnp.cumsum(run, axis=0) - run
    nused = b_end[-1:].astype(i32)
    i = jnp.arange(nblk, dtype=i32)
    be = jnp.minimum(jnp.sum((i[:, None] >= b_end[None, :]).astype(i32), axis=1), E - 1)
    e_ids = jnp.arange(E, dtype=i32)
    later = (e_ids[None, :] > e_ids[:, None]) & (seg_blocks > 0)[None, :]
    next_e = jnp.min(jnp.where(later, e_ids[None, :], E), axis=1)
    next_e = jnp.where(next_e < E, next_e, -1)
    mine = be[:, None] == e_ids[None, :]
    pick = lambda v: jnp.sum(jnp.where(mine, v[None, :], 0), axis=1)
    nxt = pick(next_e)
    rows_left = pick(seg_len) - (i - pick(b_end - seg_blocks)) * MOE_BLOCK
    live_parts = jnp.clip((rows_left + EXPERT_PART - 1) // EXPERT_PART, 1, MOE_BLOCK // EXPERT_PART).astype(i32)
    tail = seg_off + seg_len
    ntail = seg_blocks * MOE_BLOCK - seg_len
    piece_tables = (run_start.reshape(-1).astype(i32), dest.reshape(-1).astype(i32),
                    run.reshape(-1).astype(i32))
    return piece_tables, (tail.astype(i32), ntail.astype(i32)), (be.astype(i32), nxt.astype(i32), live_parts, nused)


def kernel(x, c, positions, w_ada, b_ada, w_in, pool_w, pool_scale, w_pool_out, w_attn_out, w_o,
           ln1_g, ln1_b, w_router, b_router, w_gate, b_gate, w_up, b_up, w_down, b_down, ln2_g, ln2_b):
    B, S, D = x.shape
    N = B * S
    assert D == D_MODEL and S % PROJ_TILE == 0 and S % (16 * ATTN_BLOCK) == 0
    assert S % POST_TILE == 0 and LOCAL_ROWS % SORT_CHUNK == 0
    nt = N // POST_TILE
    nblk = (N * TOP_K + nt * N_EXPERTS * (ROW_ALIGN - 1)) // MOE_BLOCK + N_EXPERTS
    for l in range(DEPTH):
        ada = _ada(c, w_ada[l], b_ada[l])
        proj_out = _proj(x, positions, ada, w_in[l], pool_w[l], pool_scale[l], w_pool_out[l])
        qkv, (pg, sga) = proj_out[:9], proj_out[9:]
        attn_outs = [_attention(*qkv[3 * g:3 * g + 3]) for g in range(len(ATTN_GROUPS))]
        x1, u2, slot, prob, cnt = _post(attn_outs, pg, sga, x, ada, w_attn_out[l], w_o[l],
                                        ln1_g[l], ln1_b[l], w_router[l], b_router[l])
        piece_tables, tail_tables, block_tables = _routing_tables(cnt.reshape(nt, N_EXPERTS), nblk)
        xs = _dispatch(piece_tables + tail_tables + block_tables[3:], slot, prob, u2.reshape(N, D),
                       nblk * MOE_BLOCK)
        ys = _experts(*block_tables, xs, w_gate[l], b_gate[l], w_up[l], b_up[l], w_down[l], b_down[l])
        slot_tok = slot.transpose(0, 2, 1).reshape(N, TOP_K)
        out = _combine(piece_tables, slot_tok, x1.reshape(N, D), ada,
                       ln2_g[l], ln2_b[l], ys, S // POST_TILE)
        x = out.reshape(B, S, D)
    return x
```

```python
import functools

import jax
import jax.numpy as jnp
import numpy as np
from jax import lax
from jax.experimental import pallas as pl
from jax.experimental.pallas import tpu as pltpu

F32 = jnp.float32
BF16 = jnp.bfloat16

D_MODEL = 1024
POOL_WINDOWS = (2, 4, 8, 16)
POOL_WIDTH = D_MODEL // 2
POOL_GROUP = POOL_WIDTH // len(POOL_WINDOWS)
POOL_HALO = 16
HEAD_DIM = 64
ATTN_GROUPS = ((128, 1), (512, 4), (2048, 16))
HEADS_PER_GROUP = 4
GROUP_WIDTH = HEADS_PER_GROUP * HEAD_DIM
N_HEADS = HEADS_PER_GROUP * len(ATTN_GROUPS)
ATTN_WIDTH = N_HEADS * HEAD_DIM
ATTN_BLOCK = 128
ROT_DIM = HEAD_DIM // 4
ROPE_THETA = 500000.0
N_EXPERTS = 32
TOP_K = 4
SWIGLU_ALPHA = 1.702
SWIGLU_LIMIT = 7.0
MOE_BLOCK = 1024
EXPERT_PART = 128
DEPTH = 1
DN_ALPHA = (2.0 * DEPTH) ** 0.25
LN_EPS = 1e-5
NEG_INF = -1e30

OFF_Q = POOL_WIDTH
OFF_K = OFF_Q + ATTN_WIDTH
OFF_V = OFF_K + ATTN_WIDTH
OFF_GP = OFF_V + ATTN_WIDTH
OFF_GA = OFF_GP + D_MODEL
IN_WIDTH = OFF_GA + D_MODEL

VMEM_LIMIT_BYTES = 56 * 1024 * 1024
LANES = 128

PROJ_TILE = 512
POST_TILE = 512
ATTN_QROWS = 1024
ROW_ALIGN = 16
PIECE_ROWS = 64
ZERO_ROWS = 128
SORT_CHUNK = 512
LOCAL_ROWS = -(-(POST_TILE * TOP_K + N_EXPERTS * (ROW_ALIGN - 1)) // SORT_CHUNK) * SORT_CHUNK


def _layer_norm(x):
    mu = jnp.mean(x, axis=-1, keepdims=True)
    xc = x - mu
    var = jnp.mean(xc * xc, axis=-1, keepdims=True)
    return xc * lax.rsqrt(var + LN_EPS)


def _dot(a, b):
    return jnp.dot(a, b, preferred_element_type=F32)


def _ada_kernel(c_ref, w_ref, b_ref, o_ref):
    c = c_ref[...]
    s = c * jax.nn.sigmoid(c)
    o_ref[...] = jnp.dot(s, w_ref[...], preferred_element_type=F32,
                         precision=lax.Precision.HIGHEST) + b_ref[...]


def _ada(c, w_ada, b_ada):
    B, D = c.shape
    rows = 8
    c_pad = jnp.pad(c, ((0, rows - B), (0, 0)))
    n_out = w_ada.shape[1]
    out = pl.pallas_call(
        _ada_kernel,
        grid=(n_out // D,),
        in_specs=[pl.BlockSpec((rows, D), lambda j: (0, 0)),
                  pl.BlockSpec((D, D), lambda j: (0, j)),
                  pl.BlockSpec((1, D), lambda j: (0, j))],
        out_specs=pl.BlockSpec((rows, D), lambda j: (0, j)),
        out_shape=jax.ShapeDtypeStruct((rows, n_out), F32),
        name="ada",
    )(c_pad, w_ada, b_ada.reshape(1, n_out))
    return out[:B].reshape(B, 6, D)


ROPE_PART_ROWS = 32


def _rope_tables():
    lane = np.arange(LANES)
    li = lane % HEAD_DIM
    half = ROT_DIM // 2
    inv_freq = jnp.power(ROPE_THETA, -jnp.arange(half, dtype=F32) * (2.0 / ROT_DIM))
    invf = jnp.broadcast_to(inv_freq[:, None], (half, LANES))
    freq = np.arange(ROPE_PART_ROWS)[:, None]
    live = (freq < 3 * half) & (freq % half == (li % half)[None, :])
    place = np.stack([live & (li < ROT_DIM)[None, :],
                      -1.0 * (live & (li < half)[None, :]),
                      live & ((li >= half) & (li < ROT_DIM))[None, :]]).astype(np.float32)
    keep = (li >= ROT_DIM).astype(np.float32)[None, :]
    return invf, jnp.asarray(place, BF16), jnp.asarray(keep)


def _proj_kernel(x_ref, xh_ref, pos_ref, ada_ref, invf_ref, place_ref, keep_ref,
                 win_ref, poolw_ref, pscale_ref, wpo_ref,
                 q1_ref, k1_ref, v1_ref, q4_ref, k4_ref, v4_ref, q16_ref, k16_ref, v16_ref,
                 pg_ref, sga_ref, xpe_ref, cls_ref):
    tm = x_ref.shape[1]
    i = pl.program_id(1)
    shift1 = ada_ref[0, 0:1, :]
    scale1 = ada_ref[0, 1:2, :]

    def modulated(xv):
        return (_layer_norm(xv) * (1.0 + scale1) + shift1).astype(BF16)

    u = modulated(x_ref[0])
    uh = modulated(xh_ref[0])

    xp = _dot(u, win_ref[:, 0:POOL_WIDTH])
    xph = _dot(uh, win_ref[:, 0:POOL_WIDTH])
    xph = jnp.where(i > 0, xph, 0.0)
    xpe_ref[0:POOL_HALO, :] = xph
    xpe_ref[POOL_HALO:, :] = xp
    tok = i * tm + lax.broadcasted_iota(jnp.int32, (tm, 1), 0)
    ys = []
    for g, w in enumerate(POOL_WINDOWS):
        cols = slice(g * POOL_GROUP, (g + 1) * POOL_GROUP)
        xg = xp[:, cols]
        acc = xg
        for j in range(1, w):
            acc = acc + xpe_ref[POOL_HALO - j:POOL_HALO - j + tm, cols]
        cnt = jnp.minimum(tok + 1, w).astype(F32)
        mixed = (acc / cnt - xg).astype(BF16)
        ys.append(_dot(mixed, poolw_ref[g]) * pscale_ref[:, cols])
    y = jnp.concatenate(ys, axis=1).astype(BF16)
    pooled = _dot(y, wpo_ref[...])
    g_p = _dot(u, win_ref[:, OFF_GP:OFF_GP + D_MODEL])
    pg_ref[0] = jax.nn.sigmoid(g_p) * pooled
    g_a = _dot(u, win_ref[:, OFF_GA:OFF_GA + D_MODEL])
    sga_ref[0] = jax.nn.sigmoid(g_a)

    ang = invf_ref[:, 0:1] * pos_ref[0, 0].astype(F32)

    def on_lanes(table, j):
        hi = table.astype(BF16).astype(F32)
        mid = (table - hi).astype(BF16).astype(F32)
        lo = table - hi - mid
        pad = jnp.zeros((ROPE_PART_ROWS - 3 * table.shape[0], tm), F32)
        parts = jnp.concatenate([hi, mid, lo, pad], axis=0).astype(BF16)
        return lax.dot_general(parts, place_ref[j], (((0,), (0,)), ((), ())), preferred_element_type=F32)

    cos = jnp.cos(ang)
    sin = jnp.sin(ang)
    c_mul = on_lanes(cos, 0) + keep_ref[...]
    s_lo = on_lanes(sin, 1)
    s_hi = on_lanes(sin, 2)
    c_mul = jnp.concatenate([c_mul, c_mul], axis=1)
    s_lo = jnp.concatenate([s_lo, s_lo], axis=1)
    s_hi = jnp.concatenate([s_hi, s_hi], axis=1)
    half = ROT_DIM // 2

    def rotate(a):
        up = pltpu.roll(a, GROUP_WIDTH - half, axis=1)
        dn = pltpu.roll(a, half, axis=1)
        return a * c_mul + up * s_lo + dn * s_hi

    def emit(a, out_ref, dil):
        if dil == 1:
            out_ref[0, 0] = a.astype(BF16)
            return
        for c in range(GROUP_WIDTH // LANES):
            cls_ref[c] = a[:, c * LANES:(c + 1) * LANES]
        for r in range(dil):
            for c in range(GROUP_WIDTH // LANES):
                out_ref[0, r, :, c * LANES:(c + 1) * LANES] = (
                    cls_ref[c, pl.ds(r, tm // dil, stride=dil), :].astype(BF16))

    outs = ((q1_ref, k1_ref, v1_ref), (q4_ref, k4_ref, v4_ref), (q16_ref, k16_ref, v16_ref))
    for gi, (_, dil) in enumerate(ATTN_GROUPS):
        qo, ko, vo = outs[gi]
        c0 = gi * GROUP_WIDTH
        emit(rotate(_dot(u, win_ref[:, OFF_Q + c0:OFF_Q + c0 + GROUP_WIDTH])), qo, dil)
        emit(rotate(_dot(u, win_ref[:, OFF_K + c0:OFF_K + c0 + GROUP_WIDTH])), ko, dil)
        emit(_dot(u, win_ref[:, OFF_V + c0:OFF_V + c0 + GROUP_WIDTH]), vo, dil)


def _proj(x, positions, ada, w_in, pool_w, pool_scale, w_pool_out):
    B, S, D = x.shape
    tm = PROJ_TILE
    nt = S // tm
    halo_blocks = tm // POOL_HALO
    const2 = lambda b, i: (0, 0)
    in_specs = [
        pl.BlockSpec((1, tm, D), lambda b, i: (b, i, 0)),
        pl.BlockSpec((1, POOL_HALO, D), lambda b, i: (b, jnp.maximum(i * halo_blocks - 1, 0), 0)),
        pl.BlockSpec((1, 1, 1, tm), lambda b, i: (b, i, 0, 0)),
        pl.BlockSpec((1, 6, D), lambda b, i: (b, 0, 0)),
        pl.BlockSpec((ROT_DIM // 2, LANES), const2),
        pl.BlockSpec((3, ROPE_PART_ROWS, LANES), lambda b, i: (0, 0, 0)),
        pl.BlockSpec((1, LANES), const2),
        pl.BlockSpec((D, IN_WIDTH), const2),
        pl.BlockSpec((len(POOL_WINDOWS), POOL_GROUP, POOL_GROUP), lambda b, i: (0, 0, 0)),
        pl.BlockSpec((1, POOL_WIDTH), const2),
        pl.BlockSpec((POOL_WIDTH, D), const2),
    ]
    out_specs, out_shapes = [], []
    for _, dil in ATTN_GROUPS:
        for _ in range(3):
            out_specs.append(pl.BlockSpec((1, dil, tm // dil, GROUP_WIDTH), lambda b, i: (b, 0, i, 0)))
            out_shapes.append(jax.ShapeDtypeStruct((B, dil, S // dil, GROUP_WIDTH), BF16))
    for _ in range(2):
        out_specs.append(pl.BlockSpec((1, tm, D), lambda b, i: (b, i, 0)))
        out_shapes.append(jax.ShapeDtypeStruct((B, S, D), F32))
    return pl.pallas_call(
        _proj_kernel,
        grid=(B, nt),
        in_specs=in_specs,
        out_specs=out_specs,
        out_shape=out_shapes,
        scratch_shapes=[pltpu.VMEM((tm + POOL_HALO, POOL_WIDTH), F32),
                        pltpu.VMEM((GROUP_WIDTH // LANES, tm, LANES), F32)],
        compiler_params=pltpu.CompilerParams(
            dimension_semantics=("parallel", "parallel"), vmem_limit_bytes=VMEM_LIMIT_BYTES),
        name="proj",
    )(x, x, positions.reshape(B, nt, 1, tm), ada, *_rope_tables(), w_in.astype(BF16),
      pool_w.astype(BF16), pool_scale.reshape(1, POOL_WIDTH), w_pool_out.astype(BF16))


def _attn_kernel(q_ref, k_ref, v_ref, kh_ref, vh_ref, o_ref, lse_ref, kf_ref, vf_ref):
    n_cls, qb = q_ref.shape[1], q_ref.shape[2]
    per = qb // ATTN_BLOCK
    n = pl.program_id(2)
    for c in range(n_cls):
        kf_ref[c, 0:ATTN_BLOCK, :] = kh_ref[0, c]
        kf_ref[c, ATTN_BLOCK:, :] = k_ref[0, c]
        vf_ref[c, 0:ATTN_BLOCK, :] = vh_ref[0, c]
        vf_ref[c, ATTN_BLOCK:, :] = v_ref[0, c]
    qi = lax.broadcasted_iota(jnp.int32, (ATTN_BLOCK, 2 * ATTN_BLOCK), 0)
    kj = lax.broadcasted_iota(jnp.int32, (ATTN_BLOCK, 2 * ATTN_BLOCK), 1)
    band = (kj >= qi) & (kj <= qi + ATTN_BLOCK)
    band_bias = jnp.where(band, 0.0, NEG_INF)
    lane = lax.broadcasted_iota(jnp.int32, (ATTN_BLOCK, GROUP_WIDTH), 1)
    low_lanes = lax.broadcasted_iota(jnp.int32, (ATTN_BLOCK, LANES), 1) < HEAD_DIM
    ones = jnp.ones((2 * ATTN_BLOCK, LANES), BF16)
    nh, blk = HEADS_PER_GROUP, ATTN_BLOCK

    def block(t, carry):
        c, j = t // per, t % per
        r0 = pl.multiple_of(j * ATTN_BLOCK, ATTN_BLOCK)
        first_key = jnp.where((n > 0) | (j > 0), 0, ATTN_BLOCK)
        bias = band_bias + jnp.where(kj < first_key, NEG_INF, 0.0)
        q = q_ref[0, c, pl.ds(r0, blk), :].astype(F32)
        kk = kf_ref[c, pl.ds(r0, 2 * blk), :]
        vv = vf_ref[c, pl.ds(r0, 2 * blk), :]
        qs = jnp.concatenate([jnp.where((lane >= h * HEAD_DIM) & (lane < (h + 1) * HEAD_DIM), q, 0.0)
                              for h in range(nh)], axis=0).astype(BF16)
        s = lax.dot_general(qs, kk, (((1,), (1,)), ((), ())), preferred_element_type=F32)
        s = jnp.concatenate([s[h * blk:(h + 1) * blk] * (HEAD_DIM ** -0.5) + bias for h in range(nh)], axis=0)
        m = jnp.max(s, axis=-1, keepdims=True)
        p = jnp.exp(s - m).astype(BF16)
        den = _dot(p, ones)
        lse = m + jnp.log(den)
        for hp in range(GROUP_WIDTH // LANES):
            rows = slice(2 * hp * blk, (2 * hp + 2) * blk)
            ls = slice(hp * LANES, (hp + 1) * LANES)
            o2 = _dot(p[rows], vv[:, ls]) / den[rows]
            l2 = lse[rows]
            o_ref[0, c, pl.ds(r0, blk), ls] = jnp.where(low_lanes, o2[0:blk], o2[blk:2 * blk])
            lse_ref[0, c, pl.ds(r0, blk), ls] = jnp.where(low_lanes, l2[0:blk], l2[blk:2 * blk])
        return carry

    lax.fori_loop(0, n_cls * per, block, 0, unroll=8)


def _attention(q, k, v):
    B, dil, L, W = q.shape
    qb = min(L, ATTN_QROWS)
    per = qb // ATTN_BLOCK
    n_cls = min(ATTN_QROWS // qb, dil)
    main = pl.BlockSpec((1, n_cls, qb, W), lambda b, r, n: (b, r, n, 0))
    halo = pl.BlockSpec((1, n_cls, ATTN_BLOCK, W), lambda b, r, n: (b, r, jnp.maximum(n * per - 1, 0), 0))
    return pl.pallas_call(
        _attn_kernel,
        grid=(B, dil // n_cls, L // qb),
        in_specs=[main, main, main, halo, halo],
        out_specs=[main, main],
        out_shape=[jax.ShapeDtypeStruct((B, dil, L, W), F32)] * 2,
        scratch_shapes=[pltpu.VMEM((n_cls, qb + ATTN_BLOCK, W), BF16)] * 2,
        compiler_params=pltpu.CompilerParams(
            dimension_semantics=("parallel", "parallel", "parallel"), vmem_limit_bytes=VMEM_LIMIT_BYTES),
        name=f"attn_d{dil}",
    )(q, k, v, k, v)


def _post_kernel(o1_ref, l1_ref, o4_ref, l4_ref, o16_ref, l16_ref, pg_ref, sga_ref, x_ref, ada_ref,
                 wao_ref, wo_ref, g1_ref, b1_ref, wrt_ref, brt_ref,
                 x1_ref, u2_ref, slot_ref, prob_ref, cnt_ref,
                 s0, s1, s2, s3):
    tm = x_ref.shape[1]

    def token_major(src_ref, scr_ref, dil):
        if dil == 1:
            return src_ref[0, 0]
        for r in range(dil):
            for c in range(GROUP_WIDTH // LANES):
                scr_ref[c, pl.ds(r, tm // dil, stride=dil), :] = src_ref[0, r, :, c * LANES:(c + 1) * LANES]
        return jnp.concatenate([scr_ref[c] for c in range(GROUP_WIDTH // LANES)], axis=1)

    o1, l1 = o1_ref[0, 0], l1_ref[0, 0]
    o4, l4 = token_major(o4_ref, s0, 4), token_major(l4_ref, s1, 4)
    o16, l16 = token_major(o16_ref, s2, 16), token_major(l16_ref, s3, 16)
    mx = jnp.maximum(jnp.maximum(l1, l4), l16)
    e1, e4, e16 = jnp.exp(l1 - mx), jnp.exp(l4 - mx), jnp.exp(l16 - mx)
    attn = (e1 * o1 + e4 * o4 + e16 * o16) / (e1 + e4 + e16)

    merged = pg_ref[0] + sga_ref[0] * _dot(attn.astype(BF16), wao_ref[...])
    mix = _dot(merged.astype(BF16), wo_ref[...])
    gate1 = ada_ref[0, 2:3, :]
    shift2 = ada_ref[0, 3:4, :]
    scale2 = ada_ref[0, 4:5, :]
    x1 = _layer_norm(DN_ALPHA * x_ref[0] + (1.0 + gate1) * mix) * g1_ref[...] + b1_ref[...]
    x1_ref[0] = x1
    u2 = _layer_norm(x1) * (1.0 + scale2) + shift2
    u2_hi = u2.astype(BF16)
    u2_ref[0] = u2_hi

    u2_lo = (u2 - u2_hi.astype(F32)).astype(BF16)
    nt_dot = lambda a, b: lax.dot_general(a, b, (((1,), (1,)), ((), ())), preferred_element_type=F32)
    logits = (nt_dot(wrt_ref[0], u2_hi) + nt_dot(wrt_ref[0], u2_lo) + nt_dot(wrt_ref[1], u2_hi)
              + brt_ref[...])
    eidx = lax.broadcasted_iota(jnp.int32, (N_EXPERTS, tm), 0)
    work = logits
    vals, idxs = [], []
    for _ in range(TOP_K):
        m = jnp.max(work, axis=0, keepdims=True)
        idx = jnp.min(jnp.where(work == m, eidx, N_EXPERTS), axis=0, keepdims=True)
        vals.append(m)
        idxs.append(idx)
        work = jnp.where(eidx == idx, -jnp.inf, work)
    exps = [jnp.exp(vk - vals[0]) for vk in vals]
    tot = exps[0] + exps[1] + exps[2] + exps[3]
    sel = jnp.zeros((N_EXPERTS, tm), F32)
    for idx in idxs:
        sel = sel + (eidx == idx).astype(F32)
    tr = lax.broadcasted_iota(jnp.int32, (tm, tm), 0)
    tc = lax.broadcasted_iota(jnp.int32, (tm, tm), 1)
    rank = _dot(sel.astype(BF16), (tr < tc).astype(BF16))
    cnt = jnp.sum(sel, axis=1, keepdims=True)
    run = jnp.floor((cnt + (ROW_ALIGN - 1)) * (1.0 / ROW_ALIGN)) * ROW_ALIGN
    er = lax.broadcasted_iota(jnp.int32, (N_EXPERTS, N_EXPERTS), 0)
    ec = lax.broadcasted_iota(jnp.int32, (N_EXPERTS, N_EXPERTS), 1)
    run_start = _dot((ec < er).astype(BF16),
                     jnp.broadcast_to(run, (N_EXPERTS, LANES)).astype(BF16))[:, 0:1]
    slot = rank + run_start
    for k in range(TOP_K):
        slot_ref[0, k:k + 1, :] = jnp.sum(jnp.where(eidx == idxs[k], slot, 0.0), axis=0,
                                          keepdims=True).astype(jnp.int32)
        prob_ref[0, k:k + 1, :] = exps[k] / tot
    cnt_ref[0] = cnt.astype(jnp.int32)


def _post(attn_outs, pg, sga, x, ada, w_attn_out, w_o, ln1_g, ln1_b, w_router, b_router):
    B, S, D = x.shape
    tm = POST_TILE
    nt = S // tm
    N = B * S
    const2 = lambda b, i: (0, 0)
    in_specs, args = [], []
    for (o, lse), (_, dil) in zip(attn_outs, ATTN_GROUPS):
        spec = pl.BlockSpec((1, dil, tm // dil, GROUP_WIDTH), lambda b, i: (b, 0, i, 0))
        in_specs += [spec, spec]
        args += [o, lse]
    tok_spec = pl.BlockSpec((1, tm, D), lambda b, i: (b, i, 0))
    in_specs += [tok_spec, tok_spec, tok_spec,
                 pl.BlockSpec((1, 6, D), lambda b, i: (b, 0, 0)),
                 pl.BlockSpec((GROUP_WIDTH, D), const2),
                 pl.BlockSpec((D, D), const2),
                 pl.BlockSpec((1, D), const2),
                 pl.BlockSpec((1, D), const2),
                 pl.BlockSpec((2, N_EXPERTS, D), lambda b, i: (0, 0, 0)),
                 pl.BlockSpec((N_EXPERTS, 1), const2)]
    wr_hi = w_router.T.astype(BF16)
    wr_lo = (w_router.T - wr_hi.astype(F32)).astype(BF16)
    args += [pg, sga, x, ada, w_attn_out.astype(BF16), w_o.astype(BF16),
             ln1_g.reshape(1, D), ln1_b.reshape(1, D), jnp.stack([wr_hi, wr_lo]),
             b_router.reshape(N_EXPERTS, 1)]
    nc = N // tm
    route_spec = pl.BlockSpec((1, TOP_K, tm), lambda b, i: (b * nt + i, 0, 0))
    out_specs = [tok_spec, tok_spec, route_spec, route_spec,
                 pl.BlockSpec((1, N_EXPERTS, 1), lambda b, i: (b * nt + i, 0, 0))]
    out_shapes = [jax.ShapeDtypeStruct((B, S, D), F32), jax.ShapeDtypeStruct((B, S, D), BF16),
                  jax.ShapeDtypeStruct((nc, TOP_K, tm), jnp.int32), jax.ShapeDtypeStruct((nc, TOP_K, tm), F32),
                  jax.ShapeDtypeStruct((nc, N_EXPERTS, 1), jnp.int32)]
    return pl.pallas_call(
        _post_kernel,
        grid=(B, nt),
        in_specs=in_specs,
        out_specs=out_specs,
        out_shape=out_shapes,
        scratch_shapes=[pltpu.VMEM((GROUP_WIDTH // LANES, tm, LANES), F32)] * 4,
        compiler_params=pltpu.CompilerParams(
            dimension_semantics=("parallel", "parallel"), vmem_limit_bytes=VMEM_LIMIT_BYTES),
        name="post",
    )(*args)


def _for_each_piece(tile, run_ref, dest_ref, len_ref, fn):
    def per_expert(e, total):
        idx = tile * N_EXPERTS + e
        loc, dst, length = run_ref[idx], dest_ref[idx], len_ref[idx]
        n_big = length // PIECE_ROWS
        rest = n_big * PIECE_ROWS

        def big(p, c):
            off = p * PIECE_ROWS
            fn(pl.multiple_of(loc + off, ROW_ALIGN), pl.multiple_of(dst + off, ROW_ALIGN), PIECE_ROWS)
            return c

        def small(p, c):
            off = rest + p * ROW_ALIGN
            fn(pl.multiple_of(loc + off, ROW_ALIGN), pl.multiple_of(dst + off, ROW_ALIGN), ROW_ALIGN)
            return c

        lax.fori_loop(0, n_big, big, 0)
        lax.fori_loop(0, (length - rest) // ROW_ALIGN, small, 0)
        return total + length

    return lax.fori_loop(0, N_EXPERTS, per_expert, 0)


def _wait_rows(src_ref, dst_ref, rows, sem):
    @pl.when(rows > 0)
    def _():
        n = pl.multiple_of(rows, ROW_ALIGN)
        pltpu.make_async_copy(src_ref.at[pl.ds(0, n)], dst_ref.at[pl.ds(0, n)], sem).wait()


def _dispatch_kernel(run_ref, dest_ref, len_ref, tail_ref, ntail_ref, nused_ref,
                     slot_ref, prob_ref, u2_ref, xs_hbm, local2_ref, zero_ref, sem2, zsem, rows_ref):
    tile = pl.program_id(0)
    buf = tile % 2
    local_ref = local2_ref.at[buf]
    sem = sem2.at[buf]
    tt, d = u2_ref.shape
    u2 = u2_ref[...]
    slots = [slot_ref[0, k:k + 1, :] for k in range(TOP_K)]
    probs = [prob_ref[0, k:k + 1, :] for k in range(TOP_K)]
    lane = lax.broadcasted_iota(jnp.int32, (SORT_CHUNK, LANES), 1)
    for r0 in range(0, LOCAL_ROWS, SORT_CHUNK):
        row = r0 + lax.broadcasted_iota(jnp.int32, (SORT_CHUNK, tt), 0)
        w = jnp.zeros((SORT_CHUNK, tt), F32)
        for k in range(TOP_K):
            w = jnp.where(row == slots[k], probs[k], w)
        onehot = jnp.where(w != 0.0, 1.0, 0.0).astype(BF16)
        local_ref[r0:r0 + SORT_CHUNK, 0:d] = _dot(onehot, u2).astype(BF16)
        wr = jnp.sum(w, axis=1, keepdims=True)
        hi = wr.astype(BF16).astype(F32)
        mid = (wr - hi).astype(BF16).astype(F32)
        lo = wr - hi - mid
        parts = jnp.where(lane == 0, hi, jnp.where(lane == 1, mid, jnp.where(lane == 2, lo, 0.0)))
        local_ref[r0:r0 + SORT_CHUNK, d:d + LANES] = parts.astype(BF16)

    def start_piece(loc, dst, rows):
        pltpu.make_async_copy(local_ref.at[pl.ds(loc, rows)], xs_hbm.at[pl.ds(dst, rows)], sem).start()

    rows_ref[buf] = _for_each_piece(tile, run_ref, dest_ref, len_ref, start_piece)

    @pl.when(tile > 0)
    def _():
        _wait_rows(local2_ref.at[1 - buf], xs_hbm, rows_ref[1 - buf], sem2.at[1 - buf])

    def zero_fill(first_row, n_rows, act):
        def copy(row, rows):
            return pltpu.make_async_copy(zero_ref.at[pl.ds(0, rows)],
                                         xs_hbm.at[pl.ds(pl.multiple_of(row, ROW_ALIGN), rows)], zsem)
        n_big = n_rows // ZERO_ROWS
        rest = first_row + n_big * ZERO_ROWS
        lax.fori_loop(0, n_big, lambda p, c: (act(copy(first_row + p * ZERO_ROWS, ZERO_ROWS)), c)[1], 0)
        lax.fori_loop(0, (n_rows - n_big * ZERO_ROWS) // ROW_ALIGN,
                      lambda p, c: (act(copy(rest + p * ROW_ALIGN, ROW_ALIGN)), c)[1], 0)

    def zero_fill_all(act):
        first_spare = nused_ref[0] * MOE_BLOCK
        lax.fori_loop(0, N_EXPERTS, lambda e, c: (zero_fill(tail_ref[e], ntail_ref[e], act), c)[1], 0)
        zero_fill(first_spare, xs_hbm.shape[0] - first_spare, act)

    @pl.when(tile == 0)
    def _():
        zero_ref[...] = jnp.zeros_like(zero_ref)
        zero_fill_all(lambda c: c.start())

    @pl.when(tile == pl.num_programs(0) - 1)
    def _():
        _wait_rows(local_ref, xs_hbm, rows_ref[buf], sem)
        zero_fill_all(lambda c: c.wait())


def _dispatch(tables, slot, prob, u2, n_rows):
    N, D = u2.shape
    tt = POST_TILE
    route_spec = pl.BlockSpec((1, TOP_K, tt), lambda i, *t: (i, 0, 0))
    grid_spec = pltpu.PrefetchScalarGridSpec(
        num_scalar_prefetch=len(tables),
        grid=(N // tt,),
        in_specs=[route_spec, route_spec, pl.BlockSpec((tt, D), lambda i, *t: (i, 0))],
        out_specs=pl.BlockSpec(memory_space=pl.ANY),
        scratch_shapes=[pltpu.VMEM((2, LOCAL_ROWS, D + LANES), BF16), pltpu.VMEM((ZERO_ROWS, D + LANES), BF16),
                        pltpu.SemaphoreType.DMA((2,)), pltpu.SemaphoreType.DMA(()),
                        pltpu.SMEM((2,), jnp.int32)],
    )
    return pl.pallas_call(
        _dispatch_kernel,
        grid_spec=grid_spec,
        out_shape=jax.ShapeDtypeStruct((n_rows, D + LANES), BF16),
        compiler_params=pltpu.CompilerParams(
            dimension_semantics=("arbitrary",), vmem_limit_bytes=VMEM_LIMIT_BYTES),
        name="dispatch",
    )(*tables, slot, prob, u2)


def _expert_kernel(be_ref, next_ref, live_ref, nused_ref, xs_ref, wg_hbm, bg_ref, wu_hbm, bu_ref, wd_hbm, bd_ref,
                   ys_ref, stage, wg_s, wu_s, wd_s, sem):
    i = pl.program_id(0)
    used = i < nused_ref[0]
    prev = be_ref[jnp.maximum(i - 1, 0)]
    fresh = (i == 0) | (be_ref[i] != prev)

    def fetch(e):
        return [pltpu.make_async_copy(w_hbm.at[e], stage.at[j], sem.at[j])
                for j, w_hbm in enumerate((wg_hbm, wu_hbm, wd_hbm))]

    @pl.when(i == 0)
    def _():
        for copy in fetch(be_ref[0]):
            copy.start()

    @pl.when(used & fresh)
    def _():
        for copy in fetch(be_ref[i]):
            copy.wait()
        wg_s[...] = stage[0].astype(BF16)
        wu_s[...] = stage[1].astype(BF16)
        wd_s[...] = stage[2].astype(BF16)

        @pl.when(next_ref[i] >= 0)
        def _():
            for copy in fetch(next_ref[i]):
                copy.start()

    d = ys_ref.shape[1]

    def expert(rows):
        xb = xs_ref[0:rows, 0:d]
        parts = xs_ref[0:rows, d:d + LANES].astype(F32)
        weight = parts[:, 0:1] + parts[:, 1:2] + parts[:, 2:3]
        g = _dot(xb, wg_s[...]) + bg_ref[0]
        up = _dot(xb, wu_s[...]) + bu_ref[0]
        g = jnp.minimum(g, SWIGLU_LIMIT)
        up = jnp.clip(up, -SWIGLU_LIMIT, SWIGLU_LIMIT)
        h = g * jax.nn.sigmoid(SWIGLU_ALPHA * g) * (up + 1.0)
        return ((_dot(h.astype(BF16), wd_s[...]) + bd_ref[0]) * weight).astype(BF16)

    for parts_live in range(1, MOE_BLOCK // EXPERT_PART + 1):
        rows = parts_live * EXPERT_PART

        @pl.when(used & (live_ref[i] == parts_live))
        def _():
            if rows == MOE_BLOCK:
                ys_ref[...] = expert(rows)
            else:
                ys_ref[0:rows, :] = expert(rows)
                ys_ref[rows:, :] = jnp.zeros((MOE_BLOCK - rows, d), BF16)

    @pl.when(jnp.logical_not(used))
    def _():
        ys_ref[...] = jnp.zeros_like(ys_ref)


def _experts(be, next_expert, live_parts, nused, xs, w_gate, b_gate, w_up, b_up, w_down, b_down):
    P, width = xs.shape
    D = width - LANES
    E = w_gate.shape[0]

    def live(i, nu):
        return jnp.maximum(jnp.minimum(i, nu[0] - 1), 0)

    w_spec = pl.BlockSpec(memory_space=pl.ANY)
    b_spec = pl.BlockSpec((1, 1, D), lambda i, be, nx, hf, nu: (be[live(i, nu)], 0, 0))
    grid_spec = pltpu.PrefetchScalarGridSpec(
        num_scalar_prefetch=4,
        grid=(P // MOE_BLOCK,),
        in_specs=[pl.BlockSpec((MOE_BLOCK, width), lambda i, be, nx, hf, nu: (live(i, nu), 0)),
                  w_spec, b_spec, w_spec, b_spec, w_spec, b_spec],
        out_specs=pl.BlockSpec((MOE_BLOCK, D), lambda i, be, nx, hf, nu: (i, 0)),
        scratch_shapes=[pltpu.VMEM((3, D, D), F32)] + [pltpu.VMEM((D, D), BF16)] * 3
                       + [pltpu.SemaphoreType.DMA((3,))],
    )
    return pl.pallas_call(
        _expert_kernel,
        grid_spec=grid_spec,
        out_shape=jax.ShapeDtypeStruct((P, D), BF16),
        compiler_params=pltpu.CompilerParams(
            dimension_semantics=("arbitrary",), vmem_limit_bytes=VMEM_LIMIT_BYTES),
        name="experts",
    )(be, next_expert, live_parts, nused, xs, w_gate, b_gate.reshape(E, 1, D), w_up, b_up.reshape(E, 1, D),
      w_down, b_down.reshape(E, 1, D))


def _combine_kernel(run_ref, dest_ref, len_ref,
                    slot_ref, x1_ref, ada_ref, g2_ref, b2_ref, ys_hbm,
                    out_ref, local2_ref, sem2, rows_ref):
    tile = pl.program_id(0)
    buf = tile % 2
    tt = x1_ref.shape[0]

    def fetch(t, b):
        def start_piece(loc, dst, rows):
            pltpu.make_async_copy(ys_hbm.at[pl.ds(dst, rows)], local2_ref.at[b, pl.ds(loc, rows)],
                                  sem2.at[b]).start()
        rows_ref[b] = _for_each_piece(t, run_ref, dest_ref, len_ref, start_piece)

    @pl.when(tile == 0)
    def _():
        local2_ref[...] = jnp.zeros_like(local2_ref)
        fetch(0, 0)

    @pl.when(tile + 1 < pl.num_programs(0))
    def _():
        fetch(tile + 1, 1 - buf)

    local_ref = local2_ref.at[buf]
    _wait_rows(ys_hbm, local_ref, rows_ref[buf], sem2.at[buf])
    slots = [slot_ref[:, k:k + 1] for k in range(TOP_K)]

    ffn = jnp.zeros((tt, x1_ref.shape[1]), F32)
    for r0 in range(0, LOCAL_ROWS, SORT_CHUNK):
        col = r0 + lax.broadcasted_iota(jnp.int32, (tt, SORT_CHUNK), 1)
        onehot = jnp.zeros((tt, SORT_CHUNK), F32)
        for k in range(TOP_K):
            onehot = jnp.where(col == slots[k], 1.0, onehot)
        ffn = ffn + _dot(onehot.astype(BF16), local_ref[r0:r0 + SORT_CHUNK, :])
    gate2 = ada_ref[0, 5:6, :]
    y = DN_ALPHA * x1_ref[...] + (1.0 + gate2) * ffn
    out_ref[...] = _layer_norm(y) * g2_ref[...] + b2_ref[...]


def _combine(tables, slot_tok, x1, ada, ln2_g, ln2_b, ys, tiles_per_batch):
    N, D = x1.shape
    tt = POST_TILE
    const = lambda i, *t: (0, 0)
    tok4 = pl.BlockSpec((tt, TOP_K), lambda i, *t: (i, 0))
    grid_spec = pltpu.PrefetchScalarGridSpec(
        num_scalar_prefetch=len(tables),
        grid=(N // tt,),
        in_specs=[tok4,
                  pl.BlockSpec((tt, D), lambda i, *t: (i, 0)),
                  pl.BlockSpec((1, 6, D), lambda i, *t: (i // tiles_per_batch, 0, 0)),
                  pl.BlockSpec((1, D), const),
                  pl.BlockSpec((1, D), const),
                  pl.BlockSpec(memory_space=pl.ANY)],
        out_specs=pl.BlockSpec((tt, D), lambda i, *t: (i, 0)),
        scratch_shapes=[pltpu.VMEM((2, LOCAL_ROWS, D), BF16), pltpu.SemaphoreType.DMA((2,)),
                        pltpu.SMEM((2,), jnp.int32)],
    )
    return pl.pallas_call(
        _combine_kernel,
        grid_spec=grid_spec,
        out_shape=jax.ShapeDtypeStruct((N, D), F32),
        compiler_params=pltpu.CompilerParams(
            dimension_semantics=("arbitrary",), vmem_limit_bytes=VMEM_LIMIT_BYTES),
        name="combine",
    )(*tables, slot_tok, x1, ada, ln2_g.reshape(1, D), ln2_b.reshape(1, D), ys)


def _routing_tables(cnt, nblk):
    nt, E = cnt.shape
    i32 = jnp.int32
    run = (cnt + ROW_ALIGN - 1) // ROW_ALIGN * ROW_ALIGN
    run_start = jnp.cumsum(run, axis=1) - run
    seg_len = jnp.sum(run, axis=0)
    seg_blocks = (seg_len + MOE_BLOCK - 1) // MOE_BLOCK
    b_end = jnp.cumsum(seg_blocks)
    seg_off = (b_end - seg_blocks) * MOE_BLOCK
    dest = seg_off[None, :] + jnp.cumsum(run, axis=0) - run
    nused = b_end[-1:].astype(i32)
    i = jnp.arange(nblk, dtype=i32)
    be = jnp.minimum(jnp.sum((i[:, None] >= b_end[None, :]).astype(i32), axis=1), E - 1)
    e_ids = jnp.arange(E, dtype=i32)
    later = (e_ids[None, :] > e_ids[:, None]) & (seg_blocks > 0)[None, :]
    next_e = jnp.min(jnp.where(later, e_ids[None, :], E), axis=1)
    next_e = jnp.where(next_e < E, next_e, -1)
    mine = be[:, None] == e_ids[None, :]
    pick = lambda v: jnp.sum(jnp.where(mine, v[None, :], 0), axis=1)
    nxt = pick(next_e)
    rows_left = pick(seg_len) - (i - pick(b_end - seg_blocks)) * MOE_BLOCK
    live_parts = jnp.clip((rows_left + EXPERT_PART - 1) // EXPERT_PART, 1, MOE_BLOCK // EXPERT_PART).astype(i32)
    tail = seg_off + seg_len
    ntail = seg_blocks * MOE_BLOCK - seg_len
    piece_tables = (run_start.reshape(-1).astype(i32), dest.reshape(-1).astype(i32),
                    run.reshape(-1).astype(i32))
    return piece_tables, (tail.astype(i32), ntail.astype(i32)), (be.astype(i32), nxt.astype(i32), live_parts, nused)


def kernel(x, c, positions, w_ada, b_ada, w_in, pool_w, pool_scale, w_pool_out, w_attn_out, w_o,
           ln1_g, ln1_b, w_router, b_router, w_gate, b_gate, w_up, b_up, w_down, b_down, ln2_g, ln2_b):
    B, S, D = x.shape
    N = B * S
    assert D == D_MODEL and S % PROJ_TILE == 0 and S % (16 * ATTN_BLOCK) == 0
    assert S % POST_TILE == 0 and LOCAL_ROWS % SORT_CHUNK == 0
    nt = N // POST_TILE
    nblk = (N * TOP_K + nt * N_EXPERTS * (ROW_ALIGN - 1)) // MOE_BLOCK + N_EXPERTS
    for l in range(DEPTH):
        ada = _ada(c, w_ada[l], b_ada[l])
        proj_out = _proj(x, positions, ada, w_in[l], pool_w[l], pool_scale[l], w_pool_out[l])
        qkv, (pg, sga) = proj_out[:9], proj_out[9:]
        attn_outs = [_attention(*qkv[3 * g:3 * g + 3]) for g in range(len(ATTN_GROUPS))]
        x1, u2, slot, prob, cnt = _post(attn_outs, pg, sga, x, ada, w_attn_out[l], w_o[l],
                                        ln1_g[l], ln1_b[l], w_router[l], b_router[l])
        piece_tables, tail_tables, block_tables = _routing_tables(cnt.reshape(nt, N_EXPERTS), nblk)
        xs = _dispatch(piece_tables + tail_tables + block_tables[3:], slot, prob, u2.reshape(N, D),
                       nblk * MOE_BLOCK)
        ys = _experts(*block_tables, xs, w_gate[l], b_gate[l], w_up[l], b_up[l], w_down[l], b_down[l])
        slot_tok = slot.transpose(0, 2, 1).reshape(N, TOP_K)
        out = _combine(piece_tables, slot_tok, x1.reshape(N, D), ada,
                       ln2_g[l], ln2_b[l], ys, S // POST_TILE)
        x = out.reshape(B, S, D)
    return x
```

```python
import jax
import jax.numpy as jnp
import numpy as np
from jax import lax
from jax.experimental import pallas as pl
from jax.experimental.pallas import tpu as pltpu

F32 = jnp.float32
BF16 = jnp.bfloat16

D_MODEL = 1024
POOL_WINDOWS = (2, 4, 8, 16)
POOL_WIDTH = D_MODEL // 2
POOL_GROUP = POOL_WIDTH // len(POOL_WINDOWS)
POOL_HALO = 16
HEAD_DIM = 64
ATTN_GROUPS = ((128, 1), (512, 4), (2048, 16))
HEADS_PER_GROUP = 4
GROUP_WIDTH = HEADS_PER_GROUP * HEAD_DIM
N_HEADS = HEADS_PER_GROUP * len(ATTN_GROUPS)
ATTN_WIDTH = N_HEADS * HEAD_DIM
ATTN_BLOCK = 128
ROT_DIM = HEAD_DIM // 4
ROPE_THETA = 500000.0
N_EXPERTS = 32
TOP_K = 4
SWIGLU_ALPHA = 1.702
SWIGLU_LIMIT = 7.0
MOE_BLOCK = 1024
EXPERT_PART = 128
DEPTH = 1
DN_ALPHA = (2.0 * DEPTH) ** 0.25
LN_EPS = 1e-5
NEG_INF = -1e30

OFF_Q = POOL_WIDTH
OFF_K = OFF_Q + ATTN_WIDTH
OFF_V = OFF_K + ATTN_WIDTH
OFF_GP = OFF_V + ATTN_WIDTH
OFF_GA = OFF_GP + D_MODEL
IN_WIDTH = OFF_GA + D_MODEL

VMEM_LIMIT_BYTES = 56 * 1024 * 1024
LANES = 128
SUBLANES = 8

PROJ_TILE = 512
POST_TILE = 512
ATTN_QROWS = 1024
ROW_ALIGN = 16
PIECE_ROWS = 64
ZERO_ROWS = 128
SORT_CHUNK = 512
LOCAL_ROWS = -(-(POST_TILE * TOP_K + N_EXPERTS * (ROW_ALIGN - 1)) // SORT_CHUNK) * SORT_CHUNK


def _layer_norm(x):
    mu = jnp.mean(x, axis=-1, keepdims=True)
    xc = x - mu
    var = jnp.mean(xc * xc, axis=-1, keepdims=True)
    return xc * lax.rsqrt(var + LN_EPS)


def _dot(a, b):
    return jnp.dot(a, b, preferred_element_type=F32)


def _ada_kernel(c_ref, w_ref, b_ref, o_ref):
    c = c_ref[...]
    s = c * jax.nn.sigmoid(c)
    o_ref[...] = jnp.dot(s, w_ref[...], preferred_element_type=F32,
                         precision=lax.Precision.HIGHEST) + b_ref[...]


def _ada(c, w_ada, b_ada):
    B, D = c.shape
    rows = SUBLANES
    c_pad = jnp.pad(c, ((0, rows - B), (0, 0)))
    n_out = w_ada.shape[1]
    out = pl.pallas_call(
        _ada_kernel,
        grid=(n_out // D,),
        in_specs=[pl.BlockSpec((rows, D), lambda j: (0, 0)),
                  pl.BlockSpec((D, D), lambda j: (0, j)),
                  pl.BlockSpec((1, D), lambda j: (0, j))],
        out_specs=pl.BlockSpec((rows, D), lambda j: (0, j)),
        out_shape=jax.ShapeDtypeStruct((rows, n_out), F32),
        name="ada",
    )(c_pad, w_ada, b_ada.reshape(1, n_out))
    return out[:B].reshape(B, 6, D)


ROPE_PART_ROWS = 32


def _rope_tables():
    lane = np.arange(LANES)
    li = lane % HEAD_DIM
    half = ROT_DIM // 2
    inv_freq = jnp.power(ROPE_THETA, -jnp.arange(half, dtype=F32) * (2.0 / ROT_DIM))
    invf = jnp.broadcast_to(inv_freq[:, None], (half, LANES))
    freq = np.arange(ROPE_PART_ROWS)[:, None]
    live = (freq < 3 * half) & (freq % half == (li % half)[None, :])
    place = np.stack([live & (li < ROT_DIM)[None, :],
                      -1.0 * (live & (li < half)[None, :]),
                      live & ((li >= half) & (li < ROT_DIM))[None, :]]).astype(np.float32)
    keep = (li >= ROT_DIM).astype(np.float32)[None, :]
    return invf, jnp.asarray(place, BF16), jnp.asarray(keep)


def _proj_kernel(x_ref, xh_ref, pos_ref, ada_ref, invf_ref, place_ref, keep_ref,
                 win_ref, poolw_ref, pscale_ref, wpo_ref,
                 q1_ref, k1_ref, v1_ref, q4_ref, k4_ref, v4_ref, q16_ref, k16_ref, v16_ref,
                 pg_ref, sga_ref, xpe_ref, cls_ref):
    tm = x_ref.shape[1]
    i = pl.program_id(1)
    shift1 = ada_ref[0, 0:1, :]
    scale1 = ada_ref[0, 1:2, :]

    def modulated(xv):
        return (_layer_norm(xv) * (1.0 + scale1) + shift1).astype(BF16)

    u = modulated(x_ref[0])
    uh = modulated(xh_ref[0])

    xp = _dot(u, win_ref[:, 0:POOL_WIDTH])
    xph = _dot(uh, win_ref[:, 0:POOL_WIDTH])
    xph = jnp.where(i > 0, xph, 0.0)
    xpe_ref[0:POOL_HALO, :] = xph
    xpe_ref[POOL_HALO:, :] = xp
    tok = i * tm + lax.broadcasted_iota(jnp.int32, (tm, 1), 0)
    ys = []
    for g, w in enumerate(POOL_WINDOWS):
        cols = slice(g * POOL_GROUP, (g + 1) * POOL_GROUP)
        xg = xp[:, cols]
        acc = xg
        for j in range(1, w):
            acc = acc + xpe_ref[POOL_HALO - j:POOL_HALO - j + tm, cols]
        cnt = jnp.minimum(tok + 1, w).astype(F32)
        mixed = (acc / cnt - xg).astype(BF16)
        ys.append(_dot(mixed, poolw_ref[g]) * pscale_ref[:, cols])
    y = jnp.concatenate(ys, axis=1).astype(BF16)
    pooled = _dot(y, wpo_ref[...])
    g_p = _dot(u, win_ref[:, OFF_GP:OFF_GP + D_MODEL])
    pg_ref[0] = (jax.nn.sigmoid(g_p) * pooled).astype(BF16)
    g_a = _dot(u, win_ref[:, OFF_GA:OFF_GA + D_MODEL])
    sga_ref[0] = jax.nn.sigmoid(g_a).astype(BF16)

    ang = invf_ref[:, 0:1] * pos_ref[0, 0].astype(F32)

    def on_lanes(table, j):
        hi = table.astype(BF16).astype(F32)
        mid = (table - hi).astype(BF16).astype(F32)
        lo = table - hi - mid
        pad = jnp.zeros((ROPE_PART_ROWS - 3 * table.shape[0], tm), F32)
        parts = jnp.concatenate([hi, mid, lo, pad], axis=0).astype(BF16)
        return lax.dot_general(parts, place_ref[j], (((0,), (0,)), ((), ())), preferred_element_type=F32)

    cos = jnp.cos(ang)
    sin = jnp.sin(ang)
    c_mul = on_lanes(cos, 0) + keep_ref[...]
    s_lo = on_lanes(sin, 1)
    s_hi = on_lanes(sin, 2)
    c_mul = jnp.concatenate([c_mul, c_mul], axis=1)
    s_lo = jnp.concatenate([s_lo, s_lo], axis=1)
    s_hi = jnp.concatenate([s_hi, s_hi], axis=1)
    half = ROT_DIM // 2

    def rotate(a):
        up = pltpu.roll(a, GROUP_WIDTH - half, axis=1)
        dn = pltpu.roll(a, half, axis=1)
        return a * c_mul + up * s_lo + dn * s_hi

    def emit(a, out_ref, dil):
        if dil == 1:
            out_ref[0, 0] = a.astype(BF16)
            return
        for c in range(GROUP_WIDTH // LANES):
            cls_ref[c] = a[:, c * LANES:(c + 1) * LANES]
        for r in range(dil):
            for c in range(GROUP_WIDTH // LANES):
                out_ref[0, r, :, c * LANES:(c + 1) * LANES] = (
                    cls_ref[c, pl.ds(r, tm // dil, stride=dil), :].astype(BF16))

    outs = ((q1_ref, k1_ref, v1_ref), (q4_ref, k4_ref, v4_ref), (q16_ref, k16_ref, v16_ref))
    for gi, (_, dil) in enumerate(ATTN_GROUPS):
        qo, ko, vo = outs[gi]
        c0 = gi * GROUP_WIDTH
        emit(rotate(_dot(u, win_ref[:, OFF_Q + c0:OFF_Q + c0 + GROUP_WIDTH])), qo, dil)
        emit(rotate(_dot(u, win_ref[:, OFF_K + c0:OFF_K + c0 + GROUP_WIDTH])), ko, dil)
        emit(_dot(u, win_ref[:, OFF_V + c0:OFF_V + c0 + GROUP_WIDTH]), vo, dil)


def _proj(x, positions, ada, w_in, pool_w, pool_scale, w_pool_out):
    B, S, D = x.shape
    tm = PROJ_TILE
    nt = S // tm
    halo_blocks = tm // POOL_HALO
    const2 = lambda b, i: (0, 0)
    in_specs = [
        pl.BlockSpec((1, tm, D), lambda b, i: (b, i, 0)),
        pl.BlockSpec((1, POOL_HALO, D), lambda b, i: (b, jnp.maximum(i * halo_blocks - 1, 0), 0)),
        pl.BlockSpec((1, 1, 1, tm), lambda b, i: (b, i, 0, 0)),
        pl.BlockSpec((1, 6, D), lambda b, i: (b, 0, 0)),
        pl.BlockSpec((ROT_DIM // 2, LANES), const2),
        pl.BlockSpec((3, ROPE_PART_ROWS, LANES), lambda b, i: (0, 0, 0)),
        pl.BlockSpec((1, LANES), const2),
        pl.BlockSpec((D, IN_WIDTH), const2),
        pl.BlockSpec((len(POOL_WINDOWS), POOL_GROUP, POOL_GROUP), lambda b, i: (0, 0, 0)),
        pl.BlockSpec((1, POOL_WIDTH), const2),
        pl.BlockSpec((POOL_WIDTH, D), const2),
    ]
    out_specs, out_shapes = [], []
    for _, dil in ATTN_GROUPS:
        for _ in range(3):
            out_specs.append(pl.BlockSpec((1, dil, tm // dil, GROUP_WIDTH), lambda b, i: (b, 0, i, 0)))
            out_shapes.append(jax.ShapeDtypeStruct((B, dil, S // dil, GROUP_WIDTH), BF16))
    for _ in range(2):
        out_specs.append(pl.BlockSpec((1, tm, D), lambda b, i: (b, i, 0)))
        out_shapes.append(jax.ShapeDtypeStruct((B, S, D), BF16))
    return pl.pallas_call(
        _proj_kernel,
        grid=(B, nt),
        in_specs=in_specs,
        out_specs=out_specs,
        out_shape=out_shapes,
        scratch_shapes=[pltpu.VMEM((tm + POOL_HALO, POOL_WIDTH), F32),
                        pltpu.VMEM((GROUP_WIDTH // LANES, tm, LANES), F32)],
        compiler_params=pltpu.CompilerParams(
            dimension_semantics=("parallel", "parallel"), vmem_limit_bytes=VMEM_LIMIT_BYTES),
        name="proj",
    )(x, x, positions.reshape(B, nt, 1, tm), ada, *_rope_tables(), w_in.astype(BF16),
      pool_w.astype(BF16), pool_scale.reshape(1, POOL_WIDTH), w_pool_out.astype(BF16))


def _attn_kernel(q_ref, k_ref, v_ref, kh_ref, vh_ref, o_ref, lse_ref, kf_ref, vf_ref):
    n_cls, qb = q_ref.shape[1], q_ref.shape[2]
    per = qb // ATTN_BLOCK
    n = pl.program_id(2)
    for c in range(n_cls):
        kf_ref[c, 0:ATTN_BLOCK, :] = kh_ref[0, c]
        kf_ref[c, ATTN_BLOCK:, :] = k_ref[0, c]
        vf_ref[c, 0:ATTN_BLOCK, :] = vh_ref[0, c]
        vf_ref[c, ATTN_BLOCK:, :] = v_ref[0, c]
    qi = lax.broadcasted_iota(jnp.int32, (ATTN_BLOCK, 2 * ATTN_BLOCK), 0)
    kj = lax.broadcasted_iota(jnp.int32, (ATTN_BLOCK, 2 * ATTN_BLOCK), 1)
    band = (kj >= qi) & (kj <= qi + ATTN_BLOCK)
    band_bias = jnp.where(band, 0.0, NEG_INF)
    lane = lax.broadcasted_iota(jnp.int32, (ATTN_BLOCK, GROUP_WIDTH), 1)
    low_lanes = lax.broadcasted_iota(jnp.int32, (ATTN_BLOCK, LANES), 1) < HEAD_DIM
    ones = jnp.ones((2 * ATTN_BLOCK, LANES), BF16)
    nh, blk = HEADS_PER_GROUP, ATTN_BLOCK

    def block(t, carry):
        c, j = t // per, t % per
        r0 = pl.multiple_of(j * ATTN_BLOCK, ATTN_BLOCK)
        first_key = jnp.where((n > 0) | (j > 0), 0, ATTN_BLOCK)
        bias = band_bias + jnp.where(kj < first_key, NEG_INF, 0.0)
        q = q_ref[0, c, pl.ds(r0, blk), :].astype(F32)
        kk = kf_ref[c, pl.ds(r0, 2 * blk), :]
        vv = vf_ref[c, pl.ds(r0, 2 * blk), :]
        qs = jnp.concatenate([jnp.where((lane >= h * HEAD_DIM) & (lane < (h + 1) * HEAD_DIM), q, 0.0)
                              for h in range(nh)], axis=0).astype(BF16)
        s = lax.dot_general(qs, kk, (((1,), (1,)), ((), ())), preferred_element_type=F32)
        s = jnp.concatenate([s[h * blk:(h + 1) * blk] * (HEAD_DIM ** -0.5) + bias for h in range(nh)], axis=0)
        m = jnp.max(s, axis=-1, keepdims=True)
        p = jnp.exp(s - m).astype(BF16)
        den = _dot(p, ones)
        lse = m + jnp.log(den)
        for hp in range(GROUP_WIDTH // LANES):
            rows = slice(2 * hp * blk, (2 * hp + 2) * blk)
            ls = slice(hp * LANES, (hp + 1) * LANES)
            o2 = _dot(p[rows], vv[:, ls]) / den[rows]
            l2 = lse[rows]
            o_ref[0, c, pl.ds(r0, blk), ls] = jnp.where(low_lanes, o2[0:blk], o2[blk:2 * blk])
            lse_ref[0, c, pl.ds(r0, blk), ls] = jnp.where(low_lanes, l2[0:blk], l2[blk:2 * blk])
        return carry

    lax.fori_loop(0, n_cls * per, block, 0, unroll=8)


def _attention(q, k, v):
    B, dil, L, W = q.shape
    qb = min(L, ATTN_QROWS)
    per = qb // ATTN_BLOCK
    n_cls = min(ATTN_QROWS // qb, dil)
    main = pl.BlockSpec((1, n_cls, qb, W), lambda b, r, n: (b, r, n, 0))
    halo = pl.BlockSpec((1, n_cls, ATTN_BLOCK, W), lambda b, r, n: (b, r, jnp.maximum(n * per - 1, 0), 0))
    return pl.pallas_call(
        _attn_kernel,
        grid=(B, dil // n_cls, L // qb),
        in_specs=[main, main, main, halo, halo],
        out_specs=[main, main],
        out_shape=[jax.ShapeDtypeStruct((B, dil, L, W), F32)] * 2,
        scratch_shapes=[pltpu.VMEM((n_cls, qb + ATTN_BLOCK, W), BF16)] * 2,
        compiler_params=pltpu.CompilerParams(
            dimension_semantics=("parallel", "parallel", "parallel"), vmem_limit_bytes=VMEM_LIMIT_BYTES),
        name=f"attn_d{dil}",
    )(q, k, v, k, v)


def _post_kernel(o1_ref, l1_ref, o4_ref, l4_ref, o16_ref, l16_ref, pg_ref, sga_ref, x_ref, ada_ref,
                 wao_ref, wo_ref, g1_ref, b1_ref, wrt_ref, brt_ref,
                 x1_ref, u2_ref, slot_ref, prob_ref, cnt_ref,
                 s0, s1, s2, s3):
    tm = x_ref.shape[1]

    def token_major(src_ref, scr_ref, dil):
        if dil == 1:
            return src_ref[0, 0]
        for r in range(dil):
            for c in range(GROUP_WIDTH // LANES):
                scr_ref[c, pl.ds(r, tm // dil, stride=dil), :] = src_ref[0, r, :, c * LANES:(c + 1) * LANES]
        return jnp.concatenate([scr_ref[c] for c in range(GROUP_WIDTH // LANES)], axis=1)

    o1, l1 = o1_ref[0, 0], l1_ref[0, 0]
    o4, l4 = token_major(o4_ref, s0, 4), token_major(l4_ref, s1, 4)
    o16, l16 = token_major(o16_ref, s2, 16), token_major(l16_ref, s3, 16)
    mx = jnp.maximum(jnp.maximum(l1, l4), l16)
    e1, e4, e16 = jnp.exp(l1 - mx), jnp.exp(l4 - mx), jnp.exp(l16 - mx)
    attn = (e1 * o1 + e4 * o4 + e16 * o16) / (e1 + e4 + e16)

    merged = pg_ref[0].astype(F32) + sga_ref[0].astype(F32) * _dot(attn.astype(BF16), wao_ref[...])
    mix = _dot(merged.astype(BF16), wo_ref[...])
    gate1 = ada_ref[0, 2:3, :]
    shift2 = ada_ref[0, 3:4, :]
    scale2 = ada_ref[0, 4:5, :]
    x1 = _layer_norm(DN_ALPHA * x_ref[0] + (1.0 + gate1) * mix) * g1_ref[...] + b1_ref[...]
    x1_ref[0] = x1
    u2 = _layer_norm(x1) * (1.0 + scale2) + shift2
    u2_hi = u2.astype(BF16)
    u2_ref[0] = u2_hi

    u2_lo = (u2 - u2_hi.astype(F32)).astype(BF16)
    nt_dot = lambda a, b: lax.dot_general(a, b, (((1,), (1,)), ((), ())), preferred_element_type=F32)
    logits = (nt_dot(wrt_ref[0], u2_hi) + nt_dot(wrt_ref[0], u2_lo) + nt_dot(wrt_ref[1], u2_hi)
              + brt_ref[...])
    eidx = lax.broadcasted_iota(jnp.int32, (N_EXPERTS, tm), 0)
    work = logits
    vals, idxs = [], []
    for _ in range(TOP_K):
        m = jnp.max(work, axis=0, keepdims=True)
        idx = jnp.min(jnp.where(work == m, eidx, N_EXPERTS), axis=0, keepdims=True)
        vals.append(m)
        idxs.append(idx)
        work = jnp.where(eidx == idx, -jnp.inf, work)
    exps = [jnp.exp(vk - vals[0]) for vk in vals]
    tot = exps[0] + exps[1] + exps[2] + exps[3]
    sel = jnp.zeros((N_EXPERTS, tm), F32)
    for idx in idxs:
        sel = sel + (eidx == idx).astype(F32)
    tr = lax.broadcasted_iota(jnp.int32, (tm, tm), 0)
    tc = lax.broadcasted_iota(jnp.int32, (tm, tm), 1)
    rank = _dot(sel.astype(BF16), (tr < tc).astype(BF16))
    cnt = jnp.sum(sel, axis=1, keepdims=True)
    run = jnp.floor((cnt + (ROW_ALIGN - 1)) * (1.0 / ROW_ALIGN)) * ROW_ALIGN
    er = lax.broadcasted_iota(jnp.int32, (N_EXPERTS, N_EXPERTS), 0)
    ec = lax.broadcasted_iota(jnp.int32, (N_EXPERTS, N_EXPERTS), 1)
    run_start = _dot((ec < er).astype(BF16),
                     jnp.broadcast_to(run, (N_EXPERTS, LANES)).astype(BF16))[:, 0:1]
    slot = rank + run_start
    for k in range(TOP_K):
        slot_ref[0, k:k + 1, :] = jnp.sum(jnp.where(eidx == idxs[k], slot, 0.0), axis=0,
                                          keepdims=True).astype(jnp.int32)
        prob_ref[0, k:k + 1, :] = exps[k] / tot
    cnt_ref[0] = cnt.astype(jnp.int32)


def _post(attn_outs, pg, sga, x, ada, w_attn_out, w_o, ln1_g, ln1_b, w_router, b_router):
    B, S, D = x.shape
    tm = POST_TILE
    nt = S // tm
    N = B * S
    const2 = lambda b, i: (0, 0)
    in_specs, args = [], []
    for (o, lse), (_, dil) in zip(attn_outs, ATTN_GROUPS):
        spec = pl.BlockSpec((1, dil, tm // dil, GROUP_WIDTH), lambda b, i: (b, 0, i, 0))
        in_specs += [spec, spec]
        args += [o, lse]
    tok_spec = pl.BlockSpec((1, tm, D), lambda b, i: (b, i, 0))
    in_specs += [tok_spec, tok_spec, tok_spec,
                 pl.BlockSpec((1, 6, D), lambda b, i: (b, 0, 0)),
                 pl.BlockSpec((GROUP_WIDTH, D), const2),
                 pl.BlockSpec((D, D), const2),
                 pl.BlockSpec((1, D), const2),
                 pl.BlockSpec((1, D), const2),
                 pl.BlockSpec((2, N_EXPERTS, D), lambda b, i: (0, 0, 0)),
                 pl.BlockSpec((N_EXPERTS, 1), const2)]
    wr_hi = w_router.T.astype(BF16)
    wr_lo = (w_router.T - wr_hi.astype(F32)).astype(BF16)
    args += [pg, sga, x, ada, w_attn_out.astype(BF16), w_o.astype(BF16),
             ln1_g.reshape(1, D), ln1_b.reshape(1, D), jnp.stack([wr_hi, wr_lo]),
             b_router.reshape(N_EXPERTS, 1)]
    nc = N // tm
    route_spec = pl.BlockSpec((1, TOP_K, tm), lambda b, i: (b * nt + i, 0, 0))
    out_specs = [tok_spec, tok_spec, route_spec, route_spec,
                 pl.BlockSpec((1, N_EXPERTS, 1), lambda b, i: (b * nt + i, 0, 0))]
    out_shapes = [jax.ShapeDtypeStruct((B, S, D), F32), jax.ShapeDtypeStruct((B, S, D), BF16),
                  jax.ShapeDtypeStruct((nc, TOP_K, tm), jnp.int32), jax.ShapeDtypeStruct((nc, TOP_K, tm), F32),
                  jax.ShapeDtypeStruct((nc, N_EXPERTS, 1), jnp.int32)]
    return pl.pallas_call(
        _post_kernel,
        grid=(B, nt),
        in_specs=in_specs,
        out_specs=out_specs,
        out_shape=out_shapes,
        scratch_shapes=[pltpu.VMEM((GROUP_WIDTH // LANES, tm, LANES), F32)] * 4,
        compiler_params=pltpu.CompilerParams(
            dimension_semantics=("parallel", "parallel"), vmem_limit_bytes=VMEM_LIMIT_BYTES),
        name="post",
    )(*args)


def _for_each_piece(tile, run_ref, dest_ref, len_ref, fn):
    def per_expert(e, total):
        idx = tile * N_EXPERTS + e
        loc, dst, length = run_ref[idx], dest_ref[idx], len_ref[idx]
        n_big = length // PIECE_ROWS
        rest = n_big * PIECE_ROWS

        def big(p, c):
            off = p * PIECE_ROWS
            fn(pl.multiple_of(loc + off, ROW_ALIGN), pl.multiple_of(dst + off, ROW_ALIGN), PIECE_ROWS)
            return c

        def small(p, c):
            off = rest + p * ROW_ALIGN
            fn(pl.multiple_of(loc + off, ROW_ALIGN), pl.multiple_of(dst + off, ROW_ALIGN), ROW_ALIGN)
            return c

        lax.fori_loop(0, n_big, big, 0)
        lax.fori_loop(0, (length - rest) // ROW_ALIGN, small, 0)
        return total + length

    return lax.fori_loop(0, N_EXPERTS, per_expert, 0)


def _wait_rows(src_ref, dst_ref, rows, sem):
    @pl.when(rows > 0)
    def _():
        n = pl.multiple_of(rows, ROW_ALIGN)
        pltpu.make_async_copy(src_ref.at[pl.ds(0, n)], dst_ref.at[pl.ds(0, n)], sem).wait()


def _dispatch_kernel(run_ref, dest_ref, len_ref, tail_ref, ntail_ref, nused_ref,
                     slot_ref, prob_ref, u2_ref, xs_hbm, local2_ref, zero_ref, sem2, zsem, rows_ref):
    tile = pl.program_id(0)
    buf = tile % 2
    local_ref = local2_ref.at[buf]
    sem = sem2.at[buf]
    tt, d = u2_ref.shape
    u2 = u2_ref[...]
    slots = [slot_ref[0, k:k + 1, :] for k in range(TOP_K)]
    probs = [prob_ref[0, k:k + 1, :] for k in range(TOP_K)]
    lane = lax.broadcasted_iota(jnp.int32, (SORT_CHUNK, LANES), 1)
    for r0 in range(0, LOCAL_ROWS, SORT_CHUNK):
        row = r0 + lax.broadcasted_iota(jnp.int32, (SORT_CHUNK, tt), 0)
        w = jnp.zeros((SORT_CHUNK, tt), F32)
        for k in range(TOP_K):
            w = jnp.where(row == slots[k], probs[k], w)
        onehot = jnp.where(w != 0.0, 1.0, 0.0).astype(BF16)
        local_ref[r0:r0 + SORT_CHUNK, 0:d] = _dot(onehot, u2).astype(BF16)
        wr = jnp.sum(w, axis=1, keepdims=True)
        hi = wr.astype(BF16).astype(F32)
        mid = (wr - hi).astype(BF16).astype(F32)
        lo = wr - hi - mid
        parts = jnp.where(lane == 0, hi, jnp.where(lane == 1, mid, jnp.where(lane == 2, lo, 0.0)))
        local_ref[r0:r0 + SORT_CHUNK, d:d + LANES] = parts.astype(BF16)

    def start_piece(loc, dst, rows):
        pltpu.make_async_copy(local_ref.at[pl.ds(loc, rows)], xs_hbm.at[pl.ds(dst, rows)], sem).start()

    rows_ref[buf] = _for_each_piece(tile, run_ref, dest_ref, len_ref, start_piece)

    @pl.when(tile > 0)
    def _():
        _wait_rows(local2_ref.at[1 - buf], xs_hbm, rows_ref[1 - buf], sem2.at[1 - buf])

    def zero_fill(first_row, n_rows, act):
        def copy(row, rows):
            return pltpu.make_async_copy(zero_ref.at[pl.ds(0, rows)],
                                         xs_hbm.at[pl.ds(pl.multiple_of(row, ROW_ALIGN), rows)], zsem)
        n_big = n_rows // ZERO_ROWS
        rest = first_row + n_big * ZERO_ROWS
        lax.fori_loop(0, n_big, lambda p, c: (act(copy(first_row + p * ZERO_ROWS, ZERO_ROWS)), c)[1], 0)
        lax.fori_loop(0, (n_rows - n_big * ZERO_ROWS) // ROW_ALIGN,
                      lambda p, c: (act(copy(rest + p * ROW_ALIGN, ROW_ALIGN)), c)[1], 0)

    def zero_fill_all(act):
        first_spare = nused_ref[0] * MOE_BLOCK
        lax.fori_loop(0, N_EXPERTS, lambda e, c: (zero_fill(tail_ref[e], ntail_ref[e], act), c)[1], 0)
        zero_fill(first_spare, xs_hbm.shape[0] - first_spare, act)

    @pl.when(tile == 0)
    def _():
        zero_ref[...] = jnp.zeros_like(zero_ref)
        zero_fill_all(lambda c: c.start())

    @pl.when(tile == pl.num_programs(0) - 1)
    def _():
        _wait_rows(local_ref, xs_hbm, rows_ref[buf], sem)
        zero_fill_all(lambda c: c.wait())


def _dispatch(tables, slot, prob, u2, n_rows):
    N, D = u2.shape
    tt = POST_TILE
    route_spec = pl.BlockSpec((1, TOP_K, tt), lambda i, *t: (i, 0, 0))
    grid_spec = pltpu.PrefetchScalarGridSpec(
        num_scalar_prefetch=len(tables),
        grid=(N // tt,),
        in_specs=[route_spec, route_spec, pl.BlockSpec((tt, D), lambda i, *t: (i, 0))],
        out_specs=pl.BlockSpec(memory_space=pl.ANY),
        scratch_shapes=[pltpu.VMEM((2, LOCAL_ROWS, D + LANES), BF16), pltpu.VMEM((ZERO_ROWS, D + LANES), BF16),
                        pltpu.SemaphoreType.DMA((2,)), pltpu.SemaphoreType.DMA(()),
                        pltpu.SMEM((2,), jnp.int32)],
    )
    return pl.pallas_call(
        _dispatch_kernel,
        grid_spec=grid_spec,
        out_shape=jax.ShapeDtypeStruct((n_rows, D + LANES), BF16),
        compiler_params=pltpu.CompilerParams(
            dimension_semantics=("arbitrary",), vmem_limit_bytes=VMEM_LIMIT_BYTES),
        name="dispatch",
    )(*tables, slot, prob, u2)


def _expert_kernel(be_ref, next_ref, live_ref, nused_ref, xs_ref, wg_hbm, bg_ref, wu_hbm, bu_ref, wd_hbm, bd_ref,
                   ys_ref, stage, wg_s, wu_s, wd_s, sem):
    i = pl.program_id(0)
    used = i < nused_ref[0]
    prev = be_ref[jnp.maximum(i - 1, 0)]
    fresh = (i == 0) | (be_ref[i] != prev)

    def fetch(e):
        return [pltpu.make_async_copy(w_hbm.at[e], stage.at[j], sem.at[j])
                for j, w_hbm in enumerate((wg_hbm, wu_hbm, wd_hbm))]

    @pl.when(i == 0)
    def _():
        for copy in fetch(be_ref[0]):
            copy.start()

    @pl.when(used & fresh)
    def _():
        for copy in fetch(be_ref[i]):
            copy.wait()
        wg_s[...] = stage[0].astype(BF16)
        wu_s[...] = stage[1].astype(BF16)
        wd_s[...] = stage[2].astype(BF16)

        @pl.when(next_ref[i] >= 0)
        def _():
            for copy in fetch(next_ref[i]):
                copy.start()

    d = ys_ref.shape[1]

    def expert(rows):
        xb = xs_ref[0:rows, 0:d]
        parts = xs_ref[0:rows, d:d + LANES].astype(F32)
        weight = parts[:, 0:1] + parts[:, 1:2] + parts[:, 2:3]
        g = _dot(xb, wg_s[...]) + bg_ref[0]
        up = _dot(xb, wu_s[...]) + bu_ref[0]
        g = jnp.minimum(g, SWIGLU_LIMIT)
        up = jnp.clip(up, -SWIGLU_LIMIT, SWIGLU_LIMIT)
        h = g * jax.nn.sigmoid(SWIGLU_ALPHA * g) * (up + 1.0)
        return ((_dot(h.astype(BF16), wd_s[...]) + bd_ref[0]) * weight).astype(BF16)

    for parts_live in range(1, MOE_BLOCK // EXPERT_PART + 1):
        rows = parts_live * EXPERT_PART

        @pl.when(used & (live_ref[i] == parts_live))
        def _():
            if rows == MOE_BLOCK:
                ys_ref[...] = expert(rows)
            else:
                ys_ref[0:rows, :] = expert(rows)
                ys_ref[rows:, :] = jnp.zeros((MOE_BLOCK - rows, d), BF16)

    @pl.when(jnp.logical_not(used))
    def _():
        ys_ref[...] = jnp.zeros_like(ys_ref)


def _experts(be, next_expert, live_parts, nused, xs, w_gate, b_gate, w_up, b_up, w_down, b_down):
    P, width = xs.shape
    D = width - LANES
    E = w_gate.shape[0]

    def live(i, nu):
        return jnp.maximum(jnp.minimum(i, nu[0] - 1), 0)

    w_spec = pl.BlockSpec(memory_space=pl.ANY)
    b_spec = pl.BlockSpec((1, 1, D), lambda i, be, nx, hf, nu: (be[live(i, nu)], 0, 0))
    grid_spec = pltpu.PrefetchScalarGridSpec(
        num_scalar_prefetch=4,
        grid=(P // MOE_BLOCK,),
        in_specs=[pl.BlockSpec((MOE_BLOCK, width), lambda i, be, nx, hf, nu: (live(i, nu), 0)),
                  w_spec, b_spec, w_spec, b_spec, w_spec, b_spec],
        out_specs=pl.BlockSpec((MOE_BLOCK, D), lambda i, be, nx, hf, nu: (i, 0)),
        scratch_shapes=[pltpu.VMEM((3, D, D), F32)] + [pltpu.VMEM((D, D), BF16)] * 3
                       + [pltpu.SemaphoreType.DMA((3,))],
    )
    return pl.pallas_call(
        _expert_kernel,
        grid_spec=grid_spec,
        out_shape=jax.ShapeDtypeStruct((P, D), BF16),
        compiler_params=pltpu.CompilerParams(
            dimension_semantics=("arbitrary",), vmem_limit_bytes=VMEM_LIMIT_BYTES),
        name="experts",
    )(be, next_expert, live_parts, nused, xs, w_gate, b_gate.reshape(E, 1, D), w_up, b_up.reshape(E, 1, D),
      w_down, b_down.reshape(E, 1, D))


def _combine_kernel(run_ref, dest_ref, len_ref,
                    slot_ref, x1_ref, ada_ref, g2_ref, b2_ref, ys_hbm,
                    out_ref, local2_ref, sem2, rows_ref):
    tile = pl.program_id(0)
    buf = tile % 2
    tt = x1_ref.shape[0]

    def fetch(t, b):
        def start_piece(loc, dst, rows):
            pltpu.make_async_copy(ys_hbm.at[pl.ds(dst, rows)], local2_ref.at[b, pl.ds(loc, rows)],
                                  sem2.at[b]).start()
        rows_ref[b] = _for_each_piece(t, run_ref, dest_ref, len_ref, start_piece)

    @pl.when(tile == 0)
    def _():
        local2_ref[...] = jnp.zeros_like(local2_ref)
        fetch(0, 0)

    @pl.when(tile + 1 < pl.num_programs(0))
    def _():
        fetch(tile + 1, 1 - buf)

    local_ref = local2_ref.at[buf]
    _wait_rows(ys_hbm, local_ref, rows_ref[buf], sem2.at[buf])
    slots = [slot_ref[:, k:k + 1] for k in range(TOP_K)]

    ffn = jnp.zeros((tt, x1_ref.shape[1]), F32)
    for r0 in range(0, LOCAL_ROWS, SORT_CHUNK):
        col = r0 + lax.broadcasted_iota(jnp.int32, (tt, SORT_CHUNK), 1)
        onehot = jnp.zeros((tt, SORT_CHUNK), F32)
        for k in range(TOP_K):
            onehot = jnp.where(col == slots[k], 1.0, onehot)
        ffn = ffn + _dot(onehot.astype(BF16), local_ref[r0:r0 + SORT_CHUNK, :])
    gate2 = ada_ref[0, 5:6, :]
    y = DN_ALPHA * x1_ref[...] + (1.0 + gate2) * ffn
    out_ref[...] = _layer_norm(y) * g2_ref[...] + b2_ref[...]


def _combine(tables, slot_tok, x1, ada, ln2_g, ln2_b, ys, tiles_per_batch):
    N, D = x1.shape
    tt = POST_TILE
    const = lambda i, *t: (0, 0)
    tok4 = pl.BlockSpec((tt, TOP_K), lambda i, *t: (i, 0))
    grid_spec = pltpu.PrefetchScalarGridSpec(
        num_scalar_prefetch=len(tables),
        grid=(N // tt,),
        in_specs=[tok4,
                  pl.BlockSpec((tt, D), lambda i, *t: (i, 0)),
                  pl.BlockSpec((1, 6, D), lambda i, *t: (i // tiles_per_batch, 0, 0)),
                  pl.BlockSpec((1, D), const),
                  pl.BlockSpec((1, D), const),
                  pl.BlockSpec(memory_space=pl.ANY)],
        out_specs=pl.BlockSpec((tt, D), lambda i, *t: (i, 0)),
        scratch_shapes=[pltpu.VMEM((2, LOCAL_ROWS, D), BF16), pltpu.SemaphoreType.DMA((2,)),
                        pltpu.SMEM((2,), jnp.int32)],
    )
    return pl.pallas_call(
        _combine_kernel,
        grid_spec=grid_spec,
        out_shape=jax.ShapeDtypeStruct((N, D), F32),
        compiler_params=pltpu.CompilerParams(
            dimension_semantics=("arbitrary",), vmem_limit_bytes=VMEM_LIMIT_BYTES),
        name="combine",
    )(*tables, slot_tok, x1, ada, ln2_g.reshape(1, D), ln2_b.reshape(1, D), ys)


def _routing_tables(cnt, nblk):
    nt, E = cnt.shape
    i32 = jnp.int32
    run = (cnt + ROW_ALIGN - 1) // ROW_ALIGN * ROW_ALIGN
    run_start = jnp.cumsum(run, axis=1) - run
    seg_len = jnp.sum(run, axis=0)
    seg_blocks = (seg_len + MOE_BLOCK - 1) // MOE_BLOCK
    b_end = jnp.cumsum(seg_blocks)
    seg_off = (b_end - seg_blocks) * MOE_BLOCK
    dest = seg_off[None, :] + jnp.cumsum(run, axis=0) - run
    nused = b_end[-1:].astype(i32)
    i = jnp.arange(nblk, dtype=i32)
    be = jnp.minimum(jnp.sum((i[:, None] >= b_end[None, :]).astype(i32), axis=1), E - 1)
    e_ids = jnp.arange(E, dtype=i32)
    later = (e_ids[None, :] > e_ids[:, None]) & (seg_blocks > 0)[None, :]
    next_e = jnp.min(jnp.where(later, e_ids[None, :], E), axis=1)
    next_e = jnp.where(next_e < E, next_e, -1)
    mine = be[:, None] == e_ids[None, :]
    pick = lambda v: jnp.sum(jnp.where(mine, v[None, :], 0), axis=1)
    nxt = pick(next_e)
    rows_left = pick(seg_len) - (i - pick(b_end - seg_blocks)) * MOE_BLOCK
    live_parts = jnp.clip((rows_left + EXPERT_PART - 1) // EXPERT_PART, 1, MOE_BLOCK // EXPERT_PART).astype(i32)
    tail = seg_off + seg_len
    ntail = seg_blocks * MOE_BLOCK - seg_len
    piece_tables = (run_start.reshape(-1).astype(i32), dest.reshape(-1).astype(i32),
                    run.reshape(-1).astype(i32))
    return piece_tables, (tail.astype(i32), ntail.astype(i32)), (be.astype(i32), nxt.astype(i32), live_parts, nused)


def kernel(x, c, positions, w_ada, b_ada, w_in, pool_w, pool_scale, w_pool_out, w_attn_out, w_o,
           ln1_g, ln1_b, w_router, b_router, w_gate, b_gate, w_up, b_up, w_down, b_down, ln2_g, ln2_b):
    B, S, D = x.shape
    N = B * S
    assert D == D_MODEL and S % PROJ_TILE == 0 and S % (16 * ATTN_BLOCK) == 0
    assert S % POST_TILE == 0 and LOCAL_ROWS % SORT_CHUNK == 0
    nt = N // POST_TILE
    nblk = (N * TOP_K + nt * N_EXPERTS * (ROW_ALIGN - 1)) // MOE_BLOCK + N_EXPERTS
    for l in range(DEPTH):
        ada = _ada(c, w_ada[l], b_ada[l])
        proj_out = _proj(x, positions, ada, w_in[l], pool_w[l], pool_scale[l], w_pool_out[l])
        qkv, (pg, sga) = proj_out[:9], proj_out[9:]
        attn_outs = [_attention(*qkv[3 * g:3 * g + 3]) for g in range(len(ATTN_GROUPS))]
        x1, u2, slot, prob, cnt = _post(attn_outs, pg, sga, x, ada, w_attn_out[l], w_o[l],
                                        ln1_g[l], ln1_b[l], w_router[l], b_router[l])
        piece_tables, tail_tables, block_tables = _routing_tables(cnt.reshape(nt, N_EXPERTS), nblk)
        xs = _dispatch(piece_tables + tail_tables + block_tables[3:], slot, prob, u2.reshape(N, D),
                       nblk * MOE_BLOCK)
        ys = _experts(*block_tables, xs, w_gate[l], b_gate[l], w_up[l], b_up[l], w_down[l], b_down[l])
        slot_tok = slot.transpose(0, 2, 1).reshape(N, TOP_K)
        out = _combine(piece_tables, slot_tok, x1.reshape(N, D), ada,
                       ln2_g[l], ln2_b[l], ys, S // POST_TILE)
        x = out.reshape(B, S, D)
    return x
```

```python
import jax
import jax.numpy as jnp
import numpy as np
from jax import lax
from jax.experimental import pallas as pl
from jax.experimental.pallas import tpu as pltpu

F32 = jnp.float32
BF16 = jnp.bfloat16

D_MODEL = 1024
POOL_WINDOWS = (2, 4, 8, 16)
POOL_WIDTH = D_MODEL // 2
POOL_GROUP = POOL_WIDTH // len(POOL_WINDOWS)
POOL_HALO = 16
HEAD_DIM = 64
ATTN_GROUPS = ((128, 1), (512, 4), (2048, 16))
HEADS_PER_GROUP = 4
GROUP_WIDTH = HEADS_PER_GROUP * HEAD_DIM
N_HEADS = HEADS_PER_GROUP * len(ATTN_GROUPS)
ATTN_WIDTH = N_HEADS * HEAD_DIM
ATTN_BLOCK = 128
ROT_DIM = HEAD_DIM // 4
ROPE_THETA = 500000.0
N_EXPERTS = 32
TOP_K = 4
SWIGLU_ALPHA = 1.702
SWIGLU_LIMIT = 7.0
MOE_BLOCK = 1024
EXPERT_PART = 128
DEPTH = 1
DN_ALPHA = (2.0 * DEPTH) ** 0.25
LN_EPS = 1e-5
NEG_INF = -1e30

OFF_Q = POOL_WIDTH
OFF_K = OFF_Q + ATTN_WIDTH
OFF_V = OFF_K + ATTN_WIDTH
OFF_GP = OFF_V + ATTN_WIDTH
OFF_GA = OFF_GP + D_MODEL
IN_WIDTH = OFF_GA + D_MODEL

VMEM_LIMIT_BYTES = 56 * 1024 * 1024
LANES = 128
SUBLANES = 8

PROJ_TILE = 512
POST_TILE = 512
ATTN_QROWS = 1024
ROW_ALIGN = 16
PIECE_ROWS = 64
ZERO_ROWS = 128
SORT_CHUNK = 512
LOCAL_ROWS = -(-(POST_TILE * TOP_K + N_EXPERTS * (ROW_ALIGN - 1)) // SORT_CHUNK) * SORT_CHUNK
MAX_BIG_PIECES = LOCAL_ROWS // PIECE_ROWS
MAX_SMALL_PIECES = N_EXPERTS * (PIECE_ROWS // ROW_ALIGN - 1)


def _layer_norm(x):
    mu = jnp.mean(x, axis=-1, keepdims=True)
    xc = x - mu
    var = jnp.mean(xc * xc, axis=-1, keepdims=True)
    return xc * lax.rsqrt(var + LN_EPS)


def _dot(a, b):
    return jnp.dot(a, b, preferred_element_type=F32)


def _ada_kernel(c_ref, w_ref, b_ref, o_ref):
    c = c_ref[...]
    s = c * jax.nn.sigmoid(c)
    o_ref[...] = jnp.dot(s, w_ref[...], preferred_element_type=F32,
                         precision=lax.Precision.HIGHEST) + b_ref[...]


def _ada(c, w_ada, b_ada):
    B, D = c.shape
    rows = SUBLANES
    c_pad = jnp.pad(c, ((0, rows - B), (0, 0)))
    n_out = w_ada.shape[1]
    out = pl.pallas_call(
        _ada_kernel,
        grid=(n_out // D,),
        in_specs=[pl.BlockSpec((rows, D), lambda j: (0, 0)),
                  pl.BlockSpec((D, D), lambda j: (0, j)),
                  pl.BlockSpec((1, D), lambda j: (0, j))],
        out_specs=pl.BlockSpec((rows, D), lambda j: (0, j)),
        out_shape=jax.ShapeDtypeStruct((rows, n_out), F32),
        name="ada",
    )(c_pad, w_ada, b_ada.reshape(1, n_out))
    return out[:B].reshape(B, 6, D)


ROPE_PART_ROWS = 32


def _rope_tables():
    lane = np.arange(LANES)
    li = lane % HEAD_DIM
    half = ROT_DIM // 2
    inv_freq = jnp.power(ROPE_THETA, -jnp.arange(half, dtype=F32) * (2.0 / ROT_DIM))
    invf = jnp.broadcast_to(inv_freq[:, None], (half, LANES))
    freq = np.arange(ROPE_PART_ROWS)[:, None]
    live = (freq < 3 * half) & (freq % half == (li % half)[None, :])
    place = np.stack([live & (li < ROT_DIM)[None, :],
                      -1.0 * (live & (li < half)[None, :]),
                      live & ((li >= half) & (li < ROT_DIM))[None, :]]).astype(np.float32)
    keep = (li >= ROT_DIM).astype(np.float32)[None, :]
    return invf, jnp.asarray(place, BF16), jnp.asarray(keep)


def _proj_kernel(x_ref, xh_ref, pos_ref, ada_ref, invf_ref, place_ref, keep_ref,
                 win_ref, poolw_ref, pscale_ref, wpo_ref,
                 q1_ref, k1_ref, v1_ref, q4_ref, k4_ref, v4_ref, q16_ref, k16_ref, v16_ref,
                 pg_ref, sga_ref, xpe_ref, cls_ref):
    tm = x_ref.shape[1]
    i = pl.program_id(1)
    shift1 = ada_ref[0, 0:1, :]
    scale1 = ada_ref[0, 1:2, :]

    def modulated(xv):
        return (_layer_norm(xv) * (1.0 + scale1) + shift1).astype(BF16)

    u = modulated(x_ref[0])
    uh = modulated(xh_ref[0])

    xp = _dot(u, win_ref[:, 0:POOL_WIDTH])
    xph = _dot(uh, win_ref[:, 0:POOL_WIDTH])
    xph = jnp.where(i > 0, xph, 0.0)
    xpe_ref[0:POOL_HALO, :] = xph
    xpe_ref[POOL_HALO:, :] = xp
    tok = i * tm + lax.broadcasted_iota(jnp.int32, (tm, 1), 0)
    ys = []
    for g, w in enumerate(POOL_WINDOWS):
        cols = slice(g * POOL_GROUP, (g + 1) * POOL_GROUP)
        xg = xp[:, cols]
        acc = xg
        for j in range(1, w):
            acc = acc + xpe_ref[POOL_HALO - j:POOL_HALO - j + tm, cols]
        cnt = jnp.minimum(tok + 1, w).astype(F32)
        mixed = (acc / cnt - xg).astype(BF16)
        ys.append(_dot(mixed, poolw_ref[g]) * pscale_ref[:, cols])
    y = jnp.concatenate(ys, axis=1).astype(BF16)
    pooled = _dot(y, wpo_ref[...])
    g_p = _dot(u, win_ref[:, OFF_GP:OFF_GP + D_MODEL])
    pg_ref[0] = jax.nn.sigmoid(g_p) * pooled
    g_a = _dot(u, win_ref[:, OFF_GA:OFF_GA + D_MODEL])
    sga_ref[0] = jax.nn.sigmoid(g_a)

    ang = invf_ref[:, 0:1] * pos_ref[0, 0].astype(F32)

    def on_lanes(table, j):
        hi = table.astype(BF16).astype(F32)
        mid = (table - hi).astype(BF16).astype(F32)
        lo = table - hi - mid
        pad = jnp.zeros((ROPE_PART_ROWS - 3 * table.shape[0], tm), F32)
        parts = jnp.concatenate([hi, mid, lo, pad], axis=0).astype(BF16)
        return lax.dot_general(parts, place_ref[j], (((0,), (0,)), ((), ())), preferred_element_type=F32)

    cos = jnp.cos(ang)
    sin = jnp.sin(ang)
    c_mul = on_lanes(cos, 0) + keep_ref[...]
    s_lo = on_lanes(sin, 1)
    s_hi = on_lanes(sin, 2)
    c_mul = jnp.concatenate([c_mul, c_mul], axis=1)
    s_lo = jnp.concatenate([s_lo, s_lo], axis=1)
    s_hi = jnp.concatenate([s_hi, s_hi], axis=1)
    half = ROT_DIM // 2

    def rotate(a):
        up = pltpu.roll(a, GROUP_WIDTH - half, axis=1)
        dn = pltpu.roll(a, half, axis=1)
        return a * c_mul + up * s_lo + dn * s_hi

    def emit(a, out_ref, dil):
        if dil == 1:
            out_ref[0, 0] = a.astype(BF16)
            return
        for c in range(GROUP_WIDTH // LANES):
            cls_ref[c] = a[:, c * LANES:(c + 1) * LANES]
        for r in range(dil):
            for c in range(GROUP_WIDTH // LANES):
                out_ref[0, r, :, c * LANES:(c + 1) * LANES] = (
                    cls_ref[c, pl.ds(r, tm // dil, stride=dil), :].astype(BF16))

    outs = ((q1_ref, k1_ref, v1_ref), (q4_ref, k4_ref, v4_ref), (q16_ref, k16_ref, v16_ref))
    for gi, (_, dil) in enumerate(ATTN_GROUPS):
        qo, ko, vo = outs[gi]
        c0 = gi * GROUP_WIDTH
        emit(rotate(_dot(u, win_ref[:, OFF_Q + c0:OFF_Q + c0 + GROUP_WIDTH])), qo, dil)
        emit(rotate(_dot(u, win_ref[:, OFF_K + c0:OFF_K + c0 + GROUP_WIDTH])), ko, dil)
        emit(_dot(u, win_ref[:, OFF_V + c0:OFF_V + c0 + GROUP_WIDTH]), vo, dil)


def _proj(x, positions, ada, w_in, pool_w, pool_scale, w_pool_out):
    B, S, D = x.shape
    tm = PROJ_TILE
    nt = S // tm
    halo_blocks = tm // POOL_HALO
    const2 = lambda b, i: (0, 0)
    in_specs = [
        pl.BlockSpec((1, tm, D), lambda b, i: (b, i, 0)),
        pl.BlockSpec((1, POOL_HALO, D), lambda b, i: (b, jnp.maximum(i * halo_blocks - 1, 0), 0)),
        pl.BlockSpec((1, 1, 1, tm), lambda b, i: (b, i, 0, 0)),
        pl.BlockSpec((1, 6, D), lambda b, i: (b, 0, 0)),
        pl.BlockSpec((ROT_DIM // 2, LANES), const2),
        pl.BlockSpec((3, ROPE_PART_ROWS, LANES), lambda b, i: (0, 0, 0)),
        pl.BlockSpec((1, LANES), const2),
        pl.BlockSpec((D, IN_WIDTH), const2),
        pl.BlockSpec((len(POOL_WINDOWS), POOL_GROUP, POOL_GROUP), lambda b, i: (0, 0, 0)),
        pl.BlockSpec((1, POOL_WIDTH), const2),
        pl.BlockSpec((POOL_WIDTH, D), const2),
    ]
    out_specs, out_shapes = [], []
    for _, dil in ATTN_GROUPS:
        for _ in range(3):
            out_specs.append(pl.BlockSpec((1, dil, tm // dil, GROUP_WIDTH), lambda b, i: (b, 0, i, 0)))
            out_shapes.append(jax.ShapeDtypeStruct((B, dil, S // dil, GROUP_WIDTH), BF16))
    for _ in range(2):
        out_specs.append(pl.BlockSpec((1, tm, D), lambda b, i: (b, i, 0)))
        out_shapes.append(jax.ShapeDtypeStruct((B, S, D), F32))
    return pl.pallas_call(
        _proj_kernel,
        grid=(B, nt),
        in_specs=in_specs,
        out_specs=out_specs,
        out_shape=out_shapes,
        scratch_shapes=[pltpu.VMEM((tm + POOL_HALO, POOL_WIDTH), F32),
                        pltpu.VMEM((GROUP_WIDTH // LANES, tm, LANES), F32)],
        compiler_params=pltpu.CompilerParams(
            dimension_semantics=("parallel", "parallel"), vmem_limit_bytes=VMEM_LIMIT_BYTES),
        name="proj",
    )(x, x, positions.reshape(B, nt, 1, tm), ada, *_rope_tables(), w_in.astype(BF16),
      pool_w.astype(BF16), pool_scale.reshape(1, POOL_WIDTH), w_pool_out.astype(BF16))


def _attn_kernel(q_ref, k_ref, v_ref, kh_ref, vh_ref, o_ref, lse_ref, kf_ref, vf_ref):
    n_cls, qb = q_ref.shape[1], q_ref.shape[2]
    per = qb // ATTN_BLOCK
    n = pl.program_id(2)
    for c in range(n_cls):
        kf_ref[c, 0:ATTN_BLOCK, :] = kh_ref[0, c]
        kf_ref[c, ATTN_BLOCK:, :] = k_ref[0, c]
        vf_ref[c, 0:ATTN_BLOCK, :] = vh_ref[0, c]
        vf_ref[c, ATTN_BLOCK:, :] = v_ref[0, c]
    qi = lax.broadcasted_iota(jnp.int32, (ATTN_BLOCK, 2 * ATTN_BLOCK), 0)
    kj = lax.broadcasted_iota(jnp.int32, (ATTN_BLOCK, 2 * ATTN_BLOCK), 1)
    band = (kj >= qi) & (kj <= qi + ATTN_BLOCK)
    band_bias = jnp.where(band, 0.0, NEG_INF)
    lane = lax.broadcasted_iota(jnp.int32, (ATTN_BLOCK, GROUP_WIDTH), 1)
    low_lanes = lax.broadcasted_iota(jnp.int32, (ATTN_BLOCK, LANES), 1) < HEAD_DIM
    ones = jnp.ones((2 * ATTN_BLOCK, LANES), BF16)
    nh, blk = HEADS_PER_GROUP, ATTN_BLOCK

    def block(t, carry):
        c, j = t // per, t % per
        r0 = pl.multiple_of(j * ATTN_BLOCK, ATTN_BLOCK)
        first_key = jnp.where((n > 0) | (j > 0), 0, ATTN_BLOCK)
        bias = band_bias + jnp.where(kj < first_key, NEG_INF, 0.0)
        q = q_ref[0, c, pl.ds(r0, blk), :].astype(F32)
        kk = kf_ref[c, pl.ds(r0, 2 * blk), :]
        vv = vf_ref[c, pl.ds(r0, 2 * blk), :]
        qs = jnp.concatenate([jnp.where((lane >= h * HEAD_DIM) & (lane < (h + 1) * HEAD_DIM), q, 0.0)
                              for h in range(nh)], axis=0).astype(BF16)
        s = lax.dot_general(qs, kk, (((1,), (1,)), ((), ())), preferred_element_type=F32)
        s = jnp.concatenate([s[h * blk:(h + 1) * blk] * (HEAD_DIM ** -0.5) + bias for h in range(nh)], axis=0)
        m = jnp.max(s, axis=-1, keepdims=True)
        p = jnp.exp(s - m).astype(BF16)
        den = _dot(p, ones)
        lse = m + jnp.log(den)
        for hp in range(GROUP_WIDTH // LANES):
            rows = slice(2 * hp * blk, (2 * hp + 2) * blk)
            ls = slice(hp * LANES, (hp + 1) * LANES)
            o2 = _dot(p[rows], vv[:, ls]) / den[rows]
            l2 = lse[rows]
            o_ref[0, c, pl.ds(r0, blk), ls] = jnp.where(low_lanes, o2[0:blk], o2[blk:2 * blk])
            lse_ref[0, c, pl.ds(r0, blk), ls] = jnp.where(low_lanes, l2[0:blk], l2[blk:2 * blk])
        return carry

    lax.fori_loop(0, n_cls * per, block, 0, unroll=8)


def _attention(q, k, v):
    B, dil, L, W = q.shape
    qb = min(L, ATTN_QROWS)
    per = qb // ATTN_BLOCK
    n_cls = min(ATTN_QROWS // qb, dil)
    main = pl.BlockSpec((1, n_cls, qb, W), lambda b, r, n: (b, r, n, 0))
    halo = pl.BlockSpec((1, n_cls, ATTN_BLOCK, W), lambda b, r, n: (b, r, jnp.maximum(n * per - 1, 0), 0))
    return pl.pallas_call(
        _attn_kernel,
        grid=(B, dil // n_cls, L // qb),
        in_specs=[main, main, main, halo, halo],
        out_specs=[main, main],
        out_shape=[jax.ShapeDtypeStruct((B, dil, L, W), F32)] * 2,
        scratch_shapes=[pltpu.VMEM((n_cls, qb + ATTN_BLOCK, W), BF16)] * 2,
        compiler_params=pltpu.CompilerParams(
            dimension_semantics=("parallel", "parallel", "parallel"), vmem_limit_bytes=VMEM_LIMIT_BYTES),
        name=f"attn_d{dil}",
    )(q, k, v, k, v)


def _post_kernel(o1_ref, l1_ref, o4_ref, l4_ref, o16_ref, l16_ref, pg_ref, sga_ref, x_ref, ada_ref,
                 wao_ref, wo_ref, g1_ref, b1_ref, wrt_ref, brt_ref,
                 x1_ref, u2_ref, slot_ref, prob_ref, cnt_ref,
                 s0, s1, s2, s3):
    tm = x_ref.shape[1]

    def token_major(src_ref, scr_ref, dil):
        if dil == 1:
            return src_ref[0, 0]
        for r in range(dil):
            for c in range(GROUP_WIDTH // LANES):
                scr_ref[c, pl.ds(r, tm // dil, stride=dil), :] = src_ref[0, r, :, c * LANES:(c + 1) * LANES]
        return jnp.concatenate([scr_ref[c] for c in range(GROUP_WIDTH // LANES)], axis=1)

    o1, l1 = o1_ref[0, 0], l1_ref[0, 0]
    o4, l4 = token_major(o4_ref, s0, 4), token_major(l4_ref, s1, 4)
    o16, l16 = token_major(o16_ref, s2, 16), token_major(l16_ref, s3, 16)
    mx = jnp.maximum(jnp.maximum(l1, l4), l16)
    e1, e4, e16 = jnp.exp(l1 - mx), jnp.exp(l4 - mx), jnp.exp(l16 - mx)
    attn = (e1 * o1 + e4 * o4 + e16 * o16) / (e1 + e4 + e16)

    merged = pg_ref[0] + sga_ref[0] * _dot(attn.astype(BF16), wao_ref[...])
    mix = _dot(merged.astype(BF16), wo_ref[...])
    gate1 = ada_ref[0, 2:3, :]
    shift2 = ada_ref[0, 3:4, :]
    scale2 = ada_ref[0, 4:5, :]
    x1 = _layer_norm(DN_ALPHA * x_ref[0] + (1.0 + gate1) * mix) * g1_ref[...] + b1_ref[...]
    x1_ref[0] = x1
    u2 = _layer_norm(x1) * (1.0 + scale2) + shift2
    u2_hi = u2.astype(BF16)
    u2_ref[0] = u2_hi

    u2_lo = (u2 - u2_hi.astype(F32)).astype(BF16)
    nt_dot = lambda a, b: lax.dot_general(a, b, (((1,), (1,)), ((), ())), preferred_element_type=F32)
    logits = (nt_dot(wrt_ref[0], u2_hi) + nt_dot(wrt_ref[0], u2_lo) + nt_dot(wrt_ref[1], u2_hi)
              + brt_ref[...])
    eidx = lax.broadcasted_iota(jnp.int32, (N_EXPERTS, tm), 0)
    work = logits
    vals, idxs = [], []
    for _ in range(TOP_K):
        m = jnp.max(work, axis=0, keepdims=True)
        idx = jnp.min(jnp.where(work == m, eidx, N_EXPERTS), axis=0, keepdims=True)
        vals.append(m)
        idxs.append(idx)
        work = jnp.where(eidx == idx, -jnp.inf, work)
    exps = [jnp.exp(vk - vals[0]) for vk in vals]
    tot = exps[0] + exps[1] + exps[2] + exps[3]
    sel = jnp.zeros((N_EXPERTS, tm), F32)
    for idx in idxs:
        sel = sel + (eidx == idx).astype(F32)
    tr = lax.broadcasted_iota(jnp.int32, (tm, tm), 0)
    tc = lax.broadcasted_iota(jnp.int32, (tm, tm), 1)
    rank = _dot(sel.astype(BF16), (tr < tc).astype(BF16))
    cnt = jnp.sum(sel, axis=1, keepdims=True)
    run = jnp.floor((cnt + (ROW_ALIGN - 1)) * (1.0 / ROW_ALIGN)) * ROW_ALIGN
    er = lax.broadcasted_iota(jnp.int32, (N_EXPERTS, N_EXPERTS), 0)
    ec = lax.broadcasted_iota(jnp.int32, (N_EXPERTS, N_EXPERTS), 1)
    run_start = _dot((ec < er).astype(BF16),
                     jnp.broadcast_to(run, (N_EXPERTS, LANES)).astype(BF16))[:, 0:1]
    slot = rank + run_start
    for k in range(TOP_K):
        slot_ref[0, k:k + 1, :] = jnp.sum(jnp.where(eidx == idxs[k], slot, 0.0), axis=0,
                                          keepdims=True).astype(jnp.int32)
        prob_ref[0, k:k + 1, :] = exps[k] / tot
    cnt_ref[0] = cnt.astype(jnp.int32)


def _post(attn_outs, pg, sga, x, ada, w_attn_out, w_o, ln1_g, ln1_b, w_router, b_router):
    B, S, D = x.shape
    tm = POST_TILE
    nt = S // tm
    N = B * S
    const2 = lambda b, i: (0, 0)
    in_specs, args = [], []
    for (o, lse), (_, dil) in zip(attn_outs, ATTN_GROUPS):
        spec = pl.BlockSpec((1, dil, tm // dil, GROUP_WIDTH), lambda b, i: (b, 0, i, 0))
        in_specs += [spec, spec]
        args += [o, lse]
    tok_spec = pl.BlockSpec((1, tm, D), lambda b, i: (b, i, 0))
    in_specs += [tok_spec, tok_spec, tok_spec,
                 pl.BlockSpec((1, 6, D), lambda b, i: (b, 0, 0)),
                 pl.BlockSpec((GROUP_WIDTH, D), const2),
                 pl.BlockSpec((D, D), const2),
                 pl.BlockSpec((1, D), const2),
                 pl.BlockSpec((1, D), const2),
                 pl.BlockSpec((2, N_EXPERTS, D), lambda b, i: (0, 0, 0)),
                 pl.BlockSpec((N_EXPERTS, 1), const2)]
    wr_hi = w_router.T.astype(BF16)
    wr_lo = (w_router.T - wr_hi.astype(F32)).astype(BF16)
    args += [pg, sga, x, ada, w_attn_out.astype(BF16), w_o.astype(BF16),
             ln1_g.reshape(1, D), ln1_b.reshape(1, D), jnp.stack([wr_hi, wr_lo]),
             b_router.reshape(N_EXPERTS, 1)]
    nc = N // tm
    route_spec = pl.BlockSpec((1, TOP_K, tm), lambda b, i: (b * nt + i, 0, 0))
    out_specs = [tok_spec, tok_spec, route_spec, route_spec,
                 pl.BlockSpec((1, N_EXPERTS, 1), lambda b, i: (b * nt + i, 0, 0))]
    out_shapes = [jax.ShapeDtypeStruct((B, S, D), F32), jax.ShapeDtypeStruct((B, S, D), BF16),
                  jax.ShapeDtypeStruct((nc, TOP_K, tm), jnp.int32), jax.ShapeDtypeStruct((nc, TOP_K, tm), F32),
                  jax.ShapeDtypeStruct((nc, N_EXPERTS, 1), jnp.int32)]
    return pl.pallas_call(
        _post_kernel,
        grid=(B, nt),
        in_specs=in_specs,
        out_specs=out_specs,
        out_shape=out_shapes,
        scratch_shapes=[pltpu.VMEM((GROUP_WIDTH // LANES, tm, LANES), F32)] * 4,
        compiler_params=pltpu.CompilerParams(
            dimension_semantics=("parallel", "parallel"), vmem_limit_bytes=VMEM_LIMIT_BYTES),
        name="post",
    )(*args)


def _for_each_piece(tile, piece_refs, fn):
    big_loc, big_dst, small_loc, small_dst, count = piece_refs
    n_big, n_small = count[2 * tile], count[2 * tile + 1]

    def piece(loc_ref, dst_ref, cap, rows):
        def body(p, carry):
            i = tile * cap + p
            fn(pl.multiple_of(loc_ref[i], ROW_ALIGN), pl.multiple_of(dst_ref[i], ROW_ALIGN), rows)
            return carry
        return body

    lax.fori_loop(0, n_big, piece(big_loc, big_dst, MAX_BIG_PIECES, PIECE_ROWS), 0)
    lax.fori_loop(0, n_small, piece(small_loc, small_dst, MAX_SMALL_PIECES, ROW_ALIGN), 0)
    return n_big * PIECE_ROWS + n_small * ROW_ALIGN


def _wait_rows(src_ref, dst_ref, rows, sem):
    @pl.when(rows > 0)
    def _():
        n = pl.multiple_of(rows, ROW_ALIGN)
        pltpu.make_async_copy(src_ref.at[pl.ds(0, n)], dst_ref.at[pl.ds(0, n)], sem).wait()


def _dispatch_kernel(big_loc, big_dst, small_loc, small_dst, piece_count, tail_ref, ntail_ref, nused_ref,
                     slot_ref, prob_ref, u2_ref, xs_hbm, local2_ref, zero_ref, sem2, zsem, rows_ref):
    tile = pl.program_id(0)
    buf = tile % 2
    local_ref = local2_ref.at[buf]
    sem = sem2.at[buf]
    tt, d = u2_ref.shape
    u2 = u2_ref[...]
    slots = [slot_ref[0, k:k + 1, :] for k in range(TOP_K)]
    probs = [prob_ref[0, k:k + 1, :] for k in range(TOP_K)]
    lane = lax.broadcasted_iota(jnp.int32, (SORT_CHUNK, LANES), 1)
    for r0 in range(0, LOCAL_ROWS, SORT_CHUNK):
        row = r0 + lax.broadcasted_iota(jnp.int32, (SORT_CHUNK, tt), 0)
        w = jnp.zeros((SORT_CHUNK, tt), F32)
        for k in range(TOP_K):
            w = jnp.where(row == slots[k], probs[k], w)
        onehot = jnp.where(w != 0.0, 1.0, 0.0).astype(BF16)
        local_ref[r0:r0 + SORT_CHUNK, 0:d] = _dot(onehot, u2).astype(BF16)
        wr = jnp.sum(w, axis=1, keepdims=True)
        hi = wr.astype(BF16).astype(F32)
        mid = (wr - hi).astype(BF16).astype(F32)
        lo = wr - hi - mid
        parts = jnp.where(lane == 0, hi, jnp.where(lane == 1, mid, jnp.where(lane == 2, lo, 0.0)))
        local_ref[r0:r0 + SORT_CHUNK, d:d + LANES] = parts.astype(BF16)

    def start_piece(loc, dst, rows):
        pltpu.make_async_copy(local_ref.at[pl.ds(loc, rows)], xs_hbm.at[pl.ds(dst, rows)], sem).start()

    rows_ref[buf] = _for_each_piece(tile, (big_loc, big_dst, small_loc, small_dst, piece_count), start_piece)

    @pl.when(tile > 0)
    def _():
        _wait_rows(local2_ref.at[1 - buf], xs_hbm, rows_ref[1 - buf], sem2.at[1 - buf])

    def zero_fill(first_row, n_rows, act):
        def copy(row, rows):
            return pltpu.make_async_copy(zero_ref.at[pl.ds(0, rows)],
                                         xs_hbm.at[pl.ds(pl.multiple_of(row, ROW_ALIGN), rows)], zsem)
        n_big = n_rows // ZERO_ROWS
        rest = first_row + n_big * ZERO_ROWS
        lax.fori_loop(0, n_big, lambda p, c: (act(copy(first_row + p * ZERO_ROWS, ZERO_ROWS)), c)[1], 0)
        lax.fori_loop(0, (n_rows - n_big * ZERO_ROWS) // ROW_ALIGN,
                      lambda p, c: (act(copy(rest + p * ROW_ALIGN, ROW_ALIGN)), c)[1], 0)

    def zero_fill_all(act):
        first_spare = nused_ref[0] * MOE_BLOCK
        lax.fori_loop(0, N_EXPERTS, lambda e, c: (zero_fill(tail_ref[e], ntail_ref[e], act), c)[1], 0)
        zero_fill(first_spare, xs_hbm.shape[0] - first_spare, act)

    @pl.when(tile == 0)
    def _():
        zero_ref[...] = jnp.zeros_like(zero_ref)
        zero_fill_all(lambda c: c.start())

    @pl.when(tile == pl.num_programs(0) - 1)
    def _():
        _wait_rows(local_ref, xs_hbm, rows_ref[buf], sem)
        zero_fill_all(lambda c: c.wait())


def _dispatch(tables, slot, prob, u2, n_rows):
    N, D = u2.shape
    tt = POST_TILE
    route_spec = pl.BlockSpec((1, TOP_K, tt), lambda i, *t: (i, 0, 0))
    grid_spec = pltpu.PrefetchScalarGridSpec(
        num_scalar_prefetch=len(tables),
        grid=(N // tt,),
        in_specs=[route_spec, route_spec, pl.BlockSpec((tt, D), lambda i, *t: (i, 0))],
        out_specs=pl.BlockSpec(memory_space=pl.ANY),
        scratch_shapes=[pltpu.VMEM((2, LOCAL_ROWS, D + LANES), BF16), pltpu.VMEM((ZERO_ROWS, D + LANES), BF16),
                        pltpu.SemaphoreType.DMA((2,)), pltpu.SemaphoreType.DMA(()),
                        pltpu.SMEM((2,), jnp.int32)],
    )
    return pl.pallas_call(
        _dispatch_kernel,
        grid_spec=grid_spec,
        out_shape=jax.ShapeDtypeStruct((n_rows, D + LANES), BF16),
        compiler_params=pltpu.CompilerParams(
            dimension_semantics=("arbitrary",), vmem_limit_bytes=VMEM_LIMIT_BYTES),
        name="dispatch",
    )(*tables, slot, prob, u2)


def _expert_kernel(be_ref, next_ref, live_ref, nused_ref, xs_ref, wg_hbm, bg_ref, wu_hbm, bu_ref, wd_hbm, bd_ref,
                   ys_ref, stage, wg_s, wu_s, wd_s, sem):
    i = pl.program_id(0)
    used = i < nused_ref[0]
    prev = be_ref[jnp.maximum(i - 1, 0)]
    fresh = (i == 0) | (be_ref[i] != prev)

    def fetch(e):
        return [pltpu.make_async_copy(w_hbm.at[e], stage.at[j], sem.at[j])
                for j, w_hbm in enumerate((wg_hbm, wu_hbm, wd_hbm))]

    @pl.when(i == 0)
    def _():
        for copy in fetch(be_ref[0]):
            copy.start()

    @pl.when(used & fresh)
    def _():
        for copy in fetch(be_ref[i]):
            copy.wait()
        wg_s[...] = stage[0].astype(BF16)
        wu_s[...] = stage[1].astype(BF16)
        wd_s[...] = stage[2].astype(BF16)

        @pl.when(next_ref[i] >= 0)
        def _():
            for copy in fetch(next_ref[i]):
                copy.start()

    d = ys_ref.shape[1]

    def expert(rows):
        xb = xs_ref[0:rows, 0:d]
        parts = xs_ref[0:rows, d:d + LANES].astype(F32)
        weight = parts[:, 0:1] + parts[:, 1:2] + parts[:, 2:3]
        g = _dot(xb, wg_s[...]) + bg_ref[0]
        up = _dot(xb, wu_s[...]) + bu_ref[0]
        g = jnp.minimum(g, SWIGLU_LIMIT)
        up = jnp.clip(up, -SWIGLU_LIMIT, SWIGLU_LIMIT)
        h = g * jax.nn.sigmoid(SWIGLU_ALPHA * g) * (up + 1.0)
        return ((_dot(h.astype(BF16), wd_s[...]) + bd_ref[0]) * weight).astype(BF16)

    for parts_live in range(1, MOE_BLOCK // EXPERT_PART + 1):
        rows = parts_live * EXPERT_PART

        @pl.when(used & (live_ref[i] == parts_live))
        def _():
            if rows == MOE_BLOCK:
                ys_ref[...] = expert(rows)
            else:
                ys_ref[0:rows, :] = expert(rows)
                ys_ref[rows:, :] = jnp.zeros((MOE_BLOCK - rows, d), BF16)

    @pl.when(jnp.logical_not(used))
    def _():
        ys_ref[...] = jnp.zeros_like(ys_ref)


def _experts(be, next_expert, live_parts, nused, xs, w_gate, b_gate, w_up, b_up, w_down, b_down):
    P, width = xs.shape
    D = width - LANES
    E = w_gate.shape[0]

    def live(i, nu):
        return jnp.maximum(jnp.minimum(i, nu[0] - 1), 0)

    w_spec = pl.BlockSpec(memory_space=pl.ANY)
    b_spec = pl.BlockSpec((1, 1, D), lambda i, be, nx, hf, nu: (be[live(i, nu)], 0, 0))
    grid_spec = pltpu.PrefetchScalarGridSpec(
        num_scalar_prefetch=4,
        grid=(P // MOE_BLOCK,),
        in_specs=[pl.BlockSpec((MOE_BLOCK, width), lambda i, be, nx, hf, nu: (live(i, nu), 0)),
                  w_spec, b_spec, w_spec, b_spec, w_spec, b_spec],
        out_specs=pl.BlockSpec((MOE_BLOCK, D), lambda i, be, nx, hf, nu: (i, 0)),
        scratch_shapes=[pltpu.VMEM((3, D, D), F32)] + [pltpu.VMEM((D, D), BF16)] * 3
                       + [pltpu.SemaphoreType.DMA((3,))],
    )
    return pl.pallas_call(
        _expert_kernel,
        grid_spec=grid_spec,
        out_shape=jax.ShapeDtypeStruct((P, D), BF16),
        compiler_params=pltpu.CompilerParams(
            dimension_semantics=("arbitrary",), vmem_limit_bytes=VMEM_LIMIT_BYTES),
        name="experts",
    )(be, next_expert, live_parts, nused, xs, w_gate, b_gate.reshape(E, 1, D), w_up, b_up.reshape(E, 1, D),
      w_down, b_down.reshape(E, 1, D))


def _combine_kernel(big_loc, big_dst, small_loc, small_dst, piece_count,
                    slot_ref, x1_ref, ada_ref, g2_ref, b2_ref, ys_hbm,
                    out_ref, local2_ref, sem2, rows_ref):
    tile = pl.program_id(0)
    buf = tile % 2
    tt = x1_ref.shape[0]

    def fetch(t, b):
        def start_piece(loc, dst, rows):
            pltpu.make_async_copy(ys_hbm.at[pl.ds(dst, rows)], local2_ref.at[b, pl.ds(loc, rows)],
                                  sem2.at[b]).start()
        rows_ref[b] = _for_each_piece(t, (big_loc, big_dst, small_loc, small_dst, piece_count), start_piece)

    @pl.when(tile == 0)
    def _():
        local2_ref[...] = jnp.zeros_like(local2_ref)
        fetch(0, 0)

    @pl.when(tile + 1 < pl.num_programs(0))
    def _():
        fetch(tile + 1, 1 - buf)

    local_ref = local2_ref.at[buf]
    _wait_rows(ys_hbm, local_ref, rows_ref[buf], sem2.at[buf])
    slots = [slot_ref[:, k:k + 1] for k in range(TOP_K)]

    ffn = jnp.zeros((tt, x1_ref.shape[1]), F32)
    for r0 in range(0, LOCAL_ROWS, SORT_CHUNK):
        col = r0 + lax.broadcasted_iota(jnp.int32, (tt, SORT_CHUNK), 1)
        onehot = jnp.zeros((tt, SORT_CHUNK), F32)
        for k in range(TOP_K):
            onehot = jnp.where(col == slots[k], 1.0, onehot)
        ffn = ffn + _dot(onehot.astype(BF16), local_ref[r0:r0 + SORT_CHUNK, :])
    gate2 = ada_ref[0, 5:6, :]
    y = DN_ALPHA * x1_ref[...] + (1.0 + gate2) * ffn
    out_ref[...] = _layer_norm(y) * g2_ref[...] + b2_ref[...]


def _combine(tables, slot_tok, x1, ada, ln2_g, ln2_b, ys, tiles_per_batch):
    N, D = x1.shape
    tt = POST_TILE
    const = lambda i, *t: (0, 0)
    tok4 = pl.BlockSpec((tt, TOP_K), lambda i, *t: (i, 0))
    grid_spec = pltpu.PrefetchScalarGridSpec(
        num_scalar_prefetch=len(tables),
        grid=(N // tt,),
        in_specs=[tok4,
                  pl.BlockSpec((tt, D), lambda i, *t: (i, 0)),
                  pl.BlockSpec((1, 6, D), lambda i, *t: (i // tiles_per_batch, 0, 0)),
                  pl.BlockSpec((1, D), const),
                  pl.BlockSpec((1, D), const),
                  pl.BlockSpec(memory_space=pl.ANY)],
        out_specs=pl.BlockSpec((tt, D), lambda i, *t: (i, 0)),
        scratch_shapes=[pltpu.VMEM((2, LOCAL_ROWS, D), BF16), pltpu.SemaphoreType.DMA((2,)),
                        pltpu.SMEM((2,), jnp.int32)],
    )
    return pl.pallas_call(
        _combine_kernel,
        grid_spec=grid_spec,
        out_shape=jax.ShapeDtypeStruct((N, D), F32),
        compiler_params=pltpu.CompilerParams(
            dimension_semantics=("arbitrary",), vmem_limit_bytes=VMEM_LIMIT_BYTES),
        name="combine",
    )(*tables, slot_tok, x1, ada, ln2_g.reshape(1, D), ln2_b.reshape(1, D), ys)


def _routing_tables(cnt, nblk):
    nt, E = cnt.shape
    i32 = jnp.int32
    run = (cnt + ROW_ALIGN - 1) // ROW_ALIGN * ROW_ALIGN
    run_start = jnp.cumsum(run, axis=1) - run
    seg_len = jnp.sum(run, axis=0)
    seg_blocks = (seg_len + MOE_BLOCK - 1) // MOE_BLOCK
    b_end = jnp.cumsum(seg_blocks)
    seg_off = (b_end - seg_blocks) * MOE_BLOCK
    dest = seg_off[None, :] + jnp.cumsum(run, axis=0) - run
    nused = b_end[-1:].astype(i32)
    i = jnp.arange(nblk, dtype=i32)
    be = jnp.minimum(jnp.sum((i[:, None] >= b_end[None, :]).astype(i32), axis=1), E - 1)
    e_ids = jnp.arange(E, dtype=i32)
    later = (e_ids[None, :] > e_ids[:, None]) & (seg_blocks > 0)[None, :]
    next_e = jnp.min(jnp.where(later, e_ids[None, :], E), axis=1)
    next_e = jnp.where(next_e < E, next_e, -1)
    mine = be[:, None] == e_ids[None, :]
    pick = lambda v: jnp.sum(jnp.where(mine, v[None, :], 0), axis=1)
    nxt = pick(next_e)
    rows_left = pick(seg_len) - (i - pick(b_end - seg_blocks)) * MOE_BLOCK
    live_parts = jnp.clip((rows_left + EXPERT_PART - 1) // EXPERT_PART, 1, MOE_BLOCK // EXPERT_PART).astype(i32)
    tail = seg_off + seg_len
    ntail = seg_blocks * MOE_BLOCK - seg_len
    def flat_pieces(n, first_loc, first_dst, rows, cap):
        end = jnp.cumsum(n, axis=1)
        start = end - n
        p = jnp.arange(cap, dtype=i32)[None, :, None]
        off = p - start[:, None, :]
        owner = (off >= 0) & (p < end[:, None, :])
        loc = jnp.sum(jnp.where(owner, first_loc[:, None, :] + rows * off, 0), axis=2)
        dst = jnp.sum(jnp.where(owner, first_dst[:, None, :] + rows * off, 0), axis=2)
        return loc.reshape(-1).astype(i32), dst.reshape(-1).astype(i32), end[:, -1]

    n_big = run // PIECE_ROWS
    big_rows = n_big * PIECE_ROWS
    big_loc, big_dst, big_count = flat_pieces(n_big, run_start, dest, PIECE_ROWS, MAX_BIG_PIECES)
    small_loc, small_dst, small_count = flat_pieces((run - big_rows) // ROW_ALIGN, run_start + big_rows,
                                                    dest + big_rows, ROW_ALIGN, MAX_SMALL_PIECES)
    piece_count = jnp.stack([big_count, small_count], axis=1).reshape(-1).astype(i32)
    piece_tables = (big_loc, big_dst, small_loc, small_dst, piece_count)
    return piece_tables, (tail.astype(i32), ntail.astype(i32)), (be.astype(i32), nxt.astype(i32), live_parts, nused)


def kernel(x, c, positions, w_ada, b_ada, w_in, pool_w, pool_scale, w_pool_out, w_attn_out, w_o,
           ln1_g, ln1_b, w_router, b_router, w_gate, b_gate, w_up, b_up, w_down, b_down, ln2_g, ln2_b):
    B, S, D = x.shape
    N = B * S
    assert D == D_MODEL and S % PROJ_TILE == 0 and S % (16 * ATTN_BLOCK) == 0
    assert S % POST_TILE == 0 and LOCAL_ROWS % SORT_CHUNK == 0
    nt = N // POST_TILE
    nblk = (N * TOP_K + nt * N_EXPERTS * (ROW_ALIGN - 1)) // MOE_BLOCK + N_EXPERTS
    for l in range(DEPTH):
        ada = _ada(c, w_ada[l], b_ada[l])
        proj_out = _proj(x, positions, ada, w_in[l], pool_w[l], pool_scale[l], w_pool_out[l])
        qkv, (pg, sga) = proj_out[:9], proj_out[9:]
        attn_outs = [_attention(*qkv[3 * g:3 * g + 3]) for g in range(len(ATTN_GROUPS))]
        x1, u2, slot, prob, cnt = _post(attn_outs, pg, sga, x, ada, w_attn_out[l], w_o[l],
                                        ln1_g[l], ln1_b[l], w_router[l], b_router[l])
        piece_tables, tail_tables, block_tables = _routing_tables(cnt.reshape(nt, N_EXPERTS), nblk)
        xs = _dispatch(piece_tables + tail_tables + block_tables[3:], slot, prob, u2.reshape(N, D),
                       nblk * MOE_BLOCK)
        ys = _experts(*block_tables, xs, w_gate[l], b_gate[l], w_up[l], b_up[l], w_down[l], b_down[l])
        slot_tok = slot.transpose(0, 2, 1).reshape(N, TOP_K)
        out = _combine(piece_tables, slot_tok, x1.reshape(N, D), ada,
                       ln2_g[l], ln2_b[l], ys, S // POST_TILE)
        x = out.reshape(B, S, D)
    return x
```

```python
import jax
import jax.numpy as jnp
import numpy as np
from jax import lax
from jax.experimental import pallas as pl
from jax.experimental.pallas import tpu as pltpu

F32 = jnp.float32
BF16 = jnp.bfloat16

D_MODEL = 1024
POOL_WINDOWS = (2, 4, 8, 16)
POOL_WIDTH = D_MODEL // 2
POOL_GROUP = POOL_WIDTH // len(POOL_WINDOWS)
POOL_HALO = 16
HEAD_DIM = 64
ATTN_GROUPS = ((128, 1), (512, 4), (2048, 16))
HEADS_PER_GROUP = 4
GROUP_WIDTH = HEADS_PER_GROUP * HEAD_DIM
N_HEADS = HEADS_PER_GROUP * len(ATTN_GROUPS)
ATTN_WIDTH = N_HEADS * HEAD_DIM
ATTN_BLOCK = 128
ROT_DIM = HEAD_DIM // 4
ROPE_THETA = 500000.0
N_EXPERTS = 32
TOP_K = 4
SWIGLU_ALPHA = 1.702
SWIGLU_LIMIT = 7.0
MOE_BLOCK = 1024
EXPERT_PART = 128
DEPTH = 1
DN_ALPHA = (2.0 * DEPTH) ** 0.25
LN_EPS = 1e-5
NEG_INF = -1e30

OFF_Q = POOL_WIDTH
OFF_K = OFF_Q + ATTN_WIDTH
OFF_V = OFF_K + ATTN_WIDTH
OFF_GP = OFF_V + ATTN_WIDTH
OFF_GA = OFF_GP + D_MODEL
IN_WIDTH = OFF_GA + D_MODEL

VMEM_LIMIT_BYTES = 56 * 1024 * 1024
LANES = 128
SUBLANES = 8

PROJ_TILE = 512
POST_TILE = 512
ATTN_QROWS = 1024
ROW_ALIGN = 16
PIECE_ROWS = 64
ZERO_ROWS = 128
SORT_CHUNK = 512
LOCAL_ROWS = -(-(POST_TILE * TOP_K + N_EXPERTS * (ROW_ALIGN - 1)) // SORT_CHUNK) * SORT_CHUNK
MAX_BIG_PIECES = LOCAL_ROWS // PIECE_ROWS
MAX_SMALL_PIECES = N_EXPERTS * (PIECE_ROWS // ROW_ALIGN - 1)


def _layer_norm(x):
    mu = jnp.mean(x, axis=-1, keepdims=True)
    xc = x - mu
    var = jnp.mean(xc * xc, axis=-1, keepdims=True)
    return xc * lax.rsqrt(var + LN_EPS)


def _dot(a, b):
    return jnp.dot(a, b, preferred_element_type=F32)


def _ada_kernel(c_ref, w_ref, b_ref, o_ref):
    c = c_ref[...]
    s = c * jax.nn.sigmoid(c)
    o_ref[...] = jnp.dot(s, w_ref[...], preferred_element_type=F32,
                         precision=lax.Precision.HIGHEST) + b_ref[...]


def _ada(c, w_ada, b_ada):
    B, D = c.shape
    rows = SUBLANES
    c_pad = jnp.pad(c, ((0, rows - B), (0, 0)))
    n_out = w_ada.shape[1]
    out = pl.pallas_call(
        _ada_kernel,
        grid=(n_out // D,),
        in_specs=[pl.BlockSpec((rows, D), lambda j: (0, 0)),
                  pl.BlockSpec((D, D), lambda j: (0, j)),
                  pl.BlockSpec((1, D), lambda j: (0, j))],
        out_specs=pl.BlockSpec((rows, D), lambda j: (0, j)),
        out_shape=jax.ShapeDtypeStruct((rows, n_out), F32),
        name="ada",
    )(c_pad, w_ada, b_ada.reshape(1, n_out))
    return out[:B].reshape(B, 6, D)


ROPE_PART_ROWS = 32


def _rope_tables():
    lane = np.arange(LANES)
    li = lane % HEAD_DIM
    half = ROT_DIM // 2
    inv_freq = jnp.power(ROPE_THETA, -jnp.arange(half, dtype=F32) * (2.0 / ROT_DIM))
    invf = jnp.broadcast_to(inv_freq[:, None], (half, LANES))
    freq = np.arange(ROPE_PART_ROWS)[:, None]
    live = (freq < 3 * half) & (freq % half == (li % half)[None, :])
    place = np.stack([live & (li < ROT_DIM)[None, :],
                      -1.0 * (live & (li < half)[None, :]),
                      live & ((li >= half) & (li < ROT_DIM))[None, :]]).astype(np.float32)
    keep = (li >= ROT_DIM).astype(np.float32)[None, :]
    return invf, jnp.asarray(place, BF16), jnp.asarray(keep)


def _proj_kernel(x_ref, xh_ref, pos_ref, ada_ref, invf_ref, place_ref, keep_ref,
                 win_ref, poolw_ref, pscale_ref, wpo_ref,
                 q1_ref, k1_ref, v1_ref, q4_ref, k4_ref, v4_ref, q16_ref, k16_ref, v16_ref,
                 pg_ref, sga_ref, xpe_ref, cls_ref):
    tm = x_ref.shape[1]
    i = pl.program_id(1)
    shift1 = ada_ref[0, 0:1, :]
    scale1 = ada_ref[0, 1:2, :]

    def modulated(xv):
        return (_layer_norm(xv) * (1.0 + scale1) + shift1).astype(BF16)

    u = modulated(x_ref[0])
    uh = modulated(xh_ref[0])

    xp = _dot(u, win_ref[:, 0:POOL_WIDTH])
    xph = _dot(uh, win_ref[:, 0:POOL_WIDTH])
    xph = jnp.where(i > 0, xph, 0.0)
    xpe_ref[0:POOL_HALO, :] = xph
    xpe_ref[POOL_HALO:, :] = xp
    tok = i * tm + lax.broadcasted_iota(jnp.int32, (tm, 1), 0)

    def pooled_branch():
        ys = []
        for g, w in enumerate(POOL_WINDOWS):
            cols = slice(g * POOL_GROUP, (g + 1) * POOL_GROUP)
            xg = xpe_ref[POOL_HALO:, cols]
            acc = xg
            for j in range(1, w):
                acc = acc + xpe_ref[POOL_HALO - j:POOL_HALO - j + tm, cols]
            cnt = jnp.minimum(tok + 1, w).astype(F32)
            mixed = (acc / cnt - xg).astype(BF16)
            ys.append(_dot(mixed, poolw_ref[g]) * pscale_ref[:, cols])
        y = jnp.concatenate(ys, axis=1).astype(BF16)
        return _dot(y, wpo_ref[...])

    ang = invf_ref[:, 0:1] * pos_ref[0, 0].astype(F32)

    def on_lanes(table, j):
        hi = table.astype(BF16).astype(F32)
        mid = (table - hi).astype(BF16).astype(F32)
        lo = table - hi - mid
        pad = jnp.zeros((ROPE_PART_ROWS - 3 * table.shape[0], tm), F32)
        parts = jnp.concatenate([hi, mid, lo, pad], axis=0).astype(BF16)
        return lax.dot_general(parts, place_ref[j], (((0,), (0,)), ((), ())), preferred_element_type=F32)

    cos = jnp.cos(ang)
    sin = jnp.sin(ang)
    c_mul = on_lanes(cos, 0) + keep_ref[...]
    s_lo = on_lanes(sin, 1)
    s_hi = on_lanes(sin, 2)
    c_mul = jnp.concatenate([c_mul, c_mul], axis=1)
    s_lo = jnp.concatenate([s_lo, s_lo], axis=1)
    s_hi = jnp.concatenate([s_hi, s_hi], axis=1)
    half = ROT_DIM // 2

    def rotate(a):
        up = pltpu.roll(a, GROUP_WIDTH - half, axis=1)
        dn = pltpu.roll(a, half, axis=1)
        return a * c_mul + up * s_lo + dn * s_hi

    def emit(a, out_ref, dil):
        if dil == 1:
            out_ref[0, 0] = a.astype(BF16)
            return
        for c in range(GROUP_WIDTH // LANES):
            cls_ref[c] = a[:, c * LANES:(c + 1) * LANES]
        for r in range(dil):
            for c in range(GROUP_WIDTH // LANES):
                out_ref[0, r, :, c * LANES:(c + 1) * LANES] = (
                    cls_ref[c, pl.ds(r, tm // dil, stride=dil), :].astype(BF16))

    outs = ((q1_ref, k1_ref, v1_ref), (q4_ref, k4_ref, v4_ref), (q16_ref, k16_ref, v16_ref))

    def attention_group(gi):
        qo, ko, vo = outs[gi]
        dil = ATTN_GROUPS[gi][1]
        c0 = gi * GROUP_WIDTH
        emit(rotate(_dot(u, win_ref[:, OFF_Q + c0:OFF_Q + c0 + GROUP_WIDTH])), qo, dil)
        emit(rotate(_dot(u, win_ref[:, OFF_K + c0:OFF_K + c0 + GROUP_WIDTH])), ko, dil)
        emit(_dot(u, win_ref[:, OFF_V + c0:OFF_V + c0 + GROUP_WIDTH]), vo, dil)

    attention_group(2)
    pooled = pooled_branch()
    attention_group(1)
    g_p = _dot(u, win_ref[:, OFF_GP:OFF_GP + D_MODEL])
    pg_ref[0] = jax.nn.sigmoid(g_p) * pooled
    attention_group(0)
    g_a = _dot(u, win_ref[:, OFF_GA:OFF_GA + D_MODEL])
    sga_ref[0] = jax.nn.sigmoid(g_a)


def _proj(x, positions, ada, w_in, pool_w, pool_scale, w_pool_out):
    B, S, D = x.shape
    tm = PROJ_TILE
    nt = S // tm
    halo_blocks = tm // POOL_HALO
    const2 = lambda b, i: (0, 0)
    in_specs = [
        pl.BlockSpec((1, tm, D), lambda b, i: (b, i, 0)),
        pl.BlockSpec((1, POOL_HALO, D), lambda b, i: (b, jnp.maximum(i * halo_blocks - 1, 0), 0)),
        pl.BlockSpec((1, 1, 1, tm), lambda b, i: (b, i, 0, 0)),
        pl.BlockSpec((1, 6, D), lambda b, i: (b, 0, 0)),
        pl.BlockSpec((ROT_DIM // 2, LANES), const2),
        pl.BlockSpec((3, ROPE_PART_ROWS, LANES), lambda b, i: (0, 0, 0)),
        pl.BlockSpec((1, LANES), const2),
        pl.BlockSpec((D, IN_WIDTH), const2),
        pl.BlockSpec((len(POOL_WINDOWS), POOL_GROUP, POOL_GROUP), lambda b, i: (0, 0, 0)),
        pl.BlockSpec((1, POOL_WIDTH), const2),
        pl.BlockSpec((POOL_WIDTH, D), const2),
    ]
    out_specs, out_shapes = [], []
    for _, dil in ATTN_GROUPS:
        for _ in range(3):
            out_specs.append(pl.BlockSpec((1, dil, tm // dil, GROUP_WIDTH), lambda b, i: (b, 0, i, 0)))
            out_shapes.append(jax.ShapeDtypeStruct((B, dil, S // dil, GROUP_WIDTH), BF16))
    for _ in range(2):
        out_specs.append(pl.BlockSpec((1, tm, D), lambda b, i: (b, i, 0)))
        out_shapes.append(jax.ShapeDtypeStruct((B, S, D), F32))
    return pl.pallas_call(
        _proj_kernel,
        grid=(B, nt),
        in_specs=in_specs,
        out_specs=out_specs,
        out_shape=out_shapes,
        scratch_shapes=[pltpu.VMEM((tm + POOL_HALO, POOL_WIDTH), F32),
                        pltpu.VMEM((GROUP_WIDTH // LANES, tm, LANES), F32)],
        compiler_params=pltpu.CompilerParams(
            dimension_semantics=("parallel", "parallel"), vmem_limit_bytes=VMEM_LIMIT_BYTES),
        name="proj",
    )(x, x, positions.reshape(B, nt, 1, tm), ada, *_rope_tables(), w_in.astype(BF16),
      pool_w.astype(BF16), pool_scale.reshape(1, POOL_WIDTH), w_pool_out.astype(BF16))


def _attn_kernel(q_ref, k_ref, v_ref, kh_ref, vh_ref, o_ref, lse_ref, kf_ref, vf_ref):
    n_cls, qb = q_ref.shape[1], q_ref.shape[2]
    per = qb // ATTN_BLOCK
    n = pl.program_id(2)
    for c in range(n_cls):
        kf_ref[c, 0:ATTN_BLOCK, :] = kh_ref[0, c]
        kf_ref[c, ATTN_BLOCK:, :] = k_ref[0, c]
        vf_ref[c, 0:ATTN_BLOCK, :] = vh_ref[0, c]
        vf_ref[c, ATTN_BLOCK:, :] = v_ref[0, c]
    qi = lax.broadcasted_iota(jnp.int32, (ATTN_BLOCK, 2 * ATTN_BLOCK), 0)
    kj = lax.broadcasted_iota(jnp.int32, (ATTN_BLOCK, 2 * ATTN_BLOCK), 1)
    band = (kj >= qi) & (kj <= qi + ATTN_BLOCK)
    band_bias = jnp.where(band, 0.0, NEG_INF)
    lane = lax.broadcasted_iota(jnp.int32, (ATTN_BLOCK, GROUP_WIDTH), 1)
    low_lanes = lax.broadcasted_iota(jnp.int32, (ATTN_BLOCK, LANES), 1) < HEAD_DIM
    ones = jnp.ones((2 * ATTN_BLOCK, LANES), BF16)
    nh, blk = HEADS_PER_GROUP, ATTN_BLOCK

    def block(t, carry):
        c, j = t // per, t % per
        r0 = pl.multiple_of(j * ATTN_BLOCK, ATTN_BLOCK)
        first_key = jnp.where((n > 0) | (j > 0), 0, ATTN_BLOCK)
        bias = band_bias + jnp.where(kj < first_key, NEG_INF, 0.0)
        q = q_ref[0, c, pl.ds(r0, blk), :].astype(F32)
        kk = kf_ref[c, pl.ds(r0, 2 * blk), :]
        vv = vf_ref[c, pl.ds(r0, 2 * blk), :]
        qs = jnp.concatenate([jnp.where((lane >= h * HEAD_DIM) & (lane < (h + 1) * HEAD_DIM), q, 0.0)
                              for h in range(nh)], axis=0).astype(BF16)
        s = lax.dot_general(qs, kk, (((1,), (1,)), ((), ())), preferred_element_type=F32)
        s = jnp.concatenate([s[h * blk:(h + 1) * blk] * (HEAD_DIM ** -0.5) + bias for h in range(nh)], axis=0)
        m = jnp.max(s, axis=-1, keepdims=True)
        p = jnp.exp(s - m).astype(BF16)
        den = _dot(p, ones)
        lse = m + jnp.log(den)
        for hp in range(GROUP_WIDTH // LANES):
            rows = slice(2 * hp * blk, (2 * hp + 2) * blk)
            ls = slice(hp * LANES, (hp + 1) * LANES)
            o2 = _dot(p[rows], vv[:, ls]) / den[rows]
            l2 = lse[rows]
            o_ref[0, c, pl.ds(r0, blk), ls] = jnp.where(low_lanes, o2[0:blk], o2[blk:2 * blk])
            lse_ref[0, c, pl.ds(r0, blk), ls] = jnp.where(low_lanes, l2[0:blk], l2[blk:2 * blk])
        return carry

    lax.fori_loop(0, n_cls * per, block, 0, unroll=8)


def _attention(q, k, v):
    B, dil, L, W = q.shape
    qb = min(L, ATTN_QROWS)
    per = qb // ATTN_BLOCK
    n_cls = min(ATTN_QROWS // qb, dil)
    main = pl.BlockSpec((1, n_cls, qb, W), lambda b, r, n: (b, r, n, 0))
    halo = pl.BlockSpec((1, n_cls, ATTN_BLOCK, W), lambda b, r, n: (b, r, jnp.maximum(n * per - 1, 0), 0))
    return pl.pallas_call(
        _attn_kernel,
        grid=(B, dil // n_cls, L // qb),
        in_specs=[main, main, main, halo, halo],
        out_specs=[main, main],
        out_shape=[jax.ShapeDtypeStruct((B, dil, L, W), F32)] * 2,
        scratch_shapes=[pltpu.VMEM((n_cls, qb + ATTN_BLOCK, W), BF16)] * 2,
        compiler_params=pltpu.CompilerParams(
            dimension_semantics=("parallel", "parallel", "parallel"), vmem_limit_bytes=VMEM_LIMIT_BYTES),
        name=f"attn_d{dil}",
    )(q, k, v, k, v)


def _post_kernel(o1_ref, l1_ref, o4_ref, l4_ref, o16_ref, l16_ref, pg_ref, sga_ref, x_ref, ada_ref,
                 wao_ref, wo_ref, g1_ref, b1_ref, wrt_ref, brt_ref,
                 x1_ref, u2_ref, slot_ref, prob_ref, cnt_ref,
                 s0, s1, s2, s3):
    tm = x_ref.shape[1]

    def token_major(src_ref, scr_ref, dil):
        if dil == 1:
            return src_ref[0, 0]
        for r in range(dil):
            for c in range(GROUP_WIDTH // LANES):
                scr_ref[c, pl.ds(r, tm // dil, stride=dil), :] = src_ref[0, r, :, c * LANES:(c + 1) * LANES]
        return jnp.concatenate([scr_ref[c] for c in range(GROUP_WIDTH // LANES)], axis=1)

    o1, l1 = o1_ref[0, 0], l1_ref[0, 0]
    o4, l4 = token_major(o4_ref, s0, 4), token_major(l4_ref, s1, 4)
    o16, l16 = token_major(o16_ref, s2, 16), token_major(l16_ref, s3, 16)
    mx = jnp.maximum(jnp.maximum(l1, l4), l16)
    e1, e4, e16 = jnp.exp(l1 - mx), jnp.exp(l4 - mx), jnp.exp(l16 - mx)
    attn = (e1 * o1 + e4 * o4 + e16 * o16) / (e1 + e4 + e16)

    merged = pg_ref[0] + sga_ref[0] * _dot(attn.astype(BF16), wao_ref[...])
    mix = _dot(merged.astype(BF16), wo_ref[...])
    gate1 = ada_ref[0, 2:3, :]
    shift2 = ada_ref[0, 3:4, :]
    scale2 = ada_ref[0, 4:5, :]
    x1 = _layer_norm(DN_ALPHA * x_ref[0] + (1.0 + gate1) * mix) * g1_ref[...] + b1_ref[...]
    x1_ref[0] = x1
    u2 = _layer_norm(x1) * (1.0 + scale2) + shift2
    u2_hi = u2.astype(BF16)
    u2_ref[0] = u2_hi

    u2_lo = (u2 - u2_hi.astype(F32)).astype(BF16)
    nt_dot = lambda a, b: lax.dot_general(a, b, (((1,), (1,)), ((), ())), preferred_element_type=F32)
    logits = (nt_dot(wrt_ref[0], u2_hi) + nt_dot(wrt_ref[0], u2_lo) + nt_dot(wrt_ref[1], u2_hi)
              + brt_ref[...])
    eidx = lax.broadcasted_iota(jnp.int32, (N_EXPERTS, tm), 0)
    work = logits
    vals, idxs = [], []
    for _ in range(TOP_K):
        m = jnp.max(work, axis=0, keepdims=True)
        idx = jnp.min(jnp.where(work == m, eidx, N_EXPERTS), axis=0, keepdims=True)
        vals.append(m)
        idxs.append(idx)
        work = jnp.where(eidx == idx, -jnp.inf, work)
    exps = [jnp.exp(vk - vals[0]) for vk in vals]
    tot = exps[0] + exps[1] + exps[2] + exps[3]
    sel = jnp.zeros((N_EXPERTS, tm), F32)
    for idx in idxs:
        sel = sel + (eidx == idx).astype(F32)
    tr = lax.broadcasted_iota(jnp.int32, (tm, tm), 0)
    tc = lax.broadcasted_iota(jnp.int32, (tm, tm), 1)
    rank = _dot(sel.astype(BF16), (tr < tc).astype(BF16))
    cnt = jnp.sum(sel, axis=1, keepdims=True)
    run = jnp.floor((cnt + (ROW_ALIGN - 1)) * (1.0 / ROW_ALIGN)) * ROW_ALIGN
    er = lax.broadcasted_iota(jnp.int32, (N_EXPERTS, N_EXPERTS), 0)
    ec = lax.broadcasted_iota(jnp.int32, (N_EXPERTS, N_EXPERTS), 1)
    run_start = _dot((ec < er).astype(BF16),
                     jnp.broadcast_to(run, (N_EXPERTS, LANES)).astype(BF16))[:, 0:1]
    slot = rank + run_start
    for k in range(TOP_K):
        slot_ref[0, k:k + 1, :] = jnp.sum(jnp.where(eidx == idxs[k], slot, 0.0), axis=0,
                                          keepdims=True).astype(jnp.int32)
        prob_ref[0, k:k + 1, :] = exps[k] / tot
    cnt_ref[0] = cnt.astype(jnp.int32)


def _post(attn_outs, pg, sga, x, ada, w_attn_out, w_o, ln1_g, ln1_b, w_router, b_router):
    B, S, D = x.shape
    tm = POST_TILE
    nt = S // tm
    N = B * S
    const2 = lambda b, i: (0, 0)
    in_specs, args = [], []
    for (o, lse), (_, dil) in zip(attn_outs, ATTN_GROUPS):
        spec = pl.BlockSpec((1, dil, tm // dil, GROUP_WIDTH), lambda b, i: (b, 0, i, 0))
        in_specs += [spec, spec]
        args += [o, lse]
    tok_spec = pl.BlockSpec((1, tm, D), lambda b, i: (b, i, 0))
    in_specs += [tok_spec, tok_spec, tok_spec,
                 pl.BlockSpec((1, 6, D), lambda b, i: (b, 0, 0)),
                 pl.BlockSpec((GROUP_WIDTH, D), const2),
                 pl.BlockSpec((D, D), const2),
                 pl.BlockSpec((1, D), const2),
                 pl.BlockSpec((1, D), const2),
                 pl.BlockSpec((2, N_EXPERTS, D), lambda b, i: (0, 0, 0)),
                 pl.BlockSpec((N_EXPERTS, 1), const2)]
    wr_hi = w_router.T.astype(BF16)
    wr_lo = (w_router.T - wr_hi.astype(F32)).astype(BF16)
    args += [pg, sga, x, ada, w_attn_out.astype(BF16), w_o.astype(BF16),
             ln1_g.reshape(1, D), ln1_b.reshape(1, D), jnp.stack([wr_hi, wr_lo]),
             b_router.reshape(N_EXPERTS, 1)]
    nc = N // tm
    route_spec = pl.BlockSpec((1, TOP_K, tm), lambda b, i: (b * nt + i, 0, 0))
    out_specs = [tok_spec, tok_spec, route_spec, route_spec,
                 pl.BlockSpec((1, N_EXPERTS, 1), lambda b, i: (b * nt + i, 0, 0))]
    out_shapes = [jax.ShapeDtypeStruct((B, S, D), F32), jax.ShapeDtypeStruct((B, S, D), BF16),
                  jax.ShapeDtypeStruct((nc, TOP_K, tm), jnp.int32), jax.ShapeDtypeStruct((nc, TOP_K, tm), F32),
                  jax.ShapeDtypeStruct((nc, N_EXPERTS, 1), jnp.int32)]
    return pl.pallas_call(
        _post_kernel,
        grid=(B, nt),
        in_specs=in_specs,
        out_specs=out_specs,
        out_shape=out_shapes,
        scratch_shapes=[pltpu.VMEM((GROUP_WIDTH // LANES, tm, LANES), F32)] * 4,
        compiler_params=pltpu.CompilerParams(
            dimension_semantics=("parallel", "parallel"), vmem_limit_bytes=VMEM_LIMIT_BYTES),
        name="post",
    )(*args)


def _for_each_piece(tile, piece_refs, fn):
    big_loc, big_dst, small_loc, small_dst, count = piece_refs
    n_big, n_small = count[2 * tile], count[2 * tile + 1]

    def piece(loc_ref, dst_ref, cap, rows):
        def body(p, carry):
            i = tile * cap + p
            fn(pl.multiple_of(loc_ref[i], ROW_ALIGN), pl.multiple_of(dst_ref[i], ROW_ALIGN), rows)
            return carry
        return body

    lax.fori_loop(0, n_big, piece(big_loc, big_dst, MAX_BIG_PIECES, PIECE_ROWS), 0)
    lax.fori_loop(0, n_small, piece(small_loc, small_dst, MAX_SMALL_PIECES, ROW_ALIGN), 0)
    return n_big * PIECE_ROWS + n_small * ROW_ALIGN


def _wait_rows(src_ref, dst_ref, rows, sem):
    @pl.when(rows > 0)
    def _():
        n = pl.multiple_of(rows, ROW_ALIGN)
        pltpu.make_async_copy(src_ref.at[pl.ds(0, n)], dst_ref.at[pl.ds(0, n)], sem).wait()


def _dispatch_kernel(big_loc, big_dst, small_loc, small_dst, piece_count, tail_ref, ntail_ref, nused_ref,
                     slot_ref, prob_ref, u2_ref, xs_hbm, local2_ref, zero_ref, sem2, zsem, rows_ref):
    tile = pl.program_id(0)
    buf = tile % 2
    local_ref = local2_ref.at[buf]
    sem = sem2.at[buf]
    tt, d = u2_ref.shape
    u2 = u2_ref[...]
    slots = [slot_ref[0, k:k + 1, :] for k in range(TOP_K)]
    probs = [prob_ref[0, k:k + 1, :] for k in range(TOP_K)]
    lane = lax.broadcasted_iota(jnp.int32, (SORT_CHUNK, LANES), 1)
    for r0 in range(0, LOCAL_ROWS, SORT_CHUNK):
        row = r0 + lax.broadcasted_iota(jnp.int32, (SORT_CHUNK, tt), 0)
        w = jnp.zeros((SORT_CHUNK, tt), F32)
        for k in range(TOP_K):
            w = jnp.where(row == slots[k], probs[k], w)
        onehot = jnp.where(w != 0.0, 1.0, 0.0).astype(BF16)
        local_ref[r0:r0 + SORT_CHUNK, 0:d] = _dot(onehot, u2).astype(BF16)
        wr = jnp.sum(w, axis=1, keepdims=True)
        hi = wr.astype(BF16).astype(F32)
        mid = (wr - hi).astype(BF16).astype(F32)
        lo = wr - hi - mid
        parts = jnp.where(lane == 0, hi, jnp.where(lane == 1, mid, jnp.where(lane == 2, lo, 0.0)))
        local_ref[r0:r0 + SORT_CHUNK, d:d + LANES] = parts.astype(BF16)

    def start_piece(loc, dst, rows):
        pltpu.make_async_copy(local_ref.at[pl.ds(loc, rows)], xs_hbm.at[pl.ds(dst, rows)], sem).start()

    rows_ref[buf] = _for_each_piece(tile, (big_loc, big_dst, small_loc, small_dst, piece_count), start_piece)

    @pl.when(tile > 0)
    def _():
        _wait_rows(local2_ref.at[1 - buf], xs_hbm, rows_ref[1 - buf], sem2.at[1 - buf])

    def zero_fill(first_row, n_rows, act):
        def copy(row, rows):
            return pltpu.make_async_copy(zero_ref.at[pl.ds(0, rows)],
                                         xs_hbm.at[pl.ds(pl.multiple_of(row, ROW_ALIGN), rows)], zsem)
        n_big = n_rows // ZERO_ROWS
        rest = first_row + n_big * ZERO_ROWS
        lax.fori_loop(0, n_big, lambda p, c: (act(copy(first_row + p * ZERO_ROWS, ZERO_ROWS)), c)[1], 0)
        lax.fori_loop(0, (n_rows - n_big * ZERO_ROWS) // ROW_ALIGN,
                      lambda p, c: (act(copy(rest + p * ROW_ALIGN, ROW_ALIGN)), c)[1], 0)

    def zero_fill_all(act):
        first_spare = nused_ref[0] * MOE_BLOCK
        lax.fori_loop(0, N_EXPERTS, lambda e, c: (zero_fill(tail_ref[e], ntail_ref[e], act), c)[1], 0)
        zero_fill(first_spare, xs_hbm.shape[0] - first_spare, act)

    @pl.when(tile == 0)
    def _():
        zero_ref[...] = jnp.zeros_like(zero_ref)
        zero_fill_all(lambda c: c.start())

    @pl.when(tile == pl.num_programs(0) - 1)
    def _():
        _wait_rows(local_ref, xs_hbm, rows_ref[buf], sem)
        zero_fill_all(lambda c: c.wait())


def _dispatch(tables, slot, prob, u2, n_rows):
    N, D = u2.shape
    tt = POST_TILE
    route_spec = pl.BlockSpec((1, TOP_K, tt), lambda i, *t: (i, 0, 0))
    grid_spec = pltpu.PrefetchScalarGridSpec(
        num_scalar_prefetch=len(tables),
        grid=(N // tt,),
        in_specs=[route_spec, route_spec, pl.BlockSpec((tt, D), lambda i, *t: (i, 0))],
        out_specs=pl.BlockSpec(memory_space=pl.ANY),
        scratch_shapes=[pltpu.VMEM((2, LOCAL_ROWS, D + LANES), BF16), pltpu.VMEM((ZERO_ROWS, D + LANES), BF16),
                        pltpu.SemaphoreType.DMA((2,)), pltpu.SemaphoreType.DMA(()),
                        pltpu.SMEM((2,), jnp.int32)],
    )
    return pl.pallas_call(
        _dispatch_kernel,
        grid_spec=grid_spec,
        out_shape=jax.ShapeDtypeStruct((n_rows, D + LANES), BF16),
        compiler_params=pltpu.CompilerParams(
            dimension_semantics=("arbitrary",), vmem_limit_bytes=VMEM_LIMIT_BYTES),
        name="dispatch",
    )(*tables, slot, prob, u2)


def _expert_kernel(be_ref, next_ref, live_ref, nused_ref, xs_ref, wg_hbm, bg_ref, wu_hbm, bu_ref, wd_hbm, bd_ref,
                   ys_ref, stage, wg_s, wu_s, wd_s, sem):
    i = pl.program_id(0)
    used = i < nused_ref[0]
    prev = be_ref[jnp.maximum(i - 1, 0)]
    fresh = (i == 0) | (be_ref[i] != prev)

    def fetch(e):
        return [pltpu.make_async_copy(w_hbm.at[e], stage.at[j], sem.at[j])
                for j, w_hbm in enumerate((wg_hbm, wu_hbm, wd_hbm))]

    @pl.when(i == 0)
    def _():
        for copy in fetch(be_ref[0]):
            copy.start()

    @pl.when(used & fresh)
    def _():
        for copy in fetch(be_ref[i]):
            copy.wait()
        wg_s[...] = stage[0].astype(BF16)
        wu_s[...] = stage[1].astype(BF16)
        wd_s[...] = stage[2].astype(BF16)

        @pl.when(next_ref[i] >= 0)
        def _():
            for copy in fetch(next_ref[i]):
                copy.start()

    d = ys_ref.shape[1]

    def expert(rows):
        xb = xs_ref[0:rows, 0:d]
        parts = xs_ref[0:rows, d:d + LANES].astype(F32)
        weight = parts[:, 0:1] + parts[:, 1:2] + parts[:, 2:3]
        g = _dot(xb, wg_s[...]) + bg_ref[0]
        up = _dot(xb, wu_s[...]) + bu_ref[0]
        g = jnp.minimum(g, SWIGLU_LIMIT)
        up = jnp.clip(up, -SWIGLU_LIMIT, SWIGLU_LIMIT)
        h = g * jax.nn.sigmoid(SWIGLU_ALPHA * g) * (up + 1.0)
        return ((_dot(h.astype(BF16), wd_s[...]) + bd_ref[0]) * weight).astype(BF16)

    for parts_live in range(1, MOE_BLOCK // EXPERT_PART + 1):
        rows = parts_live * EXPERT_PART

        @pl.when(used & (live_ref[i] == parts_live))
        def _():
            if rows == MOE_BLOCK:
                ys_ref[...] = expert(rows)
            else:
                ys_ref[0:rows, :] = expert(rows)
                ys_ref[rows:, :] = jnp.zeros((MOE_BLOCK - rows, d), BF16)

    @pl.when(jnp.logical_not(used))
    def _():
        ys_ref[...] = jnp.zeros_like(ys_ref)


def _experts(be, next_expert, live_parts, nused, xs, w_gate, b_gate, w_up, b_up, w_down, b_down):
    P, width = xs.shape
    D = width - LANES
    E = w_gate.shape[0]

    def live(i, nu):
        return jnp.maximum(jnp.minimum(i, nu[0] - 1), 0)

    w_spec = pl.BlockSpec(memory_space=pl.ANY)
    b_spec = pl.BlockSpec((1, 1, D), lambda i, be, nx, hf, nu: (be[live(i, nu)], 0, 0))
    grid_spec = pltpu.PrefetchScalarGridSpec(
        num_scalar_prefetch=4,
        grid=(P // MOE_BLOCK,),
        in_specs=[pl.BlockSpec((MOE_BLOCK, width), lambda i, be, nx, hf, nu: (live(i, nu), 0)),
                  w_spec, b_spec, w_spec, b_spec, w_spec, b_spec],
        out_specs=pl.BlockSpec((MOE_BLOCK, D), lambda i, be, nx, hf, nu: (i, 0)),
        scratch_shapes=[pltpu.VMEM((3, D, D), F32)] + [pltpu.VMEM((D, D), BF16)] * 3
                       + [pltpu.SemaphoreType.DMA((3,))],
    )
    return pl.pallas_call(
        _expert_kernel,
        grid_spec=grid_spec,
        out_shape=jax.ShapeDtypeStruct((P, D), BF16),
        compiler_params=pltpu.CompilerParams(
            dimension_semantics=("arbitrary",), vmem_limit_bytes=VMEM_LIMIT_BYTES),
        name="experts",
    )(be, next_expert, live_parts, nused, xs, w_gate, b_gate.reshape(E, 1, D), w_up, b_up.reshape(E, 1, D),
      w_down, b_down.reshape(E, 1, D))


def _combine_kernel(big_loc, big_dst, small_loc, small_dst, piece_count,
                    slot_ref, x1_ref, ada_ref, g2_ref, b2_ref, ys_hbm,
                    out_ref, local2_ref, sem2, rows_ref):
    tile = pl.program_id(0)
    buf = tile % 2
    tt = x1_ref.shape[0]

    def fetch(t, b):
        def start_piece(loc, dst, rows):
            pltpu.make_async_copy(ys_hbm.at[pl.ds(dst, rows)], local2_ref.at[b, pl.ds(loc, rows)],
                                  sem2.at[b]).start()
        rows_ref[b] = _for_each_piece(t, (big_loc, big_dst, small_loc, small_dst, piece_count), start_piece)

    @pl.when(tile == 0)
    def _():
        local2_ref[...] = jnp.zeros_like(local2_ref)
        fetch(0, 0)

    @pl.when(tile + 1 < pl.num_programs(0))
    def _():
        fetch(tile + 1, 1 - buf)

    local_ref = local2_ref.at[buf]
    _wait_rows(ys_hbm, local_ref, rows_ref[buf], sem2.at[buf])
    slots = [slot_ref[:, k:k + 1] for k in range(TOP_K)]

    ffn = jnp.zeros((tt, x1_ref.shape[1]), F32)
    for r0 in range(0, LOCAL_ROWS, SORT_CHUNK):
        col = r0 + lax.broadcasted_iota(jnp.int32, (tt, SORT_CHUNK), 1)
        onehot = jnp.zeros((tt, SORT_CHUNK), F32)
        for k in range(TOP_K):
            onehot = jnp.where(col == slots[k], 1.0, onehot)
        ffn = ffn + _dot(onehot.astype(BF16), local_ref[r0:r0 + SORT_CHUNK, :])
    gate2 = ada_ref[0, 5:6, :]
    y = DN_ALPHA * x1_ref[...] + (1.0 + gate2) * ffn
    out_ref[...] = _layer_norm(y) * g2_ref[...] + b2_ref[...]


def _combine(tables, slot_tok, x1, ada, ln2_g, ln2_b, ys, tiles_per_batch):
    N, D = x1.shape
    tt = POST_TILE
    const = lambda i, *t: (0, 0)
    tok4 = pl.BlockSpec((tt, TOP_K), lambda i, *t: (i, 0))
    grid_spec = pltpu.PrefetchScalarGridSpec(
        num_scalar_prefetch=len(tables),
        grid=(N // tt,),
        in_specs=[tok4,
                  pl.BlockSpec((tt, D), lambda i, *t: (i, 0)),
                  pl.BlockSpec((1, 6, D), lambda i, *t: (i // tiles_per_batch, 0, 0)),
                  pl.BlockSpec((1, D), const),
                  pl.BlockSpec((1, D), const),
                  pl.BlockSpec(memory_space=pl.ANY)],
        out_specs=pl.BlockSpec((tt, D), lambda i, *t: (i, 0)),
        scratch_shapes=[pltpu.VMEM((2, LOCAL_ROWS, D), BF16), pltpu.SemaphoreType.DMA((2,)),
                        pltpu.SMEM((2,), jnp.int32)],
    )
    return pl.pallas_call(
        _combine_kernel,
        grid_spec=grid_spec,
        out_shape=jax.ShapeDtypeStruct((N, D), F32),
        compiler_params=pltpu.CompilerParams(
            dimension_semantics=("arbitrary",), vmem_limit_bytes=VMEM_LIMIT_BYTES),
        name="combine",
    )(*tables, slot_tok, x1, ada, ln2_g.reshape(1, D), ln2_b.reshape(1, D), ys)


def _routing_tables(cnt, nblk):
    nt, E = cnt.shape
    i32 = jnp.int32
    run = (cnt + ROW_ALIGN - 1) // ROW_ALIGN * ROW_ALIGN
    run_start = jnp.cumsum(run, axis=1) - run
    seg_len = jnp.sum(run, axis=0)
    seg_blocks = (seg_len + MOE_BLOCK - 1) // MOE_BLOCK
    b_end = jnp.cumsum(seg_blocks)
    seg_off = (b_end - seg_blocks) * MOE_BLOCK
    dest = seg_off[None, :] + jnp.cumsum(run, axis=0) - run
    nused = b_end[-1:].astype(i32)
    i = jnp.arange(nblk, dtype=i32)
    be = jnp.minimum(jnp.sum((i[:, None] >= b_end[None, :]).astype(i32), axis=1), E - 1)
    e_ids = jnp.arange(E, dtype=i32)
    later = (e_ids[None, :] > e_ids[:, None]) & (seg_blocks > 0)[None, :]
    next_e = jnp.min(jnp.where(later, e_ids[None, :], E), axis=1)
    next_e = jnp.where(next_e < E, next_e, -1)
    mine = be[:, None] == e_ids[None, :]
    pick = lambda v: jnp.sum(jnp.where(mine, v[None, :], 0), axis=1)
    nxt = pick(next_e)
    rows_left = pick(seg_len) - (i - pick(b_end - seg_blocks)) * MOE_BLOCK
    live_parts = jnp.clip((rows_left + EXPERT_PART - 1) // EXPERT_PART, 1, MOE_BLOCK // EXPERT_PART).astype(i32)
    tail = seg_off + seg_len
    ntail = seg_blocks * MOE_BLOCK - seg_len
    def flat_pieces(n, first_loc, first_dst, rows, cap):
        end = jnp.cumsum(n, axis=1)
        start = end - n
        p = jnp.arange(cap, dtype=i32)[None, :, None]
        off = p - start[:, None, :]
        owner = (off >= 0) & (p < end[:, None, :])
        loc = jnp.sum(jnp.where(owner, first_loc[:, None, :] + rows * off, 0), axis=2)
        dst = jnp.sum(jnp.where(owner, first_dst[:, None, :] + rows * off, 0), axis=2)
        return loc.reshape(-1).astype(i32), dst.reshape(-1).astype(i32), end[:, -1]

    n_big = run // PIECE_ROWS
    big_rows = n_big * PIECE_ROWS
    big_loc, big_dst, big_count = flat_pieces(n_big, run_start, dest, PIECE_ROWS, MAX_BIG_PIECES)
    small_loc, small_dst, small_count = flat_pieces((run - big_rows) // ROW_ALIGN, run_start + big_rows,
                                                    dest + big_rows, ROW_ALIGN, MAX_SMALL_PIECES)
    piece_count = jnp.stack([big_count, small_count], axis=1).reshape(-1).astype(i32)
    piece_tables = (big_loc, big_dst, small_loc, small_dst, piece_count)
    return piece_tables, (tail.astype(i32), ntail.astype(i32)), (be.astype(i32), nxt.astype(i32), live_parts, nused)


def kernel(x, c, positions, w_ada, b_ada, w_in, pool_w, pool_scale, w_pool_out, w_attn_out, w_o,
           ln1_g, ln1_b, w_router, b_router, w_gate, b_gate, w_up, b_up, w_down, b_down, ln2_g, ln2_b):
    B, S, D = x.shape
    N = B * S
    assert D == D_MODEL and S % PROJ_TILE == 0 and S % (16 * ATTN_BLOCK) == 0
    assert S % POST_TILE == 0 and LOCAL_ROWS % SORT_CHUNK == 0
    nt = N // POST_TILE
    nblk = (N * TOP_K + nt * N_EXPERTS * (ROW_ALIGN - 1)) // MOE_BLOCK + N_EXPERTS
    for l in range(DEPTH):
        ada = _ada(c, w_ada[l], b_ada[l])
        proj_out = _proj(x, positions, ada, w_in[l], pool_w[l], pool_scale[l], w_pool_out[l])
        qkv, (pg, sga) = proj_out[:9], proj_out[9:]
        attn_outs = [_attention(*qkv[3 * g:3 * g + 3]) for g in range(len(ATTN_GROUPS))]
        x1, u2, slot, prob, cnt = _post(attn_outs, pg, sga, x, ada, w_attn_out[l], w_o[l],
                                        ln1_g[l], ln1_b[l], w_router[l], b_router[l])
        piece_tables, tail_tables, block_tables = _routing_tables(cnt.reshape(nt, N_EXPERTS), nblk)
        xs = _dispatch(piece_tables + tail_tables + block_tables[3:], slot, prob, u2.reshape(N, D),
                       nblk * MOE_BLOCK)
        ys = _experts(*block_tables, xs, w_gate[l], b_gate[l], w_up[l], b_up[l], w_down[l], b_down[l])
        slot_tok = slot.transpose(0, 2, 1).reshape(N, TOP_K)
        out = _combine(piece_tables, slot_tok, x1.reshape(N, D), ada,
                       ln2_g[l], ln2_b[l], ys, S // POST_TILE)
        x = out.reshape(B, S, D)
    return x
```

```python
import jax
import jax.numpy as jnp
import numpy as np
from jax import lax
from jax.experimental import pallas as pl
from jax.experimental.pallas import tpu as pltpu

F32 = jnp.float32
BF16 = jnp.bfloat16

D_MODEL = 1024
POOL_WINDOWS = (2, 4, 8, 16)
POOL_WIDTH = D_MODEL // 2
POOL_GROUP = POOL_WIDTH // len(POOL_WINDOWS)
POOL_HALO = 16
HEAD_DIM = 64
ATTN_GROUPS = ((128, 1), (512, 4), (2048, 16))
HEADS_PER_GROUP = 4
GROUP_WIDTH = HEADS_PER_GROUP * HEAD_DIM
N_HEADS = HEADS_PER_GROUP * len(ATTN_GROUPS)
ATTN_WIDTH = N_HEADS * HEAD_DIM
ATTN_BLOCK = 128
ROT_DIM = HEAD_DIM // 4
ROPE_THETA = 500000.0
N_EXPERTS = 32
TOP_K = 4
SWIGLU_ALPHA = 1.702
SWIGLU_LIMIT = 7.0
MOE_BLOCK = 1024
EXPERT_PART = 128
DEPTH = 1
DN_ALPHA = (2.0 * DEPTH) ** 0.25
LN_EPS = 1e-5
NEG_INF = -1e30

OFF_Q = POOL_WIDTH
OFF_K = OFF_Q + ATTN_WIDTH
OFF_V = OFF_K + ATTN_WIDTH
OFF_GP = OFF_V + ATTN_WIDTH
OFF_GA = OFF_GP + D_MODEL
IN_WIDTH = OFF_GA + D_MODEL

VMEM_LIMIT_BYTES = 56 * 1024 * 1024
LANES = 128
SUBLANES = 8

PROJ_TILE = 512
POST_TILE = 512
POST_GROUPS = 4
ATTN_QROWS = 1024
ROW_ALIGN = 16
PIECE_ROWS = 64
ZERO_ROWS = 128
SORT_CHUNK = 512
LOCAL_ROWS = -(-(POST_TILE * TOP_K + N_EXPERTS * (ROW_ALIGN - 1)) // SORT_CHUNK) * SORT_CHUNK
MAX_BIG_PIECES = LOCAL_ROWS // PIECE_ROWS
MAX_SMALL_PIECES = N_EXPERTS * (PIECE_ROWS // ROW_ALIGN - 1)


def _layer_norm(x):
    mu = jnp.mean(x, axis=-1, keepdims=True)
    xc = x - mu
    var = jnp.mean(xc * xc, axis=-1, keepdims=True)
    return xc * lax.rsqrt(var + LN_EPS)


def _dot(a, b):
    return jnp.dot(a, b, preferred_element_type=F32)


def _ada_kernel(c_ref, w_ref, b_ref, o_ref):
    c = c_ref[...]
    s = c * jax.nn.sigmoid(c)
    o_ref[...] = jnp.dot(s, w_ref[...], preferred_element_type=F32,
                         precision=lax.Precision.HIGHEST) + b_ref[...]


def _ada(c, w_ada, b_ada):
    B, D = c.shape
    rows = SUBLANES
    c_pad = jnp.pad(c, ((0, rows - B), (0, 0)))
    n_out = w_ada.shape[1]
    out = pl.pallas_call(
        _ada_kernel,
        grid=(n_out // D,),
        in_specs=[pl.BlockSpec((rows, D), lambda j: (0, 0)),
                  pl.BlockSpec((D, D), lambda j: (0, j)),
                  pl.BlockSpec((1, D), lambda j: (0, j))],
        out_specs=pl.BlockSpec((rows, D), lambda j: (0, j)),
        out_shape=jax.ShapeDtypeStruct((rows, n_out), F32),
        name="ada",
    )(c_pad, w_ada, b_ada.reshape(1, n_out))
    return out[:B].reshape(B, 6, D)


ROPE_PART_ROWS = 32


def _rope_tables():
    lane = np.arange(LANES)
    li = lane % HEAD_DIM
    half = ROT_DIM // 2
    inv_freq = jnp.power(ROPE_THETA, -jnp.arange(half, dtype=F32) * (2.0 / ROT_DIM))
    invf = jnp.broadcast_to(inv_freq[:, None], (half, LANES))
    freq = np.arange(ROPE_PART_ROWS)[:, None]
    live = (freq < 3 * half) & (freq % half == (li % half)[None, :])
    place = np.stack([live & (li < ROT_DIM)[None, :],
                      -1.0 * (live & (li < half)[None, :]),
                      live & ((li >= half) & (li < ROT_DIM))[None, :]]).astype(np.float32)
    keep = (li >= ROT_DIM).astype(np.float32)[None, :]
    return invf, jnp.asarray(place, BF16), jnp.asarray(keep)


def _proj_kernel(x_ref, xh_ref, pos_ref, ada_ref, invf_ref, place_ref, keep_ref,
                 win_ref, poolw_ref, pscale_ref, wpo_ref,
                 q1_ref, k1_ref, v1_ref, q4_ref, k4_ref, v4_ref, q16_ref, k16_ref, v16_ref,
                 pg_ref, sga_ref, xpe_ref, cls_ref):
    tm = x_ref.shape[1]
    i = pl.program_id(1)
    shift1 = ada_ref[0, 0:1, :]
    scale1 = ada_ref[0, 1:2, :]

    def modulated(xv):
        return (_layer_norm(xv) * (1.0 + scale1) + shift1).astype(BF16)

    u = modulated(x_ref[0])
    uh = modulated(xh_ref[0])

    xp = _dot(u, win_ref[:, 0:POOL_WIDTH])
    xph = _dot(uh, win_ref[:, 0:POOL_WIDTH])
    xph = jnp.where(i > 0, xph, 0.0)
    xpe_ref[0:POOL_HALO, :] = xph
    xpe_ref[POOL_HALO:, :] = xp
    tok = i * tm + lax.broadcasted_iota(jnp.int32, (tm, 1), 0)

    def pooled_branch():
        ys = []
        for g, w in enumerate(POOL_WINDOWS):
            cols = slice(g * POOL_GROUP, (g + 1) * POOL_GROUP)
            xg = xpe_ref[POOL_HALO:, cols]
            acc = xg
            for j in range(1, w):
                acc = acc + xpe_ref[POOL_HALO - j:POOL_HALO - j + tm, cols]
            cnt = jnp.minimum(tok + 1, w).astype(F32)
            mixed = (acc / cnt - xg).astype(BF16)
            ys.append(_dot(mixed, poolw_ref[g]) * pscale_ref[:, cols])
        y = jnp.concatenate(ys, axis=1).astype(BF16)
        return _dot(y, wpo_ref[...])

    ang = invf_ref[:, 0:1] * pos_ref[0, 0].astype(F32)

    def on_lanes(table, j):
        hi = table.astype(BF16).astype(F32)
        mid = (table - hi).astype(BF16).astype(F32)
        lo = table - hi - mid
        pad = jnp.zeros((ROPE_PART_ROWS - 3 * table.shape[0], tm), F32)
        parts = jnp.concatenate([hi, mid, lo, pad], axis=0).astype(BF16)
        return lax.dot_general(parts, place_ref[j], (((0,), (0,)), ((), ())), preferred_element_type=F32)

    cos = jnp.cos(ang)
    sin = jnp.sin(ang)
    c_mul = on_lanes(cos, 0) + keep_ref[...]
    s_lo = on_lanes(sin, 1)
    s_hi = on_lanes(sin, 2)
    c_mul = jnp.concatenate([c_mul, c_mul], axis=1)
    s_lo = jnp.concatenate([s_lo, s_lo], axis=1)
    s_hi = jnp.concatenate([s_hi, s_hi], axis=1)
    half = ROT_DIM // 2

    def rotate(a):
        up = pltpu.roll(a, GROUP_WIDTH - half, axis=1)
        dn = pltpu.roll(a, half, axis=1)
        return a * c_mul + up * s_lo + dn * s_hi

    def emit(a, out_ref, dil):
        if dil == 1:
            out_ref[0, 0] = a.astype(BF16)
            return
        for c in range(GROUP_WIDTH // LANES):
            cls_ref[c] = a[:, c * LANES:(c + 1) * LANES]
        for r in range(dil):
            for c in range(GROUP_WIDTH // LANES):
                out_ref[0, r, :, c * LANES:(c + 1) * LANES] = (
                    cls_ref[c, pl.ds(r, tm // dil, stride=dil), :].astype(BF16))

    outs = ((q1_ref, k1_ref, v1_ref), (q4_ref, k4_ref, v4_ref), (q16_ref, k16_ref, v16_ref))

    def attention_group(gi):
        qo, ko, vo = outs[gi]
        dil = ATTN_GROUPS[gi][1]
        c0 = gi * GROUP_WIDTH
        emit(rotate(_dot(u, win_ref[:, OFF_Q + c0:OFF_Q + c0 + GROUP_WIDTH])), qo, dil)
        emit(rotate(_dot(u, win_ref[:, OFF_K + c0:OFF_K + c0 + GROUP_WIDTH])), ko, dil)
        emit(_dot(u, win_ref[:, OFF_V + c0:OFF_V + c0 + GROUP_WIDTH]), vo, dil)

    attention_group(2)
    pooled = pooled_branch()
    attention_group(1)
    g_p = _dot(u, win_ref[:, OFF_GP:OFF_GP + D_MODEL])
    pg_ref[0] = jax.nn.sigmoid(g_p) * pooled
    attention_group(0)
    g_a = _dot(u, win_ref[:, OFF_GA:OFF_GA + D_MODEL])
    sga_ref[0] = jax.nn.sigmoid(g_a)


def _proj(x, positions, ada, w_in, pool_w, pool_scale, w_pool_out):
    B, S, D = x.shape
    tm = PROJ_TILE
    nt = S // tm
    halo_blocks = tm // POOL_HALO
    const2 = lambda b, i: (0, 0)
    in_specs = [
        pl.BlockSpec((1, tm, D), lambda b, i: (b, i, 0)),
        pl.BlockSpec((1, POOL_HALO, D), lambda b, i: (b, jnp.maximum(i * halo_blocks - 1, 0), 0)),
        pl.BlockSpec((1, 1, 1, tm), lambda b, i: (b, i, 0, 0)),
        pl.BlockSpec((1, 6, D), lambda b, i: (b, 0, 0)),
        pl.BlockSpec((ROT_DIM // 2, LANES), const2),
        pl.BlockSpec((3, ROPE_PART_ROWS, LANES), lambda b, i: (0, 0, 0)),
        pl.BlockSpec((1, LANES), const2),
        pl.BlockSpec((D, IN_WIDTH), const2),
        pl.BlockSpec((len(POOL_WINDOWS), POOL_GROUP, POOL_GROUP), lambda b, i: (0, 0, 0)),
        pl.BlockSpec((1, POOL_WIDTH), const2),
        pl.BlockSpec((POOL_WIDTH, D), const2),
    ]
    out_specs, out_shapes = [], []
    for _, dil in ATTN_GROUPS:
        for _ in range(3):
            out_specs.append(pl.BlockSpec((1, dil, tm // dil, GROUP_WIDTH), lambda b, i: (b, 0, i, 0)))
            out_shapes.append(jax.ShapeDtypeStruct((B, dil, S // dil, GROUP_WIDTH), BF16))
    for _ in range(2):
        out_specs.append(pl.BlockSpec((1, tm, D), lambda b, i: (b, i, 0)))
        out_shapes.append(jax.ShapeDtypeStruct((B, S, D), F32))
    return pl.pallas_call(
        _proj_kernel,
        grid=(B, nt),
        in_specs=in_specs,
        out_specs=out_specs,
        out_shape=out_shapes,
        scratch_shapes=[pltpu.VMEM((tm + POOL_HALO, POOL_WIDTH), F32),
                        pltpu.VMEM((GROUP_WIDTH // LANES, tm, LANES), F32)],
        compiler_params=pltpu.CompilerParams(
            dimension_semantics=("parallel", "parallel"), vmem_limit_bytes=VMEM_LIMIT_BYTES),
        name="proj",
    )(x, x, positions.reshape(B, nt, 1, tm), ada, *_rope_tables(), w_in.astype(BF16),
      pool_w.astype(BF16), pool_scale.reshape(1, POOL_WIDTH), w_pool_out.astype(BF16))


def _attn_kernel(q_ref, k_ref, v_ref, kh_ref, vh_ref, o_ref, lse_ref, kf_ref, vf_ref):
    n_cls, qb = q_ref.shape[1], q_ref.shape[2]
    per = qb // ATTN_BLOCK
    n = pl.program_id(2)
    for c in range(n_cls):
        kf_ref[c, 0:ATTN_BLOCK, :] = kh_ref[0, c]
        kf_ref[c, ATTN_BLOCK:, :] = k_ref[0, c]
        vf_ref[c, 0:ATTN_BLOCK, :] = vh_ref[0, c]
        vf_ref[c, ATTN_BLOCK:, :] = v_ref[0, c]
    qi = lax.broadcasted_iota(jnp.int32, (ATTN_BLOCK, 2 * ATTN_BLOCK), 0)
    kj = lax.broadcasted_iota(jnp.int32, (ATTN_BLOCK, 2 * ATTN_BLOCK), 1)
    band = (kj >= qi) & (kj <= qi + ATTN_BLOCK)
    band_bias = jnp.where(band, 0.0, NEG_INF)
    lane = lax.broadcasted_iota(jnp.int32, (ATTN_BLOCK, GROUP_WIDTH), 1)
    low_lanes = lax.broadcasted_iota(jnp.int32, (ATTN_BLOCK, LANES), 1) < HEAD_DIM
    ones = jnp.ones((2 * ATTN_BLOCK, LANES), BF16)
    nh, blk = HEADS_PER_GROUP, ATTN_BLOCK

    def block(t, carry):
        c, j = t // per, t % per
        r0 = pl.multiple_of(j * ATTN_BLOCK, ATTN_BLOCK)
        first_key = jnp.where((n > 0) | (j > 0), 0, ATTN_BLOCK)
        bias = band_bias + jnp.where(kj < first_key, NEG_INF, 0.0)
        q = q_ref[0, c, pl.ds(r0, blk), :].astype(F32)
        kk = kf_ref[c, pl.ds(r0, 2 * blk), :]
        vv = vf_ref[c, pl.ds(r0, 2 * blk), :]
        qs = jnp.concatenate([jnp.where((lane >= h * HEAD_DIM) & (lane < (h + 1) * HEAD_DIM), q, 0.0)
                              for h in range(nh)], axis=0).astype(BF16)
        s = lax.dot_general(qs, kk, (((1,), (1,)), ((), ())), preferred_element_type=F32)
        s = jnp.concatenate([s[h * blk:(h + 1) * blk] * (HEAD_DIM ** -0.5) + bias for h in range(nh)], axis=0)
        m = jnp.max(s, axis=-1, keepdims=True)
        p = jnp.exp(s - m).astype(BF16)
        den = _dot(p, ones)
        lse = m + jnp.log(den)
        for hp in range(GROUP_WIDTH // LANES):
            rows = slice(2 * hp * blk, (2 * hp + 2) * blk)
            ls = slice(hp * LANES, (hp + 1) * LANES)
            o2 = _dot(p[rows], vv[:, ls]) / den[rows]
            l2 = lse[rows]
            o_ref[0, c, pl.ds(r0, blk), ls] = jnp.where(low_lanes, o2[0:blk], o2[blk:2 * blk])
            lse_ref[0, c, pl.ds(r0, blk), ls] = jnp.where(low_lanes, l2[0:blk], l2[blk:2 * blk])
        return carry

    lax.fori_loop(0, n_cls * per, block, 0, unroll=8)


def _attention(q, k, v):
    B, dil, L, W = q.shape
    qb = min(L, ATTN_QROWS)
    per = qb // ATTN_BLOCK
    n_cls = min(ATTN_QROWS // qb, dil)
    main = pl.BlockSpec((1, n_cls, qb, W), lambda b, r, n: (b, r, n, 0))
    halo = pl.BlockSpec((1, n_cls, ATTN_BLOCK, W), lambda b, r, n: (b, r, jnp.maximum(n * per - 1, 0), 0))
    return pl.pallas_call(
        _attn_kernel,
        grid=(B, dil // n_cls, L // qb),
        in_specs=[main, main, main, halo, halo],
        out_specs=[main, main],
        out_shape=[jax.ShapeDtypeStruct((B, dil, L, W), F32)] * 2,
        scratch_shapes=[pltpu.VMEM((n_cls, qb + ATTN_BLOCK, W), BF16)] * 2,
        compiler_params=pltpu.CompilerParams(
            dimension_semantics=("parallel", "parallel", "parallel"), vmem_limit_bytes=VMEM_LIMIT_BYTES),
        name=f"attn_d{dil}",
    )(q, k, v, k, v)


def _post_kernel(o1_ref, l1_ref, o4_ref, l4_ref, o16_ref, l16_ref, pg_ref, sga_ref, x_ref, ada_ref,
                 wao_ref, wo_ref, g1_ref, b1_ref, wrt_ref, brt_ref,
                 x1_ref, u2_ref, slot_ref, prob_ref, cnt_ref,
                 s0, s1, s2, s3):
    tm = x_ref.shape[1]

    def token_major(src_ref, scr_ref, dil):
        if dil == 1:
            return src_ref[0, 0]
        for r in range(dil):
            for c in range(GROUP_WIDTH // LANES):
                scr_ref[c, pl.ds(r, tm // dil, stride=dil), :] = src_ref[0, r, :, c * LANES:(c + 1) * LANES]
        return jnp.concatenate([scr_ref[c] for c in range(GROUP_WIDTH // LANES)], axis=1)

    o1, l1 = o1_ref[0, 0], l1_ref[0, 0]
    o4, l4 = token_major(o4_ref, s0, 4), token_major(l4_ref, s1, 4)
    o16, l16 = token_major(o16_ref, s2, 16), token_major(l16_ref, s3, 16)
    gate1 = ada_ref[0, 2:3, :]
    shift2 = ada_ref[0, 3:4, :]
    scale2 = ada_ref[0, 4:5, :]
    nt_dot = lambda a, b: lax.dot_general(a, b, (((1,), (1,)), ((), ())), preferred_element_type=F32)
    groups = [slice(h * (tm // POST_GROUPS), (h + 1) * (tm // POST_GROUPS)) for h in range(POST_GROUPS)]

    def merge(rows):
        a1, a4, a16 = l1[rows], l4[rows], l16[rows]
        mx = jnp.maximum(jnp.maximum(a1, a4), a16)
        e1, e4, e16 = jnp.exp(a1 - mx), jnp.exp(a4 - mx), jnp.exp(a16 - mx)
        return ((e1 * o1[rows] + e4 * o4[rows] + e16 * o16[rows]) / (e1 + e4 + e16)).astype(BF16)

    def project(rows, attn):
        merged = pg_ref[0, rows, :] + sga_ref[0, rows, :] * _dot(attn, wao_ref[...])
        return _dot(merged.astype(BF16), wo_ref[...])

    def norms(rows, mix):
        x1 = _layer_norm(DN_ALPHA * x_ref[0, rows, :] + (1.0 + gate1) * mix) * g1_ref[...] + b1_ref[...]
        x1_ref[0, rows, :] = x1
        u2 = _layer_norm(x1) * (1.0 + scale2) + shift2
        u2_hi = u2.astype(BF16)
        u2_ref[0, rows, :] = u2_hi
        return u2_hi, (u2 - u2_hi.astype(F32)).astype(BF16)

    def route(u2_hi, u2_lo):
        return nt_dot(wrt_ref[0], u2_hi) + nt_dot(wrt_ref[0], u2_lo) + nt_dot(wrt_ref[1], u2_hi)

    attns = [merge(rows) for rows in groups]
    mixes = [project(rows, attn) for rows, attn in zip(groups, attns)]
    u2s = [norms(rows, mix) for rows, mix in zip(groups, mixes)]
    logits = jnp.concatenate([route(*u2) for u2 in u2s], axis=1) + brt_ref[...]
    eidx = lax.broadcasted_iota(jnp.int32, (N_EXPERTS, tm), 0)
    work = logits
    vals, idxs = [], []
    for _ in range(TOP_K):
        m = jnp.max(work, axis=0, keepdims=True)
        idx = jnp.min(jnp.where(work == m, eidx, N_EXPERTS), axis=0, keepdims=True)
        vals.append(m)
        idxs.append(idx)
        work = jnp.where(eidx == idx, -jnp.inf, work)
    exps = [jnp.exp(vk - vals[0]) for vk in vals]
    tot = exps[0] + exps[1] + exps[2] + exps[3]
    sel = jnp.zeros((N_EXPERTS, tm), F32)
    for idx in idxs:
        sel = sel + (eidx == idx).astype(F32)
    tr = lax.broadcasted_iota(jnp.int32, (tm, tm), 0)
    tc = lax.broadcasted_iota(jnp.int32, (tm, tm), 1)
    rank = _dot(sel.astype(BF16), (tr < tc).astype(BF16))
    cnt = jnp.sum(sel, axis=1, keepdims=True)
    run = jnp.floor((cnt + (ROW_ALIGN - 1)) * (1.0 / ROW_ALIGN)) * ROW_ALIGN
    er = lax.broadcasted_iota(jnp.int32, (N_EXPERTS, N_EXPERTS), 0)
    ec = lax.broadcasted_iota(jnp.int32, (N_EXPERTS, N_EXPERTS), 1)
    run_start = _dot((ec < er).astype(BF16),
                     jnp.broadcast_to(run, (N_EXPERTS, LANES)).astype(BF16))[:, 0:1]
    slot = rank + run_start
    for k in range(TOP_K):
        slot_ref[0, k:k + 1, :] = jnp.sum(jnp.where(eidx == idxs[k], slot, 0.0), axis=0,
                                          keepdims=True).astype(jnp.int32)
        prob_ref[0, k:k + 1, :] = exps[k] / tot
    cnt_ref[0] = cnt.astype(jnp.int32)


def _post(attn_outs, pg, sga, x, ada, w_attn_out, w_o, ln1_g, ln1_b, w_router, b_router):
    B, S, D = x.shape
    tm = POST_TILE
    nt = S // tm
    N = B * S
    const2 = lambda b, i: (0, 0)
    in_specs, args = [], []
    for (o, lse), (_, dil) in zip(attn_outs, ATTN_GROUPS):
        spec = pl.BlockSpec((1, dil, tm // dil, GROUP_WIDTH), lambda b, i: (b, 0, i, 0))
        in_specs += [spec, spec]
        args += [o, lse]
    tok_spec = pl.BlockSpec((1, tm, D), lambda b, i: (b, i, 0))
    in_specs += [tok_spec, tok_spec, tok_spec,
                 pl.BlockSpec((1, 6, D), lambda b, i: (b, 0, 0)),
                 pl.BlockSpec((GROUP_WIDTH, D), const2),
                 pl.BlockSpec((D, D), const2),
                 pl.BlockSpec((1, D), const2),
                 pl.BlockSpec((1, D), const2),
                 pl.BlockSpec((2, N_EXPERTS, D), lambda b, i: (0, 0, 0)),
                 pl.BlockSpec((N_EXPERTS, 1), const2)]
    wr_hi = w_router.T.astype(BF16)
    wr_lo = (w_router.T - wr_hi.astype(F32)).astype(BF16)
    args += [pg, sga, x, ada, w_attn_out.astype(BF16), w_o.astype(BF16),
             ln1_g.reshape(1, D), ln1_b.reshape(1, D), jnp.stack([wr_hi, wr_lo]),
             b_router.reshape(N_EXPERTS, 1)]
    nc = N // tm
    route_spec = pl.BlockSpec((1, TOP_K, tm), lambda b, i: (b * nt + i, 0, 0))
    out_specs = [tok_spec, tok_spec, route_spec, route_spec,
                 pl.BlockSpec((1, N_EXPERTS, 1), lambda b, i: (b * nt + i, 0, 0))]
    out_shapes = [jax.ShapeDtypeStruct((B, S, D), F32), jax.ShapeDtypeStruct((B, S, D), BF16),
                  jax.ShapeDtypeStruct((nc, TOP_K, tm), jnp.int32), jax.ShapeDtypeStruct((nc, TOP_K, tm), F32),
                  jax.ShapeDtypeStruct((nc, N_EXPERTS, 1), jnp.int32)]
    return pl.pallas_call(
        _post_kernel,
        grid=(B, nt),
        in_specs=in_specs,
        out_specs=out_specs,
        out_shape=out_shapes,
        scratch_shapes=[pltpu.VMEM((GROUP_WIDTH // LANES, tm, LANES), F32)] * 4,
        compiler_params=pltpu.CompilerParams(
            dimension_semantics=("parallel", "parallel"), vmem_limit_bytes=VMEM_LIMIT_BYTES),
        name="post",
    )(*args)


def _for_each_piece(tile, piece_refs, fn):
    big_loc, big_dst, small_loc, small_dst, count = piece_refs
    n_big, n_small = count[2 * tile], count[2 * tile + 1]

    def piece(loc_ref, dst_ref, cap, rows):
        def body(p, carry):
            i = tile * cap + p
            fn(pl.multiple_of(loc_ref[i], ROW_ALIGN), pl.multiple_of(dst_ref[i], ROW_ALIGN), rows)
            return carry
        return body

    lax.fori_loop(0, n_big, piece(big_loc, big_dst, MAX_BIG_PIECES, PIECE_ROWS), 0)
    lax.fori_loop(0, n_small, piece(small_loc, small_dst, MAX_SMALL_PIECES, ROW_ALIGN), 0)
    return n_big * PIECE_ROWS + n_small * ROW_ALIGN


def _wait_rows(src_ref, dst_ref, rows, sem):
    @pl.when(rows > 0)
    def _():
        n = pl.multiple_of(rows, ROW_ALIGN)
        pltpu.make_async_copy(src_ref.at[pl.ds(0, n)], dst_ref.at[pl.ds(0, n)], sem).wait()


def _dispatch_kernel(big_loc, big_dst, small_loc, small_dst, piece_count, tail_ref, ntail_ref, nused_ref,
                     slot_ref, prob_ref, u2_ref, xs_hbm, local2_ref, zero_ref, sem2, zsem, rows_ref):
    tile = pl.program_id(0)
    buf = tile % 2
    local_ref = local2_ref.at[buf]
    sem = sem2.at[buf]
    tt, d = u2_ref.shape
    u2 = u2_ref[...]
    slots = [slot_ref[0, k:k + 1, :] for k in range(TOP_K)]
    probs = [prob_ref[0, k:k + 1, :] for k in range(TOP_K)]
    lane = lax.broadcasted_iota(jnp.int32, (SORT_CHUNK, LANES), 1)
    for r0 in range(0, LOCAL_ROWS, SORT_CHUNK):
        row = r0 + lax.broadcasted_iota(jnp.int32, (SORT_CHUNK, tt), 0)
        w = jnp.zeros((SORT_CHUNK, tt), F32)
        for k in range(TOP_K):
            w = jnp.where(row == slots[k], probs[k], w)
        onehot = jnp.where(w != 0.0, 1.0, 0.0).astype(BF16)
        local_ref[r0:r0 + SORT_CHUNK, 0:d] = _dot(onehot, u2).astype(BF16)
        wr = jnp.sum(w, axis=1, keepdims=True)
        hi = wr.astype(BF16).astype(F32)
        mid = (wr - hi).astype(BF16).astype(F32)
        lo = wr - hi - mid
        parts = jnp.where(lane == 0, hi, jnp.where(lane == 1, mid, jnp.where(lane == 2, lo, 0.0)))
        local_ref[r0:r0 + SORT_CHUNK, d:d + LANES] = parts.astype(BF16)

    def start_piece(loc, dst, rows):
        pltpu.make_async_copy(local_ref.at[pl.ds(loc, rows)], xs_hbm.at[pl.ds(dst, rows)], sem).start()

    rows_ref[buf] = _for_each_piece(tile, (big_loc, big_dst, small_loc, small_dst, piece_count), start_piece)

    @pl.when(tile > 0)
    def _():
        _wait_rows(local2_ref.at[1 - buf], xs_hbm, rows_ref[1 - buf], sem2.at[1 - buf])

    def zero_fill(first_row, n_rows, act):
        def copy(row, rows):
            return pltpu.make_async_copy(zero_ref.at[pl.ds(0, rows)],
                                         xs_hbm.at[pl.ds(pl.multiple_of(row, ROW_ALIGN), rows)], zsem)
        n_big = n_rows // ZERO_ROWS
        rest = first_row + n_big * ZERO_ROWS
        lax.fori_loop(0, n_big, lambda p, c: (act(copy(first_row + p * ZERO_ROWS, ZERO_ROWS)), c)[1], 0)
        lax.fori_loop(0, (n_rows - n_big * ZERO_ROWS) // ROW_ALIGN,
                      lambda p, c: (act(copy(rest + p * ROW_ALIGN, ROW_ALIGN)), c)[1], 0)

    def zero_fill_all(act):
        first_spare = nused_ref[0] * MOE_BLOCK
        lax.fori_loop(0, N_EXPERTS, lambda e, c: (zero_fill(tail_ref[e], ntail_ref[e], act), c)[1], 0)
        zero_fill(first_spare, xs_hbm.shape[0] - first_spare, act)

    @pl.when(tile == 0)
    def _():
        zero_ref[...] = jnp.zeros_like(zero_ref)
        zero_fill_all(lambda c: c.start())

    @pl.when(tile == pl.num_programs(0) - 1)
    def _():
        _wait_rows(local_ref, xs_hbm, rows_ref[buf], sem)
        zero_fill_all(lambda c: c.wait())


def _dispatch(tables, slot, prob, u2, n_rows):
    N, D = u2.shape
    tt = POST_TILE
    route_spec = pl.BlockSpec((1, TOP_K, tt), lambda i, *t: (i, 0, 0))
    grid_spec = pltpu.PrefetchScalarGridSpec(
        num_scalar_prefetch=len(tables),
        grid=(N // tt,),
        in_specs=[route_spec, route_spec, pl.BlockSpec((tt, D), lambda i, *t: (i, 0))],
        out_specs=pl.BlockSpec(memory_space=pl.ANY),
        scratch_shapes=[pltpu.VMEM((2, LOCAL_ROWS, D + LANES), BF16), pltpu.VMEM((ZERO_ROWS, D + LANES), BF16),
                        pltpu.SemaphoreType.DMA((2,)), pltpu.SemaphoreType.DMA(()),
                        pltpu.SMEM((2,), jnp.int32)],
    )
    return pl.pallas_call(
        _dispatch_kernel,
        grid_spec=grid_spec,
        out_shape=jax.ShapeDtypeStruct((n_rows, D + LANES), BF16),
        compiler_params=pltpu.CompilerParams(
            dimension_semantics=("arbitrary",), vmem_limit_bytes=VMEM_LIMIT_BYTES),
        name="dispatch",
    )(*tables, slot, prob, u2)


def _expert_kernel(be_ref, next_ref, live_ref, nused_ref, xs_ref, wg_hbm, bg_ref, wu_hbm, bu_ref, wd_hbm, bd_ref,
                   ys_ref, stage, wg_s, wu_s, wd_s, sem):
    i = pl.program_id(0)
    used = i < nused_ref[0]
    prev = be_ref[jnp.maximum(i - 1, 0)]
    fresh = (i == 0) | (be_ref[i] != prev)

    def fetch(e):
        return [pltpu.make_async_copy(w_hbm.at[e], stage.at[j], sem.at[j])
                for j, w_hbm in enumerate((wg_hbm, wu_hbm, wd_hbm))]

    @pl.when(i == 0)
    def _():
        for copy in fetch(be_ref[0]):
            copy.start()

    @pl.when(used & fresh)
    def _():
        for copy in fetch(be_ref[i]):
            copy.wait()
        wg_s[...] = stage[0].astype(BF16)
        wu_s[...] = stage[1].astype(BF16)
        wd_s[...] = stage[2].astype(BF16)

        @pl.when(next_ref[i] >= 0)
        def _():
            for copy in fetch(next_ref[i]):
                copy.start()

    d = ys_ref.shape[1]

    def expert(rows):
        xb = xs_ref[0:rows, 0:d]
        parts = xs_ref[0:rows, d:d + LANES].astype(F32)
        weight = parts[:, 0:1] + parts[:, 1:2] + parts[:, 2:3]
        g = _dot(xb, wg_s[...]) + bg_ref[0]
        up = _dot(xb, wu_s[...]) + bu_ref[0]
        g = jnp.minimum(g, SWIGLU_LIMIT)
        up = jnp.clip(up, -SWIGLU_LIMIT, SWIGLU_LIMIT)
        h = g * jax.nn.sigmoid(SWIGLU_ALPHA * g) * (up + 1.0)
        return ((_dot(h.astype(BF16), wd_s[...]) + bd_ref[0]) * weight).astype(BF16)

    for parts_live in range(1, MOE_BLOCK // EXPERT_PART + 1):
        rows = parts_live * EXPERT_PART

        @pl.when(used & (live_ref[i] == parts_live))
        def _():
            if rows == MOE_BLOCK:
                ys_ref[...] = expert(rows)
            else:
                ys_ref[0:rows, :] = expert(rows)
                ys_ref[rows:, :] = jnp.zeros((MOE_BLOCK - rows, d), BF16)

    @pl.when(jnp.logical_not(used))
    def _():
        ys_ref[...] = jnp.zeros_like(ys_ref)


def _experts(be, next_expert, live_parts, nused, xs, w_gate, b_gate, w_up, b_up, w_down, b_down):
    P, width = xs.shape
    D = width - LANES
    E = w_gate.shape[0]

    def live(i, nu):
        return jnp.maximum(jnp.minimum(i, nu[0] - 1), 0)

    w_spec = pl.BlockSpec(memory_space=pl.ANY)
    b_spec = pl.BlockSpec((1, 1, D), lambda i, be, nx, hf, nu: (be[live(i, nu)], 0, 0))
    grid_spec = pltpu.PrefetchScalarGridSpec(
        num_scalar_prefetch=4,
        grid=(P // MOE_BLOCK,),
        in_specs=[pl.BlockSpec((MOE_BLOCK, width), lambda i, be, nx, hf, nu: (live(i, nu), 0)),
                  w_spec, b_spec, w_spec, b_spec, w_spec, b_spec],
        out_specs=pl.BlockSpec((MOE_BLOCK, D), lambda i, be, nx, hf, nu: (i, 0)),
        scratch_shapes=[pltpu.VMEM((3, D, D), F32)] + [pltpu.VMEM((D, D), BF16)] * 3
                       + [pltpu.SemaphoreType.DMA((3,))],
    )
    return pl.pallas_call(
        _expert_kernel,
        grid_spec=grid_spec,
        out_shape=jax.ShapeDtypeStruct((P, D), BF16),
        compiler_params=pltpu.CompilerParams(
            dimension_semantics=("arbitrary",), vmem_limit_bytes=VMEM_LIMIT_BYTES),
        name="experts",
    )(be, next_expert, live_parts, nused, xs, w_gate, b_gate.reshape(E, 1, D), w_up, b_up.reshape(E, 1, D),
      w_down, b_down.reshape(E, 1, D))


def _combine_kernel(big_loc, big_dst, small_loc, small_dst, piece_count,
                    slot_ref, x1_ref, ada_ref, g2_ref, b2_ref, ys_hbm,
                    out_ref, local2_ref, sem2, rows_ref):
    tile = pl.program_id(0)
    buf = tile % 2
    tt = x1_ref.shape[0]

    def fetch(t, b):
        def start_piece(loc, dst, rows):
            pltpu.make_async_copy(ys_hbm.at[pl.ds(dst, rows)], local2_ref.at[b, pl.ds(loc, rows)],
                                  sem2.at[b]).start()
        rows_ref[b] = _for_each_piece(t, (big_loc, big_dst, small_loc, small_dst, piece_count), start_piece)

    @pl.when(tile == 0)
    def _():
        local2_ref[...] = jnp.zeros_like(local2_ref)
        fetch(0, 0)

    @pl.when(tile + 1 < pl.num_programs(0))
    def _():
        fetch(tile + 1, 1 - buf)

    local_ref = local2_ref.at[buf]
    _wait_rows(ys_hbm, local_ref, rows_ref[buf], sem2.at[buf])
    slots = [slot_ref[:, k:k + 1] for k in range(TOP_K)]

    ffn = jnp.zeros((tt, x1_ref.shape[1]), F32)
    for r0 in range(0, LOCAL_ROWS, SORT_CHUNK):
        col = r0 + lax.broadcasted_iota(jnp.int32, (tt, SORT_CHUNK), 1)
        onehot = jnp.zeros((tt, SORT_CHUNK), F32)
        for k in range(TOP_K):
            onehot = jnp.where(col == slots[k], 1.0, onehot)
        ffn = ffn + _dot(onehot.astype(BF16), local_ref[r0:r0 + SORT_CHUNK, :])
    gate2 = ada_ref[0, 5:6, :]
    y = DN_ALPHA * x1_ref[...] + (1.0 + gate2) * ffn
    out_ref[...] = _layer_norm(y) * g2_ref[...] + b2_ref[...]


def _combine(tables, slot_tok, x1, ada, ln2_g, ln2_b, ys, tiles_per_batch):
    N, D = x1.shape
    tt = POST_TILE
    const = lambda i, *t: (0, 0)
    tok4 = pl.BlockSpec((tt, TOP_K), lambda i, *t: (i, 0))
    grid_spec = pltpu.PrefetchScalarGridSpec(
        num_scalar_prefetch=len(tables),
        grid=(N // tt,),
        in_specs=[tok4,
                  pl.BlockSpec((tt, D), lambda i, *t: (i, 0)),
                  pl.BlockSpec((1, 6, D), lambda i, *t: (i // tiles_per_batch, 0, 0)),
                  pl.BlockSpec((1, D), const),
                  pl.BlockSpec((1, D), const),
                  pl.BlockSpec(memory_space=pl.ANY)],
        out_specs=pl.BlockSpec((tt, D), lambda i, *t: (i, 0)),
        scratch_shapes=[pltpu.VMEM((2, LOCAL_ROWS, D), BF16), pltpu.SemaphoreType.DMA((2,)),
                        pltpu.SMEM((2,), jnp.int32)],
    )
    return pl.pallas_call(
        _combine_kernel,
        grid_spec=grid_spec,
        out_shape=jax.ShapeDtypeStruct((N, D), F32),
        compiler_params=pltpu.CompilerParams(
            dimension_semantics=("arbitrary",), vmem_limit_bytes=VMEM_LIMIT_BYTES),
        name="combine",
    )(*tables, slot_tok, x1, ada, ln2_g.reshape(1, D), ln2_b.reshape(1, D), ys)


def _routing_tables(cnt, nblk):
    nt, E = cnt.shape
    i32 = jnp.int32
    run = (cnt + ROW_ALIGN - 1) // ROW_ALIGN * ROW_ALIGN
    run_start = jnp.cumsum(run, axis=1) - run
    seg_len = jnp.sum(run, axis=0)
    seg_blocks = (seg_len + MOE_BLOCK - 1) // MOE_BLOCK
    b_end = jnp.cumsum(seg_blocks)
    seg_off = (b_end - seg_blocks) * MOE_BLOCK
    dest = seg_off[None, :] + jnp.cumsum(run, axis=0) - run
    nused = b_end[-1:].astype(i32)
    i = jnp.arange(nblk, dtype=i32)
    be = jnp.minimum(jnp.sum((i[:, None] >= b_end[None, :]).astype(i32), axis=1), E - 1)
    e_ids = jnp.arange(E, dtype=i32)
    later = (e_ids[None, :] > e_ids[:, None]) & (seg_blocks > 0)[None, :]
    next_e = jnp.min(jnp.where(later, e_ids[None, :], E), axis=1)
    next_e = jnp.where(next_e < E, next_e, -1)
    mine = be[:, None] == e_ids[None, :]
    pick = lambda v: jnp.sum(jnp.where(mine, v[None, :], 0), axis=1)
    nxt = pick(next_e)
    rows_left = pick(seg_len) - (i - pick(b_end - seg_blocks)) * MOE_BLOCK
    live_parts = jnp.clip((rows_left + EXPERT_PART - 1) // EXPERT_PART, 1, MOE_BLOCK // EXPERT_PART).astype(i32)
    tail = seg_off + seg_len
    ntail = seg_blocks * MOE_BLOCK - seg_len
    def flat_pieces(n, first_loc, first_dst, rows, cap):
        end = jnp.cumsum(n, axis=1)
        start = end - n
        p = jnp.arange(cap, dtype=i32)[None, :, None]
        off = p - start[:, None, :]
        owner = (off >= 0) & (p < end[:, None, :])
        loc = jnp.sum(jnp.where(owner, first_loc[:, None, :] + rows * off, 0), axis=2)
        dst = jnp.sum(jnp.where(owner, first_dst[:, None, :] + rows * off, 0), axis=2)
        return loc.reshape(-1).astype(i32), dst.reshape(-1).astype(i32), end[:, -1]

    n_big = run // PIECE_ROWS
    big_rows = n_big * PIECE_ROWS
    big_loc, big_dst, big_count = flat_pieces(n_big, run_start, dest, PIECE_ROWS, MAX_BIG_PIECES)
    small_loc, small_dst, small_count = flat_pieces((run - big_rows) // ROW_ALIGN, run_start + big_rows,
                                                    dest + big_rows, ROW_ALIGN, MAX_SMALL_PIECES)
    piece_count = jnp.stack([big_count, small_count], axis=1).reshape(-1).astype(i32)
    piece_tables = (big_loc, big_dst, small_loc, small_dst, piece_count)
    return piece_tables, (tail.astype(i32), ntail.astype(i32)), (be.astype(i32), nxt.astype(i32), live_parts, nused)


def kernel(x, c, positions, w_ada, b_ada, w_in, pool_w, pool_scale, w_pool_out, w_attn_out, w_o,
           ln1_g, ln1_b, w_router, b_router, w_gate, b_gate, w_up, b_up, w_down, b_down, ln2_g, ln2_b):
    B, S, D = x.shape
    N = B * S
    assert D == D_MODEL and S % PROJ_TILE == 0 and S % (16 * ATTN_BLOCK) == 0
    assert S % POST_TILE == 0 and LOCAL_ROWS % SORT_CHUNK == 0
    nt = N // POST_TILE
    nblk = (N * TOP_K + nt * N_EXPERTS * (ROW_ALIGN - 1)) // MOE_BLOCK + N_EXPERTS
    for l in range(DEPTH):
        ada = _ada(c, w_ada[l], b_ada[l])
        proj_out = _proj(x, positions, ada, w_in[l], pool_w[l], pool_scale[l], w_pool_out[l])
        qkv, (pg, sga) = proj_out[:9], proj_out[9:]
        attn_outs = [_attention(*qkv[3 * g:3 * g + 3]) for g in range(len(ATTN_GROUPS))]
        x1, u2, slot, prob, cnt = _post(attn_outs, pg, sga, x, ada, w_attn_out[l], w_o[l],
                                        ln1_g[l], ln1_b[l], w_router[l], b_router[l])
        piece_tables, tail_tables, block_tables = _routing_tables(cnt.reshape(nt, N_EXPERTS), nblk)
        xs = _dispatch(piece_tables + tail_tables + block_tables[3:], slot, prob, u2.reshape(N, D),
                       nblk * MOE_BLOCK)
        ys = _experts(*block_tables, xs, w_gate[l], b_gate[l], w_up[l], b_up[l], w_down[l], b_down[l])
        slot_tok = slot.transpose(0, 2, 1).reshape(N, TOP_K)
        out = _combine(piece_tables, slot_tok, x1.reshape(N, D), ada,
                       ln2_g[l], ln2_b[l], ys, S // POST_TILE)
        x = out.reshape(B, S, D)
    return x
```

```python
import jax
import jax.numpy as jnp
import numpy as np
from jax import lax
from jax.experimental import pallas as pl
from jax.experimental.pallas import tpu as pltpu

F32 = jnp.float32
BF16 = jnp.bfloat16

D_MODEL = 1024
POOL_WINDOWS = (2, 4, 8, 16)
POOL_WIDTH = D_MODEL // 2
POOL_GROUP = POOL_WIDTH // len(POOL_WINDOWS)
POOL_HALO = 16
HEAD_DIM = 64
ATTN_GROUPS = ((128, 1), (512, 4), (2048, 16))
HEADS_PER_GROUP = 4
GROUP_WIDTH = HEADS_PER_GROUP * HEAD_DIM
N_HEADS = HEADS_PER_GROUP * len(ATTN_GROUPS)
ATTN_WIDTH = N_HEADS * HEAD_DIM
ATTN_BLOCK = 128
ROT_DIM = HEAD_DIM // 4
ROPE_THETA = 500000.0
N_EXPERTS = 32
TOP_K = 4
SWIGLU_ALPHA = 1.702
SWIGLU_LIMIT = 7.0
MOE_BLOCK = 1024
EXPERT_PART = 128
DEPTH = 1
DN_ALPHA = (2.0 * DEPTH) ** 0.25
LN_EPS = 1e-5
NEG_INF = -1e30

OFF_Q = POOL_WIDTH
OFF_K = OFF_Q + ATTN_WIDTH
OFF_V = OFF_K + ATTN_WIDTH
OFF_GP = OFF_V + ATTN_WIDTH
OFF_GA = OFF_GP + D_MODEL
IN_WIDTH = OFF_GA + D_MODEL

VMEM_LIMIT_BYTES = 56 * 1024 * 1024
LANES = 128
SUBLANES = 8

PROJ_TILE = 512
POST_TILE = 512
POST_GROUPS = 4
ATTN_QROWS = 1024
ROW_ALIGN = 16
PIECE_ROWS = 64
ZERO_ROWS = 128
SORT_CHUNK = 512
LOCAL_ROWS = -(-(POST_TILE * TOP_K + N_EXPERTS * (ROW_ALIGN - 1)) // SORT_CHUNK) * SORT_CHUNK
MAX_BIG_PIECES = LOCAL_ROWS // PIECE_ROWS
MAX_SMALL_PIECES = N_EXPERTS * (PIECE_ROWS // ROW_ALIGN - 1)


def _layer_norm(x):
    mu = jnp.mean(x, axis=-1, keepdims=True)
    xc = x - mu
    var = jnp.mean(xc * xc, axis=-1, keepdims=True)
    return xc * lax.rsqrt(var + LN_EPS)


def _dot(a, b):
    return jnp.dot(a, b, preferred_element_type=F32)


def _ada_kernel(c_ref, w_ref, b_ref, o_ref):
    c = c_ref[...]
    s = c * jax.nn.sigmoid(c)
    o_ref[...] = jnp.dot(s, w_ref[...], preferred_element_type=F32,
                         precision=lax.Precision.HIGHEST) + b_ref[...]


def _ada(c, w_ada, b_ada):
    B, D = c.shape
    rows = SUBLANES
    c_pad = jnp.pad(c, ((0, rows - B), (0, 0)))
    n_out = w_ada.shape[1]
    out = pl.pallas_call(
        _ada_kernel,
        grid=(n_out // D,),
        in_specs=[pl.BlockSpec((rows, D), lambda j: (0, 0)),
                  pl.BlockSpec((D, D), lambda j: (0, j)),
                  pl.BlockSpec((1, D), lambda j: (0, j))],
        out_specs=pl.BlockSpec((rows, D), lambda j: (0, j)),
        out_shape=jax.ShapeDtypeStruct((rows, n_out), F32),
        name="ada",
    )(c_pad, w_ada, b_ada.reshape(1, n_out))
    return out[:B].reshape(B, 6, D)


ROPE_PART_ROWS = 32


def _rope_tables():
    lane = np.arange(LANES)
    li = lane % HEAD_DIM
    half = ROT_DIM // 2
    inv_freq = jnp.power(ROPE_THETA, -jnp.arange(half, dtype=F32) * (2.0 / ROT_DIM))
    invf = jnp.broadcast_to(inv_freq[:, None], (half, LANES))
    freq = np.arange(ROPE_PART_ROWS)[:, None]
    live = (freq < 3 * half) & (freq % half == (li % half)[None, :])
    place = np.stack([live & (li < ROT_DIM)[None, :],
                      -1.0 * (live & (li < half)[None, :]),
                      live & ((li >= half) & (li < ROT_DIM))[None, :]]).astype(np.float32)
    keep = (li >= ROT_DIM).astype(np.float32)[None, :]
    return invf, jnp.asarray(place, BF16), jnp.asarray(keep)


def _proj_kernel(x_ref, xh_ref, pos_ref, ada_ref, invf_ref, place_ref, keep_ref,
                 win_ref, poolw_ref, pscale_ref, wpo_ref,
                 q1_ref, k1_ref, v1_ref, q4_ref, k4_ref, v4_ref, q16_ref, k16_ref, v16_ref,
                 pg_ref, sga_ref, xpe_ref, cls_ref):
    tm = x_ref.shape[1]
    i = pl.program_id(1)
    shift1 = ada_ref[0, 0:1, :]
    scale1 = ada_ref[0, 1:2, :]

    def modulated(xv):
        return (_layer_norm(xv) * (1.0 + scale1) + shift1).astype(BF16)

    u = modulated(x_ref[0])
    uh = modulated(xh_ref[0])

    xp = _dot(u, win_ref[:, 0:POOL_WIDTH])
    xph = _dot(uh, win_ref[:, 0:POOL_WIDTH])
    xph = jnp.where(i > 0, xph, 0.0)
    xpe_ref[0:POOL_HALO, :] = xph
    xpe_ref[POOL_HALO:, :] = xp
    tok = i * tm + lax.broadcasted_iota(jnp.int32, (tm, 1), 0)

    def pooled_branch():
        ys = []
        for g, w in enumerate(POOL_WINDOWS):
            cols = slice(g * POOL_GROUP, (g + 1) * POOL_GROUP)
            xg = xpe_ref[POOL_HALO:, cols]
            acc = xg
            for j in range(1, w):
                acc = acc + xpe_ref[POOL_HALO - j:POOL_HALO - j + tm, cols]
            cnt = jnp.minimum(tok + 1, w).astype(F32)
            mixed = (acc / cnt - xg).astype(BF16)
            ys.append(_dot(mixed, poolw_ref[g]) * pscale_ref[:, cols])
        y = jnp.concatenate(ys, axis=1).astype(BF16)
        return _dot(y, wpo_ref[...])

    ang = invf_ref[:, 0:1] * pos_ref[0, 0].astype(F32)

    def on_lanes(table, j):
        hi = table.astype(BF16).astype(F32)
        mid = (table - hi).astype(BF16).astype(F32)
        lo = table - hi - mid
        pad = jnp.zeros((ROPE_PART_ROWS - 3 * table.shape[0], tm), F32)
        parts = jnp.concatenate([hi, mid, lo, pad], axis=0).astype(BF16)
        return lax.dot_general(parts, place_ref[j], (((0,), (0,)), ((), ())), preferred_element_type=F32)

    cos = jnp.cos(ang)
    sin = jnp.sin(ang)
    c_mul = on_lanes(cos, 0) + keep_ref[...]
    s_lo = on_lanes(sin, 1)
    s_hi = on_lanes(sin, 2)
    c_mul = jnp.concatenate([c_mul, c_mul], axis=1)
    s_lo = jnp.concatenate([s_lo, s_lo], axis=1)
    s_hi = jnp.concatenate([s_hi, s_hi], axis=1)
    half = ROT_DIM // 2

    def rotate(a):
        up = pltpu.roll(a, GROUP_WIDTH - half, axis=1)
        dn = pltpu.roll(a, half, axis=1)
        return a * c_mul + up * s_lo + dn * s_hi

    def emit(a, out_ref, dil):
        if dil == 1:
            out_ref[0, 0] = a.astype(BF16)
            return
        for c in range(GROUP_WIDTH // LANES):
            cls_ref[c] = a[:, c * LANES:(c + 1) * LANES]
        for r in range(dil):
            for c in range(GROUP_WIDTH // LANES):
                out_ref[0, r, :, c * LANES:(c + 1) * LANES] = (
                    cls_ref[c, pl.ds(r, tm // dil, stride=dil), :].astype(BF16))

    outs = ((q1_ref, k1_ref, v1_ref), (q4_ref, k4_ref, v4_ref), (q16_ref, k16_ref, v16_ref))

    def attention_group(gi):
        qo, ko, vo = outs[gi]
        dil = ATTN_GROUPS[gi][1]
        c0 = gi * GROUP_WIDTH
        emit(rotate(_dot(u, win_ref[:, OFF_Q + c0:OFF_Q + c0 + GROUP_WIDTH])), qo, dil)
        emit(rotate(_dot(u, win_ref[:, OFF_K + c0:OFF_K + c0 + GROUP_WIDTH])), ko, dil)
        emit(_dot(u, win_ref[:, OFF_V + c0:OFF_V + c0 + GROUP_WIDTH]), vo, dil)

    attention_group(2)
    pooled = pooled_branch()
    attention_group(1)
    g_p = _dot(u, win_ref[:, OFF_GP:OFF_GP + D_MODEL])
    pg_ref[0] = jax.nn.sigmoid(g_p) * pooled
    attention_group(0)
    g_a = _dot(u, win_ref[:, OFF_GA:OFF_GA + D_MODEL])
    sga_ref[0] = jax.nn.sigmoid(g_a)


def _proj(x, positions, ada, w_in, pool_w, pool_scale, w_pool_out):
    B, S, D = x.shape
    tm = PROJ_TILE
    nt = S // tm
    halo_blocks = tm // POOL_HALO
    const2 = lambda b, i: (0, 0)
    in_specs = [
        pl.BlockSpec((1, tm, D), lambda b, i: (b, i, 0)),
        pl.BlockSpec((1, POOL_HALO, D), lambda b, i: (b, jnp.maximum(i * halo_blocks - 1, 0), 0)),
        pl.BlockSpec((1, 1, 1, tm), lambda b, i: (b, i, 0, 0)),
        pl.BlockSpec((1, 6, D), lambda b, i: (b, 0, 0)),
        pl.BlockSpec((ROT_DIM // 2, LANES), const2),
        pl.BlockSpec((3, ROPE_PART_ROWS, LANES), lambda b, i: (0, 0, 0)),
        pl.BlockSpec((1, LANES), const2),
        pl.BlockSpec((D, IN_WIDTH), const2),
        pl.BlockSpec((len(POOL_WINDOWS), POOL_GROUP, POOL_GROUP), lambda b, i: (0, 0, 0)),
        pl.BlockSpec((1, POOL_WIDTH), const2),
        pl.BlockSpec((POOL_WIDTH, D), const2),
    ]
    out_specs, out_shapes = [], []
    for _, dil in ATTN_GROUPS:
        for _ in range(3):
            out_specs.append(pl.BlockSpec((1, dil, tm // dil, GROUP_WIDTH), lambda b, i: (b, 0, i, 0)))
            out_shapes.append(jax.ShapeDtypeStruct((B, dil, S // dil, GROUP_WIDTH), BF16))
    for _ in range(2):
        out_specs.append(pl.BlockSpec((1, tm, D), lambda b, i: (b, i, 0)))
        out_shapes.append(jax.ShapeDtypeStruct((B, S, D), F32))
    return pl.pallas_call(
        _proj_kernel,
        grid=(B, nt),
        in_specs=in_specs,
        out_specs=out_specs,
        out_shape=out_shapes,
        scratch_shapes=[pltpu.VMEM((tm + POOL_HALO, POOL_WIDTH), F32),
                        pltpu.VMEM((GROUP_WIDTH // LANES, tm, LANES), F32)],
        compiler_params=pltpu.CompilerParams(
            dimension_semantics=("parallel", "parallel"), vmem_limit_bytes=VMEM_LIMIT_BYTES),
        name="proj",
    )(x, x, positions.reshape(B, nt, 1, tm), ada, *_rope_tables(), w_in.astype(BF16),
      pool_w.astype(BF16), pool_scale.reshape(1, POOL_WIDTH), w_pool_out.astype(BF16))


def _attn_kernel(q_ref, k_ref, v_ref, kh_ref, vh_ref, o_ref, lse_ref, kf_ref, vf_ref):
    n_cls, qb = q_ref.shape[1], q_ref.shape[2]
    per = qb // ATTN_BLOCK
    n = pl.program_id(2)
    for c in range(n_cls):
        kf_ref[c, 0:ATTN_BLOCK, :] = kh_ref[0, c]
        kf_ref[c, ATTN_BLOCK:, :] = k_ref[0, c]
        vf_ref[c, 0:ATTN_BLOCK, :] = vh_ref[0, c]
        vf_ref[c, ATTN_BLOCK:, :] = v_ref[0, c]
    qi = lax.broadcasted_iota(jnp.int32, (ATTN_BLOCK, 2 * ATTN_BLOCK), 0)
    kj = lax.broadcasted_iota(jnp.int32, (ATTN_BLOCK, 2 * ATTN_BLOCK), 1)
    band = (kj >= qi) & (kj <= qi + ATTN_BLOCK)
    band_bias = jnp.where(band, 0.0, NEG_INF)
    lane = lax.broadcasted_iota(jnp.int32, (ATTN_BLOCK, GROUP_WIDTH), 1)
    low_lanes = lax.broadcasted_iota(jnp.int32, (ATTN_BLOCK, LANES), 1) < HEAD_DIM
    ones = jnp.ones((2 * ATTN_BLOCK, LANES), BF16)
    nh, blk = HEADS_PER_GROUP, ATTN_BLOCK

    def block(t, carry):
        c, j = t // per, t % per
        r0 = pl.multiple_of(j * ATTN_BLOCK, ATTN_BLOCK)
        first_key = jnp.where((n > 0) | (j > 0), 0, ATTN_BLOCK)
        bias = band_bias + jnp.where(kj < first_key, NEG_INF, 0.0)
        q = q_ref[0, c, pl.ds(r0, blk), :].astype(F32)
        kk = kf_ref[c, pl.ds(r0, 2 * blk), :]
        vv = vf_ref[c, pl.ds(r0, 2 * blk), :]
        qs = jnp.concatenate([jnp.where((lane >= h * HEAD_DIM) & (lane < (h + 1) * HEAD_DIM), q, 0.0)
                              for h in range(nh)], axis=0).astype(BF16)
        s = lax.dot_general(qs, kk, (((1,), (1,)), ((), ())), preferred_element_type=F32)
        s = jnp.concatenate([s[h * blk:(h + 1) * blk] * (HEAD_DIM ** -0.5) + bias for h in range(nh)], axis=0)
        m = jnp.max(s, axis=-1, keepdims=True)
        p = jnp.exp(s - m).astype(BF16)
        den = _dot(p, ones)
        lse = m + jnp.log(den)
        for hp in range(GROUP_WIDTH // LANES):
            rows = slice(2 * hp * blk, (2 * hp + 2) * blk)
            ls = slice(hp * LANES, (hp + 1) * LANES)
            o2 = _dot(p[rows], vv[:, ls]) / den[rows]
            l2 = lse[rows]
            o_ref[0, c, pl.ds(r0, blk), ls] = jnp.where(low_lanes, o2[0:blk], o2[blk:2 * blk])
            lse_ref[0, c, pl.ds(r0, blk), ls] = jnp.where(low_lanes, l2[0:blk], l2[blk:2 * blk])
        return carry

    lax.fori_loop(0, n_cls * per, block, 0, unroll=8)


def _attention(q, k, v):
    B, dil, L, W = q.shape
    qb = min(L, ATTN_QROWS)
    per = qb // ATTN_BLOCK
    n_cls = min(ATTN_QROWS // qb, dil)
    main = pl.BlockSpec((1, n_cls, qb, W), lambda b, r, n: (b, r, n, 0))
    halo = pl.BlockSpec((1, n_cls, ATTN_BLOCK, W), lambda b, r, n: (b, r, jnp.maximum(n * per - 1, 0), 0))
    return pl.pallas_call(
        _attn_kernel,
        grid=(B, dil // n_cls, L // qb),
        in_specs=[main, main, main, halo, halo],
        out_specs=[main, main],
        out_shape=[jax.ShapeDtypeStruct((B, dil, L, W), F32)] * 2,
        scratch_shapes=[pltpu.VMEM((n_cls, qb + ATTN_BLOCK, W), BF16)] * 2,
        compiler_params=pltpu.CompilerParams(
            dimension_semantics=("parallel", "parallel", "parallel"), vmem_limit_bytes=VMEM_LIMIT_BYTES),
        name=f"attn_d{dil}",
    )(q, k, v, k, v)


def _post_kernel(o1_ref, l1_ref, o4_ref, l4_ref, o16_ref, l16_ref, pg_ref, sga_ref, x_ref, ada_ref,
                 wao_ref, wo_ref, g1_ref, b1_ref, wrt_ref, brt_ref,
                 x1_ref, u2_ref, slot_ref, prob_ref, cnt_ref,
                 s0, s1, s2, s3):
    tm = x_ref.shape[1]

    def token_major(src_ref, scr_ref, dil):
        if dil == 1:
            return src_ref[0, 0]
        for r in range(dil):
            for c in range(GROUP_WIDTH // LANES):
                scr_ref[c, pl.ds(r, tm // dil, stride=dil), :] = src_ref[0, r, :, c * LANES:(c + 1) * LANES]
        return jnp.concatenate([scr_ref[c] for c in range(GROUP_WIDTH // LANES)], axis=1)

    o1, l1 = o1_ref[0, 0], l1_ref[0, 0]
    o4, l4 = token_major(o4_ref, s0, 4), token_major(l4_ref, s1, 4)
    o16, l16 = token_major(o16_ref, s2, 16), token_major(l16_ref, s3, 16)
    gate1 = ada_ref[0, 2:3, :]
    shift2 = ada_ref[0, 3:4, :]
    scale2 = ada_ref[0, 4:5, :]
    nt_dot = lambda a, b: lax.dot_general(a, b, (((1,), (1,)), ((), ())), preferred_element_type=F32)
    groups = [slice(h * (tm // POST_GROUPS), (h + 1) * (tm // POST_GROUPS)) for h in range(POST_GROUPS)]

    def merge(rows):
        a1, a4, a16 = l1[rows], l4[rows], l16[rows]
        mx = jnp.maximum(jnp.maximum(a1, a4), a16)
        e1, e4, e16 = jnp.exp(a1 - mx), jnp.exp(a4 - mx), jnp.exp(a16 - mx)
        return ((e1 * o1[rows] + e4 * o4[rows] + e16 * o16[rows]) / (e1 + e4 + e16)).astype(BF16)

    def project(rows, attn):
        merged = pg_ref[0, rows, :] + sga_ref[0, rows, :] * _dot(attn, wao_ref[...])
        return _dot(merged.astype(BF16), wo_ref[...])

    def norms(rows, mix):
        x1 = _layer_norm(DN_ALPHA * x_ref[0, rows, :] + (1.0 + gate1) * mix) * g1_ref[...] + b1_ref[...]
        x1_ref[0, rows, :] = x1
        u2 = _layer_norm(x1) * (1.0 + scale2) + shift2
        u2_hi = u2.astype(BF16)
        u2_ref[0, rows, :] = u2_hi
        return u2_hi, (u2 - u2_hi.astype(F32)).astype(BF16)

    def route(u2_hi, u2_lo):
        return nt_dot(wrt_ref[0], u2_hi) + nt_dot(wrt_ref[0], u2_lo) + nt_dot(wrt_ref[1], u2_hi)

    attns = [merge(rows) for rows in groups]
    mixes = [project(rows, attn) for rows, attn in zip(groups, attns)]
    u2s = [norms(rows, mix) for rows, mix in zip(groups, mixes)]
    logits = jnp.concatenate([route(*u2) for u2 in u2s], axis=1) + brt_ref[...]
    eidx = lax.broadcasted_iota(jnp.int32, (N_EXPERTS, tm), 0)
    work = logits
    vals, idxs = [], []
    for _ in range(TOP_K):
        m = jnp.max(work, axis=0, keepdims=True)
        idx = jnp.min(jnp.where(work == m, eidx, N_EXPERTS), axis=0, keepdims=True)
        vals.append(m)
        idxs.append(idx)
        work = jnp.where(eidx == idx, -jnp.inf, work)
    exps = [jnp.exp(vk - vals[0]) for vk in vals]
    tot = exps[0] + exps[1] + exps[2] + exps[3]
    sel = jnp.zeros((N_EXPERTS, tm), F32)
    for idx in idxs:
        sel = sel + (eidx == idx).astype(F32)
    tr = lax.broadcasted_iota(jnp.int32, (tm, tm), 0)
    tc = lax.broadcasted_iota(jnp.int32, (tm, tm), 1)
    rank = _dot(sel.astype(BF16), (tr < tc).astype(BF16))
    cnt = jnp.sum(sel, axis=1, keepdims=True)
    run = jnp.floor((cnt + (ROW_ALIGN - 1)) * (1.0 / ROW_ALIGN)) * ROW_ALIGN
    er = lax.broadcasted_iota(jnp.int32, (N_EXPERTS, N_EXPERTS), 0)
    ec = lax.broadcasted_iota(jnp.int32, (N_EXPERTS, N_EXPERTS), 1)
    run_start = _dot((ec < er).astype(BF16),
                     jnp.broadcast_to(run, (N_EXPERTS, LANES)).astype(BF16))[:, 0:1]
    slot = rank + run_start
    for k in range(TOP_K):
        slot_ref[0, k:k + 1, :] = jnp.sum(jnp.where(eidx == idxs[k], slot, 0.0), axis=0,
                                          keepdims=True).astype(jnp.int32)
        prob_ref[0, k:k + 1, :] = exps[k] / tot
    cnt_ref[0] = cnt.astype(jnp.int32)


def _post(attn_outs, pg, sga, x, ada, w_attn_out, w_o, ln1_g, ln1_b, w_router, b_router):
    B, S, D = x.shape
    tm = POST_TILE
    nt = S // tm
    N = B * S
    const2 = lambda b, i: (0, 0)
    in_specs, args = [], []
    for (o, lse), (_, dil) in zip(attn_outs, ATTN_GROUPS):
        spec = pl.BlockSpec((1, dil, tm // dil, GROUP_WIDTH), lambda b, i: (b, 0, i, 0))
        in_specs += [spec, spec]
        args += [o, lse]
    tok_spec = pl.BlockSpec((1, tm, D), lambda b, i: (b, i, 0))
    in_specs += [tok_spec, tok_spec, tok_spec,
                 pl.BlockSpec((1, 6, D), lambda b, i: (b, 0, 0)),
                 pl.BlockSpec((GROUP_WIDTH, D), const2),
                 pl.BlockSpec((D, D), const2),
                 pl.BlockSpec((1, D), const2),
                 pl.BlockSpec((1, D), const2),
                 pl.BlockSpec((2, N_EXPERTS, D), lambda b, i: (0, 0, 0)),
                 pl.BlockSpec((N_EXPERTS, 1), const2)]
    wr_hi = w_router.T.astype(BF16)
    wr_lo = (w_router.T - wr_hi.astype(F32)).astype(BF16)
    args += [pg, sga, x, ada, w_attn_out.astype(BF16), w_o.astype(BF16),
             ln1_g.reshape(1, D), ln1_b.reshape(1, D), jnp.stack([wr_hi, wr_lo]),
             b_router.reshape(N_EXPERTS, 1)]
    nc = N // tm
    route_spec = pl.BlockSpec((1, TOP_K, tm), lambda b, i: (b * nt + i, 0, 0))
    out_specs = [tok_spec, tok_spec, route_spec, route_spec,
                 pl.BlockSpec((1, N_EXPERTS, 1), lambda b, i: (b * nt + i, 0, 0))]
    out_shapes = [jax.ShapeDtypeStruct((B, S, D), F32), jax.ShapeDtypeStruct((B, S, D), BF16),
                  jax.ShapeDtypeStruct((nc, TOP_K, tm), jnp.int32), jax.ShapeDtypeStruct((nc, TOP_K, tm), F32),
                  jax.ShapeDtypeStruct((nc, N_EXPERTS, 1), jnp.int32)]
    return pl.pallas_call(
        _post_kernel,
        grid=(B, nt),
        in_specs=in_specs,
        out_specs=out_specs,
        out_shape=out_shapes,
        scratch_shapes=[pltpu.VMEM((GROUP_WIDTH // LANES, tm, LANES), F32)] * 4,
        compiler_params=pltpu.CompilerParams(
            dimension_semantics=("parallel", "parallel"), vmem_limit_bytes=VMEM_LIMIT_BYTES),
        name="post",
    )(*args)


def _for_each_piece(tile, piece_refs, fn):
    big_loc, big_dst, small_loc, small_dst, count = piece_refs
    n_big, n_small = count[2 * tile], count[2 * tile + 1]

    def piece(loc_ref, dst_ref, cap, rows):
        def body(p, carry):
            i = tile * cap + p
            fn(pl.multiple_of(loc_ref[i], ROW_ALIGN), pl.multiple_of(dst_ref[i], ROW_ALIGN), rows)
            return carry
        return body

    lax.fori_loop(0, n_big, piece(big_loc, big_dst, MAX_BIG_PIECES, PIECE_ROWS), 0)
    lax.fori_loop(0, n_small, piece(small_loc, small_dst, MAX_SMALL_PIECES, ROW_ALIGN), 0)
    return n_big * PIECE_ROWS + n_small * ROW_ALIGN


def _wait_rows(src_ref, dst_ref, rows, sem):
    @pl.when(rows > 0)
    def _():
        n = pl.multiple_of(rows, ROW_ALIGN)
        pltpu.make_async_copy(src_ref.at[pl.ds(0, n)], dst_ref.at[pl.ds(0, n)], sem).wait()


def _dispatch_kernel(big_loc, big_dst, small_loc, small_dst, piece_count, tail_ref, ntail_ref, nused_ref,
                     slot_ref, prob_ref, u2_ref, xs_hbm, local2_ref, zero_ref, sem2, zsem, rows_ref):
    tile = pl.program_id(0)
    buf = tile % 2
    local_ref = local2_ref.at[buf]
    sem = sem2.at[buf]
    tt, d = u2_ref.shape
    u2 = u2_ref[...]
    slots = [slot_ref[0, k:k + 1, :] for k in range(TOP_K)]
    probs = [prob_ref[0, k:k + 1, :] for k in range(TOP_K)]
    lane = lax.broadcasted_iota(jnp.int32, (SORT_CHUNK, LANES), 1)
    for r0 in range(0, LOCAL_ROWS, SORT_CHUNK):
        row = r0 + lax.broadcasted_iota(jnp.int32, (SORT_CHUNK, tt), 0)
        w = jnp.zeros((SORT_CHUNK, tt), F32)
        for k in range(TOP_K):
            w = jnp.where(row == slots[k], probs[k], w)
        onehot = jnp.where(w != 0.0, 1.0, 0.0).astype(BF16)
        local_ref[r0:r0 + SORT_CHUNK, 0:d] = _dot(onehot, u2).astype(BF16)
        wr = jnp.sum(w, axis=1, keepdims=True)
        hi = wr.astype(BF16).astype(F32)
        mid = (wr - hi).astype(BF16).astype(F32)
        lo = wr - hi - mid
        parts = jnp.where(lane == 0, hi, jnp.where(lane == 1, mid, jnp.where(lane == 2, lo, 0.0)))
        local_ref[r0:r0 + SORT_CHUNK, d:d + LANES] = parts.astype(BF16)

    def start_piece(loc, dst, rows):
        pltpu.make_async_copy(local_ref.at[pl.ds(loc, rows)], xs_hbm.at[pl.ds(dst, rows)], sem).start()

    rows_ref[buf] = _for_each_piece(tile, (big_loc, big_dst, small_loc, small_dst, piece_count), start_piece)

    @pl.when(tile > 0)
    def _():
        _wait_rows(local2_ref.at[1 - buf], xs_hbm, rows_ref[1 - buf], sem2.at[1 - buf])

    def zero_fill(first_row, n_rows, act):
        def copy(row, rows):
            return pltpu.make_async_copy(zero_ref.at[pl.ds(0, rows)],
                                         xs_hbm.at[pl.ds(pl.multiple_of(row, ROW_ALIGN), rows)], zsem)
        n_big = n_rows // ZERO_ROWS
        rest = first_row + n_big * ZERO_ROWS
        lax.fori_loop(0, n_big, lambda p, c: (act(copy(first_row + p * ZERO_ROWS, ZERO_ROWS)), c)[1], 0)
        lax.fori_loop(0, (n_rows - n_big * ZERO_ROWS) // ROW_ALIGN,
                      lambda p, c: (act(copy(rest + p * ROW_ALIGN, ROW_ALIGN)), c)[1], 0)

    def zero_fill_all(act):
        first_spare = nused_ref[0] * MOE_BLOCK
        lax.fori_loop(0, N_EXPERTS, lambda e, c: (zero_fill(tail_ref[e], ntail_ref[e], act), c)[1], 0)
        zero_fill(first_spare, xs_hbm.shape[0] - first_spare, act)

    @pl.when(tile == 0)
    def _():
        zero_ref[...] = jnp.zeros_like(zero_ref)
        zero_fill_all(lambda c: c.start())

    @pl.when(tile == pl.num_programs(0) - 1)
    def _():
        _wait_rows(local_ref, xs_hbm, rows_ref[buf], sem)
        zero_fill_all(lambda c: c.wait())


def _dispatch(tables, slot, prob, u2, n_rows):
    N, D = u2.shape
    tt = POST_TILE
    route_spec = pl.BlockSpec((1, TOP_K, tt), lambda i, *t: (i, 0, 0))
    grid_spec = pltpu.PrefetchScalarGridSpec(
        num_scalar_prefetch=len(tables),
        grid=(N // tt,),
        in_specs=[route_spec, route_spec, pl.BlockSpec((tt, D), lambda i, *t: (i, 0))],
        out_specs=pl.BlockSpec(memory_space=pl.ANY),
        scratch_shapes=[pltpu.VMEM((2, LOCAL_ROWS, D + LANES), BF16), pltpu.VMEM((ZERO_ROWS, D + LANES), BF16),
                        pltpu.SemaphoreType.DMA((2,)), pltpu.SemaphoreType.DMA(()),
                        pltpu.SMEM((2,), jnp.int32)],
    )
    return pl.pallas_call(
        _dispatch_kernel,
        grid_spec=grid_spec,
        out_shape=jax.ShapeDtypeStruct((n_rows, D + LANES), BF16),
        compiler_params=pltpu.CompilerParams(
            dimension_semantics=("arbitrary",), vmem_limit_bytes=VMEM_LIMIT_BYTES),
        name="dispatch",
    )(*tables, slot, prob, u2)


def _expert_kernel(be_ref, next_ref, live_ref, nused_ref, xs_ref, wg_hbm, bg_ref, wu_hbm, bu_ref, wd_hbm, bd_ref,
                   ys_ref, stage, wg_s, wu_s, wd_s, sem):
    i = pl.program_id(0)
    used = i < nused_ref[0]
    prev = be_ref[jnp.maximum(i - 1, 0)]
    fresh = (i == 0) | (be_ref[i] != prev)

    def fetch(e):
        return [pltpu.make_async_copy(w_hbm.at[e], stage.at[j], sem.at[j])
                for j, w_hbm in enumerate((wg_hbm, wu_hbm, wd_hbm))]

    @pl.when(i == 0)
    def _():
        for copy in fetch(be_ref[0]):
            copy.start()

    d = ys_ref.shape[1]
    full = MOE_BLOCK // EXPERT_PART

    def arrive():
        for copy in fetch(be_ref[i]):
            copy.wait()

    def prefetch():
        @pl.when(next_ref[i] >= 0)
        def _():
            for copy in fetch(next_ref[i]):
                copy.start()

    def expert(rows, round_weights=False):
        xb = xs_ref[0:rows, 0:d]
        parts = xs_ref[0:rows, d:d + LANES].astype(F32)
        weight = parts[:, 0:1] + parts[:, 1:2] + parts[:, 2:3]
        if round_weights:
            wg_s[...] = stage[0].astype(BF16)
            wu_s[...] = stage[1].astype(BF16)
        g = _dot(xb, wg_s[...]) + bg_ref[0]
        up = _dot(xb, wu_s[...]) + bu_ref[0]
        if round_weights:
            wd_s[...] = stage[2].astype(BF16)
        g = jnp.minimum(g, SWIGLU_LIMIT)
        up = jnp.clip(up, -SWIGLU_LIMIT, SWIGLU_LIMIT)
        h = g * jax.nn.sigmoid(SWIGLU_ALPHA * g) * (up + 1.0)
        return ((_dot(h.astype(BF16), wd_s[...]) + bd_ref[0]) * weight).astype(BF16)

    @pl.when(used & fresh & (live_ref[i] == full))
    def _():
        arrive()
        ys_ref[...] = expert(MOE_BLOCK, round_weights=True)
        prefetch()

    @pl.when(used & fresh & (live_ref[i] != full))
    def _():
        arrive()
        wg_s[...] = stage[0].astype(BF16)
        wu_s[...] = stage[1].astype(BF16)
        wd_s[...] = stage[2].astype(BF16)
        prefetch()

    for parts_live in range(1, full + 1):
        rows = parts_live * EXPERT_PART
        done_above = fresh if parts_live == full else False

        @pl.when(used & (live_ref[i] == parts_live) & jnp.logical_not(done_above))
        def _():
            if rows == MOE_BLOCK:
                ys_ref[...] = expert(rows)
            else:
                ys_ref[0:rows, :] = expert(rows)
                ys_ref[rows:, :] = jnp.zeros((MOE_BLOCK - rows, d), BF16)

    @pl.when(jnp.logical_not(used))
    def _():
        ys_ref[...] = jnp.zeros_like(ys_ref)


def _experts(be, next_expert, live_parts, nused, xs, w_gate, b_gate, w_up, b_up, w_down, b_down):
    P, width = xs.shape
    D = width - LANES
    E = w_gate.shape[0]

    def live(i, nu):
        return jnp.maximum(jnp.minimum(i, nu[0] - 1), 0)

    w_spec = pl.BlockSpec(memory_space=pl.ANY)
    b_spec = pl.BlockSpec((1, 1, D), lambda i, be, nx, hf, nu: (be[live(i, nu)], 0, 0))
    grid_spec = pltpu.PrefetchScalarGridSpec(
        num_scalar_prefetch=4,
        grid=(P // MOE_BLOCK,),
        in_specs=[pl.BlockSpec((MOE_BLOCK, width), lambda i, be, nx, hf, nu: (live(i, nu), 0)),
                  w_spec, b_spec, w_spec, b_spec, w_spec, b_spec],
        out_specs=pl.BlockSpec((MOE_BLOCK, D), lambda i, be, nx, hf, nu: (i, 0)),
        scratch_shapes=[pltpu.VMEM((3, D, D), F32)] + [pltpu.VMEM((D, D), BF16)] * 3
                       + [pltpu.SemaphoreType.DMA((3,))],
    )
    return pl.pallas_call(
        _expert_kernel,
        grid_spec=grid_spec,
        out_shape=jax.ShapeDtypeStruct((P, D), BF16),
        compiler_params=pltpu.CompilerParams(
            dimension_semantics=("arbitrary",), vmem_limit_bytes=VMEM_LIMIT_BYTES),
        name="experts",
    )(be, next_expert, live_parts, nused, xs, w_gate, b_gate.reshape(E, 1, D), w_up, b_up.reshape(E, 1, D),
      w_down, b_down.reshape(E, 1, D))


def _combine_kernel(big_loc, big_dst, small_loc, small_dst, piece_count,
                    slot_ref, x1_ref, ada_ref, g2_ref, b2_ref, ys_hbm,
                    out_ref, local2_ref, sem2, rows_ref):
    tile = pl.program_id(0)
    buf = tile % 2
    tt = x1_ref.shape[0]

    def fetch(t, b):
        def start_piece(loc, dst, rows):
            pltpu.make_async_copy(ys_hbm.at[pl.ds(dst, rows)], local2_ref.at[b, pl.ds(loc, rows)],
                                  sem2.at[b]).start()
        rows_ref[b] = _for_each_piece(t, (big_loc, big_dst, small_loc, small_dst, piece_count), start_piece)

    @pl.when(tile == 0)
    def _():
        local2_ref[...] = jnp.zeros_like(local2_ref)
        fetch(0, 0)

    @pl.when(tile + 1 < pl.num_programs(0))
    def _():
        fetch(tile + 1, 1 - buf)

    local_ref = local2_ref.at[buf]
    _wait_rows(ys_hbm, local_ref, rows_ref[buf], sem2.at[buf])
    slots = [slot_ref[:, k:k + 1] for k in range(TOP_K)]

    ffn = jnp.zeros((tt, x1_ref.shape[1]), F32)
    for r0 in range(0, LOCAL_ROWS, SORT_CHUNK):
        col = r0 + lax.broadcasted_iota(jnp.int32, (tt, SORT_CHUNK), 1)
        onehot = jnp.zeros((tt, SORT_CHUNK), F32)
        for k in range(TOP_K):
            onehot = jnp.where(col == slots[k], 1.0, onehot)
        ffn = ffn + _dot(onehot.astype(BF16), local_ref[r0:r0 + SORT_CHUNK, :])
    gate2 = ada_ref[0, 5:6, :]
    y = DN_ALPHA * x1_ref[...] + (1.0 + gate2) * ffn
    out_ref[...] = _layer_norm(y) * g2_ref[...] + b2_ref[...]


def _combine(tables, slot_tok, x1, ada, ln2_g, ln2_b, ys, tiles_per_batch):
    N, D = x1.shape
    tt = POST_TILE
    const = lambda i, *t: (0, 0)
    tok4 = pl.BlockSpec((tt, TOP_K), lambda i, *t: (i, 0))
    grid_spec = pltpu.PrefetchScalarGridSpec(
        num_scalar_prefetch=len(tables),
        grid=(N // tt,),
        in_specs=[tok4,
                  pl.BlockSpec((tt, D), lambda i, *t: (i, 0)),
                  pl.BlockSpec((1, 6, D), lambda i, *t: (i // tiles_per_batch, 0, 0)),
                  pl.BlockSpec((1, D), const),
                  pl.BlockSpec((1, D), const),
                  pl.BlockSpec(memory_space=pl.ANY)],
        out_specs=pl.BlockSpec((tt, D), lambda i, *t: (i, 0)),
        scratch_shapes=[pltpu.VMEM((2, LOCAL_ROWS, D), BF16), pltpu.SemaphoreType.DMA((2,)),
                        pltpu.SMEM((2,), jnp.int32)],
    )
    return pl.pallas_call(
        _combine_kernel,
        grid_spec=grid_spec,
        out_shape=jax.ShapeDtypeStruct((N, D), F32),
        compiler_params=pltpu.CompilerParams(
            dimension_semantics=("arbitrary",), vmem_limit_bytes=VMEM_LIMIT_BYTES),
        name="combine",
    )(*tables, slot_tok, x1, ada, ln2_g.reshape(1, D), ln2_b.reshape(1, D), ys)


def _routing_tables(cnt, nblk):
    nt, E = cnt.shape
    i32 = jnp.int32
    run = (cnt + ROW_ALIGN - 1) // ROW_ALIGN * ROW_ALIGN
    run_start = jnp.cumsum(run, axis=1) - run
    seg_len = jnp.sum(run, axis=0)
    seg_blocks = (seg_len + MOE_BLOCK - 1) // MOE_BLOCK
    b_end = jnp.cumsum(seg_blocks)
    seg_off = (b_end - seg_blocks) * MOE_BLOCK
    dest = seg_off[None, :] + jnp.cumsum(run, axis=0) - run
    nused = b_end[-1:].astype(i32)
    i = jnp.arange(nblk, dtype=i32)
    be = jnp.minimum(jnp.sum((i[:, None] >= b_end[None, :]).astype(i32), axis=1), E - 1)
    e_ids = jnp.arange(E, dtype=i32)
    later = (e_ids[None, :] > e_ids[:, None]) & (seg_blocks > 0)[None, :]
    next_e = jnp.min(jnp.where(later, e_ids[None, :], E), axis=1)
    next_e = jnp.where(next_e < E, next_e, -1)
    mine = be[:, None] == e_ids[None, :]
    pick = lambda v: jnp.sum(jnp.where(mine, v[None, :], 0), axis=1)
    nxt = pick(next_e)
    rows_left = pick(seg_len) - (i - pick(b_end - seg_blocks)) * MOE_BLOCK
    live_parts = jnp.clip((rows_left + EXPERT_PART - 1) // EXPERT_PART, 1, MOE_BLOCK // EXPERT_PART).astype(i32)
    tail = seg_off + seg_len
    ntail = seg_blocks * MOE_BLOCK - seg_len
    def flat_pieces(n, first_loc, first_dst, rows, cap):
        end = jnp.cumsum(n, axis=1)
        start = end - n
        p = jnp.arange(cap, dtype=i32)[None, :, None]
        off = p - start[:, None, :]
        owner = (off >= 0) & (p < end[:, None, :])
        loc = jnp.sum(jnp.where(owner, first_loc[:, None, :] + rows * off, 0), axis=2)
        dst = jnp.sum(jnp.where(owner, first_dst[:, None, :] + rows * off, 0), axis=2)
        return loc.reshape(-1).astype(i32), dst.reshape(-1).astype(i32), end[:, -1]

    n_big = run // PIECE_ROWS
    big_rows = n_big * PIECE_ROWS
    big_loc, big_dst, big_count = flat_pieces(n_big, run_start, dest, PIECE_ROWS, MAX_BIG_PIECES)
    small_loc, small_dst, small_count = flat_pieces((run - big_rows) // ROW_ALIGN, run_start + big_rows,
                                                    dest + big_rows, ROW_ALIGN, MAX_SMALL_PIECES)
    piece_count = jnp.stack([big_count, small_count], axis=1).reshape(-1).astype(i32)
    piece_tables = (big_loc, big_dst, small_loc, small_dst, piece_count)
    return piece_tables, (tail.astype(i32), ntail.astype(i32)), (be.astype(i32), nxt.astype(i32), live_parts, nused)


def kernel(x, c, positions, w_ada, b_ada, w_in, pool_w, pool_scale, w_pool_out, w_attn_out, w_o,
           ln1_g, ln1_b, w_router, b_router, w_gate, b_gate, w_up, b_up, w_down, b_down, ln2_g, ln2_b):
    B, S, D = x.shape
    N = B * S
    assert D == D_MODEL and S % PROJ_TILE == 0 and S % (16 * ATTN_BLOCK) == 0
    assert S % POST_TILE == 0 and LOCAL_ROWS % SORT_CHUNK == 0
    nt = N // POST_TILE
    nblk = (N * TOP_K + nt * N_EXPERTS * (ROW_ALIGN - 1)) // MOE_BLOCK + N_EXPERTS
    for l in range(DEPTH):
        ada = _ada(c, w_ada[l], b_ada[l])
        proj_out = _proj(x, positions, ada, w_in[l], pool_w[l], pool_scale[l], w_pool_out[l])
        qkv, (pg, sga) = proj_out[:9], proj_out[9:]
        attn_outs = [_attention(*qkv[3 * g:3 * g + 3]) for g in range(len(ATTN_GROUPS))]
        x1, u2, slot, prob, cnt = _post(attn_outs, pg, sga, x, ada, w_attn_out[l], w_o[l],
                                        ln1_g[l], ln1_b[l], w_router[l], b_router[l])
        piece_tables, tail_tables, block_tables = _routing_tables(cnt.reshape(nt, N_EXPERTS), nblk)
        xs = _dispatch(piece_tables + tail_tables + block_tables[3:], slot, prob, u2.reshape(N, D),
                       nblk * MOE_BLOCK)
        ys = _experts(*block_tables, xs, w_gate[l], b_gate[l], w_up[l], b_up[l], w_down[l], b_down[l])
        slot_tok = slot.transpose(0, 2, 1).reshape(N, TOP_K)
        out = _combine(piece_tables, slot_tok, x1.reshape(N, D), ada,
                       ln2_g[l], ln2_b[l], ys, S // POST_TILE)
        x = out.reshape(B, S, D)
    return x
```

```python
import jax
import jax.numpy as jnp
import numpy as np
from jax import lax
from jax.experimental import pallas as pl
from jax.experimental.pallas import tpu as pltpu

F32 = jnp.float32
BF16 = jnp.bfloat16

D_MODEL = 1024
POOL_WINDOWS = (2, 4, 8, 16)
POOL_WIDTH = D_MODEL // 2
POOL_GROUP = POOL_WIDTH // len(POOL_WINDOWS)
POOL_HALO = 16
HEAD_DIM = 64
ATTN_GROUPS = ((128, 1), (512, 4), (2048, 16))
HEADS_PER_GROUP = 4
GROUP_WIDTH = HEADS_PER_GROUP * HEAD_DIM
N_HEADS = HEADS_PER_GROUP * len(ATTN_GROUPS)
ATTN_WIDTH = N_HEADS * HEAD_DIM
ATTN_BLOCK = 128
ROT_DIM = HEAD_DIM // 4
ROPE_THETA = 500000.0
N_EXPERTS = 32
TOP_K = 4
SWIGLU_ALPHA = 1.702
SWIGLU_LIMIT = 7.0
MOE_BLOCK = 1024
EXPERT_PART = 128
DEPTH = 1
DN_ALPHA = (2.0 * DEPTH) ** 0.25
LN_EPS = 1e-5
NEG_INF = -1e30

OFF_Q = POOL_WIDTH
OFF_K = OFF_Q + ATTN_WIDTH
OFF_V = OFF_K + ATTN_WIDTH
OFF_GP = OFF_V + ATTN_WIDTH
OFF_GA = OFF_GP + D_MODEL
IN_WIDTH = OFF_GA + D_MODEL

VMEM_LIMIT_BYTES = 56 * 1024 * 1024
LANES = 128
SUBLANES = 8

PROJ_TILE = 512
POST_TILE = 512
POST_GROUPS = 4
ATTN_QROWS = 2048
ROW_ALIGN = 16
PIECE_ROWS = 64
ZERO_ROWS = 128
SORT_CHUNK = 512
LOCAL_ROWS = -(-(POST_TILE * TOP_K + N_EXPERTS * (ROW_ALIGN - 1)) // SORT_CHUNK) * SORT_CHUNK
MAX_BIG_PIECES = LOCAL_ROWS // PIECE_ROWS
MAX_SMALL_PIECES = N_EXPERTS * (PIECE_ROWS // ROW_ALIGN - 1)


def _layer_norm(x):
    mu = jnp.mean(x, axis=-1, keepdims=True)
    xc = x - mu
    var = jnp.mean(xc * xc, axis=-1, keepdims=True)
    return xc * lax.rsqrt(var + LN_EPS)


def _dot(a, b):
    return jnp.dot(a, b, preferred_element_type=F32)


def _ada_kernel(c_ref, w_ref, b_ref, o_ref):
    c = c_ref[...]
    s = c * jax.nn.sigmoid(c)
    o_ref[...] = jnp.dot(s, w_ref[...], preferred_element_type=F32,
                         precision=lax.Precision.HIGHEST) + b_ref[...]


def _ada(c, w_ada, b_ada):
    B, D = c.shape
    rows = SUBLANES
    c_pad = jnp.pad(c, ((0, rows - B), (0, 0)))
    n_out = w_ada.shape[1]
    out = pl.pallas_call(
        _ada_kernel,
        grid=(n_out // D,),
        in_specs=[pl.BlockSpec((rows, D), lambda j: (0, 0)),
                  pl.BlockSpec((D, D), lambda j: (0, j)),
                  pl.BlockSpec((1, D), lambda j: (0, j))],
        out_specs=pl.BlockSpec((rows, D), lambda j: (0, j)),
        out_shape=jax.ShapeDtypeStruct((rows, n_out), F32),
        name="ada",
    )(c_pad, w_ada, b_ada.reshape(1, n_out))
    return out[:B].reshape(B, 6, D)


ROPE_PART_ROWS = 32


def _rope_tables():
    lane = np.arange(LANES)
    li = lane % HEAD_DIM
    half = ROT_DIM // 2
    inv_freq = jnp.power(ROPE_THETA, -jnp.arange(half, dtype=F32) * (2.0 / ROT_DIM))
    invf = jnp.broadcast_to(inv_freq[:, None], (half, LANES))
    freq = np.arange(ROPE_PART_ROWS)[:, None]
    live = (freq < 3 * half) & (freq % half == (li % half)[None, :])
    place = np.stack([live & (li < ROT_DIM)[None, :],
                      -1.0 * (live & (li < half)[None, :]),
                      live & ((li >= half) & (li < ROT_DIM))[None, :]]).astype(np.float32)
    keep = (li >= ROT_DIM).astype(np.float32)[None, :]
    return invf, jnp.asarray(place, BF16), jnp.asarray(keep)


def _proj_kernel(x_ref, xh_ref, pos_ref, ada_ref, invf_ref, place_ref, keep_ref,
                 win_ref, poolw_ref, pscale_ref, wpo_ref,
                 q1_ref, k1_ref, v1_ref, q4_ref, k4_ref, v4_ref, q16_ref, k16_ref, v16_ref,
                 pg_ref, sga_ref, xpe_ref, cls_ref):
    tm = x_ref.shape[1]
    i = pl.program_id(1)
    shift1 = ada_ref[0, 0:1, :]
    scale1 = ada_ref[0, 1:2, :]

    def modulated(xv):
        return (_layer_norm(xv) * (1.0 + scale1) + shift1).astype(BF16)

    u = modulated(x_ref[0])
    uh = modulated(xh_ref[0])

    xp = _dot(u, win_ref[:, 0:POOL_WIDTH])
    xph = _dot(uh, win_ref[:, 0:POOL_WIDTH])
    xph = jnp.where(i > 0, xph, 0.0)
    xpe_ref[0:POOL_HALO, :] = xph
    xpe_ref[POOL_HALO:, :] = xp
    tok = i * tm + lax.broadcasted_iota(jnp.int32, (tm, 1), 0)

    def pooled_branch():
        ys = []
        for g, w in enumerate(POOL_WINDOWS):
            cols = slice(g * POOL_GROUP, (g + 1) * POOL_GROUP)
            xg = xpe_ref[POOL_HALO:, cols]
            acc = xg
            for j in range(1, w):
                acc = acc + xpe_ref[POOL_HALO - j:POOL_HALO - j + tm, cols]
            cnt = jnp.minimum(tok + 1, w).astype(F32)
            mixed = (acc / cnt - xg).astype(BF16)
            ys.append(_dot(mixed, poolw_ref[g]) * pscale_ref[:, cols])
        y = jnp.concatenate(ys, axis=1).astype(BF16)
        return _dot(y, wpo_ref[...])

    ang = invf_ref[:, 0:1] * pos_ref[0, 0].astype(F32)

    def on_lanes(table, j):
        hi = table.astype(BF16).astype(F32)
        mid = (table - hi).astype(BF16).astype(F32)
        lo = table - hi - mid
        pad = jnp.zeros((ROPE_PART_ROWS - 3 * table.shape[0], tm), F32)
        parts = jnp.concatenate([hi, mid, lo, pad], axis=0).astype(BF16)
        return lax.dot_general(parts, place_ref[j], (((0,), (0,)), ((), ())), preferred_element_type=F32)

    cos = jnp.cos(ang)
    sin = jnp.sin(ang)
    c_mul = on_lanes(cos, 0) + keep_ref[...]
    s_lo = on_lanes(sin, 1)
    s_hi = on_lanes(sin, 2)
    c_mul = jnp.concatenate([c_mul, c_mul], axis=1)
    s_lo = jnp.concatenate([s_lo, s_lo], axis=1)
    s_hi = jnp.concatenate([s_hi, s_hi], axis=1)
    half = ROT_DIM // 2

    def rotate(a):
        up = pltpu.roll(a, GROUP_WIDTH - half, axis=1)
        dn = pltpu.roll(a, half, axis=1)
        return a * c_mul + up * s_lo + dn * s_hi

    def emit(a, out_ref, dil):
        if dil == 1:
            out_ref[0, 0] = a.astype(BF16)
            return
        for c in range(GROUP_WIDTH // LANES):
            cls_ref[c] = a[:, c * LANES:(c + 1) * LANES]
        for r in range(dil):
            for c in range(GROUP_WIDTH // LANES):
                out_ref[0, r, :, c * LANES:(c + 1) * LANES] = (
                    cls_ref[c, pl.ds(r, tm // dil, stride=dil), :].astype(BF16))

    outs = ((q1_ref, k1_ref, v1_ref), (q4_ref, k4_ref, v4_ref), (q16_ref, k16_ref, v16_ref))

    def attention_group(gi):
        qo, ko, vo = outs[gi]
        dil = ATTN_GROUPS[gi][1]
        c0 = gi * GROUP_WIDTH
        emit(rotate(_dot(u, win_ref[:, OFF_Q + c0:OFF_Q + c0 + GROUP_WIDTH])), qo, dil)
        emit(rotate(_dot(u, win_ref[:, OFF_K + c0:OFF_K + c0 + GROUP_WIDTH])), ko, dil)
        emit(_dot(u, win_ref[:, OFF_V + c0:OFF_V + c0 + GROUP_WIDTH]), vo, dil)

    attention_group(2)
    pooled = pooled_branch()
    attention_group(1)
    g_p = _dot(u, win_ref[:, OFF_GP:OFF_GP + D_MODEL])
    pg_ref[0] = jax.nn.sigmoid(g_p) * pooled
    attention_group(0)
    g_a = _dot(u, win_ref[:, OFF_GA:OFF_GA + D_MODEL])
    sga_ref[0] = jax.nn.sigmoid(g_a)


def _proj(x, positions, ada, w_in, pool_w, pool_scale, w_pool_out):
    B, S, D = x.shape
    tm = PROJ_TILE
    nt = S // tm
    halo_blocks = tm // POOL_HALO
    const2 = lambda b, i: (0, 0)
    in_specs = [
        pl.BlockSpec((1, tm, D), lambda b, i: (b, i, 0)),
        pl.BlockSpec((1, POOL_HALO, D), lambda b, i: (b, jnp.maximum(i * halo_blocks - 1, 0), 0)),
        pl.BlockSpec((1, 1, 1, tm), lambda b, i: (b, i, 0, 0)),
        pl.BlockSpec((1, 6, D), lambda b, i: (b, 0, 0)),
        pl.BlockSpec((ROT_DIM // 2, LANES), const2),
        pl.BlockSpec((3, ROPE_PART_ROWS, LANES), lambda b, i: (0, 0, 0)),
        pl.BlockSpec((1, LANES), const2),
        pl.BlockSpec((D, IN_WIDTH), const2),
        pl.BlockSpec((len(POOL_WINDOWS), POOL_GROUP, POOL_GROUP), lambda b, i: (0, 0, 0)),
        pl.BlockSpec((1, POOL_WIDTH), const2),
        pl.BlockSpec((POOL_WIDTH, D), const2),
    ]
    out_specs, out_shapes = [], []
    for _, dil in ATTN_GROUPS:
        for _ in range(3):
            out_specs.append(pl.BlockSpec((1, dil, tm // dil, GROUP_WIDTH), lambda b, i: (b, 0, i, 0)))
            out_shapes.append(jax.ShapeDtypeStruct((B, dil, S // dil, GROUP_WIDTH), BF16))
    for _ in range(2):
        out_specs.append(pl.BlockSpec((1, tm, D), lambda b, i: (b, i, 0)))
        out_shapes.append(jax.ShapeDtypeStruct((B, S, D), F32))
    return pl.pallas_call(
        _proj_kernel,
        grid=(B, nt),
        in_specs=in_specs,
        out_specs=out_specs,
        out_shape=out_shapes,
        scratch_shapes=[pltpu.VMEM((tm + POOL_HALO, POOL_WIDTH), F32),
                        pltpu.VMEM((GROUP_WIDTH // LANES, tm, LANES), F32)],
        compiler_params=pltpu.CompilerParams(
            dimension_semantics=("parallel", "parallel"), vmem_limit_bytes=VMEM_LIMIT_BYTES),
        name="proj",
    )(x, x, positions.reshape(B, nt, 1, tm), ada, *_rope_tables(), w_in.astype(BF16),
      pool_w.astype(BF16), pool_scale.reshape(1, POOL_WIDTH), w_pool_out.astype(BF16))


def _attn_kernel(q_ref, k_ref, v_ref, kh_ref, vh_ref, o_ref, lse_ref, kf_ref, vf_ref):
    n_cls, qb = q_ref.shape[1], q_ref.shape[2]
    per = qb // ATTN_BLOCK
    n = pl.program_id(2)
    for c in range(n_cls):
        kf_ref[c, 0:ATTN_BLOCK, :] = kh_ref[0, c]
        kf_ref[c, ATTN_BLOCK:, :] = k_ref[0, c]
        vf_ref[c, 0:ATTN_BLOCK, :] = vh_ref[0, c]
        vf_ref[c, ATTN_BLOCK:, :] = v_ref[0, c]
    qi = lax.broadcasted_iota(jnp.int32, (ATTN_BLOCK, 2 * ATTN_BLOCK), 0)
    kj = lax.broadcasted_iota(jnp.int32, (ATTN_BLOCK, 2 * ATTN_BLOCK), 1)
    band = (kj >= qi) & (kj <= qi + ATTN_BLOCK)
    band_bias = jnp.where(band, 0.0, NEG_INF)
    lane = lax.broadcasted_iota(jnp.int32, (ATTN_BLOCK, GROUP_WIDTH), 1)
    low_lanes = lax.broadcasted_iota(jnp.int32, (ATTN_BLOCK, LANES), 1) < HEAD_DIM
    ones = jnp.ones((2 * ATTN_BLOCK, LANES), BF16)
    nh, blk = HEADS_PER_GROUP, ATTN_BLOCK

    def block(t, carry):
        c, j = t // per, t % per
        r0 = pl.multiple_of(j * ATTN_BLOCK, ATTN_BLOCK)
        first_key = jnp.where((n > 0) | (j > 0), 0, ATTN_BLOCK)
        bias = band_bias + jnp.where(kj < first_key, NEG_INF, 0.0)
        q = q_ref[0, c, pl.ds(r0, blk), :].astype(F32)
        kk = kf_ref[c, pl.ds(r0, 2 * blk), :]
        vv = vf_ref[c, pl.ds(r0, 2 * blk), :]
        qs = jnp.concatenate([jnp.where((lane >= h * HEAD_DIM) & (lane < (h + 1) * HEAD_DIM), q, 0.0)
                              for h in range(nh)], axis=0).astype(BF16)
        s = lax.dot_general(qs, kk, (((1,), (1,)), ((), ())), preferred_element_type=F32)
        s = jnp.concatenate([s[h * blk:(h + 1) * blk] * (HEAD_DIM ** -0.5) + bias for h in range(nh)], axis=0)
        m = jnp.max(s, axis=-1, keepdims=True)
        p = jnp.exp(s - m).astype(BF16)
        den = _dot(p, ones)
        lse = m + jnp.log(den)
        for hp in range(GROUP_WIDTH // LANES):
            rows = slice(2 * hp * blk, (2 * hp + 2) * blk)
            ls = slice(hp * LANES, (hp + 1) * LANES)
            o2 = _dot(p[rows], vv[:, ls]) / den[rows]
            l2 = lse[rows]
            o_ref[0, c, pl.ds(r0, blk), ls] = jnp.where(low_lanes, o2[0:blk], o2[blk:2 * blk])
            lse_ref[0, c, pl.ds(r0, blk), ls] = jnp.where(low_lanes, l2[0:blk], l2[blk:2 * blk])
        return carry

    lax.fori_loop(0, n_cls * per, block, 0, unroll=8)


def _attention(q, k, v):
    B, dil, L, W = q.shape
    qb = min(L, ATTN_QROWS)
    per = qb // ATTN_BLOCK
    n_cls = min(ATTN_QROWS // qb, dil)
    main = pl.BlockSpec((1, n_cls, qb, W), lambda b, r, n: (b, r, n, 0))
    halo = pl.BlockSpec((1, n_cls, ATTN_BLOCK, W), lambda b, r, n: (b, r, jnp.maximum(n * per - 1, 0), 0))
    return pl.pallas_call(
        _attn_kernel,
        grid=(B, dil // n_cls, L // qb),
        in_specs=[main, main, main, halo, halo],
        out_specs=[main, main],
        out_shape=[jax.ShapeDtypeStruct((B, dil, L, W), F32)] * 2,
        scratch_shapes=[pltpu.VMEM((n_cls, qb + ATTN_BLOCK, W), BF16)] * 2,
        compiler_params=pltpu.CompilerParams(
            dimension_semantics=("parallel", "parallel", "parallel"), vmem_limit_bytes=VMEM_LIMIT_BYTES),
        name=f"attn_d{dil}",
    )(q, k, v, k, v)


def _post_kernel(o1_ref, l1_ref, o4_ref, l4_ref, o16_ref, l16_ref, pg_ref, sga_ref, x_ref, ada_ref,
                 wao_ref, wo_ref, g1_ref, b1_ref, wrt_ref, brt_ref,
                 x1_ref, u2_ref, slot_ref, prob_ref, cnt_ref,
                 s0, s1, s2, s3):
    tm = x_ref.shape[1]

    def token_major(src_ref, scr_ref, dil):
        if dil == 1:
            return src_ref[0, 0]
        for r in range(dil):
            for c in range(GROUP_WIDTH // LANES):
                scr_ref[c, pl.ds(r, tm // dil, stride=dil), :] = src_ref[0, r, :, c * LANES:(c + 1) * LANES]
        return jnp.concatenate([scr_ref[c] for c in range(GROUP_WIDTH // LANES)], axis=1)

    o1, l1 = o1_ref[0, 0], l1_ref[0, 0]
    o4, l4 = token_major(o4_ref, s0, 4), token_major(l4_ref, s1, 4)
    o16, l16 = token_major(o16_ref, s2, 16), token_major(l16_ref, s3, 16)
    gate1 = ada_ref[0, 2:3, :]
    shift2 = ada_ref[0, 3:4, :]
    scale2 = ada_ref[0, 4:5, :]
    nt_dot = lambda a, b: lax.dot_general(a, b, (((1,), (1,)), ((), ())), preferred_element_type=F32)
    groups = [slice(h * (tm // POST_GROUPS), (h + 1) * (tm // POST_GROUPS)) for h in range(POST_GROUPS)]

    def merge(rows):
        a1, a4, a16 = l1[rows], l4[rows], l16[rows]
        mx = jnp.maximum(jnp.maximum(a1, a4), a16)
        e1, e4, e16 = jnp.exp(a1 - mx), jnp.exp(a4 - mx), jnp.exp(a16 - mx)
        return ((e1 * o1[rows] + e4 * o4[rows] + e16 * o16[rows]) / (e1 + e4 + e16)).astype(BF16)

    def project(rows, attn):
        merged = pg_ref[0, rows, :] + sga_ref[0, rows, :] * _dot(attn, wao_ref[...])
        return _dot(merged.astype(BF16), wo_ref[...])

    def norms(rows, mix):
        x1 = _layer_norm(DN_ALPHA * x_ref[0, rows, :] + (1.0 + gate1) * mix) * g1_ref[...] + b1_ref[...]
        x1_ref[0, rows, :] = x1
        u2 = _layer_norm(x1) * (1.0 + scale2) + shift2
        u2_hi = u2.astype(BF16)
        u2_ref[0, rows, :] = u2_hi
        return u2_hi, (u2 - u2_hi.astype(F32)).astype(BF16)

    def route(u2_hi, u2_lo):
        return nt_dot(wrt_ref[0], u2_hi) + nt_dot(wrt_ref[0], u2_lo) + nt_dot(wrt_ref[1], u2_hi)

    attns = [merge(rows) for rows in groups]
    mixes = [project(rows, attn) for rows, attn in zip(groups, attns)]
    u2s = [norms(rows, mix) for rows, mix in zip(groups, mixes)]
    logits = jnp.concatenate([route(*u2) for u2 in u2s], axis=1) + brt_ref[...]
    eidx = lax.broadcasted_iota(jnp.int32, (N_EXPERTS, tm), 0)
    work = logits
    vals, idxs = [], []
    for _ in range(TOP_K):
        m = jnp.max(work, axis=0, keepdims=True)
        idx = jnp.min(jnp.where(work == m, eidx, N_EXPERTS), axis=0, keepdims=True)
        vals.append(m)
        idxs.append(idx)
        work = jnp.where(eidx == idx, -jnp.inf, work)
    exps = [jnp.exp(vk - vals[0]) for vk in vals]
    tot = exps[0] + exps[1] + exps[2] + exps[3]
    sel = jnp.zeros((N_EXPERTS, tm), F32)
    for idx in idxs:
        sel = sel + (eidx == idx).astype(F32)
    tr = lax.broadcasted_iota(jnp.int32, (tm, tm), 0)
    tc = lax.broadcasted_iota(jnp.int32, (tm, tm), 1)
    rank = _dot(sel.astype(BF16), (tr < tc).astype(BF16))
    cnt = jnp.sum(sel, axis=1, keepdims=True)
    run = jnp.floor((cnt + (ROW_ALIGN - 1)) * (1.0 / ROW_ALIGN)) * ROW_ALIGN
    er = lax.broadcasted_iota(jnp.int32, (N_EXPERTS, N_EXPERTS), 0)
    ec = lax.broadcasted_iota(jnp.int32, (N_EXPERTS, N_EXPERTS), 1)
    run_start = _dot((ec < er).astype(BF16),
                     jnp.broadcast_to(run, (N_EXPERTS, LANES)).astype(BF16))[:, 0:1]
    slot = rank + run_start
    for k in range(TOP_K):
        slot_ref[0, k:k + 1, :] = jnp.sum(jnp.where(eidx == idxs[k], slot, 0.0), axis=0,
                                          keepdims=True).astype(jnp.int32)
        prob_ref[0, k:k + 1, :] = exps[k] / tot
    cnt_ref[0] = cnt.astype(jnp.int32)


def _post(attn_outs, pg, sga, x, ada, w_attn_out, w_o, ln1_g, ln1_b, w_router, b_router):
    B, S, D = x.shape
    tm = POST_TILE
    nt = S // tm
    N = B * S
    const2 = lambda b, i: (0, 0)
    in_specs, args = [], []
    for (o, lse), (_, dil) in zip(attn_outs, ATTN_GROUPS):
        spec = pl.BlockSpec((1, dil, tm // dil, GROUP_WIDTH), lambda b, i: (b, 0, i, 0))
        in_specs += [spec, spec]
        args += [o, lse]
    tok_spec = pl.BlockSpec((1, tm, D), lambda b, i: (b, i, 0))
    in_specs += [tok_spec, tok_spec, tok_spec,
                 pl.BlockSpec((1, 6, D), lambda b, i: (b, 0, 0)),
                 pl.BlockSpec((GROUP_WIDTH, D), const2),
                 pl.BlockSpec((D, D), const2),
                 pl.BlockSpec((1, D), const2),
                 pl.BlockSpec((1, D), const2),
                 pl.BlockSpec((2, N_EXPERTS, D), lambda b, i: (0, 0, 0)),
                 pl.BlockSpec((N_EXPERTS, 1), const2)]
    wr_hi = w_router.T.astype(BF16)
    wr_lo = (w_router.T - wr_hi.astype(F32)).astype(BF16)
    args += [pg, sga, x, ada, w_attn_out.astype(BF16), w_o.astype(BF16),
             ln1_g.reshape(1, D), ln1_b.reshape(1, D), jnp.stack([wr_hi, wr_lo]),
             b_router.reshape(N_EXPERTS, 1)]
    nc = N // tm
    route_spec = pl.BlockSpec((1, TOP_K, tm), lambda b, i: (b * nt + i, 0, 0))
    out_specs = [tok_spec, tok_spec, route_spec, route_spec,
                 pl.BlockSpec((1, N_EXPERTS, 1), lambda b, i: (b * nt + i, 0, 0))]
    out_shapes = [jax.ShapeDtypeStruct((B, S, D), F32), jax.ShapeDtypeStruct((B, S, D), BF16),
                  jax.ShapeDtypeStruct((nc, TOP_K, tm), jnp.int32), jax.ShapeDtypeStruct((nc, TOP_K, tm), F32),
                  jax.ShapeDtypeStruct((nc, N_EXPERTS, 1), jnp.int32)]
    return pl.pallas_call(
        _post_kernel,
        grid=(B, nt),
        in_specs=in_specs,
        out_specs=out_specs,
        out_shape=out_shapes,
        scratch_shapes=[pltpu.VMEM((GROUP_WIDTH // LANES, tm, LANES), F32)] * 4,
        compiler_params=pltpu.CompilerParams(
            dimension_semantics=("parallel", "parallel"), vmem_limit_bytes=VMEM_LIMIT_BYTES),
        name="post",
    )(*args)


def _for_each_piece(tile, piece_refs, fn):
    big_loc, big_dst, small_loc, small_dst, count = piece_refs
    n_big, n_small = count[2 * tile], count[2 * tile + 1]

    def piece(loc_ref, dst_ref, cap, rows):
        def body(p, carry):
            i = tile * cap + p
            fn(pl.multiple_of(loc_ref[i], ROW_ALIGN), pl.multiple_of(dst_ref[i], ROW_ALIGN), rows)
            return carry
        return body

    lax.fori_loop(0, n_big, piece(big_loc, big_dst, MAX_BIG_PIECES, PIECE_ROWS), 0)
    lax.fori_loop(0, n_small, piece(small_loc, small_dst, MAX_SMALL_PIECES, ROW_ALIGN), 0)
    return n_big * PIECE_ROWS + n_small * ROW_ALIGN


def _wait_rows(src_ref, dst_ref, rows, sem):
    @pl.when(rows > 0)
    def _():
        n = pl.multiple_of(rows, ROW_ALIGN)
        pltpu.make_async_copy(src_ref.at[pl.ds(0, n)], dst_ref.at[pl.ds(0, n)], sem).wait()


def _dispatch_kernel(big_loc, big_dst, small_loc, small_dst, piece_count, tail_ref, ntail_ref, nused_ref,
                     slot_ref, prob_ref, u2_ref, xs_hbm, local2_ref, zero_ref, sem2, zsem, rows_ref):
    tile = pl.program_id(0)
    buf = tile % 2
    local_ref = local2_ref.at[buf]
    sem = sem2.at[buf]
    tt, d = u2_ref.shape
    u2 = u2_ref[...]
    slots = [slot_ref[0, k:k + 1, :] for k in range(TOP_K)]
    probs = [prob_ref[0, k:k + 1, :] for k in range(TOP_K)]
    lane = lax.broadcasted_iota(jnp.int32, (SORT_CHUNK, LANES), 1)
    for r0 in range(0, LOCAL_ROWS, SORT_CHUNK):
        row = r0 + lax.broadcasted_iota(jnp.int32, (SORT_CHUNK, tt), 0)
        w = jnp.zeros((SORT_CHUNK, tt), F32)
        for k in range(TOP_K):
            w = jnp.where(row == slots[k], probs[k], w)
        onehot = jnp.where(w != 0.0, 1.0, 0.0).astype(BF16)
        local_ref[r0:r0 + SORT_CHUNK, 0:d] = _dot(onehot, u2).astype(BF16)
        wr = jnp.sum(w, axis=1, keepdims=True)
        hi = wr.astype(BF16).astype(F32)
        mid = (wr - hi).astype(BF16).astype(F32)
        lo = wr - hi - mid
        parts = jnp.where(lane == 0, hi, jnp.where(lane == 1, mid, jnp.where(lane == 2, lo, 0.0)))
        local_ref[r0:r0 + SORT_CHUNK, d:d + LANES] = parts.astype(BF16)

    def start_piece(loc, dst, rows):
        pltpu.make_async_copy(local_ref.at[pl.ds(loc, rows)], xs_hbm.at[pl.ds(dst, rows)], sem).start()

    rows_ref[buf] = _for_each_piece(tile, (big_loc, big_dst, small_loc, small_dst, piece_count), start_piece)

    @pl.when(tile > 0)
    def _():
        _wait_rows(local2_ref.at[1 - buf], xs_hbm, rows_ref[1 - buf], sem2.at[1 - buf])

    def zero_fill(first_row, n_rows, act):
        def copy(row, rows):
            return pltpu.make_async_copy(zero_ref.at[pl.ds(0, rows)],
                                         xs_hbm.at[pl.ds(pl.multiple_of(row, ROW_ALIGN), rows)], zsem)
        n_big = n_rows // ZERO_ROWS
        rest = first_row + n_big * ZERO_ROWS
        lax.fori_loop(0, n_big, lambda p, c: (act(copy(first_row + p * ZERO_ROWS, ZERO_ROWS)), c)[1], 0)
        lax.fori_loop(0, (n_rows - n_big * ZERO_ROWS) // ROW_ALIGN,
                      lambda p, c: (act(copy(rest + p * ROW_ALIGN, ROW_ALIGN)), c)[1], 0)

    def zero_fill_all(act):
        first_spare = nused_ref[0] * MOE_BLOCK
        lax.fori_loop(0, N_EXPERTS, lambda e, c: (zero_fill(tail_ref[e], ntail_ref[e], act), c)[1], 0)
        zero_fill(first_spare, xs_hbm.shape[0] - first_spare, act)

    @pl.when(tile == 0)
    def _():
        zero_ref[...] = jnp.zeros_like(zero_ref)
        zero_fill_all(lambda c: c.start())

    @pl.when(tile == pl.num_programs(0) - 1)
    def _():
        _wait_rows(local_ref, xs_hbm, rows_ref[buf], sem)
        zero_fill_all(lambda c: c.wait())


def _dispatch(tables, slot, prob, u2, n_rows):
    N, D = u2.shape
    tt = POST_TILE
    route_spec = pl.BlockSpec((1, TOP_K, tt), lambda i, *t: (i, 0, 0))
    grid_spec = pltpu.PrefetchScalarGridSpec(
        num_scalar_prefetch=len(tables),
        grid=(N // tt,),
        in_specs=[route_spec, route_spec, pl.BlockSpec((tt, D), lambda i, *t: (i, 0))],
        out_specs=pl.BlockSpec(memory_space=pl.ANY),
        scratch_shapes=[pltpu.VMEM((2, LOCAL_ROWS, D + LANES), BF16), pltpu.VMEM((ZERO_ROWS, D + LANES), BF16),
                        pltpu.SemaphoreType.DMA((2,)), pltpu.SemaphoreType.DMA(()),
                        pltpu.SMEM((2,), jnp.int32)],
    )
    return pl.pallas_call(
        _dispatch_kernel,
        grid_spec=grid_spec,
        out_shape=jax.ShapeDtypeStruct((n_rows, D + LANES), BF16),
        compiler_params=pltpu.CompilerParams(
            dimension_semantics=("arbitrary",), vmem_limit_bytes=VMEM_LIMIT_BYTES),
        name="dispatch",
    )(*tables, slot, prob, u2)


def _expert_kernel(be_ref, next_ref, live_ref, nused_ref, xs_ref, wg_hbm, bg_ref, wu_hbm, bu_ref, wd_hbm, bd_ref,
                   ys_ref, stage, wg_s, wu_s, wd_s, sem):
    i = pl.program_id(0)
    used = i < nused_ref[0]
    prev = be_ref[jnp.maximum(i - 1, 0)]
    fresh = (i == 0) | (be_ref[i] != prev)

    def fetch(e):
        return [pltpu.make_async_copy(w_hbm.at[e], stage.at[j], sem.at[j])
                for j, w_hbm in enumerate((wg_hbm, wu_hbm, wd_hbm))]

    @pl.when(i == 0)
    def _():
        for copy in fetch(be_ref[0]):
            copy.start()

    d = ys_ref.shape[1]
    full = MOE_BLOCK // EXPERT_PART

    def arrive():
        for copy in fetch(be_ref[i]):
            copy.wait()

    def prefetch():
        @pl.when(next_ref[i] >= 0)
        def _():
            for copy in fetch(next_ref[i]):
                copy.start()

    def expert(rows, round_weights=False):
        xb = xs_ref[0:rows, 0:d]
        parts = xs_ref[0:rows, d:d + LANES].astype(F32)
        weight = parts[:, 0:1] + parts[:, 1:2] + parts[:, 2:3]
        if round_weights:
            wg_s[...] = stage[0].astype(BF16)
            wu_s[...] = stage[1].astype(BF16)
        g = _dot(xb, wg_s[...]) + bg_ref[0]
        up = _dot(xb, wu_s[...]) + bu_ref[0]
        if round_weights:
            wd_s[...] = stage[2].astype(BF16)
        g = jnp.minimum(g, SWIGLU_LIMIT)
        up = jnp.clip(up, -SWIGLU_LIMIT, SWIGLU_LIMIT)
        h = g * jax.nn.sigmoid(SWIGLU_ALPHA * g) * (up + 1.0)
        return ((_dot(h.astype(BF16), wd_s[...]) + bd_ref[0]) * weight).astype(BF16)

    @pl.when(used & fresh & (live_ref[i] == full))
    def _():
        arrive()
        ys_ref[...] = expert(MOE_BLOCK, round_weights=True)
        prefetch()

    @pl.when(used & fresh & (live_ref[i] != full))
    def _():
        arrive()
        wg_s[...] = stage[0].astype(BF16)
        wu_s[...] = stage[1].astype(BF16)
        wd_s[...] = stage[2].astype(BF16)
        prefetch()

    for parts_live in range(1, full + 1):
        rows = parts_live * EXPERT_PART
        done_above = fresh if parts_live == full else False

        @pl.when(used & (live_ref[i] == parts_live) & jnp.logical_not(done_above))
        def _():
            if rows == MOE_BLOCK:
                ys_ref[...] = expert(rows)
            else:
                ys_ref[0:rows, :] = expert(rows)
                ys_ref[rows:, :] = jnp.zeros((MOE_BLOCK - rows, d), BF16)

    @pl.when(jnp.logical_not(used))
    def _():
        ys_ref[...] = jnp.zeros_like(ys_ref)


def _experts(be, next_expert, live_parts, nused, xs, w_gate, b_gate, w_up, b_up, w_down, b_down):
    P, width = xs.shape
    D = width - LANES
    E = w_gate.shape[0]

    def live(i, nu):
        return jnp.maximum(jnp.minimum(i, nu[0] - 1), 0)

    w_spec = pl.BlockSpec(memory_space=pl.ANY)
    b_spec = pl.BlockSpec((1, 1, D), lambda i, be, nx, hf, nu: (be[live(i, nu)], 0, 0))
    grid_spec = pltpu.PrefetchScalarGridSpec(
        num_scalar_prefetch=4,
        grid=(P // MOE_BLOCK,),
        in_specs=[pl.BlockSpec((MOE_BLOCK, width), lambda i, be, nx, hf, nu: (live(i, nu), 0)),
                  w_spec, b_spec, w_spec, b_spec, w_spec, b_spec],
        out_specs=pl.BlockSpec((MOE_BLOCK, D), lambda i, be, nx, hf, nu: (i, 0)),
        scratch_shapes=[pltpu.VMEM((3, D, D), F32)] + [pltpu.VMEM((D, D), BF16)] * 3
                       + [pltpu.SemaphoreType.DMA((3,))],
    )
    return pl.pallas_call(
        _expert_kernel,
        grid_spec=grid_spec,
        out_shape=jax.ShapeDtypeStruct((P, D), BF16),
        compiler_params=pltpu.CompilerParams(
            dimension_semantics=("arbitrary",), vmem_limit_bytes=VMEM_LIMIT_BYTES),
        name="experts",
    )(be, next_expert, live_parts, nused, xs, w_gate, b_gate.reshape(E, 1, D), w_up, b_up.reshape(E, 1, D),
      w_down, b_down.reshape(E, 1, D))


def _combine_kernel(big_loc, big_dst, small_loc, small_dst, piece_count,
                    slot_ref, x1_ref, ada_ref, g2_ref, b2_ref, ys_hbm,
                    out_ref, local2_ref, sem2, rows_ref):
    tile = pl.program_id(0)
    buf = tile % 2
    tt = x1_ref.shape[0]

    def fetch(t, b):
        def start_piece(loc, dst, rows):
            pltpu.make_async_copy(ys_hbm.at[pl.ds(dst, rows)], local2_ref.at[b, pl.ds(loc, rows)],
                                  sem2.at[b]).start()
        rows_ref[b] = _for_each_piece(t, (big_loc, big_dst, small_loc, small_dst, piece_count), start_piece)

    @pl.when(tile == 0)
    def _():
        local2_ref[...] = jnp.zeros_like(local2_ref)
        fetch(0, 0)

    @pl.when(tile + 1 < pl.num_programs(0))
    def _():
        fetch(tile + 1, 1 - buf)

    local_ref = local2_ref.at[buf]
    _wait_rows(ys_hbm, local_ref, rows_ref[buf], sem2.at[buf])
    slots = [slot_ref[:, k:k + 1] for k in range(TOP_K)]

    ffn = jnp.zeros((tt, x1_ref.shape[1]), F32)
    for r0 in range(0, LOCAL_ROWS, SORT_CHUNK):
        col = r0 + lax.broadcasted_iota(jnp.int32, (tt, SORT_CHUNK), 1)
        onehot = jnp.zeros((tt, SORT_CHUNK), F32)
        for k in range(TOP_K):
            onehot = jnp.where(col == slots[k], 1.0, onehot)
        ffn = ffn + _dot(onehot.astype(BF16), local_ref[r0:r0 + SORT_CHUNK, :])
    gate2 = ada_ref[0, 5:6, :]
    y = DN_ALPHA * x1_ref[...] + (1.0 + gate2) * ffn
    out_ref[...] = _layer_norm(y) * g2_ref[...] + b2_ref[...]


def _combine(tables, slot_tok, x1, ada, ln2_g, ln2_b, ys, tiles_per_batch):
    N, D = x1.shape
    tt = POST_TILE
    const = lambda i, *t: (0, 0)
    tok4 = pl.BlockSpec((tt, TOP_K), lambda i, *t: (i, 0))
    grid_spec = pltpu.PrefetchScalarGridSpec(
        num_scalar_prefetch=len(tables),
        grid=(N // tt,),
        in_specs=[tok4,
                  pl.BlockSpec((tt, D), lambda i, *t: (i, 0)),
                  pl.BlockSpec((1, 6, D), lambda i, *t: (i // tiles_per_batch, 0, 0)),
                  pl.BlockSpec((1, D), const),
                  pl.BlockSpec((1, D), const),
                  pl.BlockSpec(memory_space=pl.ANY)],
        out_specs=pl.BlockSpec((tt, D), lambda i, *t: (i, 0)),
        scratch_shapes=[pltpu.VMEM((2, LOCAL_ROWS, D), BF16), pltpu.SemaphoreType.DMA((2,)),
                        pltpu.SMEM((2,), jnp.int32)],
    )
    return pl.pallas_call(
        _combine_kernel,
        grid_spec=grid_spec,
        out_shape=jax.ShapeDtypeStruct((N, D), F32),
        compiler_params=pltpu.CompilerParams(
            dimension_semantics=("arbitrary",), vmem_limit_bytes=VMEM_LIMIT_BYTES),
        name="combine",
    )(*tables, slot_tok, x1, ada, ln2_g.reshape(1, D), ln2_b.reshape(1, D), ys)


def _routing_tables(cnt, nblk):
    nt, E = cnt.shape
    i32 = jnp.int32
    run = (cnt + ROW_ALIGN - 1) // ROW_ALIGN * ROW_ALIGN
    run_start = jnp.cumsum(run, axis=1) - run
    seg_len = jnp.sum(run, axis=0)
    seg_blocks = (seg_len + MOE_BLOCK - 1) // MOE_BLOCK
    b_end = jnp.cumsum(seg_blocks)
    seg_off = (b_end - seg_blocks) * MOE_BLOCK
    dest = seg_off[None, :] + jnp.cumsum(run, axis=0) - run
    nused = b_end[-1:].astype(i32)
    i = jnp.arange(nblk, dtype=i32)
    be = jnp.minimum(jnp.sum((i[:, None] >= b_end[None, :]).astype(i32), axis=1), E - 1)
    e_ids = jnp.arange(E, dtype=i32)
    later = (e_ids[None, :] > e_ids[:, None]) & (seg_blocks > 0)[None, :]
    next_e = jnp.min(jnp.where(later, e_ids[None, :], E), axis=1)
    next_e = jnp.where(next_e < E, next_e, -1)
    mine = be[:, None] == e_ids[None, :]
    pick = lambda v: jnp.sum(jnp.where(mine, v[None, :], 0), axis=1)
    nxt = pick(next_e)
    rows_left = pick(seg_len) - (i - pick(b_end - seg_blocks)) * MOE_BLOCK
    live_parts = jnp.clip((rows_left + EXPERT_PART - 1) // EXPERT_PART, 1, MOE_BLOCK // EXPERT_PART).astype(i32)
    tail = seg_off + seg_len
    ntail = seg_blocks * MOE_BLOCK - seg_len
    def flat_pieces(n, first_loc, first_dst, rows, cap):
        end = jnp.cumsum(n, axis=1)
        start = end - n
        p = jnp.arange(cap, dtype=i32)[None, :, None]
        off = p - start[:, None, :]
        owner = (off >= 0) & (p < end[:, None, :])
        loc = jnp.sum(jnp.where(owner, first_loc[:, None, :] + rows * off, 0), axis=2)
        dst = jnp.sum(jnp.where(owner, first_dst[:, None, :] + rows * off, 0), axis=2)
        return loc.reshape(-1).astype(i32), dst.reshape(-1).astype(i32), end[:, -1]

    n_big = run // PIECE_ROWS
    big_rows = n_big * PIECE_ROWS
    big_loc, big_dst, big_count = flat_pieces(n_big, run_start, dest, PIECE_ROWS, MAX_BIG_PIECES)
    small_loc, small_dst, small_count = flat_pieces((run - big_rows) // ROW_ALIGN, run_start + big_rows,
                                                    dest + big_rows, ROW_ALIGN, MAX_SMALL_PIECES)
    piece_count = jnp.stack([big_count, small_count], axis=1).reshape(-1).astype(i32)
    piece_tables = (big_loc, big_dst, small_loc, small_dst, piece_count)
    return piece_tables, (tail.astype(i32), ntail.astype(i32)), (be.astype(i32), nxt.astype(i32), live_parts, nused)


def kernel(x, c, positions, w_ada, b_ada, w_in, pool_w, pool_scale, w_pool_out, w_attn_out, w_o,
           ln1_g, ln1_b, w_router, b_router, w_gate, b_gate, w_up, b_up, w_down, b_down, ln2_g, ln2_b):
    B, S, D = x.shape
    N = B * S
    assert D == D_MODEL and S % PROJ_TILE == 0 and S % (16 * ATTN_BLOCK) == 0
    assert S % POST_TILE == 0 and LOCAL_ROWS % SORT_CHUNK == 0
    nt = N // POST_TILE
    nblk = (N * TOP_K + nt * N_EXPERTS * (ROW_ALIGN - 1)) // MOE_BLOCK + N_EXPERTS
    for l in range(DEPTH):
        ada = _ada(c, w_ada[l], b_ada[l])
        proj_out = _proj(x, positions, ada, w_in[l], pool_w[l], pool_scale[l], w_pool_out[l])
        qkv, (pg, sga) = proj_out[:9], proj_out[9:]
        attn_outs = [_attention(*qkv[3 * g:3 * g + 3]) for g in range(len(ATTN_GROUPS))]
        x1, u2, slot, prob, cnt = _post(attn_outs, pg, sga, x, ada, w_attn_out[l], w_o[l],
                                        ln1_g[l], ln1_b[l], w_router[l], b_router[l])
        piece_tables, tail_tables, block_tables = _routing_tables(cnt.reshape(nt, N_EXPERTS), nblk)
        xs = _dispatch(piece_tables + tail_tables + block_tables[3:], slot, prob, u2.reshape(N, D),
                       nblk * MOE_BLOCK)
        ys = _experts(*block_tables, xs, w_gate[l], b_gate[l], w_up[l], b_up[l], w_down[l], b_down[l])
        slot_tok = slot.transpose(0, 2, 1).reshape(N, TOP_K)
        out = _combine(piece_tables, slot_tok, x1.reshape(N, D), ada,
                       ln2_g[l], ln2_b[l], ys, S // POST_TILE)
        x = out.reshape(B, S, D)
    return x
```

```python
import jax
import jax.numpy as jnp
import numpy as np
from jax import lax
from jax.experimental import pallas as pl
from jax.experimental.pallas import tpu as pltpu

F32 = jnp.float32
BF16 = jnp.bfloat16

D_MODEL = 1024
POOL_WINDOWS = (2, 4, 8, 16)
POOL_WIDTH = D_MODEL // 2
POOL_GROUP = POOL_WIDTH // len(POOL_WINDOWS)
POOL_HALO = 16
HEAD_DIM = 64
ATTN_GROUPS = ((128, 1), (512, 4), (2048, 16))
HEADS_PER_GROUP = 4
GROUP_WIDTH = HEADS_PER_GROUP * HEAD_DIM
N_HEADS = HEADS_PER_GROUP * len(ATTN_GROUPS)
ATTN_WIDTH = N_HEADS * HEAD_DIM
ATTN_BLOCK = 128
ROT_DIM = HEAD_DIM // 4
ROPE_THETA = 500000.0
N_EXPERTS = 32
TOP_K = 4
SWIGLU_ALPHA = 1.702
SWIGLU_LIMIT = 7.0
MOE_BLOCK = 1024
EXPERT_PART = 128
DEPTH = 1
DN_ALPHA = (2.0 * DEPTH) ** 0.25
LN_EPS = 1e-5
NEG_INF = -1e30

OFF_Q = POOL_WIDTH
OFF_K = OFF_Q + ATTN_WIDTH
OFF_V = OFF_K + ATTN_WIDTH
OFF_GP = OFF_V + ATTN_WIDTH
OFF_GA = OFF_GP + D_MODEL
IN_WIDTH = OFF_GA + D_MODEL

VMEM_LIMIT_BYTES = 56 * 1024 * 1024
LANES = 128
SUBLANES = 8

PROJ_TILE = 512
POST_TILE = 512
POST_GROUPS = 4
ATTN_QROWS = 1024
ROW_ALIGN = 16
PIECE_ROWS = 64
ZERO_ROWS = 128
SORT_CHUNK = 512
LOCAL_ROWS = -(-(POST_TILE * TOP_K + N_EXPERTS * (ROW_ALIGN - 1)) // SORT_CHUNK) * SORT_CHUNK
MAX_BIG_PIECES = LOCAL_ROWS // PIECE_ROWS
MAX_SMALL_PIECES = N_EXPERTS * (PIECE_ROWS // ROW_ALIGN - 1)


def _layer_norm(x):
    mu = jnp.mean(x, axis=-1, keepdims=True)
    xc = x - mu
    var = jnp.mean(xc * xc, axis=-1, keepdims=True)
    return xc * lax.rsqrt(var + LN_EPS)


def _dot(a, b):
    return jnp.dot(a, b, preferred_element_type=F32)


def _ada_kernel(c_ref, w_ref, b_ref, o_ref):
    c = c_ref[...]
    s = c * jax.nn.sigmoid(c)
    o_ref[...] = jnp.dot(s, w_ref[...], preferred_element_type=F32,
                         precision=lax.Precision.HIGHEST) + b_ref[...]


def _ada(c, w_ada, b_ada):
    B, D = c.shape
    rows = SUBLANES
    c_pad = jnp.pad(c, ((0, rows - B), (0, 0)))
    n_out = w_ada.shape[1]
    out = pl.pallas_call(
        _ada_kernel,
        grid=(n_out // D,),
        in_specs=[pl.BlockSpec((rows, D), lambda j: (0, 0)),
                  pl.BlockSpec((D, D), lambda j: (0, j)),
                  pl.BlockSpec((1, D), lambda j: (0, j))],
        out_specs=pl.BlockSpec((rows, D), lambda j: (0, j)),
        out_shape=jax.ShapeDtypeStruct((rows, n_out), F32),
        name="ada",
    )(c_pad, w_ada, b_ada.reshape(1, n_out))
    return out[:B].reshape(B, 6, D)


ROPE_PART_ROWS = 32


def _rope_tables():
    lane = np.arange(LANES)
    li = lane % HEAD_DIM
    half = ROT_DIM // 2
    inv_freq = jnp.power(ROPE_THETA, -jnp.arange(half, dtype=F32) * (2.0 / ROT_DIM))
    invf = jnp.broadcast_to(inv_freq[:, None], (half, LANES))
    freq = np.arange(ROPE_PART_ROWS)[:, None]
    live = (freq < 3 * half) & (freq % half == (li % half)[None, :])
    place = np.stack([live & (li < ROT_DIM)[None, :],
                      -1.0 * (live & (li < half)[None, :]),
                      live & ((li >= half) & (li < ROT_DIM))[None, :]]).astype(np.float32)
    keep = (li >= ROT_DIM).astype(np.float32)[None, :]
    return invf, jnp.asarray(place, BF16), jnp.asarray(keep)


def _proj_kernel(x_ref, xh_ref, pos_ref, ada_ref, invf_ref, place_ref, keep_ref,
                 win_ref, poolw_ref, pscale_ref, wpo_ref,
                 q1_ref, k1_ref, v1_ref, q4_ref, k4_ref, v4_ref, q16_ref, k16_ref, v16_ref,
                 pg_ref, sga_ref, xpe_ref, cls_ref):
    tm = x_ref.shape[1]
    i = pl.program_id(1)
    shift1 = ada_ref[0, 0:1, :]
    scale1 = ada_ref[0, 1:2, :]

    def modulated(xv):
        return (_layer_norm(xv) * (1.0 + scale1) + shift1).astype(BF16)

    u = modulated(x_ref[0])
    uh = modulated(xh_ref[0])

    xp = _dot(u, win_ref[:, 0:POOL_WIDTH])
    xph = _dot(uh, win_ref[:, 0:POOL_WIDTH])
    xph = jnp.where(i > 0, xph, 0.0)
    xpe_ref[0:POOL_HALO, :] = xph
    xpe_ref[POOL_HALO:, :] = xp
    tok = i * tm + lax.broadcasted_iota(jnp.int32, (tm, 1), 0)

    def pooled_branch():
        ys = []
        for g, w in enumerate(POOL_WINDOWS):
            cols = slice(g * POOL_GROUP, (g + 1) * POOL_GROUP)
            xg = xpe_ref[POOL_HALO:, cols]
            acc = xg
            for j in range(1, w):
                acc = acc + xpe_ref[POOL_HALO - j:POOL_HALO - j + tm, cols]
            cnt = jnp.minimum(tok + 1, w).astype(F32)
            mixed = (acc / cnt - xg).astype(BF16)
            ys.append(_dot(mixed, poolw_ref[g]) * pscale_ref[:, cols])
        y = jnp.concatenate(ys, axis=1).astype(BF16)
        return _dot(y, wpo_ref[...])

    ang = invf_ref[:, 0:1] * pos_ref[0, 0].astype(F32)

    def on_lanes(table, j):
        hi = table.astype(BF16).astype(F32)
        mid = (table - hi).astype(BF16).astype(F32)
        lo = table - hi - mid
        pad = jnp.zeros((ROPE_PART_ROWS - 3 * table.shape[0], tm), F32)
        parts = jnp.concatenate([hi, mid, lo, pad], axis=0).astype(BF16)
        return lax.dot_general(parts, place_ref[j], (((0,), (0,)), ((), ())), preferred_element_type=F32)

    cos = jnp.cos(ang)
    sin = jnp.sin(ang)
    c_mul = on_lanes(cos, 0) + keep_ref[...]
    s_lo = on_lanes(sin, 1)
    s_hi = on_lanes(sin, 2)
    c_mul = jnp.concatenate([c_mul, c_mul], axis=1)
    s_lo = jnp.concatenate([s_lo, s_lo], axis=1)
    s_hi = jnp.concatenate([s_hi, s_hi], axis=1)
    half = ROT_DIM // 2

    def rotate(a):
        up = pltpu.roll(a, GROUP_WIDTH - half, axis=1)
        dn = pltpu.roll(a, half, axis=1)
        return a * c_mul + up * s_lo + dn * s_hi

    def emit(a, out_ref, dil):
        if dil == 1:
            out_ref[0, 0] = a.astype(BF16)
            return
        for c in range(GROUP_WIDTH // LANES):
            cls_ref[c] = a[:, c * LANES:(c + 1) * LANES]
        for r in range(dil):
            for c in range(GROUP_WIDTH // LANES):
                out_ref[0, r, :, c * LANES:(c + 1) * LANES] = (
                    cls_ref[c, pl.ds(r, tm // dil, stride=dil), :].astype(BF16))

    outs = ((q1_ref, k1_ref, v1_ref), (q4_ref, k4_ref, v4_ref), (q16_ref, k16_ref, v16_ref))

    def attention_group(gi):
        qo, ko, vo = outs[gi]
        dil = ATTN_GROUPS[gi][1]
        c0 = gi * GROUP_WIDTH
        emit(rotate(_dot(u, win_ref[:, OFF_Q + c0:OFF_Q + c0 + GROUP_WIDTH])), qo, dil)
        emit(rotate(_dot(u, win_ref[:, OFF_K + c0:OFF_K + c0 + GROUP_WIDTH])), ko, dil)
        emit(_dot(u, win_ref[:, OFF_V + c0:OFF_V + c0 + GROUP_WIDTH]), vo, dil)

    attention_group(2)
    pooled = pooled_branch()
    attention_group(1)
    g_p = _dot(u, win_ref[:, OFF_GP:OFF_GP + D_MODEL])
    pg_ref[0] = jax.nn.sigmoid(g_p) * pooled
    attention_group(0)
    g_a = _dot(u, win_ref[:, OFF_GA:OFF_GA + D_MODEL])
    sga_ref[0] = jax.nn.sigmoid(g_a)


def _proj(x, positions, ada, w_in, pool_w, pool_scale, w_pool_out):
    B, S, D = x.shape
    tm = PROJ_TILE
    nt = S // tm
    halo_blocks = tm // POOL_HALO
    const2 = lambda b, i: (0, 0)
    in_specs = [
        pl.BlockSpec((1, tm, D), lambda b, i: (b, i, 0)),
        pl.BlockSpec((1, POOL_HALO, D), lambda b, i: (b, jnp.maximum(i * halo_blocks - 1, 0), 0)),
        pl.BlockSpec((1, 1, 1, tm), lambda b, i: (b, i, 0, 0)),
        pl.BlockSpec((1, 6, D), lambda b, i: (b, 0, 0)),
        pl.BlockSpec((ROT_DIM // 2, LANES), const2),
        pl.BlockSpec((3, ROPE_PART_ROWS, LANES), lambda b, i: (0, 0, 0)),
        pl.BlockSpec((1, LANES), const2),
        pl.BlockSpec((D, IN_WIDTH), const2),
        pl.BlockSpec((len(POOL_WINDOWS), POOL_GROUP, POOL_GROUP), lambda b, i: (0, 0, 0)),
        pl.BlockSpec((1, POOL_WIDTH), const2),
        pl.BlockSpec((POOL_WIDTH, D), const2),
    ]
    out_specs, out_shapes = [], []
    for _, dil in ATTN_GROUPS:
        for _ in range(3):
            out_specs.append(pl.BlockSpec((1, dil, tm // dil, GROUP_WIDTH), lambda b, i: (b, 0, i, 0)))
            out_shapes.append(jax.ShapeDtypeStruct((B, dil, S // dil, GROUP_WIDTH), BF16))
    for _ in range(2):
        out_specs.append(pl.BlockSpec((1, tm, D), lambda b, i: (b, i, 0)))
        out_shapes.append(jax.ShapeDtypeStruct((B, S, D), F32))
    return pl.pallas_call(
        _proj_kernel,
        grid=(B, nt),
        in_specs=in_specs,
        out_specs=out_specs,
        out_shape=out_shapes,
        scratch_shapes=[pltpu.VMEM((tm + POOL_HALO, POOL_WIDTH), F32),
                        pltpu.VMEM((GROUP_WIDTH // LANES, tm, LANES), F32)],
        compiler_params=pltpu.CompilerParams(
            dimension_semantics=("parallel", "parallel"), vmem_limit_bytes=VMEM_LIMIT_BYTES),
        name="proj",
    )(x, x, positions.reshape(B, nt, 1, tm), ada, *_rope_tables(), w_in.astype(BF16),
      pool_w.astype(BF16), pool_scale.reshape(1, POOL_WIDTH), w_pool_out.astype(BF16))


def _attn_kernel(q_ref, k_ref, v_ref, kh_ref, vh_ref, o_ref, lse_ref, kf_ref, vf_ref):
    n_cls, qb = q_ref.shape[1], q_ref.shape[2]
    per = qb // ATTN_BLOCK
    n = pl.program_id(2)
    for c in range(n_cls):
        kf_ref[c, 0:ATTN_BLOCK, :] = kh_ref[0, c]
        kf_ref[c, ATTN_BLOCK:, :] = k_ref[0, c]
        vf_ref[c, 0:ATTN_BLOCK, :] = vh_ref[0, c]
        vf_ref[c, ATTN_BLOCK:, :] = v_ref[0, c]
    qi = lax.broadcasted_iota(jnp.int32, (ATTN_BLOCK, 2 * ATTN_BLOCK), 0)
    kj = lax.broadcasted_iota(jnp.int32, (ATTN_BLOCK, 2 * ATTN_BLOCK), 1)
    band = (kj >= qi) & (kj <= qi + ATTN_BLOCK)
    band_bias = jnp.where(band, 0.0, NEG_INF)
    lane = lax.broadcasted_iota(jnp.int32, (ATTN_BLOCK, GROUP_WIDTH), 1)
    low_lanes = lax.broadcasted_iota(jnp.int32, (ATTN_BLOCK, LANES), 1) < HEAD_DIM
    ones = jnp.ones((2 * ATTN_BLOCK, LANES), BF16)
    nh, blk = HEADS_PER_GROUP, ATTN_BLOCK

    def block(t, carry):
        c, j = t // per, t % per
        r0 = pl.multiple_of(j * ATTN_BLOCK, ATTN_BLOCK)
        first_key = jnp.where((n > 0) | (j > 0), 0, ATTN_BLOCK)
        bias = band_bias + jnp.where(kj < first_key, NEG_INF, 0.0)
        q = q_ref[0, c, pl.ds(r0, blk), :].astype(F32)
        kk = kf_ref[c, pl.ds(r0, 2 * blk), :]
        vv = vf_ref[c, pl.ds(r0, 2 * blk), :]
        qs = jnp.concatenate([jnp.where((lane >= h * HEAD_DIM) & (lane < (h + 1) * HEAD_DIM), q, 0.0)
                              for h in range(nh)], axis=0).astype(BF16)
        s = lax.dot_general(qs, kk, (((1,), (1,)), ((), ())), preferred_element_type=F32)
        s = jnp.concatenate([s[h * blk:(h + 1) * blk] * (HEAD_DIM ** -0.5) + bias for h in range(nh)], axis=0)
        m = jnp.max(s, axis=-1, keepdims=True)
        p = jnp.exp(s - m).astype(BF16)
        den = _dot(p, ones)
        lse = m + jnp.log(den)
        for hp in range(GROUP_WIDTH // LANES):
            rows = slice(2 * hp * blk, (2 * hp + 2) * blk)
            ls = slice(hp * LANES, (hp + 1) * LANES)
            o2 = _dot(p[rows], vv[:, ls]) / den[rows]
            l2 = lse[rows]
            o_ref[0, c, pl.ds(r0, blk), ls] = jnp.where(low_lanes, o2[0:blk], o2[blk:2 * blk])
            lse_ref[0, c, pl.ds(r0, blk), ls] = jnp.where(low_lanes, l2[0:blk], l2[blk:2 * blk])
        return carry

    lax.fori_loop(0, n_cls * per, block, 0, unroll=8)


def _attention(q, k, v):
    B, dil, L, W = q.shape
    qb = min(L, ATTN_QROWS)
    per = qb // ATTN_BLOCK
    n_cls = min(ATTN_QROWS // qb, dil)
    main = pl.BlockSpec((1, n_cls, qb, W), lambda b, r, n: (b, r, n, 0))
    halo = pl.BlockSpec((1, n_cls, ATTN_BLOCK, W), lambda b, r, n: (b, r, jnp.maximum(n * per - 1, 0), 0))
    return pl.pallas_call(
        _attn_kernel,
        grid=(B, dil // n_cls, L // qb),
        in_specs=[main, main, main, halo, halo],
        out_specs=[main, main],
        out_shape=[jax.ShapeDtypeStruct((B, dil, L, W), F32)] * 2,
        scratch_shapes=[pltpu.VMEM((n_cls, qb + ATTN_BLOCK, W), BF16)] * 2,
        compiler_params=pltpu.CompilerParams(
            dimension_semantics=("parallel", "parallel", "parallel"), vmem_limit_bytes=VMEM_LIMIT_BYTES),
        name=f"attn_d{dil}",
    )(q, k, v, k, v)


def _post_kernel(o1_ref, l1_ref, o4_ref, l4_ref, o16_ref, l16_ref, pg_ref, sga_ref, x_ref, ada_ref,
                 wao_ref, wo_ref, g1_ref, b1_ref, wrt_ref, brt_ref,
                 x1_ref, u2_ref, slot_ref, prob_ref, cnt_ref,
                 s0, s1, s2, s3):
    tm = x_ref.shape[1]

    def token_major(src_ref, scr_ref, dil):
        if dil == 1:
            return src_ref[0, 0]
        for r in range(dil):
            for c in range(GROUP_WIDTH // LANES):
                scr_ref[c, pl.ds(r, tm // dil, stride=dil), :] = src_ref[0, r, :, c * LANES:(c + 1) * LANES]
        return jnp.concatenate([scr_ref[c] for c in range(GROUP_WIDTH // LANES)], axis=1)

    o1, l1 = o1_ref[0, 0], l1_ref[0, 0]
    o4, l4 = token_major(o4_ref, s0, 4), token_major(l4_ref, s1, 4)
    o16, l16 = token_major(o16_ref, s2, 16), token_major(l16_ref, s3, 16)
    gate1 = ada_ref[0, 2:3, :]
    shift2 = ada_ref[0, 3:4, :]
    scale2 = ada_ref[0, 4:5, :]
    nt_dot = lambda a, b: lax.dot_general(a, b, (((1,), (1,)), ((), ())), preferred_element_type=F32)
    groups = [slice(h * (tm // POST_GROUPS), (h + 1) * (tm // POST_GROUPS)) for h in range(POST_GROUPS)]

    def merge(rows):
        a1, a4, a16 = l1[rows], l4[rows], l16[rows]
        mx = jnp.maximum(jnp.maximum(a1, a4), a16)
        e1, e4, e16 = jnp.exp(a1 - mx), jnp.exp(a4 - mx), jnp.exp(a16 - mx)
        return ((e1 * o1[rows] + e4 * o4[rows] + e16 * o16[rows]) / (e1 + e4 + e16)).astype(BF16)

    def project(rows, attn):
        merged = pg_ref[0, rows, :] + sga_ref[0, rows, :] * _dot(attn, wao_ref[...])
        return _dot(merged.astype(BF16), wo_ref[...])

    def norms(rows, mix):
        x1 = _layer_norm(DN_ALPHA * x_ref[0, rows, :] + (1.0 + gate1) * mix) * g1_ref[...] + b1_ref[...]
        x1_ref[0, rows, :] = x1
        u2 = _layer_norm(x1) * (1.0 + scale2) + shift2
        u2_hi = u2.astype(BF16)
        u2_ref[0, rows, :] = u2_hi
        return u2_hi, (u2 - u2_hi.astype(F32)).astype(BF16)

    def route(u2_hi, u2_lo):
        return nt_dot(wrt_ref[0], u2_hi) + nt_dot(wrt_ref[0], u2_lo) + nt_dot(wrt_ref[1], u2_hi)

    attns = [merge(rows) for rows in groups]
    mixes = [project(rows, attn) for rows, attn in zip(groups, attns)]
    u2s = [norms(rows, mix) for rows, mix in zip(groups, mixes)]
    logits = jnp.concatenate([route(*u2) for u2 in u2s], axis=1) + brt_ref[...]
    eidx = lax.broadcasted_iota(jnp.int32, (N_EXPERTS, tm), 0)
    work = logits
    vals, idxs = [], []
    for _ in range(TOP_K):
        m = jnp.max(work, axis=0, keepdims=True)
        idx = jnp.min(jnp.where(work == m, eidx, N_EXPERTS), axis=0, keepdims=True)
        vals.append(m)
        idxs.append(idx)
        work = jnp.where(eidx == idx, -jnp.inf, work)
    exps = [jnp.exp(vk - vals[0]) for vk in vals]
    tot = exps[0] + exps[1] + exps[2] + exps[3]
    sel = jnp.zeros((N_EXPERTS, tm), F32)
    for idx in idxs:
        sel = sel + (eidx == idx).astype(F32)
    tr = lax.broadcasted_iota(jnp.int32, (tm, tm), 0)
    tc = lax.broadcasted_iota(jnp.int32, (tm, tm), 1)
    rank = _dot(sel.astype(BF16), (tr < tc).astype(BF16))
    cnt = jnp.sum(sel, axis=1, keepdims=True)
    run = jnp.floor((cnt + (ROW_ALIGN - 1)) * (1.0 / ROW_ALIGN)) * ROW_ALIGN
    er = lax.broadcasted_iota(jnp.int32, (N_EXPERTS, N_EXPERTS), 0)
    ec = lax.broadcasted_iota(jnp.int32, (N_EXPERTS, N_EXPERTS), 1)
    run_start = _dot((ec < er).astype(BF16),
                     jnp.broadcast_to(run, (N_EXPERTS, LANES)).astype(BF16))[:, 0:1]
    slot = rank + run_start
    for k in range(TOP_K):
        slot_ref[0, k:k + 1, :] = jnp.sum(jnp.where(eidx == idxs[k], slot, 0.0), axis=0,
                                          keepdims=True).astype(jnp.int32)
        prob_ref[0, k:k + 1, :] = exps[k] / tot
    cnt_ref[0] = cnt.astype(jnp.int32)


def _post(attn_outs, pg, sga, x, ada, w_attn_out, w_o, ln1_g, ln1_b, w_router, b_router):
    B, S, D = x.shape
    tm = POST_TILE
    nt = S // tm
    N = B * S
    const2 = lambda b, i: (0, 0)
    in_specs, args = [], []
    for (o, lse), (_, dil) in zip(attn_outs, ATTN_GROUPS):
        spec = pl.BlockSpec((1, dil, tm // dil, GROUP_WIDTH), lambda b, i: (b, 0, i, 0))
        in_specs += [spec, spec]
        args += [o, lse]
    tok_spec = pl.BlockSpec((1, tm, D), lambda b, i: (b, i, 0))
    in_specs += [tok_spec, tok_spec, tok_spec,
                 pl.BlockSpec((1, 6, D), lambda b, i: (b, 0, 0)),
                 pl.BlockSpec((GROUP_WIDTH, D), const2),
                 pl.BlockSpec((D, D), const2),
                 pl.BlockSpec((1, D), const2),
                 pl.BlockSpec((1, D), const2),
                 pl.BlockSpec((2, N_EXPERTS, D), lambda b, i: (0, 0, 0)),
                 pl.BlockSpec((N_EXPERTS, 1), const2)]
    wr_hi = w_router.T.astype(BF16)
    wr_lo = (w_router.T - wr_hi.astype(F32)).astype(BF16)
    args += [pg, sga, x, ada, w_attn_out.astype(BF16), w_o.astype(BF16),
             ln1_g.reshape(1, D), ln1_b.reshape(1, D), jnp.stack([wr_hi, wr_lo]),
             b_router.reshape(N_EXPERTS, 1)]
    nc = N // tm
    route_spec = pl.BlockSpec((1, TOP_K, tm), lambda b, i: (b * nt + i, 0, 0))
    out_specs = [tok_spec, tok_spec, route_spec, route_spec,
                 pl.BlockSpec((1, N_EXPERTS, 1), lambda b, i: (b * nt + i, 0, 0))]
    out_shapes = [jax.ShapeDtypeStruct((B, S, D), F32), jax.ShapeDtypeStruct((B, S, D), BF16),
                  jax.ShapeDtypeStruct((nc, TOP_K, tm), jnp.int32), jax.ShapeDtypeStruct((nc, TOP_K, tm), F32),
                  jax.ShapeDtypeStruct((nc, N_EXPERTS, 1), jnp.int32)]
    return pl.pallas_call(
        _post_kernel,
        grid=(B, nt),
        in_specs=in_specs,
        out_specs=out_specs,
        out_shape=out_shapes,
        scratch_shapes=[pltpu.VMEM((GROUP_WIDTH // LANES, tm, LANES), F32)] * 4,
        compiler_params=pltpu.CompilerParams(
            dimension_semantics=("parallel", "parallel"), vmem_limit_bytes=VMEM_LIMIT_BYTES),
        name="post",
    )(*args)


def _for_each_piece(tile, piece_refs, fn):
    big_loc, big_dst, small_loc, small_dst, count = piece_refs
    n_big, n_small = count[2 * tile], count[2 * tile + 1]

    def piece(loc_ref, dst_ref, cap, rows):
        def body(p, carry):
            i = tile * cap + p
            fn(pl.multiple_of(loc_ref[i], ROW_ALIGN), pl.multiple_of(dst_ref[i], ROW_ALIGN), rows)
            return carry
        return body

    lax.fori_loop(0, n_big, piece(big_loc, big_dst, MAX_BIG_PIECES, PIECE_ROWS), 0)
    lax.fori_loop(0, n_small, piece(small_loc, small_dst, MAX_SMALL_PIECES, ROW_ALIGN), 0)
    return n_big * PIECE_ROWS + n_small * ROW_ALIGN


def _wait_rows(src_ref, dst_ref, rows, sem):
    @pl.when(rows > 0)
    def _():
        n = pl.multiple_of(rows, ROW_ALIGN)
        pltpu.make_async_copy(src_ref.at[pl.ds(0, n)], dst_ref.at[pl.ds(0, n)], sem).wait()


def _dispatch_kernel(big_loc, big_dst, small_loc, small_dst, piece_count, tail_ref, ntail_ref, nused_ref,
                     slot_ref, prob_ref, u2_ref, xs_hbm, local2_ref, zero_ref, sem2, zsem, rows_ref):
    tile = pl.program_id(0)
    buf = tile % 2
    local_ref = local2_ref.at[buf]
    sem = sem2.at[buf]
    tt, d = u2_ref.shape
    u2 = u2_ref[...]
    slots = [slot_ref[0, k:k + 1, :] for k in range(TOP_K)]
    probs = [prob_ref[0, k:k + 1, :] for k in range(TOP_K)]
    lane = lax.broadcasted_iota(jnp.int32, (SORT_CHUNK, LANES), 1)
    for r0 in range(0, LOCAL_ROWS, SORT_CHUNK):
        row = r0 + lax.broadcasted_iota(jnp.int32, (SORT_CHUNK, tt), 0)
        w = jnp.zeros((SORT_CHUNK, tt), F32)
        for k in range(TOP_K):
            w = jnp.where(row == slots[k], probs[k], w)
        onehot = jnp.where(w != 0.0, 1.0, 0.0).astype(BF16)
        local_ref[r0:r0 + SORT_CHUNK, 0:d] = _dot(onehot, u2).astype(BF16)
        wr = jnp.sum(w, axis=1, keepdims=True)
        hi = wr.astype(BF16).astype(F32)
        mid = (wr - hi).astype(BF16).astype(F32)
        lo = wr - hi - mid
        parts = jnp.where(lane == 0, hi, jnp.where(lane == 1, mid, jnp.where(lane == 2, lo, 0.0)))
        local_ref[r0:r0 + SORT_CHUNK, d:d + LANES] = parts.astype(BF16)

    def start_piece(loc, dst, rows):
        pltpu.make_async_copy(local_ref.at[pl.ds(loc, rows)], xs_hbm.at[pl.ds(dst, rows)], sem).start()

    rows_ref[buf] = _for_each_piece(tile, (big_loc, big_dst, small_loc, small_dst, piece_count), start_piece)

    @pl.when(tile > 0)
    def _():
        _wait_rows(local2_ref.at[1 - buf], xs_hbm, rows_ref[1 - buf], sem2.at[1 - buf])

    def zero_fill(first_row, n_rows, act):
        def copy(row, rows):
            return pltpu.make_async_copy(zero_ref.at[pl.ds(0, rows)],
                                         xs_hbm.at[pl.ds(pl.multiple_of(row, ROW_ALIGN), rows)], zsem)
        n_big = n_rows // ZERO_ROWS
        rest = first_row + n_big * ZERO_ROWS
        lax.fori_loop(0, n_big, lambda p, c: (act(copy(first_row + p * ZERO_ROWS, ZERO_ROWS)), c)[1], 0)
        lax.fori_loop(0, (n_rows - n_big * ZERO_ROWS) // ROW_ALIGN,
                      lambda p, c: (act(copy(rest + p * ROW_ALIGN, ROW_ALIGN)), c)[1], 0)

    def zero_fill_all(act):
        first_spare = nused_ref[0] * MOE_BLOCK
        lax.fori_loop(0, N_EXPERTS, lambda e, c: (zero_fill(tail_ref[e], ntail_ref[e], act), c)[1], 0)
        zero_fill(first_spare, xs_hbm.shape[0] - first_spare, act)

    @pl.when(tile == 0)
    def _():
        zero_ref[...] = jnp.zeros_like(zero_ref)
        zero_fill_all(lambda c: c.start())

    @pl.when(tile == pl.num_programs(0) - 1)
    def _():
        _wait_rows(local_ref, xs_hbm, rows_ref[buf], sem)
        zero_fill_all(lambda c: c.wait())


def _dispatch(tables, slot, prob, u2, n_rows):
    N, D = u2.shape
    tt = POST_TILE
    route_spec = pl.BlockSpec((1, TOP_K, tt), lambda i, *t: (i, 0, 0))
    grid_spec = pltpu.PrefetchScalarGridSpec(
        num_scalar_prefetch=len(tables),
        grid=(N // tt,),
        in_specs=[route_spec, route_spec, pl.BlockSpec((tt, D), lambda i, *t: (i, 0))],
        out_specs=pl.BlockSpec(memory_space=pl.ANY),
        scratch_shapes=[pltpu.VMEM((2, LOCAL_ROWS, D + LANES), BF16), pltpu.VMEM((ZERO_ROWS, D + LANES), BF16),
                        pltpu.SemaphoreType.DMA((2,)), pltpu.SemaphoreType.DMA(()),
                        pltpu.SMEM((2,), jnp.int32)],
    )
    return pl.pallas_call(
        _dispatch_kernel,
        grid_spec=grid_spec,
        out_shape=jax.ShapeDtypeStruct((n_rows, D + LANES), BF16),
        compiler_params=pltpu.CompilerParams(
            dimension_semantics=("arbitrary",), vmem_limit_bytes=VMEM_LIMIT_BYTES),
        name="dispatch",
    )(*tables, slot, prob, u2)


def _expert_kernel(be_ref, next_ref, live_ref, nused_ref, xs_ref, wg_hbm, bg_ref, wu_hbm, bu_ref, wd_hbm, bd_ref,
                   ys_ref, stage, wg_s, wu_s, wd_s, sem):
    i = pl.program_id(0)
    used = i < nused_ref[0]
    prev = be_ref[jnp.maximum(i - 1, 0)]
    fresh = (i == 0) | (be_ref[i] != prev)

    def fetch(e):
        return [pltpu.make_async_copy(w_hbm.at[e], stage.at[j], sem.at[j])
                for j, w_hbm in enumerate((wg_hbm, wu_hbm, wd_hbm))]

    @pl.when(i == 0)
    def _():
        for copy in fetch(be_ref[0]):
            copy.start()

    d = ys_ref.shape[1]
    full = MOE_BLOCK // EXPERT_PART

    def arrive():
        for copy in fetch(be_ref[i]):
            copy.wait()

    def prefetch():
        @pl.when(next_ref[i] >= 0)
        def _():
            for copy in fetch(next_ref[i]):
                copy.start()

    def expert(rows, round_weights=False):
        xb = xs_ref[0:rows, 0:d]
        parts = xs_ref[0:rows, d:d + LANES].astype(F32)
        weight = parts[:, 0:1] + parts[:, 1:2] + parts[:, 2:3]
        if round_weights:
            wg_s[...] = stage[0].astype(BF16)
            wu_s[...] = stage[1].astype(BF16)
        g = _dot(xb, wg_s[...]) + bg_ref[0]
        up = _dot(xb, wu_s[...]) + bu_ref[0]
        if round_weights:
            wd_s[...] = stage[2].astype(BF16)
        g = jnp.minimum(g, SWIGLU_LIMIT)
        up = jnp.clip(up, -SWIGLU_LIMIT, SWIGLU_LIMIT)
        h = g * jax.nn.sigmoid(SWIGLU_ALPHA * g) * (up + 1.0)
        return ((_dot(h.astype(BF16), wd_s[...]) + bd_ref[0]) * weight).astype(BF16)

    @pl.when(used & fresh & (live_ref[i] == full))
    def _():
        arrive()
        ys_ref[...] = expert(MOE_BLOCK, round_weights=True)
        prefetch()

    @pl.when(used & fresh & (live_ref[i] != full))
    def _():
        arrive()
        wg_s[...] = stage[0].astype(BF16)
        wu_s[...] = stage[1].astype(BF16)
        wd_s[...] = stage[2].astype(BF16)
        prefetch()

    for parts_live in range(1, full + 1):
        rows = parts_live * EXPERT_PART
        done_above = fresh if parts_live == full else False

        @pl.when(used & (live_ref[i] == parts_live) & jnp.logical_not(done_above))
        def _():
            if rows == MOE_BLOCK:
                ys_ref[...] = expert(rows)
            else:
                ys_ref[0:rows, :] = expert(rows)
                ys_ref[rows:, :] = jnp.zeros((MOE_BLOCK - rows, d), BF16)

    @pl.when(jnp.logical_not(used))
    def _():
        ys_ref[...] = jnp.zeros_like(ys_ref)


def _experts(be, next_expert, live_parts, nused, xs, w_gate, b_gate, w_up, b_up, w_down, b_down):
    P, width = xs.shape
    D = width - LANES
    E = w_gate.shape[0]

    def live(i, nu):
        return jnp.maximum(jnp.minimum(i, nu[0] - 1), 0)

    w_spec = pl.BlockSpec(memory_space=pl.ANY)
    b_spec = pl.BlockSpec((1, 1, D), lambda i, be, nx, hf, nu: (be[live(i, nu)], 0, 0))
    grid_spec = pltpu.PrefetchScalarGridSpec(
        num_scalar_prefetch=4,
        grid=(P // MOE_BLOCK,),
        in_specs=[pl.BlockSpec((MOE_BLOCK, width), lambda i, be, nx, hf, nu: (live(i, nu), 0)),
                  w_spec, b_spec, w_spec, b_spec, w_spec, b_spec],
        out_specs=pl.BlockSpec((MOE_BLOCK, D), lambda i, be, nx, hf, nu: (i, 0)),
        scratch_shapes=[pltpu.VMEM((3, D, D), F32)] + [pltpu.VMEM((D, D), BF16)] * 3
                       + [pltpu.SemaphoreType.DMA((3,))],
    )
    return pl.pallas_call(
        _expert_kernel,
        grid_spec=grid_spec,
        out_shape=jax.ShapeDtypeStruct((P, D), BF16),
        compiler_params=pltpu.CompilerParams(
            dimension_semantics=("arbitrary",), vmem_limit_bytes=VMEM_LIMIT_BYTES),
        name="experts",
    )(be, next_expert, live_parts, nused, xs, w_gate, b_gate.reshape(E, 1, D), w_up, b_up.reshape(E, 1, D),
      w_down, b_down.reshape(E, 1, D))


def _combine_kernel(big_loc, big_dst, small_loc, small_dst, piece_count,
                    slot_ref, x1_ref, ada_ref, g2_ref, b2_ref, ys_hbm,
                    out_ref, local2_ref, sem2, rows_ref):
    tile = pl.program_id(0)
    buf = tile % 2
    tt = x1_ref.shape[0]

    def fetch(t, b):
        def start_piece(loc, dst, rows):
            pltpu.make_async_copy(ys_hbm.at[pl.ds(dst, rows)], local2_ref.at[b, pl.ds(loc, rows)],
                                  sem2.at[b]).start()
        rows_ref[b] = _for_each_piece(t, (big_loc, big_dst, small_loc, small_dst, piece_count), start_piece)

    @pl.when(tile == 0)
    def _():
        local2_ref[...] = jnp.zeros_like(local2_ref)
        fetch(0, 0)

    @pl.when(tile + 1 < pl.num_programs(0))
    def _():
        fetch(tile + 1, 1 - buf)

    local_ref = local2_ref.at[buf]
    _wait_rows(ys_hbm, local_ref, rows_ref[buf], sem2.at[buf])
    slots = [slot_ref[:, k:k + 1] for k in range(TOP_K)]

    ffn = jnp.zeros((tt, x1_ref.shape[1]), F32)
    for r0 in range(0, LOCAL_ROWS, SORT_CHUNK):
        col = r0 + lax.broadcasted_iota(jnp.int32, (tt, SORT_CHUNK), 1)
        onehot = jnp.zeros((tt, SORT_CHUNK), F32)
        for k in range(TOP_K):
            onehot = jnp.where(col == slots[k], 1.0, onehot)
        ffn = ffn + _dot(onehot.astype(BF16), local_ref[r0:r0 + SORT_CHUNK, :])
    gate2 = ada_ref[0, 5:6, :]
    y = DN_ALPHA * x1_ref[...] + (1.0 + gate2) * ffn
    out_ref[...] = _layer_norm(y) * g2_ref[...] + b2_ref[...]


def _combine(tables, slot_tok, x1, ada, ln2_g, ln2_b, ys, tiles_per_batch):
    N, D = x1.shape
    tt = POST_TILE
    const = lambda i, *t: (0, 0)
    tok4 = pl.BlockSpec((tt, TOP_K), lambda i, *t: (i, 0))
    grid_spec = pltpu.PrefetchScalarGridSpec(
        num_scalar_prefetch=len(tables),
        grid=(N // tt,),
        in_specs=[tok4,
                  pl.BlockSpec((tt, D), lambda i, *t: (i, 0)),
                  pl.BlockSpec((1, 6, D), lambda i, *t: (i // tiles_per_batch, 0, 0)),
                  pl.BlockSpec((1, D), const),
                  pl.BlockSpec((1, D), const),
                  pl.BlockSpec(memory_space=pl.ANY)],
        out_specs=pl.BlockSpec((tt, D), lambda i, *t: (i, 0)),
        scratch_shapes=[pltpu.VMEM((2, LOCAL_ROWS, D), BF16), pltpu.SemaphoreType.DMA((2,)),
                        pltpu.SMEM((2,), jnp.int32)],
    )
    return pl.pallas_call(
        _combine_kernel,
        grid_spec=grid_spec,
        out_shape=jax.ShapeDtypeStruct((N, D), F32),
        compiler_params=pltpu.CompilerParams(
            dimension_semantics=("arbitrary",), vmem_limit_bytes=VMEM_LIMIT_BYTES),
        name="combine",
    )(*tables, slot_tok, x1, ada, ln2_g.reshape(1, D), ln2_b.reshape(1, D), ys)


def _routing_tables(cnt, nblk):
    nt, E = cnt.shape
    i32 = jnp.int32
    run = (cnt + ROW_ALIGN - 1) // ROW_ALIGN * ROW_ALIGN
    run_start = jnp.cumsum(run, axis=1) - run
    seg_len = jnp.sum(run, axis=0)
    seg_blocks = (seg_len + MOE_BLOCK - 1) // MOE_BLOCK
    b_end = jnp.cumsum(seg_blocks)
    seg_off = (b_end - seg_blocks) * MOE_BLOCK
    dest = seg_off[None, :] + jnp.cumsum(run, axis=0) - run
    nused = b_end[-1:].astype(i32)
    i = jnp.arange(nblk, dtype=i32)
    be = jnp.minimum(jnp.sum((i[:, None] >= b_end[None, :]).astype(i32), axis=1), E - 1)
    e_ids = jnp.arange(E, dtype=i32)
    later = (e_ids[None, :] > e_ids[:, None]) & (seg_blocks > 0)[None, :]
    next_e = jnp.min(jnp.where(later, e_ids[None, :], E), axis=1)
    next_e = jnp.where(next_e < E, next_e, -1)
    mine = be[:, None] == e_ids[None, :]
    pick = lambda v: jnp.sum(jnp.where(mine, v[None, :], 0), axis=1)
    nxt = pick(next_e)
    rows_left = pick(seg_len) - (i - pick(b_end - seg_blocks)) * MOE_BLOCK
    live_parts = jnp.clip((rows_left + EXPERT_PART - 1) // EXPERT_PART, 1, MOE_BLOCK // EXPERT_PART).astype(i32)
    tail = seg_off + seg_len
    ntail = seg_blocks * MOE_BLOCK - seg_len
    def flat_pieces(n, first_loc, first_dst, rows, cap):
        end = jnp.cumsum(n, axis=1)
        start = end - n
        p = jnp.arange(cap, dtype=i32)[None, :, None]
        off = p - start[:, None, :]
        owner = (off >= 0) & (p < end[:, None, :])
        loc = jnp.sum(jnp.where(owner, first_loc[:, None, :] + rows * off, 0), axis=2)
        dst = jnp.sum(jnp.where(owner, first_dst[:, None, :] + rows * off, 0), axis=2)
        return loc.reshape(-1).astype(i32), dst.reshape(-1).astype(i32), end[:, -1]

    n_big = run // PIECE_ROWS
    big_rows = n_big * PIECE_ROWS
    big_loc, big_dst, big_count = flat_pieces(n_big, run_start, dest, PIECE_ROWS, MAX_BIG_PIECES)
    small_loc, small_dst, small_count = flat_pieces((run - big_rows) // ROW_ALIGN, run_start + big_rows,
                                                    dest + big_rows, ROW_ALIGN, MAX_SMALL_PIECES)
    piece_count = jnp.stack([big_count, small_count], axis=1).reshape(-1).astype(i32)
    piece_tables = (big_loc, big_dst, small_loc, small_dst, piece_count)
    return piece_tables, (tail.astype(i32), ntail.astype(i32)), (be.astype(i32), nxt.astype(i32), live_parts, nused)


def kernel(x, c, positions, w_ada, b_ada, w_in, pool_w, pool_scale, w_pool_out, w_attn_out, w_o,
           ln1_g, ln1_b, w_router, b_router, w_gate, b_gate, w_up, b_up, w_down, b_down, ln2_g, ln2_b):
    B, S, D = x.shape
    N = B * S
    assert D == D_MODEL and S % PROJ_TILE == 0 and S % (16 * ATTN_BLOCK) == 0
    assert S % POST_TILE == 0 and LOCAL_ROWS % SORT_CHUNK == 0
    nt = N // POST_TILE
    nblk = (N * TOP_K + nt * N_EXPERTS * (ROW_ALIGN - 1)) // MOE_BLOCK + N_EXPERTS
    for l in range(DEPTH):
        ada = _ada(c, w_ada[l], b_ada[l])
        proj_out = _proj(x, positions, ada, w_in[l], pool_w[l], pool_scale[l], w_pool_out[l])
        qkv, (pg, sga) = proj_out[:9], proj_out[9:]
        attn_outs = [_attention(*qkv[3 * g:3 * g + 3]) for g in range(len(ATTN_GROUPS))]
        x1, u2, slot, prob, cnt = _post(attn_outs, pg, sga, x, ada, w_attn_out[l], w_o[l],
                                        ln1_g[l], ln1_b[l], w_router[l], b_router[l])
        piece_tables, tail_tables, block_tables = _routing_tables(cnt.reshape(nt, N_EXPERTS), nblk)
        xs = _dispatch(piece_tables + tail_tables + block_tables[3:], slot, prob, u2.reshape(N, D),
                       nblk * MOE_BLOCK)
        ys = _experts(*block_tables, xs, w_gate[l], b_gate[l], w_up[l], b_up[l], w_down[l], b_down[l])
        slot_tok = slot.transpose(0, 2, 1).reshape(N, TOP_K)
        out = _combine(piece_tables, slot_tok, x1.reshape(N, D), ada,
                       ln2_g[l], ln2_b[l], ys, S // POST_TILE)
        x = out.reshape(B, S, D)
    return x
```

```python
import jax
import jax.numpy as jnp
import numpy as np
from jax import lax
from jax.experimental import pallas as pl
from jax.experimental.pallas import tpu as pltpu

F32 = jnp.float32
BF16 = jnp.bfloat16

D_MODEL = 1024
POOL_WINDOWS = (2, 4, 8, 16)
POOL_WIDTH = D_MODEL // 2
POOL_GROUP = POOL_WIDTH // len(POOL_WINDOWS)
POOL_HALO = 16
HEAD_DIM = 64
ATTN_GROUPS = ((128, 1), (512, 4), (2048, 16))
HEADS_PER_GROUP = 4
GROUP_WIDTH = HEADS_PER_GROUP * HEAD_DIM
N_HEADS = HEADS_PER_GROUP * len(ATTN_GROUPS)
ATTN_WIDTH = N_HEADS * HEAD_DIM
ATTN_BLOCK = 128
ROT_DIM = HEAD_DIM // 4
ROPE_THETA = 500000.0
N_EXPERTS = 32
TOP_K = 4
SWIGLU_ALPHA = 1.702
SWIGLU_LIMIT = 7.0
MOE_BLOCK = 1024
EXPERT_PART = 128
DEPTH = 1
DN_ALPHA = (2.0 * DEPTH) ** 0.25
LN_EPS = 1e-5
NEG_INF = -1e30

OFF_Q = POOL_WIDTH
OFF_K = OFF_Q + ATTN_WIDTH
OFF_V = OFF_K + ATTN_WIDTH
OFF_GP = OFF_V + ATTN_WIDTH
OFF_GA = OFF_GP + D_MODEL
IN_WIDTH = OFF_GA + D_MODEL

VMEM_LIMIT_BYTES = 56 * 1024 * 1024
LANES = 128
SUBLANES = 8

PROJ_TILE = 512
POST_TILE = 512
POST_GROUPS = 4
ATTN_QROWS = 1024
ROW_ALIGN = 16
PIECE_ROWS = 64
ZERO_ROWS = 128
SORT_CHUNK = 512
LOCAL_ROWS = -(-(POST_TILE * TOP_K + N_EXPERTS * (ROW_ALIGN - 1)) // SORT_CHUNK) * SORT_CHUNK
MAX_BIG_PIECES = LOCAL_ROWS // PIECE_ROWS
MAX_SMALL_PIECES = N_EXPERTS * (PIECE_ROWS // ROW_ALIGN - 1)


def _layer_norm(x):
    mu = jnp.mean(x, axis=-1, keepdims=True)
    xc = x - mu
    var = jnp.mean(xc * xc, axis=-1, keepdims=True)
    return xc * lax.rsqrt(var + LN_EPS)


def _dot(a, b):
    return jnp.dot(a, b, preferred_element_type=F32)


def _ada_kernel(c_ref, w_ref, b_ref, o_ref):
    c = c_ref[...]
    s = c * jax.nn.sigmoid(c)
    o_ref[...] = jnp.dot(s, w_ref[...], preferred_element_type=F32,
                         precision=lax.Precision.HIGHEST) + b_ref[...]


def _ada(c, w_ada, b_ada):
    B, D = c.shape
    rows = SUBLANES
    c_pad = jnp.pad(c, ((0, rows - B), (0, 0)))
    n_out = w_ada.shape[1]
    out = pl.pallas_call(
        _ada_kernel,
        grid=(n_out // D,),
        in_specs=[pl.BlockSpec((rows, D), lambda j: (0, 0)),
                  pl.BlockSpec((D, D), lambda j: (0, j)),
                  pl.BlockSpec((1, D), lambda j: (0, j))],
        out_specs=pl.BlockSpec((rows, D), lambda j: (0, j)),
        out_shape=jax.ShapeDtypeStruct((rows, n_out), F32),
        name="ada",
    )(c_pad, w_ada, b_ada.reshape(1, n_out))
    return out[:B].reshape(B, 6, D)


ROPE_PART_ROWS = 32


def _rope_tables():
    lane = np.arange(LANES)
    li = lane % HEAD_DIM
    half = ROT_DIM // 2
    inv_freq = jnp.power(ROPE_THETA, -jnp.arange(half, dtype=F32) * (2.0 / ROT_DIM))
    invf = jnp.broadcast_to(inv_freq[:, None], (half, LANES))
    freq = np.arange(ROPE_PART_ROWS)[:, None]
    live = (freq < 3 * half) & (freq % half == (li % half)[None, :])
    place = np.stack([live & (li < ROT_DIM)[None, :],
                      -1.0 * (live & (li < half)[None, :]),
                      live & ((li >= half) & (li < ROT_DIM))[None, :]]).astype(np.float32)
    keep = (li >= ROT_DIM).astype(np.float32)[None, :]
    return invf, jnp.asarray(place, BF16), jnp.asarray(keep)


def _proj_kernel(x_ref, xh_ref, pos_ref, ada_ref, invf_ref, place_ref, keep_ref,
                 win_ref, poolw_ref, pscale_ref, wpo_ref,
                 q1_ref, k1_ref, v1_ref, q4_ref, k4_ref, v4_ref, q16_ref, k16_ref, v16_ref,
                 pg_ref, sga_ref, xpe_ref, cls_ref):
    tm = x_ref.shape[1]
    i = pl.program_id(1)
    shift1 = ada_ref[0, 0:1, :]
    scale1 = ada_ref[0, 1:2, :]

    def modulated(xv):
        return (_layer_norm(xv) * (1.0 + scale1) + shift1).astype(BF16)

    u = modulated(x_ref[0])
    uh = modulated(xh_ref[0])

    xp = _dot(u, win_ref[:, 0:POOL_WIDTH])
    xph = _dot(uh, win_ref[:, 0:POOL_WIDTH])
    xph = jnp.where(i > 0, xph, 0.0)
    xpe_ref[0:POOL_HALO, :] = xph
    xpe_ref[POOL_HALO:, :] = xp
    tok = i * tm + lax.broadcasted_iota(jnp.int32, (tm, 1), 0)

    def pooled_branch():
        ys = []
        for g, w in enumerate(POOL_WINDOWS):
            cols = slice(g * POOL_GROUP, (g + 1) * POOL_GROUP)
            xg = xpe_ref[POOL_HALO:, cols]
            acc = xg
            for j in range(1, w):
                acc = acc + xpe_ref[POOL_HALO - j:POOL_HALO - j + tm, cols]
            cnt = jnp.minimum(tok + 1, w).astype(F32)
            mixed = (acc / cnt - xg).astype(BF16)
            ys.append(_dot(mixed, poolw_ref[g]) * pscale_ref[:, cols])
        y = jnp.concatenate(ys, axis=1).astype(BF16)
        return _dot(y, wpo_ref[...])

    ang = invf_ref[:, 0:1] * pos_ref[0, 0].astype(F32)

    def on_lanes(table, j):
        hi = table.astype(BF16).astype(F32)
        mid = (table - hi).astype(BF16).astype(F32)
        lo = table - hi - mid
        pad = jnp.zeros((ROPE_PART_ROWS - 3 * table.shape[0], tm), F32)
        parts = jnp.concatenate([hi, mid, lo, pad], axis=0).astype(BF16)
        return lax.dot_general(parts, place_ref[j], (((0,), (0,)), ((), ())), preferred_element_type=F32)

    cos = jnp.cos(ang)
    sin = jnp.sin(ang)
    c_mul = on_lanes(cos, 0) + keep_ref[...]
    s_lo = on_lanes(sin, 1)
    s_hi = on_lanes(sin, 2)
    c_mul = jnp.concatenate([c_mul, c_mul], axis=1)
    s_lo = jnp.concatenate([s_lo, s_lo], axis=1)
    s_hi = jnp.concatenate([s_hi, s_hi], axis=1)
    half = ROT_DIM // 2

    def rotate(a):
        up = pltpu.roll(a, GROUP_WIDTH - half, axis=1)
        dn = pltpu.roll(a, half, axis=1)
        return a * c_mul + up * s_lo + dn * s_hi

    def emit(a, out_ref, dil):
        if dil == 1:
            out_ref[0, 0] = a.astype(BF16)
            return
        for c in range(GROUP_WIDTH // LANES):
            cls_ref[c] = a[:, c * LANES:(c + 1) * LANES]
        for r in range(dil):
            for c in range(GROUP_WIDTH // LANES):
                out_ref[0, r, :, c * LANES:(c + 1) * LANES] = (
                    cls_ref[c, pl.ds(r, tm // dil, stride=dil), :].astype(BF16))

    outs = ((q1_ref, k1_ref, v1_ref), (q4_ref, k4_ref, v4_ref), (q16_ref, k16_ref, v16_ref))

    def attention_group(gi):
        qo, ko, vo = outs[gi]
        dil = ATTN_GROUPS[gi][1]
        c0 = gi * GROUP_WIDTH
        emit(rotate(_dot(u, win_ref[:, OFF_Q + c0:OFF_Q + c0 + GROUP_WIDTH])), qo, dil)
        emit(rotate(_dot(u, win_ref[:, OFF_K + c0:OFF_K + c0 + GROUP_WIDTH])), ko, dil)
        emit(_dot(u, win_ref[:, OFF_V + c0:OFF_V + c0 + GROUP_WIDTH]), vo, dil)

    attention_group(2)
    pooled = pooled_branch()
    attention_group(1)
    g_p = _dot(u, win_ref[:, OFF_GP:OFF_GP + D_MODEL])
    pg_ref[0] = jax.nn.sigmoid(g_p) * pooled
    attention_group(0)
    g_a = _dot(u, win_ref[:, OFF_GA:OFF_GA + D_MODEL])
    sga_ref[0] = jax.nn.sigmoid(g_a)


def _proj(x, positions, ada, w_in, pool_w, pool_scale, w_pool_out):
    B, S, D = x.shape
    tm = PROJ_TILE
    nt = S // tm
    halo_blocks = tm // POOL_HALO
    const2 = lambda b, i: (0, 0)
    in_specs = [
        pl.BlockSpec((1, tm, D), lambda b, i: (b, i, 0)),
        pl.BlockSpec((1, POOL_HALO, D), lambda b, i: (b, jnp.maximum(i * halo_blocks - 1, 0), 0)),
        pl.BlockSpec((1, 1, 1, tm), lambda b, i: (b, i, 0, 0)),
        pl.BlockSpec((1, 6, D), lambda b, i: (b, 0, 0)),
        pl.BlockSpec((ROT_DIM // 2, LANES), const2),
        pl.BlockSpec((3, ROPE_PART_ROWS, LANES), lambda b, i: (0, 0, 0)),
        pl.BlockSpec((1, LANES), const2),
        pl.BlockSpec((D, IN_WIDTH), const2),
        pl.BlockSpec((len(POOL_WINDOWS), POOL_GROUP, POOL_GROUP), lambda b, i: (0, 0, 0)),
        pl.BlockSpec((1, POOL_WIDTH), const2),
        pl.BlockSpec((POOL_WIDTH, D), const2),
    ]
    out_specs, out_shapes = [], []
    for _, dil in ATTN_GROUPS:
        for _ in range(3):
            out_specs.append(pl.BlockSpec((1, dil, tm // dil, GROUP_WIDTH), lambda b, i: (b, 0, i, 0)))
            out_shapes.append(jax.ShapeDtypeStruct((B, dil, S // dil, GROUP_WIDTH), BF16))
    for _ in range(2):
        out_specs.append(pl.BlockSpec((1, tm, D), lambda b, i: (b, i, 0)))
        out_shapes.append(jax.ShapeDtypeStruct((B, S, D), F32))
    return pl.pallas_call(
        _proj_kernel,
        grid=(B, nt),
        in_specs=in_specs,
        out_specs=out_specs,
        out_shape=out_shapes,
        scratch_shapes=[pltpu.VMEM((tm + POOL_HALO, POOL_WIDTH), F32),
                        pltpu.VMEM((GROUP_WIDTH // LANES, tm, LANES), F32)],
        compiler_params=pltpu.CompilerParams(
            dimension_semantics=("parallel", "parallel"), vmem_limit_bytes=VMEM_LIMIT_BYTES),
        name="proj",
    )(x, x, positions.reshape(B, nt, 1, tm), ada, *_rope_tables(), w_in.astype(BF16),
      pool_w.astype(BF16), pool_scale.reshape(1, POOL_WIDTH), w_pool_out.astype(BF16))


def _attn_kernel(q_ref, k_ref, v_ref, kh_ref, vh_ref, o_ref, lse_ref, kf_ref, vf_ref):
    n_cls, qb = q_ref.shape[1], q_ref.shape[2]
    per = qb // ATTN_BLOCK
    n = pl.program_id(2)
    for c in range(n_cls):
        kf_ref[c, 0:ATTN_BLOCK, :] = kh_ref[0, c]
        kf_ref[c, ATTN_BLOCK:, :] = k_ref[0, c]
        vf_ref[c, 0:ATTN_BLOCK, :] = vh_ref[0, c]
        vf_ref[c, ATTN_BLOCK:, :] = v_ref[0, c]
    qi = lax.broadcasted_iota(jnp.int32, (ATTN_BLOCK, 2 * ATTN_BLOCK), 0)
    kj = lax.broadcasted_iota(jnp.int32, (ATTN_BLOCK, 2 * ATTN_BLOCK), 1)
    band = (kj >= qi) & (kj <= qi + ATTN_BLOCK)
    band_bias = jnp.where(band, 0.0, NEG_INF)
    lane = lax.broadcasted_iota(jnp.int32, (ATTN_BLOCK, GROUP_WIDTH), 1)
    low_lanes = lax.broadcasted_iota(jnp.int32, (ATTN_BLOCK, LANES), 1) < HEAD_DIM
    ones = jnp.ones((2 * ATTN_BLOCK, LANES), BF16)
    nh, blk = HEADS_PER_GROUP, ATTN_BLOCK

    def block(t, carry):
        c, j = t // per, t % per
        r0 = pl.multiple_of(j * ATTN_BLOCK, ATTN_BLOCK)
        first_key = jnp.where((n > 0) | (j > 0), 0, ATTN_BLOCK)
        bias = band_bias + jnp.where(kj < first_key, NEG_INF, 0.0)
        q = q_ref[0, c, pl.ds(r0, blk), :].astype(F32)
        kk = kf_ref[c, pl.ds(r0, 2 * blk), :]
        vv = vf_ref[c, pl.ds(r0, 2 * blk), :]
        qs = jnp.concatenate([jnp.where((lane >= h * HEAD_DIM) & (lane < (h + 1) * HEAD_DIM), q, 0.0)
                              for h in range(nh)], axis=0).astype(BF16)
        s = lax.dot_general(qs, kk, (((1,), (1,)), ((), ())), preferred_element_type=F32)
        s = jnp.concatenate([s[h * blk:(h + 1) * blk] * (HEAD_DIM ** -0.5) + bias for h in range(nh)], axis=0)
        m = jnp.max(s, axis=-1, keepdims=True)
        p = jnp.exp(s - m).astype(BF16)
        den = _dot(p, ones)
        lse = m + jnp.log(den)
        for hp in range(GROUP_WIDTH // LANES):
            rows = slice(2 * hp * blk, (2 * hp + 2) * blk)
            ls = slice(hp * LANES, (hp + 1) * LANES)
            o2 = _dot(p[rows], vv[:, ls]) / den[rows]
            l2 = lse[rows]
            o_ref[0, c, pl.ds(r0, blk), ls] = jnp.where(low_lanes, o2[0:blk], o2[blk:2 * blk])
            lse_ref[0, c, pl.ds(r0, blk), ls] = jnp.where(low_lanes, l2[0:blk], l2[blk:2 * blk])
        return carry

    lax.fori_loop(0, n_cls * per, block, 0, unroll=8)


def _attention(q, k, v):
    B, dil, L, W = q.shape
    qb = min(L, ATTN_QROWS)
    per = qb // ATTN_BLOCK
    n_cls = min(ATTN_QROWS // qb, dil)
    main = pl.BlockSpec((1, n_cls, qb, W), lambda b, r, n: (b, r, n, 0))
    halo = pl.BlockSpec((1, n_cls, ATTN_BLOCK, W), lambda b, r, n: (b, r, jnp.maximum(n * per - 1, 0), 0))
    return pl.pallas_call(
        _attn_kernel,
        grid=(B, dil // n_cls, L // qb),
        in_specs=[main, main, main, halo, halo],
        out_specs=[main, main],
        out_shape=[jax.ShapeDtypeStruct((B, dil, L, W), F32)] * 2,
        scratch_shapes=[pltpu.VMEM((n_cls, qb + ATTN_BLOCK, W), BF16)] * 2,
        compiler_params=pltpu.CompilerParams(
            dimension_semantics=("parallel", "parallel", "parallel"), vmem_limit_bytes=VMEM_LIMIT_BYTES),
        name=f"attn_d{dil}",
    )(q, k, v, k, v)


def _post_kernel(o1_ref, l1_ref, o4_ref, l4_ref, o16_ref, l16_ref, pg_ref, sga_ref, x_ref, ada_ref,
                 wao_ref, wo_ref, ln1_ref, wrt_ref, brt_ref,
                 x1_ref, u2_ref, slot_ref, prob_ref, cnt_ref,
                 s0, s1, s2, s3):
    tm = x_ref.shape[1]

    def token_major(src_ref, scr_ref, dil):
        if dil == 1:
            return src_ref[0, 0]
        for r in range(dil):
            for c in range(GROUP_WIDTH // LANES):
                scr_ref[c, pl.ds(r, tm // dil, stride=dil), :] = src_ref[0, r, :, c * LANES:(c + 1) * LANES]
        return jnp.concatenate([scr_ref[c] for c in range(GROUP_WIDTH // LANES)], axis=1)

    o1, l1 = o1_ref[0, 0], l1_ref[0, 0]
    o4, l4 = token_major(o4_ref, s0, 4), token_major(l4_ref, s1, 4)
    o16, l16 = token_major(o16_ref, s2, 16), token_major(l16_ref, s3, 16)
    gate1 = ada_ref[0, 2:3, :]
    shift2 = ada_ref[0, 3:4, :]
    scale2 = ada_ref[0, 4:5, :]
    nt_dot = lambda a, b: lax.dot_general(a, b, (((1,), (1,)), ((), ())), preferred_element_type=F32)
    groups = [slice(h * (tm // POST_GROUPS), (h + 1) * (tm // POST_GROUPS)) for h in range(POST_GROUPS)]

    def merge(rows):
        a1, a4, a16 = l1[rows], l4[rows], l16[rows]
        mx = jnp.maximum(jnp.maximum(a1, a4), a16)
        e1, e4, e16 = jnp.exp(a1 - mx), jnp.exp(a4 - mx), jnp.exp(a16 - mx)
        return ((e1 * o1[rows] + e4 * o4[rows] + e16 * o16[rows]) / (e1 + e4 + e16)).astype(BF16)

    def project(rows, attn):
        merged = pg_ref[0, rows, :] + sga_ref[0, rows, :] * _dot(attn, wao_ref[...])
        return _dot(merged.astype(BF16), wo_ref[...])

    def norms(rows, mix):
        x1 = (_layer_norm(DN_ALPHA * x_ref[0, rows, :] + (1.0 + gate1) * mix) * ln1_ref[0:1, :]
              + ln1_ref[1:2, :])
        x1_ref[0, rows, :] = x1
        u2 = _layer_norm(x1) * (1.0 + scale2) + shift2
        u2_hi = u2.astype(BF16)
        u2_ref[0, rows, :] = u2_hi
        return u2_hi, (u2 - u2_hi.astype(F32)).astype(BF16)

    def route(u2_hi, u2_lo):
        return nt_dot(wrt_ref[0], u2_hi) + nt_dot(wrt_ref[0], u2_lo) + nt_dot(wrt_ref[1], u2_hi)

    attns = [merge(rows) for rows in groups]
    mixes = [project(rows, attn) for rows, attn in zip(groups, attns)]
    u2s = [norms(rows, mix) for rows, mix in zip(groups, mixes)]
    logits = jnp.concatenate([route(*u2) for u2 in u2s], axis=1) + brt_ref[...]
    eidx = lax.broadcasted_iota(jnp.int32, (N_EXPERTS, tm), 0)
    work = logits
    vals, idxs = [], []
    for _ in range(TOP_K):
        m = jnp.max(work, axis=0, keepdims=True)
        idx = jnp.min(jnp.where(work == m, eidx, N_EXPERTS), axis=0, keepdims=True)
        vals.append(m)
        idxs.append(idx)
        work = jnp.where(eidx == idx, -jnp.inf, work)
    exps = [jnp.exp(vk - vals[0]) for vk in vals]
    tot = exps[0] + exps[1] + exps[2] + exps[3]
    sel = jnp.zeros((N_EXPERTS, tm), F32)
    for idx in idxs:
        sel = sel + (eidx == idx).astype(F32)
    tr = lax.broadcasted_iota(jnp.int32, (tm, tm), 0)
    tc = lax.broadcasted_iota(jnp.int32, (tm, tm), 1)
    rank = _dot(sel.astype(BF16), (tr < tc).astype(BF16))
    cnt = jnp.sum(sel, axis=1, keepdims=True)
    run = jnp.floor((cnt + (ROW_ALIGN - 1)) * (1.0 / ROW_ALIGN)) * ROW_ALIGN
    er = lax.broadcasted_iota(jnp.int32, (N_EXPERTS, N_EXPERTS), 0)
    ec = lax.broadcasted_iota(jnp.int32, (N_EXPERTS, N_EXPERTS), 1)
    run_start = _dot((ec < er).astype(BF16),
                     jnp.broadcast_to(run, (N_EXPERTS, LANES)).astype(BF16))[:, 0:1]
    slot = rank + run_start
    for k in range(TOP_K):
        slot_ref[0, k:k + 1, :] = jnp.sum(jnp.where(eidx == idxs[k], slot, 0.0), axis=0,
                                          keepdims=True).astype(jnp.int32)
        prob_ref[0, k:k + 1, :] = exps[k] / tot
    cnt_ref[0] = cnt.astype(jnp.int32)


def _post(attn_outs, pg, sga, x, ada, w_attn_out, w_o, ln1_g, ln1_b, w_router, b_router):
    B, S, D = x.shape
    tm = POST_TILE
    nt = S // tm
    N = B * S
    const2 = lambda b, i: (0, 0)
    in_specs, args = [], []
    for (o, lse), (_, dil) in zip(attn_outs, ATTN_GROUPS):
        spec = pl.BlockSpec((1, dil, tm // dil, GROUP_WIDTH), lambda b, i: (b, 0, i, 0))
        in_specs += [spec, spec]
        args += [o, lse]
    tok_spec = pl.BlockSpec((1, tm, D), lambda b, i: (b, i, 0))
    in_specs += [tok_spec, tok_spec, tok_spec,
                 pl.BlockSpec((1, 6, D), lambda b, i: (b, 0, 0)),
                 pl.BlockSpec((GROUP_WIDTH, D), const2),
                 pl.BlockSpec((D, D), const2),
                 pl.BlockSpec((2, D), const2),
                 pl.BlockSpec((2, N_EXPERTS, D), lambda b, i: (0, 0, 0)),
                 pl.BlockSpec((N_EXPERTS, 1), const2)]
    wr_hi = w_router.T.astype(BF16)
    wr_lo = (w_router.T - wr_hi.astype(F32)).astype(BF16)
    args += [pg, sga, x, ada, w_attn_out.astype(BF16), w_o.astype(BF16),
             jnp.stack([ln1_g, ln1_b]), jnp.stack([wr_hi, wr_lo]),
             b_router.reshape(N_EXPERTS, 1)]
    nc = N // tm
    route_spec = pl.BlockSpec((1, TOP_K, tm), lambda b, i: (b * nt + i, 0, 0))
    out_specs = [tok_spec, tok_spec, route_spec, route_spec,
                 pl.BlockSpec((1, N_EXPERTS, 1), lambda b, i: (b * nt + i, 0, 0))]
    out_shapes = [jax.ShapeDtypeStruct((B, S, D), F32), jax.ShapeDtypeStruct((B, S, D), BF16),
                  jax.ShapeDtypeStruct((nc, TOP_K, tm), jnp.int32), jax.ShapeDtypeStruct((nc, TOP_K, tm), F32),
                  jax.ShapeDtypeStruct((nc, N_EXPERTS, 1), jnp.int32)]
    return pl.pallas_call(
        _post_kernel,
        grid=(B, nt),
        in_specs=in_specs,
        out_specs=out_specs,
        out_shape=out_shapes,
        scratch_shapes=[pltpu.VMEM((GROUP_WIDTH // LANES, tm, LANES), F32)] * 4,
        compiler_params=pltpu.CompilerParams(
            dimension_semantics=("parallel", "parallel"), vmem_limit_bytes=VMEM_LIMIT_BYTES),
        name="post",
    )(*args)


def _for_each_piece(tile, piece_refs, fn):
    big_loc, big_dst, small_loc, small_dst, count = piece_refs
    n_big, n_small = count[2 * tile], count[2 * tile + 1]

    def piece(loc_ref, dst_ref, cap, rows):
        def body(p, carry):
            i = tile * cap + p
            fn(pl.multiple_of(loc_ref[i], ROW_ALIGN), pl.multiple_of(dst_ref[i], ROW_ALIGN), rows)
            return carry
        return body

    lax.fori_loop(0, n_big, piece(big_loc, big_dst, MAX_BIG_PIECES, PIECE_ROWS), 0)
    lax.fori_loop(0, n_small, piece(small_loc, small_dst, MAX_SMALL_PIECES, ROW_ALIGN), 0)
    return n_big * PIECE_ROWS + n_small * ROW_ALIGN


def _wait_rows(src_ref, dst_ref, rows, sem):
    @pl.when(rows > 0)
    def _():
        n = pl.multiple_of(rows, ROW_ALIGN)
        pltpu.make_async_copy(src_ref.at[pl.ds(0, n)], dst_ref.at[pl.ds(0, n)], sem).wait()


def _dispatch_kernel(big_loc, big_dst, small_loc, small_dst, piece_count, tail_ref, ntail_ref, nused_ref,
                     slot_ref, prob_ref, u2_ref, xs_hbm, local2_ref, zero_ref, sem2, zsem, rows_ref):
    tile = pl.program_id(0)
    buf = tile % 2
    local_ref = local2_ref.at[buf]
    sem = sem2.at[buf]
    tt, d = u2_ref.shape
    u2 = u2_ref[...]
    slots = [slot_ref[0, k:k + 1, :] for k in range(TOP_K)]
    probs = [prob_ref[0, k:k + 1, :] for k in range(TOP_K)]
    lane = lax.broadcasted_iota(jnp.int32, (SORT_CHUNK, LANES), 1)
    for r0 in range(0, LOCAL_ROWS, SORT_CHUNK):
        row = r0 + lax.broadcasted_iota(jnp.int32, (SORT_CHUNK, tt), 0)
        w = jnp.zeros((SORT_CHUNK, tt), F32)
        for k in range(TOP_K):
            w = jnp.where(row == slots[k], probs[k], w)
        onehot = jnp.where(w != 0.0, 1.0, 0.0).astype(BF16)
        local_ref[r0:r0 + SORT_CHUNK, 0:d] = _dot(onehot, u2).astype(BF16)
        wr = jnp.sum(w, axis=1, keepdims=True)
        hi = wr.astype(BF16).astype(F32)
        mid = (wr - hi).astype(BF16).astype(F32)
        lo = wr - hi - mid
        parts = jnp.where(lane == 0, hi, jnp.where(lane == 1, mid, jnp.where(lane == 2, lo, 0.0)))
        local_ref[r0:r0 + SORT_CHUNK, d:d + LANES] = parts.astype(BF16)

    def start_piece(loc, dst, rows):
        pltpu.make_async_copy(local_ref.at[pl.ds(loc, rows)], xs_hbm.at[pl.ds(dst, rows)], sem).start()

    rows_ref[buf] = _for_each_piece(tile, (big_loc, big_dst, small_loc, small_dst, piece_count), start_piece)

    @pl.when(tile > 0)
    def _():
        _wait_rows(local2_ref.at[1 - buf], xs_hbm, rows_ref[1 - buf], sem2.at[1 - buf])

    def zero_fill(first_row, n_rows, act):
        def copy(row, rows):
            return pltpu.make_async_copy(zero_ref.at[pl.ds(0, rows)],
                                         xs_hbm.at[pl.ds(pl.multiple_of(row, ROW_ALIGN), rows)], zsem)
        n_big = n_rows // ZERO_ROWS
        rest = first_row + n_big * ZERO_ROWS
        lax.fori_loop(0, n_big, lambda p, c: (act(copy(first_row + p * ZERO_ROWS, ZERO_ROWS)), c)[1], 0)
        lax.fori_loop(0, (n_rows - n_big * ZERO_ROWS) // ROW_ALIGN,
                      lambda p, c: (act(copy(rest + p * ROW_ALIGN, ROW_ALIGN)), c)[1], 0)

    def zero_fill_all(act):
        first_spare = nused_ref[0] * MOE_BLOCK
        lax.fori_loop(0, N_EXPERTS, lambda e, c: (zero_fill(tail_ref[e], ntail_ref[e], act), c)[1], 0)
        zero_fill(first_spare, xs_hbm.shape[0] - first_spare, act)

    @pl.when(tile == 0)
    def _():
        zero_ref[...] = jnp.zeros_like(zero_ref)
        zero_fill_all(lambda c: c.start())

    @pl.when(tile == pl.num_programs(0) - 1)
    def _():
        _wait_rows(local_ref, xs_hbm, rows_ref[buf], sem)
        zero_fill_all(lambda c: c.wait())


def _dispatch(tables, slot, prob, u2, n_rows):
    N, D = u2.shape
    tt = POST_TILE
    route_spec = pl.BlockSpec((1, TOP_K, tt), lambda i, *t: (i, 0, 0))
    grid_spec = pltpu.PrefetchScalarGridSpec(
        num_scalar_prefetch=len(tables),
        grid=(N // tt,),
        in_specs=[route_spec, route_spec, pl.BlockSpec((tt, D), lambda i, *t: (i, 0))],
        out_specs=pl.BlockSpec(memory_space=pl.ANY),
        scratch_shapes=[pltpu.VMEM((2, LOCAL_ROWS, D + LANES), BF16), pltpu.VMEM((ZERO_ROWS, D + LANES), BF16),
                        pltpu.SemaphoreType.DMA((2,)), pltpu.SemaphoreType.DMA(()),
                        pltpu.SMEM((2,), jnp.int32)],
    )
    return pl.pallas_call(
        _dispatch_kernel,
        grid_spec=grid_spec,
        out_shape=jax.ShapeDtypeStruct((n_rows, D + LANES), BF16),
        compiler_params=pltpu.CompilerParams(
            dimension_semantics=("arbitrary",), vmem_limit_bytes=VMEM_LIMIT_BYTES),
        name="dispatch",
    )(*tables, slot, prob, u2)


def _expert_kernel(be_ref, next_ref, live_ref, nused_ref, xs_ref, wg_hbm, wu_hbm, wd_hbm, bias_ref,
                   ys_ref, stage, wg_s, wu_s, wd_s, sem):
    i = pl.program_id(0)
    used = i < nused_ref[0]
    prev = be_ref[jnp.maximum(i - 1, 0)]
    fresh = (i == 0) | (be_ref[i] != prev)

    def fetch(e):
        return [pltpu.make_async_copy(w_hbm.at[e], stage.at[j], sem.at[j])
                for j, w_hbm in enumerate((wg_hbm, wu_hbm, wd_hbm))]

    @pl.when(i == 0)
    def _():
        for copy in fetch(be_ref[0]):
            copy.start()

    d = ys_ref.shape[1]
    full = MOE_BLOCK // EXPERT_PART

    def arrive():
        for copy in fetch(be_ref[i]):
            copy.wait()

    def prefetch():
        @pl.when(next_ref[i] >= 0)
        def _():
            for copy in fetch(next_ref[i]):
                copy.start()

    def expert(rows, round_weights=False):
        xb = xs_ref[0:rows, 0:d]
        parts = xs_ref[0:rows, d:d + LANES].astype(F32)
        weight = parts[:, 0:1] + parts[:, 1:2] + parts[:, 2:3]
        if round_weights:
            wg_s[...] = stage[0].astype(BF16)
            wu_s[...] = stage[1].astype(BF16)
        g = _dot(xb, wg_s[...]) + bias_ref[0, 0:1, :]
        up = _dot(xb, wu_s[...]) + bias_ref[0, 1:2, :]
        if round_weights:
            wd_s[...] = stage[2].astype(BF16)
        g = jnp.minimum(g, SWIGLU_LIMIT)
        up = jnp.clip(up, -SWIGLU_LIMIT, SWIGLU_LIMIT)
        h = g * jax.nn.sigmoid(SWIGLU_ALPHA * g) * (up + 1.0)
        return ((_dot(h.astype(BF16), wd_s[...]) + bias_ref[0, 2:3, :]) * weight).astype(BF16)

    @pl.when(used & fresh & (live_ref[i] == full))
    def _():
        arrive()
        ys_ref[...] = expert(MOE_BLOCK, round_weights=True)
        prefetch()

    @pl.when(used & fresh & (live_ref[i] != full))
    def _():
        arrive()
        wg_s[...] = stage[0].astype(BF16)
        wu_s[...] = stage[1].astype(BF16)
        wd_s[...] = stage[2].astype(BF16)
        prefetch()

    for parts_live in range(1, full + 1):
        rows = parts_live * EXPERT_PART
        done_above = fresh if parts_live == full else False

        @pl.when(used & (live_ref[i] == parts_live) & jnp.logical_not(done_above))
        def _():
            if rows == MOE_BLOCK:
                ys_ref[...] = expert(rows)
            else:
                ys_ref[0:rows, :] = expert(rows)
                ys_ref[rows:, :] = jnp.zeros((MOE_BLOCK - rows, d), BF16)

    @pl.when(jnp.logical_not(used))
    def _():
        ys_ref[...] = jnp.zeros_like(ys_ref)


def _experts(be, next_expert, live_parts, nused, xs, w_gate, b_gate, w_up, b_up, w_down, b_down):
    P, width = xs.shape
    D = width - LANES
    E = w_gate.shape[0]

    def live(i, nu):
        return jnp.maximum(jnp.minimum(i, nu[0] - 1), 0)

    w_spec = pl.BlockSpec(memory_space=pl.ANY)
    b_spec = pl.BlockSpec((1, 3, D), lambda i, be, nx, hf, nu: (be[live(i, nu)], 0, 0))
    grid_spec = pltpu.PrefetchScalarGridSpec(
        num_scalar_prefetch=4,
        grid=(P // MOE_BLOCK,),
        in_specs=[pl.BlockSpec((MOE_BLOCK, width), lambda i, be, nx, hf, nu: (live(i, nu), 0)),
                  w_spec, w_spec, w_spec, b_spec],
        out_specs=pl.BlockSpec((MOE_BLOCK, D), lambda i, be, nx, hf, nu: (i, 0)),
        scratch_shapes=[pltpu.VMEM((3, D, D), F32)] + [pltpu.VMEM((D, D), BF16)] * 3
                       + [pltpu.SemaphoreType.DMA((3,))],
    )
    return pl.pallas_call(
        _expert_kernel,
        grid_spec=grid_spec,
        out_shape=jax.ShapeDtypeStruct((P, D), BF16),
        compiler_params=pltpu.CompilerParams(
            dimension_semantics=("arbitrary",), vmem_limit_bytes=VMEM_LIMIT_BYTES),
        name="experts",
    )(be, next_expert, live_parts, nused, xs, w_gate, w_up, w_down,
      jnp.stack([b_gate, b_up, b_down], axis=1))


def _combine_kernel(big_loc, big_dst, small_loc, small_dst, piece_count,
                    slot_ref, x1_ref, ada_ref, ln2_ref, ys_hbm,
                    out_ref, local2_ref, sem2, rows_ref):
    tile = pl.program_id(0)
    buf = tile % 2
    tt = x1_ref.shape[0]

    def fetch(t, b):
        def start_piece(loc, dst, rows):
            pltpu.make_async_copy(ys_hbm.at[pl.ds(dst, rows)], local2_ref.at[b, pl.ds(loc, rows)],
                                  sem2.at[b]).start()
        rows_ref[b] = _for_each_piece(t, (big_loc, big_dst, small_loc, small_dst, piece_count), start_piece)

    @pl.when(tile == 0)
    def _():
        local2_ref[...] = jnp.zeros_like(local2_ref)
        fetch(0, 0)

    @pl.when(tile + 1 < pl.num_programs(0))
    def _():
        fetch(tile + 1, 1 - buf)

    local_ref = local2_ref.at[buf]
    _wait_rows(ys_hbm, local_ref, rows_ref[buf], sem2.at[buf])
    slots = [slot_ref[:, k:k + 1] for k in range(TOP_K)]

    ffn = jnp.zeros((tt, x1_ref.shape[1]), F32)
    for r0 in range(0, LOCAL_ROWS, SORT_CHUNK):
        col = r0 + lax.broadcasted_iota(jnp.int32, (tt, SORT_CHUNK), 1)
        onehot = jnp.zeros((tt, SORT_CHUNK), F32)
        for k in range(TOP_K):
            onehot = jnp.where(col == slots[k], 1.0, onehot)
        ffn = ffn + _dot(onehot.astype(BF16), local_ref[r0:r0 + SORT_CHUNK, :])
    gate2 = ada_ref[0, 5:6, :]
    y = DN_ALPHA * x1_ref[...] + (1.0 + gate2) * ffn
    out_ref[...] = _layer_norm(y) * ln2_ref[0:1, :] + ln2_ref[1:2, :]


def _combine(tables, slot_tok, x1, ada, ln2_g, ln2_b, ys, tiles_per_batch):
    N, D = x1.shape
    tt = POST_TILE
    const = lambda i, *t: (0, 0)
    tok4 = pl.BlockSpec((tt, TOP_K), lambda i, *t: (i, 0))
    grid_spec = pltpu.PrefetchScalarGridSpec(
        num_scalar_prefetch=len(tables),
        grid=(N // tt,),
        in_specs=[tok4,
                  pl.BlockSpec((tt, D), lambda i, *t: (i, 0)),
                  pl.BlockSpec((1, 6, D), lambda i, *t: (i // tiles_per_batch, 0, 0)),
                  pl.BlockSpec((2, D), const),
                  pl.BlockSpec(memory_space=pl.ANY)],
        out_specs=pl.BlockSpec((tt, D), lambda i, *t: (i, 0)),
        scratch_shapes=[pltpu.VMEM((2, LOCAL_ROWS, D), BF16), pltpu.SemaphoreType.DMA((2,)),
                        pltpu.SMEM((2,), jnp.int32)],
    )
    return pl.pallas_call(
        _combine_kernel,
        grid_spec=grid_spec,
        out_shape=jax.ShapeDtypeStruct((N, D), F32),
        compiler_params=pltpu.CompilerParams(
            dimension_semantics=("arbitrary",), vmem_limit_bytes=VMEM_LIMIT_BYTES),
        name="combine",
    )(*tables, slot_tok, x1, ada, jnp.stack([ln2_g, ln2_b]), ys)


def _routing_tables(cnt, nblk):
    nt, E = cnt.shape
    i32 = jnp.int32
    run = (cnt + ROW_ALIGN - 1) // ROW_ALIGN * ROW_ALIGN
    run_start = jnp.cumsum(run, axis=1) - run
    seg_len = jnp.sum(run, axis=0)
    seg_blocks = (seg_len + MOE_BLOCK - 1) // MOE_BLOCK
    b_end = jnp.cumsum(seg_blocks)
    seg_off = (b_end - seg_blocks) * MOE_BLOCK
    dest = seg_off[None, :] + jnp.cumsum(run, axis=0) - run
    nused = b_end[-1:].astype(i32)
    i = jnp.arange(nblk, dtype=i32)
    be = jnp.minimum(jnp.sum((i[:, None] >= b_end[None, :]).astype(i32), axis=1), E - 1)
    e_ids = jnp.arange(E, dtype=i32)
    later = (e_ids[None, :] > e_ids[:, None]) & (seg_blocks > 0)[None, :]
    next_e = jnp.min(jnp.where(later, e_ids[None, :], E), axis=1)
    next_e = jnp.where(next_e < E, next_e, -1)
    mine = be[:, None] == e_ids[None, :]
    pick = lambda v: jnp.sum(jnp.where(mine, v[None, :], 0), axis=1)
    nxt = pick(next_e)
    rows_left = pick(seg_len) - (i - pick(b_end - seg_blocks)) * MOE_BLOCK
    live_parts = jnp.clip((rows_left + EXPERT_PART - 1) // EXPERT_PART, 1, MOE_BLOCK // EXPERT_PART).astype(i32)
    tail = seg_off + seg_len
    ntail = seg_blocks * MOE_BLOCK - seg_len
    def flat_pieces(n, first_loc, first_dst, rows, cap):
        end = jnp.cumsum(n, axis=1)
        start = end - n
        p = jnp.arange(cap, dtype=i32)[None, :, None]
        off = p - start[:, None, :]
        owner = (off >= 0) & (p < end[:, None, :])
        loc = jnp.sum(jnp.where(owner, first_loc[:, None, :] + rows * off, 0), axis=2)
        dst = jnp.sum(jnp.where(owner, first_dst[:, None, :] + rows * off, 0), axis=2)
        return loc.reshape(-1).astype(i32), dst.reshape(-1).astype(i32), end[:, -1]

    n_big = run // PIECE_ROWS
    big_rows = n_big * PIECE_ROWS
    big_loc, big_dst, big_count = flat_pieces(n_big, run_start, dest, PIECE_ROWS, MAX_BIG_PIECES)
    small_loc, small_dst, small_count = flat_pieces((run - big_rows) // ROW_ALIGN, run_start + big_rows,
                                                    dest + big_rows, ROW_ALIGN, MAX_SMALL_PIECES)
    piece_count = jnp.stack([big_count, small_count], axis=1).reshape(-1).astype(i32)
    piece_tables = (big_loc, big_dst, small_loc, small_dst, piece_count)
    return piece_tables, (tail.astype(i32), ntail.astype(i32)), (be.astype(i32), nxt.astype(i32), live_parts, nused)


def kernel(x, c, positions, w_ada, b_ada, w_in, pool_w, pool_scale, w_pool_out, w_attn_out, w_o,
           ln1_g, ln1_b, w_router, b_router, w_gate, b_gate, w_up, b_up, w_down, b_down, ln2_g, ln2_b):
    B, S, D = x.shape
    N = B * S
    assert D == D_MODEL and S % PROJ_TILE == 0 and S % (16 * ATTN_BLOCK) == 0
    assert S % POST_TILE == 0 and LOCAL_ROWS % SORT_CHUNK == 0
    nt = N // POST_TILE
    nblk = (N * TOP_K + nt * N_EXPERTS * (ROW_ALIGN - 1)) // MOE_BLOCK + N_EXPERTS
    for l in range(DEPTH):
        ada = _ada(c, w_ada[l], b_ada[l])
        proj_out = _proj(x, positions, ada, w_in[l], pool_w[l], pool_scale[l], w_pool_out[l])
        qkv, (pg, sga) = proj_out[:9], proj_out[9:]
        attn_outs = [_attention(*qkv[3 * g:3 * g + 3]) for g in range(len(ATTN_GROUPS))]
        x1, u2, slot, prob, cnt = _post(attn_outs, pg, sga, x, ada, w_attn_out[l], w_o[l],
                                        ln1_g[l], ln1_b[l], w_router[l], b_router[l])
        piece_tables, tail_tables, block_tables = _routing_tables(cnt.reshape(nt, N_EXPERTS), nblk)
        xs = _dispatch(piece_tables + tail_tables + block_tables[3:], slot, prob, u2.reshape(N, D),
                       nblk * MOE_BLOCK)
        ys = _experts(*block_tables, xs, w_gate[l], b_gate[l], w_up[l], b_up[l], w_down[l], b_down[l])
        slot_tok = slot.transpose(0, 2, 1).reshape(N, TOP_K)
        out = _combine(piece_tables, slot_tok, x1.reshape(N, D), ada,
                       ln2_g[l], ln2_b[l], ys, S // POST_TILE)
        x = out.reshape(B, S, D)
    return x
```

```python
import jax
import jax.numpy as jnp
import numpy as np
from jax import lax
from jax.experimental import pallas as pl
from jax.experimental.pallas import tpu as pltpu

F32 = jnp.float32
BF16 = jnp.bfloat16

D_MODEL = 1024
POOL_WINDOWS = (2, 4, 8, 16)
POOL_WIDTH = D_MODEL // 2
POOL_GROUP = POOL_WIDTH // len(POOL_WINDOWS)
POOL_HALO = 16
HEAD_DIM = 64
ATTN_GROUPS = ((128, 1), (512, 4), (2048, 16))
HEADS_PER_GROUP = 4
GROUP_WIDTH = HEADS_PER_GROUP * HEAD_DIM
N_HEADS = HEADS_PER_GROUP * len(ATTN_GROUPS)
ATTN_WIDTH = N_HEADS * HEAD_DIM
ATTN_BLOCK = 128
ROT_DIM = HEAD_DIM // 4
ROPE_THETA = 500000.0
N_EXPERTS = 32
TOP_K = 4
SWIGLU_ALPHA = 1.702
SWIGLU_LIMIT = 7.0
MOE_BLOCK = 1024
EXPERT_PART = 128
DEPTH = 1
DN_ALPHA = (2.0 * DEPTH) ** 0.25
LN_EPS = 1e-5
NEG_INF = -1e30

OFF_Q = POOL_WIDTH
OFF_K = OFF_Q + ATTN_WIDTH
OFF_V = OFF_K + ATTN_WIDTH
OFF_GP = OFF_V + ATTN_WIDTH
OFF_GA = OFF_GP + D_MODEL
IN_WIDTH = OFF_GA + D_MODEL

VMEM_LIMIT_BYTES = 56 * 1024 * 1024
LANES = 128
SUBLANES = 8

PROJ_TILE = 512
POST_TILE = 512
POST_GROUPS = 4
ATTN_QROWS = 1024
ROW_ALIGN = 16
PIECE_ROWS = 64
ZERO_ROWS = 128
SORT_CHUNK = 512
LOCAL_ROWS = -(-(POST_TILE * TOP_K + N_EXPERTS * (ROW_ALIGN - 1)) // SORT_CHUNK) * SORT_CHUNK
MAX_BIG_PIECES = LOCAL_ROWS // PIECE_ROWS
MAX_SMALL_PIECES = N_EXPERTS * (PIECE_ROWS // ROW_ALIGN - 1)


def _layer_norm(x):
    mu = jnp.mean(x, axis=-1, keepdims=True)
    xc = x - mu
    var = jnp.mean(xc * xc, axis=-1, keepdims=True)
    return xc * lax.rsqrt(var + LN_EPS)


def _dot(a, b):
    return jnp.dot(a, b, preferred_element_type=F32)


def _ada_kernel(c_ref, w_ref, b_ref, o_ref):
    c = c_ref[...]
    s = c * jax.nn.sigmoid(c)
    o_ref[...] = jnp.dot(s, w_ref[...], preferred_element_type=F32,
                         precision=lax.Precision.HIGHEST) + b_ref[...]


def _ada(c, w_ada, b_ada):
    B, D = c.shape
    rows = SUBLANES
    c_pad = jnp.pad(c, ((0, rows - B), (0, 0)))
    n_out = w_ada.shape[1]
    out = pl.pallas_call(
        _ada_kernel,
        grid=(n_out // D,),
        in_specs=[pl.BlockSpec((rows, D), lambda j: (0, 0)),
                  pl.BlockSpec((D, D), lambda j: (0, j)),
                  pl.BlockSpec((1, D), lambda j: (0, j))],
        out_specs=pl.BlockSpec((rows, D), lambda j: (0, j)),
        out_shape=jax.ShapeDtypeStruct((rows, n_out), F32),
        name="ada",
    )(c_pad, w_ada, b_ada.reshape(1, n_out))
    return out[:B].reshape(B, 6, D)


ROPE_PART_ROWS = 32


def _rope_tables():
    lane = np.arange(LANES)
    li = lane % HEAD_DIM
    half = ROT_DIM // 2
    inv_freq = jnp.power(ROPE_THETA, -jnp.arange(half, dtype=F32) * (2.0 / ROT_DIM))
    invf = jnp.broadcast_to(inv_freq[:, None], (half, LANES))
    freq = np.arange(ROPE_PART_ROWS)[:, None]
    live = (freq < 3 * half) & (freq % half == (li % half)[None, :])
    place = np.stack([live & (li < ROT_DIM)[None, :],
                      -1.0 * (live & (li < half)[None, :]),
                      live & ((li >= half) & (li < ROT_DIM))[None, :]]).astype(np.float32)
    keep = (li >= ROT_DIM).astype(np.float32)[None, :]
    return invf, jnp.asarray(place, BF16), jnp.asarray(keep)


def _proj_kernel(x_ref, xh_ref, pos_ref, ada_ref, invf_ref, place_ref, keep_ref,
                 win_ref, poolw_ref, pscale_ref, wpo_ref,
                 q1_ref, k1_ref, v1_ref, q4_ref, k4_ref, v4_ref, q16_ref, k16_ref, v16_ref,
                 pg_ref, sga_ref, xpe_ref, cls_ref):
    tm = x_ref.shape[1]
    i = pl.program_id(1)
    shift1 = ada_ref[0, 0:1, :]
    scale1 = ada_ref[0, 1:2, :]

    def modulated(xv):
        return (_layer_norm(xv) * (1.0 + scale1) + shift1).astype(BF16)

    u = modulated(x_ref[0])
    uh = modulated(xh_ref[0])

    xp = _dot(u, win_ref[:, 0:POOL_WIDTH])
    xph = _dot(uh, win_ref[:, 0:POOL_WIDTH])
    xph = jnp.where(i > 0, xph, 0.0)
    xpe_ref[0:POOL_HALO, :] = xph
    xpe_ref[POOL_HALO:, :] = xp
    tok = i * tm + lax.broadcasted_iota(jnp.int32, (tm, 1), 0)

    def pooled_branch():
        ys = []
        for g, w in enumerate(POOL_WINDOWS):
            cols = slice(g * POOL_GROUP, (g + 1) * POOL_GROUP)
            xg = xpe_ref[POOL_HALO:, cols]
            acc = xg
            for j in range(1, w):
                acc = acc + xpe_ref[POOL_HALO - j:POOL_HALO - j + tm, cols]
            cnt = jnp.minimum(tok + 1, w).astype(F32)
            mixed = (acc / cnt - xg).astype(BF16)
            ys.append(_dot(mixed, poolw_ref[g]) * pscale_ref[:, cols])
        y = jnp.concatenate(ys, axis=1).astype(BF16)
        return _dot(y, wpo_ref[...])

    ang = invf_ref[:, 0:1] * pos_ref[0, 0].astype(F32)

    def on_lanes(table, j):
        hi = table.astype(BF16).astype(F32)
        mid = (table - hi).astype(BF16).astype(F32)
        lo = table - hi - mid
        pad = jnp.zeros((ROPE_PART_ROWS - 3 * table.shape[0], tm), F32)
        parts = jnp.concatenate([hi, mid, lo, pad], axis=0).astype(BF16)
        return lax.dot_general(parts, place_ref[j], (((0,), (0,)), ((), ())), preferred_element_type=F32)

    cos = jnp.cos(ang)
    sin = jnp.sin(ang)
    c_mul = on_lanes(cos, 0) + keep_ref[...]
    s_lo = on_lanes(sin, 1)
    s_hi = on_lanes(sin, 2)
    c_mul = jnp.concatenate([c_mul, c_mul], axis=1)
    s_lo = jnp.concatenate([s_lo, s_lo], axis=1)
    s_hi = jnp.concatenate([s_hi, s_hi], axis=1)
    half = ROT_DIM // 2

    def rotate(a):
        up = pltpu.roll(a, GROUP_WIDTH - half, axis=1)
        dn = pltpu.roll(a, half, axis=1)
        return a * c_mul + up * s_lo + dn * s_hi

    def emit(a, out_ref, dil):
        if dil == 1:
            out_ref[0, 0] = a.astype(BF16)
            return
        for c in range(GROUP_WIDTH // LANES):
            cls_ref[c] = a[:, c * LANES:(c + 1) * LANES]
        for r in range(dil):
            for c in range(GROUP_WIDTH // LANES):
                out_ref[0, r, :, c * LANES:(c + 1) * LANES] = (
                    cls_ref[c, pl.ds(r, tm // dil, stride=dil), :].astype(BF16))

    outs = ((q1_ref, k1_ref, v1_ref), (q4_ref, k4_ref, v4_ref), (q16_ref, k16_ref, v16_ref))

    def attention_group(gi):
        qo, ko, vo = outs[gi]
        dil = ATTN_GROUPS[gi][1]
        c0 = gi * GROUP_WIDTH
        emit(rotate(_dot(u, win_ref[:, OFF_Q + c0:OFF_Q + c0 + GROUP_WIDTH])), qo, dil)
        emit(rotate(_dot(u, win_ref[:, OFF_K + c0:OFF_K + c0 + GROUP_WIDTH])), ko, dil)
        emit(_dot(u, win_ref[:, OFF_V + c0:OFF_V + c0 + GROUP_WIDTH]), vo, dil)

    attention_group(2)
    pooled = pooled_branch()
    attention_group(1)
    g_p = _dot(u, win_ref[:, OFF_GP:OFF_GP + D_MODEL])
    pg_ref[0] = jax.nn.sigmoid(g_p) * pooled
    attention_group(0)
    g_a = _dot(u, win_ref[:, OFF_GA:OFF_GA + D_MODEL])
    sga_ref[0] = jax.nn.sigmoid(g_a)


def _proj(x, positions, ada, w_in, pool_w, pool_scale, w_pool_out):
    B, S, D = x.shape
    tm = PROJ_TILE
    nt = S // tm
    halo_blocks = tm // POOL_HALO
    const2 = lambda b, i: (0, 0)
    in_specs = [
        pl.BlockSpec((1, tm, D), lambda b, i: (b, i, 0)),
        pl.BlockSpec((1, POOL_HALO, D), lambda b, i: (b, jnp.maximum(i * halo_blocks - 1, 0), 0)),
        pl.BlockSpec((1, 1, 1, tm), lambda b, i: (b, i, 0, 0)),
        pl.BlockSpec((1, 6, D), lambda b, i: (b, 0, 0)),
        pl.BlockSpec((ROT_DIM // 2, LANES), const2),
        pl.BlockSpec((3, ROPE_PART_ROWS, LANES), lambda b, i: (0, 0, 0)),
        pl.BlockSpec((1, LANES), const2),
        pl.BlockSpec((D, IN_WIDTH), const2),
        pl.BlockSpec((len(POOL_WINDOWS), POOL_GROUP, POOL_GROUP), lambda b, i: (0, 0, 0)),
        pl.BlockSpec((1, POOL_WIDTH), const2),
        pl.BlockSpec((POOL_WIDTH, D), const2),
    ]
    out_specs, out_shapes = [], []
    for _, dil in ATTN_GROUPS:
        for _ in range(3):
            out_specs.append(pl.BlockSpec((1, dil, tm // dil, GROUP_WIDTH), lambda b, i: (b, 0, i, 0)))
            out_shapes.append(jax.ShapeDtypeStruct((B, dil, S // dil, GROUP_WIDTH), BF16))
    for _ in range(2):
        out_specs.append(pl.BlockSpec((1, tm, D), lambda b, i: (b, i, 0)))
        out_shapes.append(jax.ShapeDtypeStruct((B, S, D), F32))
    return pl.pallas_call(
        _proj_kernel,
        grid=(B, nt),
        in_specs=in_specs,
        out_specs=out_specs,
        out_shape=out_shapes,
        scratch_shapes=[pltpu.VMEM((tm + POOL_HALO, POOL_WIDTH), F32),
                        pltpu.VMEM((GROUP_WIDTH // LANES, tm, LANES), F32)],
        compiler_params=pltpu.CompilerParams(
            dimension_semantics=("parallel", "parallel"), vmem_limit_bytes=VMEM_LIMIT_BYTES),
        name="proj",
    )(x, x, positions.reshape(B, nt, 1, tm), ada, *_rope_tables(), w_in.astype(BF16),
      pool_w.astype(BF16), pool_scale.reshape(1, POOL_WIDTH), w_pool_out.astype(BF16))


def _attn_kernel(q_ref, k_ref, v_ref, kh_ref, vh_ref, o_ref, lse_ref, kf_ref, vf_ref):
    n_cls, qb = q_ref.shape[1], q_ref.shape[2]
    per = qb // ATTN_BLOCK
    n = pl.program_id(2)
    for c in range(n_cls):
        kf_ref[c, 0:ATTN_BLOCK, :] = kh_ref[0, c]
        kf_ref[c, ATTN_BLOCK:, :] = k_ref[0, c]
        vf_ref[c, 0:ATTN_BLOCK, :] = vh_ref[0, c]
        vf_ref[c, ATTN_BLOCK:, :] = v_ref[0, c]
    qi = lax.broadcasted_iota(jnp.int32, (ATTN_BLOCK, 2 * ATTN_BLOCK), 0)
    kj = lax.broadcasted_iota(jnp.int32, (ATTN_BLOCK, 2 * ATTN_BLOCK), 1)
    band = (kj >= qi) & (kj <= qi + ATTN_BLOCK)
    band_bias = jnp.where(band, 0.0, NEG_INF)
    lane = lax.broadcasted_iota(jnp.int32, (ATTN_BLOCK, GROUP_WIDTH), 1)
    low_lanes = lax.broadcasted_iota(jnp.int32, (ATTN_BLOCK, LANES), 1) < HEAD_DIM
    ones = jnp.ones((2 * ATTN_BLOCK, LANES), BF16)
    nh, blk = HEADS_PER_GROUP, ATTN_BLOCK

    def block(t, carry):
        c, j = t // per, t % per
        r0 = pl.multiple_of(j * ATTN_BLOCK, ATTN_BLOCK)
        first_key = jnp.where((n > 0) | (j > 0), 0, ATTN_BLOCK)
        bias = band_bias + jnp.where(kj < first_key, NEG_INF, 0.0)
        q = q_ref[0, c, pl.ds(r0, blk), :].astype(F32)
        kk = kf_ref[c, pl.ds(r0, 2 * blk), :]
        vv = vf_ref[c, pl.ds(r0, 2 * blk), :]
        qs = jnp.concatenate([jnp.where((lane >= h * HEAD_DIM) & (lane < (h + 1) * HEAD_DIM), q, 0.0)
                              for h in range(nh)], axis=0).astype(BF16)
        s = lax.dot_general(qs, kk, (((1,), (1,)), ((), ())), preferred_element_type=F32)
        s = jnp.concatenate([s[h * blk:(h + 1) * blk] * (HEAD_DIM ** -0.5) + bias for h in range(nh)], axis=0)
        m = jnp.max(s, axis=-1, keepdims=True)
        p = jnp.exp(s - m).astype(BF16)
        den = _dot(p, ones)
        lse = m + jnp.log(den)
        for hp in range(GROUP_WIDTH // LANES):
            rows = slice(2 * hp * blk, (2 * hp + 2) * blk)
            ls = slice(hp * LANES, (hp + 1) * LANES)
            o2 = _dot(p[rows], vv[:, ls]) / den[rows]
            l2 = lse[rows]
            o_ref[0, c, pl.ds(r0, blk), ls] = jnp.where(low_lanes, o2[0:blk], o2[blk:2 * blk])
            lse_ref[0, c, pl.ds(r0, blk), ls] = jnp.where(low_lanes, l2[0:blk], l2[blk:2 * blk])
        return carry

    lax.fori_loop(0, n_cls * per, block, 0, unroll=8)


def _attention(q, k, v):
    B, dil, L, W = q.shape
    qb = min(L, ATTN_QROWS)
    per = qb // ATTN_BLOCK
    n_cls = min(ATTN_QROWS // qb, dil)
    main = pl.BlockSpec((1, n_cls, qb, W), lambda b, r, n: (b, r, n, 0))
    halo = pl.BlockSpec((1, n_cls, ATTN_BLOCK, W), lambda b, r, n: (b, r, jnp.maximum(n * per - 1, 0), 0))
    return pl.pallas_call(
        _attn_kernel,
        grid=(B, dil // n_cls, L // qb),
        in_specs=[main, main, main, halo, halo],
        out_specs=[main, main],
        out_shape=[jax.ShapeDtypeStruct((B, dil, L, W), F32)] * 2,
        scratch_shapes=[pltpu.VMEM((n_cls, qb + ATTN_BLOCK, W), BF16)] * 2,
        compiler_params=pltpu.CompilerParams(
            dimension_semantics=("parallel", "parallel", "parallel"), vmem_limit_bytes=VMEM_LIMIT_BYTES),
        name=f"attn_d{dil}",
    )(q, k, v, k, v)


def _post_kernel(o1_ref, l1_ref, o4_ref, l4_ref, o16_ref, l16_ref, pg_ref, sga_ref, x_ref, ada_ref,
                 wao_ref, wo_ref, ln1_ref, wrt_ref, brt_ref,
                 x1_ref, u2_ref, slot_ref, prob_ref, cnt_ref,
                 s0, s1, s2, s3):
    tm = x_ref.shape[1]

    def token_major(src_ref, scr_ref, dil):
        if dil == 1:
            return src_ref[0, 0]
        for r in range(dil):
            for c in range(GROUP_WIDTH // LANES):
                scr_ref[c, pl.ds(r, tm // dil, stride=dil), :] = src_ref[0, r, :, c * LANES:(c + 1) * LANES]
        return jnp.concatenate([scr_ref[c] for c in range(GROUP_WIDTH // LANES)], axis=1)

    o1, l1 = o1_ref[0, 0], l1_ref[0, 0]
    o4, l4 = token_major(o4_ref, s0, 4), token_major(l4_ref, s1, 4)
    o16, l16 = token_major(o16_ref, s2, 16), token_major(l16_ref, s3, 16)
    gate1 = ada_ref[0, 2:3, :]
    shift2 = ada_ref[0, 3:4, :]
    scale2 = ada_ref[0, 4:5, :]
    nt_dot = lambda a, b: lax.dot_general(a, b, (((1,), (1,)), ((), ())), preferred_element_type=F32)
    groups = [slice(h * (tm // POST_GROUPS), (h + 1) * (tm // POST_GROUPS)) for h in range(POST_GROUPS)]

    def merge(rows):
        a1, a4, a16 = l1[rows], l4[rows], l16[rows]
        mx = jnp.maximum(jnp.maximum(a1, a4), a16)
        e1, e4, e16 = jnp.exp(a1 - mx), jnp.exp(a4 - mx), jnp.exp(a16 - mx)
        return ((e1 * o1[rows] + e4 * o4[rows] + e16 * o16[rows]) / (e1 + e4 + e16)).astype(BF16)

    def project(rows, attn):
        merged = pg_ref[0, rows, :] + sga_ref[0, rows, :] * _dot(attn, wao_ref[...])
        return _dot(merged.astype(BF16), wo_ref[...])

    def norms(rows, mix):
        x1 = (_layer_norm(DN_ALPHA * x_ref[0, rows, :] + (1.0 + gate1) * mix) * ln1_ref[0:1, :]
              + ln1_ref[1:2, :])
        x1_ref[0, rows, :] = x1
        u2 = _layer_norm(x1) * (1.0 + scale2) + shift2
        u2_hi = u2.astype(BF16)
        u2_ref[0, rows, :] = u2_hi
        return u2_hi, (u2 - u2_hi.astype(F32)).astype(BF16)

    def route(u2_hi, u2_lo):
        return nt_dot(wrt_ref[0], u2_hi) + nt_dot(wrt_ref[0], u2_lo) + nt_dot(wrt_ref[1], u2_hi)

    attns = [merge(rows) for rows in groups]
    mixes = [project(rows, attn) for rows, attn in zip(groups, attns)]
    u2s = [norms(rows, mix) for rows, mix in zip(groups, mixes)]
    logits = jnp.concatenate([route(*u2) for u2 in u2s], axis=1) + brt_ref[...]
    eidx = lax.broadcasted_iota(jnp.int32, (N_EXPERTS, tm), 0)
    work = logits
    vals, idxs = [], []
    for _ in range(TOP_K):
        m = jnp.max(work, axis=0, keepdims=True)
        idx = jnp.min(jnp.where(work == m, eidx, N_EXPERTS), axis=0, keepdims=True)
        vals.append(m)
        idxs.append(idx)
        work = jnp.where(eidx == idx, -jnp.inf, work)
    exps = [jnp.exp(vk - vals[0]) for vk in vals]
    tot = exps[0] + exps[1] + exps[2] + exps[3]
    sel = jnp.zeros((N_EXPERTS, tm), F32)
    for idx in idxs:
        sel = sel + (eidx == idx).astype(F32)
    tr = lax.broadcasted_iota(jnp.int32, (tm, tm), 0)
    tc = lax.broadcasted_iota(jnp.int32, (tm, tm), 1)
    rank = _dot(sel.astype(BF16), (tr < tc).astype(BF16))
    cnt = jnp.sum(sel, axis=1, keepdims=True)
    run = jnp.floor((cnt + (ROW_ALIGN - 1)) * (1.0 / ROW_ALIGN)) * ROW_ALIGN
    er = lax.broadcasted_iota(jnp.int32, (N_EXPERTS, N_EXPERTS), 0)
    ec = lax.broadcasted_iota(jnp.int32, (N_EXPERTS, N_EXPERTS), 1)
    run_start = _dot((ec < er).astype(BF16),
                     jnp.broadcast_to(run, (N_EXPERTS, LANES)).astype(BF16))[:, 0:1]
    slot = rank + run_start
    for k in range(TOP_K):
        slot_ref[0, k:k + 1, :] = jnp.sum(jnp.where(eidx == idxs[k], slot, 0.0), axis=0,
                                          keepdims=True).astype(jnp.int32)
        prob_ref[0, k:k + 1, :] = exps[k] / tot
    cnt_ref[0] = cnt.astype(jnp.int32)


def _post(attn_outs, pg, sga, x, ada, w_attn_out, w_o, ln1_g, ln1_b, w_router, b_router):
    B, S, D = x.shape
    tm = POST_TILE
    nt = S // tm
    N = B * S
    const2 = lambda b, i: (0, 0)
    in_specs, args = [], []
    for (o, lse), (_, dil) in zip(attn_outs, ATTN_GROUPS):
        spec = pl.BlockSpec((1, dil, tm // dil, GROUP_WIDTH), lambda b, i: (b, 0, i, 0))
        in_specs += [spec, spec]
        args += [o, lse]
    tok_spec = pl.BlockSpec((1, tm, D), lambda b, i: (b, i, 0))
    in_specs += [tok_spec, tok_spec, tok_spec,
                 pl.BlockSpec((1, 6, D), lambda b, i: (b, 0, 0)),
                 pl.BlockSpec((GROUP_WIDTH, D), const2),
                 pl.BlockSpec((D, D), const2),
                 pl.BlockSpec((2, D), const2),
                 pl.BlockSpec((2, N_EXPERTS, D), lambda b, i: (0, 0, 0)),
                 pl.BlockSpec((N_EXPERTS, 1), const2)]
    wr_hi = w_router.T.astype(BF16)
    wr_lo = (w_router.T - wr_hi.astype(F32)).astype(BF16)
    args += [pg, sga, x, ada, w_attn_out.astype(BF16), w_o.astype(BF16),
             jnp.stack([ln1_g, ln1_b]), jnp.stack([wr_hi, wr_lo]),
             b_router.reshape(N_EXPERTS, 1)]
    nc = N // tm
    route_spec = pl.BlockSpec((1, TOP_K, tm), lambda b, i: (b * nt + i, 0, 0))
    out_specs = [tok_spec, tok_spec, route_spec, route_spec,
                 pl.BlockSpec((1, N_EXPERTS, 1), lambda b, i: (b * nt + i, 0, 0))]
    out_shapes = [jax.ShapeDtypeStruct((B, S, D), F32), jax.ShapeDtypeStruct((B, S, D), BF16),
                  jax.ShapeDtypeStruct((nc, TOP_K, tm), jnp.int32), jax.ShapeDtypeStruct((nc, TOP_K, tm), F32),
                  jax.ShapeDtypeStruct((nc, N_EXPERTS, 1), jnp.int32)]
    return pl.pallas_call(
        _post_kernel,
        grid=(B, nt),
        in_specs=in_specs,
        out_specs=out_specs,
        out_shape=out_shapes,
        scratch_shapes=[pltpu.VMEM((GROUP_WIDTH // LANES, tm, LANES), F32)] * 4,
        compiler_params=pltpu.CompilerParams(
            dimension_semantics=("parallel", "parallel"), vmem_limit_bytes=VMEM_LIMIT_BYTES),
        name="post",
    )(*args)


def _for_each_piece(tile, piece_refs, fn):
    big_loc, big_dst, small_loc, small_dst, count = piece_refs
    n_big, n_small = count[2 * tile], count[2 * tile + 1]

    def piece(loc_ref, dst_ref, cap, rows):
        def body(p, carry):
            i = tile * cap + p
            fn(pl.multiple_of(loc_ref[i], ROW_ALIGN), pl.multiple_of(dst_ref[i], ROW_ALIGN), rows)
            return carry
        return body

    lax.fori_loop(0, n_big, piece(big_loc, big_dst, MAX_BIG_PIECES, PIECE_ROWS), 0)
    lax.fori_loop(0, n_small, piece(small_loc, small_dst, MAX_SMALL_PIECES, ROW_ALIGN), 0)
    return n_big * PIECE_ROWS + n_small * ROW_ALIGN


def _wait_rows(src_ref, dst_ref, rows, sem):
    @pl.when(rows > 0)
    def _():
        n = pl.multiple_of(rows, ROW_ALIGN)
        pltpu.make_async_copy(src_ref.at[pl.ds(0, n)], dst_ref.at[pl.ds(0, n)], sem).wait()


def _dispatch_kernel(big_loc, big_dst, small_loc, small_dst, piece_count, tail_ref, ntail_ref, nused_ref,
                     slot_ref, prob_ref, u2_ref, xs_hbm, local2_ref, zero_ref, sem2, zsem, rows_ref):
    tile = pl.program_id(0)
    buf = tile % 2
    local_ref = local2_ref.at[buf]
    sem = sem2.at[buf]
    tt, d = u2_ref.shape
    u2 = u2_ref[...]
    slots = [slot_ref[0, k:k + 1, :] for k in range(TOP_K)]
    probs = [prob_ref[0, k:k + 1, :] for k in range(TOP_K)]
    lane = lax.broadcasted_iota(jnp.int32, (SORT_CHUNK, LANES), 1)
    for r0 in range(0, LOCAL_ROWS, SORT_CHUNK):
        row = r0 + lax.broadcasted_iota(jnp.int32, (SORT_CHUNK, tt), 0)
        w = jnp.zeros((SORT_CHUNK, tt), F32)
        for k in range(TOP_K):
            w = jnp.where(row == slots[k], probs[k], w)
        onehot = jnp.where(w != 0.0, 1.0, 0.0).astype(BF16)
        local_ref[r0:r0 + SORT_CHUNK, 0:d] = _dot(onehot, u2).astype(BF16)
        wr = jnp.sum(w, axis=1, keepdims=True)
        hi = wr.astype(BF16).astype(F32)
        mid = (wr - hi).astype(BF16).astype(F32)
        lo = wr - hi - mid
        parts = jnp.where(lane == 0, hi, jnp.where(lane == 1, mid, jnp.where(lane == 2, lo, 0.0)))
        local_ref[r0:r0 + SORT_CHUNK, d:d + LANES] = parts.astype(BF16)

    def start_piece(loc, dst, rows):
        pltpu.make_async_copy(local_ref.at[pl.ds(loc, rows)], xs_hbm.at[pl.ds(dst, rows)], sem).start(
            priority=int(rows != PIECE_ROWS))

    rows_ref[buf] = _for_each_piece(tile, (big_loc, big_dst, small_loc, small_dst, piece_count), start_piece)

    @pl.when(tile > 0)
    def _():
        _wait_rows(local2_ref.at[1 - buf], xs_hbm, rows_ref[1 - buf], sem2.at[1 - buf])

    def zero_fill(first_row, n_rows, act):
        def copy(row, rows):
            return pltpu.make_async_copy(zero_ref.at[pl.ds(0, rows)],
                                         xs_hbm.at[pl.ds(pl.multiple_of(row, ROW_ALIGN), rows)], zsem)
        n_big = n_rows // ZERO_ROWS
        rest = first_row + n_big * ZERO_ROWS
        lax.fori_loop(0, n_big, lambda p, c: (act(copy(first_row + p * ZERO_ROWS, ZERO_ROWS)), c)[1], 0)
        lax.fori_loop(0, (n_rows - n_big * ZERO_ROWS) // ROW_ALIGN,
                      lambda p, c: (act(copy(rest + p * ROW_ALIGN, ROW_ALIGN)), c)[1], 0)

    def zero_fill_all(act):
        first_spare = nused_ref[0] * MOE_BLOCK
        lax.fori_loop(0, N_EXPERTS, lambda e, c: (zero_fill(tail_ref[e], ntail_ref[e], act), c)[1], 0)
        zero_fill(first_spare, xs_hbm.shape[0] - first_spare, act)

    @pl.when(tile == 0)
    def _():
        zero_ref[...] = jnp.zeros_like(zero_ref)
        zero_fill_all(lambda c: c.start())

    @pl.when(tile == pl.num_programs(0) - 1)
    def _():
        _wait_rows(local_ref, xs_hbm, rows_ref[buf], sem)
        zero_fill_all(lambda c: c.wait())


def _dispatch(tables, slot, prob, u2, n_rows):
    N, D = u2.shape
    tt = POST_TILE
    route_spec = pl.BlockSpec((1, TOP_K, tt), lambda i, *t: (i, 0, 0))
    grid_spec = pltpu.PrefetchScalarGridSpec(
        num_scalar_prefetch=len(tables),
        grid=(N // tt,),
        in_specs=[route_spec, route_spec, pl.BlockSpec((tt, D), lambda i, *t: (i, 0))],
        out_specs=pl.BlockSpec(memory_space=pl.ANY),
        scratch_shapes=[pltpu.VMEM((2, LOCAL_ROWS, D + LANES), BF16), pltpu.VMEM((ZERO_ROWS, D + LANES), BF16),
                        pltpu.SemaphoreType.DMA((2,)), pltpu.SemaphoreType.DMA(()),
                        pltpu.SMEM((2,), jnp.int32)],
    )
    return pl.pallas_call(
        _dispatch_kernel,
        grid_spec=grid_spec,
        out_shape=jax.ShapeDtypeStruct((n_rows, D + LANES), BF16),
        compiler_params=pltpu.CompilerParams(
            dimension_semantics=("arbitrary",), vmem_limit_bytes=VMEM_LIMIT_BYTES),
        name="dispatch",
    )(*tables, slot, prob, u2)


def _expert_kernel(be_ref, next_ref, live_ref, nused_ref, xs_ref, wg_hbm, wu_hbm, wd_hbm, bias_ref,
                   ys_ref, stage, wg_s, wu_s, wd_s, sem):
    i = pl.program_id(0)
    used = i < nused_ref[0]
    prev = be_ref[jnp.maximum(i - 1, 0)]
    fresh = (i == 0) | (be_ref[i] != prev)

    def fetch(e):
        return [pltpu.make_async_copy(w_hbm.at[e], stage.at[j], sem.at[j])
                for j, w_hbm in enumerate((wg_hbm, wu_hbm, wd_hbm))]

    @pl.when(i == 0)
    def _():
        for copy in fetch(be_ref[0]):
            copy.start()

    d = ys_ref.shape[1]
    full = MOE_BLOCK // EXPERT_PART

    def arrive():
        for copy in fetch(be_ref[i]):
            copy.wait()

    def prefetch():
        @pl.when(next_ref[i] >= 0)
        def _():
            for copy in fetch(next_ref[i]):
                copy.start()

    def expert(rows, round_weights=False):
        xb = xs_ref[0:rows, 0:d]
        parts = xs_ref[0:rows, d:d + LANES].astype(F32)
        weight = parts[:, 0:1] + parts[:, 1:2] + parts[:, 2:3]
        if round_weights:
            wg_s[...] = stage[0].astype(BF16)
            wu_s[...] = stage[1].astype(BF16)
        g = _dot(xb, wg_s[...]) + bias_ref[0, 0:1, :]
        up = _dot(xb, wu_s[...]) + bias_ref[0, 1:2, :]
        if round_weights:
            wd_s[...] = stage[2].astype(BF16)
        g = jnp.minimum(g, SWIGLU_LIMIT)
        up = jnp.clip(up, -SWIGLU_LIMIT, SWIGLU_LIMIT)
        h = g * jax.nn.sigmoid(SWIGLU_ALPHA * g) * (up + 1.0)
        return ((_dot(h.astype(BF16), wd_s[...]) + bias_ref[0, 2:3, :]) * weight).astype(BF16)

    @pl.when(used & fresh & (live_ref[i] == full))
    def _():
        arrive()
        ys_ref[...] = expert(MOE_BLOCK, round_weights=True)
        prefetch()

    @pl.when(used & fresh & (live_ref[i] != full))
    def _():
        arrive()
        wg_s[...] = stage[0].astype(BF16)
        wu_s[...] = stage[1].astype(BF16)
        wd_s[...] = stage[2].astype(BF16)
        prefetch()

    for parts_live in range(1, full + 1):
        rows = parts_live * EXPERT_PART
        done_above = fresh if parts_live == full else False

        @pl.when(used & (live_ref[i] == parts_live) & jnp.logical_not(done_above))
        def _():
            if rows == MOE_BLOCK:
                ys_ref[...] = expert(rows)
            else:
                ys_ref[0:rows, :] = expert(rows)
                ys_ref[rows:, :] = jnp.zeros((MOE_BLOCK - rows, d), BF16)

    @pl.when(jnp.logical_not(used))
    def _():
        ys_ref[...] = jnp.zeros_like(ys_ref)


def _experts(be, next_expert, live_parts, nused, xs, w_gate, b_gate, w_up, b_up, w_down, b_down):
    P, width = xs.shape
    D = width - LANES
    E = w_gate.shape[0]

    def live(i, nu):
        return jnp.maximum(jnp.minimum(i, nu[0] - 1), 0)

    w_spec = pl.BlockSpec(memory_space=pl.ANY)
    b_spec = pl.BlockSpec((1, 3, D), lambda i, be, nx, hf, nu: (be[live(i, nu)], 0, 0))
    grid_spec = pltpu.PrefetchScalarGridSpec(
        num_scalar_prefetch=4,
        grid=(P // MOE_BLOCK,),
        in_specs=[pl.BlockSpec((MOE_BLOCK, width), lambda i, be, nx, hf, nu: (live(i, nu), 0)),
                  w_spec, w_spec, w_spec, b_spec],
        out_specs=pl.BlockSpec((MOE_BLOCK, D), lambda i, be, nx, hf, nu: (i, 0)),
        scratch_shapes=[pltpu.VMEM((3, D, D), F32)] + [pltpu.VMEM((D, D), BF16)] * 3
                       + [pltpu.SemaphoreType.DMA((3,))],
    )
    return pl.pallas_call(
        _expert_kernel,
        grid_spec=grid_spec,
        out_shape=jax.ShapeDtypeStruct((P, D), BF16),
        compiler_params=pltpu.CompilerParams(
            dimension_semantics=("arbitrary",), vmem_limit_bytes=VMEM_LIMIT_BYTES),
        name="experts",
    )(be, next_expert, live_parts, nused, xs, w_gate, w_up, w_down,
      jnp.stack([b_gate, b_up, b_down], axis=1))


def _combine_kernel(big_loc, big_dst, small_loc, small_dst, piece_count,
                    slot_ref, x1_ref, ada_ref, ln2_ref, ys_hbm,
                    out_ref, local2_ref, sem2, rows_ref):
    tile = pl.program_id(0)
    buf = tile % 2
    tt = x1_ref.shape[0]

    def fetch(t, b):
        def start_piece(loc, dst, rows):
            pltpu.make_async_copy(ys_hbm.at[pl.ds(dst, rows)], local2_ref.at[b, pl.ds(loc, rows)],
                                  sem2.at[b]).start(priority=int(rows != PIECE_ROWS))
        rows_ref[b] = _for_each_piece(t, (big_loc, big_dst, small_loc, small_dst, piece_count), start_piece)

    @pl.when(tile == 0)
    def _():
        local2_ref[...] = jnp.zeros_like(local2_ref)
        fetch(0, 0)

    @pl.when(tile + 1 < pl.num_programs(0))
    def _():
        fetch(tile + 1, 1 - buf)

    local_ref = local2_ref.at[buf]
    _wait_rows(ys_hbm, local_ref, rows_ref[buf], sem2.at[buf])
    slots = [slot_ref[:, k:k + 1] for k in range(TOP_K)]

    ffn = jnp.zeros((tt, x1_ref.shape[1]), F32)
    for r0 in range(0, LOCAL_ROWS, SORT_CHUNK):
        col = r0 + lax.broadcasted_iota(jnp.int32, (tt, SORT_CHUNK), 1)
        onehot = jnp.zeros((tt, SORT_CHUNK), F32)
        for k in range(TOP_K):
            onehot = jnp.where(col == slots[k], 1.0, onehot)
        ffn = ffn + _dot(onehot.astype(BF16), local_ref[r0:r0 + SORT_CHUNK, :])
    gate2 = ada_ref[0, 5:6, :]
    y = DN_ALPHA * x1_ref[...] + (1.0 + gate2) * ffn
    out_ref[...] = _layer_norm(y) * ln2_ref[0:1, :] + ln2_ref[1:2, :]


def _combine(tables, slot_tok, x1, ada, ln2_g, ln2_b, ys, tiles_per_batch):
    N, D = x1.shape
    tt = POST_TILE
    const = lambda i, *t: (0, 0)
    tok4 = pl.BlockSpec((tt, TOP_K), lambda i, *t: (i, 0))
    grid_spec = pltpu.PrefetchScalarGridSpec(
        num_scalar_prefetch=len(tables),
        grid=(N // tt,),
        in_specs=[tok4,
                  pl.BlockSpec((tt, D), lambda i, *t: (i, 0)),
                  pl.BlockSpec((1, 6, D), lambda i, *t: (i // tiles_per_batch, 0, 0)),
                  pl.BlockSpec((2, D), const),
                  pl.BlockSpec(memory_space=pl.ANY)],
        out_specs=pl.BlockSpec((tt, D), lambda i, *t: (i, 0)),
        scratch_shapes=[pltpu.VMEM((2, LOCAL_ROWS, D), BF16), pltpu.SemaphoreType.DMA((2,)),
                        pltpu.SMEM((2,), jnp.int32)],
    )
    return pl.pallas_call(
        _combine_kernel,
        grid_spec=grid_spec,
        out_shape=jax.ShapeDtypeStruct((N, D), F32),
        compiler_params=pltpu.CompilerParams(
            dimension_semantics=("arbitrary",), vmem_limit_bytes=VMEM_LIMIT_BYTES),
        name="combine",
    )(*tables, slot_tok, x1, ada, jnp.stack([ln2_g, ln2_b]), ys)


def _routing_tables(cnt, nblk):
    nt, E = cnt.shape
    i32 = jnp.int32
    run = (cnt + ROW_ALIGN - 1) // ROW_ALIGN * ROW_ALIGN
    run_start = jnp.cumsum(run, axis=1) - run
    seg_len = jnp.sum(run, axis=0)
    seg_blocks = (seg_len + MOE_BLOCK - 1) // MOE_BLOCK
    b_end = jnp.cumsum(seg_blocks)
    seg_off = (b_end - seg_blocks) * MOE_BLOCK
    dest = seg_off[None, :] + jnp.cumsum(run, axis=0) - run
    nused = b_end[-1:].astype(i32)
    i = jnp.arange(nblk, dtype=i32)
    be = jnp.minimum(jnp.sum((i[:, None] >= b_end[None, :]).astype(i32), axis=1), E - 1)
    e_ids = jnp.arange(E, dtype=i32)
    later = (e_ids[None, :] > e_ids[:, None]) & (seg_blocks > 0)[None, :]
    next_e = jnp.min(jnp.where(later, e_ids[None, :], E), axis=1)
    next_e = jnp.where(next_e < E, next_e, -1)
    mine = be[:, None] == e_ids[None, :]
    pick = lambda v: jnp.sum(jnp.where(mine, v[None, :], 0), axis=1)
    nxt = pick(next_e)
    rows_left = pick(seg_len) - (i - pick(b_end - seg_blocks)) * MOE_BLOCK
    live_parts = jnp.clip((rows_left + EXPERT_PART - 1) // EXPERT_PART, 1, MOE_BLOCK // EXPERT_PART).astype(i32)
    tail = seg_off + seg_len
    ntail = seg_blocks * MOE_BLOCK - seg_len
    def flat_pieces(n, first_loc, first_dst, rows, cap):
        end = jnp.cumsum(n, axis=1)
        start = end - n
        p = jnp.arange(cap, dtype=i32)[None, :, None]
        off = p - start[:, None, :]
        owner = (off >= 0) & (p < end[:, None, :])
        loc = jnp.sum(jnp.where(owner, first_loc[:, None, :] + rows * off, 0), axis=2)
        dst = jnp.sum(jnp.where(owner, first_dst[:, None, :] + rows * off, 0), axis=2)
        return loc.reshape(-1).astype(i32), dst.reshape(-1).astype(i32), end[:, -1]

    n_big = run // PIECE_ROWS
    big_rows = n_big * PIECE_ROWS
    big_loc, big_dst, big_count = flat_pieces(n_big, run_start, dest, PIECE_ROWS, MAX_BIG_PIECES)
    small_loc, small_dst, small_count = flat_pieces((run - big_rows) // ROW_ALIGN, run_start + big_rows,
                                                    dest + big_rows, ROW_ALIGN, MAX_SMALL_PIECES)
    piece_count = jnp.stack([big_count, small_count], axis=1).reshape(-1).astype(i32)
    piece_tables = (big_loc, big_dst, small_loc, small_dst, piece_count)
    return piece_tables, (tail.astype(i32), ntail.astype(i32)), (be.astype(i32), nxt.astype(i32), live_parts, nused)


def kernel(x, c, positions, w_ada, b_ada, w_in, pool_w, pool_scale, w_pool_out, w_attn_out, w_o,
           ln1_g, ln1_b, w_router, b_router, w_gate, b_gate, w_up, b_up, w_down, b_down, ln2_g, ln2_b):
    B, S, D = x.shape
    N = B * S
    assert D == D_MODEL and S % PROJ_TILE == 0 and S % (16 * ATTN_BLOCK) == 0
    assert S % POST_TILE == 0 and LOCAL_ROWS % SORT_CHUNK == 0
    nt = N // POST_TILE
    nblk = (N * TOP_K + nt * N_EXPERTS * (ROW_ALIGN - 1)) // MOE_BLOCK + N_EXPERTS
    for l in range(DEPTH):
        ada = _ada(c, w_ada[l], b_ada[l])
        proj_out = _proj(x, positions, ada, w_in[l], pool_w[l], pool_scale[l], w_pool_out[l])
        qkv, (pg, sga) = proj_out[:9], proj_out[9:]
        attn_outs = [_attention(*qkv[3 * g:3 * g + 3]) for g in range(len(ATTN_GROUPS))]
        x1, u2, slot, prob, cnt = _post(attn_outs, pg, sga, x, ada, w_attn_out[l], w_o[l],
                                        ln1_g[l], ln1_b[l], w_router[l], b_router[l])
        piece_tables, tail_tables, block_tables = _routing_tables(cnt.reshape(nt, N_EXPERTS), nblk)
        xs = _dispatch(piece_tables + tail_tables + block_tables[3:], slot, prob, u2.reshape(N, D),
                       nblk * MOE_BLOCK)
        ys = _experts(*block_tables, xs, w_gate[l], b_gate[l], w_up[l], b_up[l], w_down[l], b_down[l])
        slot_tok = slot.transpose(0, 2, 1).reshape(N, TOP_K)
        out = _combine(piece_tables, slot_tok, x1.reshape(N, D), ada,
                       ln2_g[l], ln2_b[l], ys, S // POST_TILE)
        x = out.reshape(B, S, D)
    return x
```
